```python
import math
import jax, jax.numpy as jnp
from jax import lax
import numpy as np

D_MODEL = 2048
BATCH = 8
SEQ = 8192
DEPTH = 4

HEAD_DIM = 128
GRID_W = 64
NA_HEADS = 8
NA_WIDTH = NA_HEADS * HEAD_DIM
NA_WIN_R = 8
NA_WIN_C = 16
WA_HEADS = 8
WA_KV_HEADS = 2
WA_WIDTH = WA_HEADS * HEAD_DIM
WA_KV_WIDTH = WA_KV_HEADS * HEAD_DIM
WA_WINDOW = 128
WA_BLOCK = 128
MIX_WIDTH = NA_WIDTH + WA_WIDTH
PROJ_SPLITS = (NA_WIDTH, NA_WIDTH, NA_WIDTH, WA_WIDTH, WA_KV_WIDTH, WA_KV_WIDTH)
PROJ_WIDTH = sum(PROJ_SPLITS)
D_FF = 5632
CONV_W = 3
ROPE_THETA = 10000.0
EPS = 1e-6
NEG = -1e30

kernel_name = "hybrid_na_swa_convffn_encoder"


def rms_norm(x, g):
    xf = x.astype(jnp.float32)
    y = xf * lax.rsqrt(jnp.mean(xf * xf, axis=-1, keepdims=True) + EPS)
    return (y * g.astype(jnp.float32)).astype(x.dtype)


def rope(x, positions):
    d = x.shape[-1]
    inv = ROPE_THETA ** (-jnp.arange(0, d, 2, dtype=jnp.float32) / d)
    ang = positions.astype(jnp.float32)[:, None] * inv[None, :]
    cos = jnp.cos(ang)[None, :, None, :]
    sin = jnp.sin(ang)[None, :, None, :]
    xf = x.astype(jnp.float32)
    x1, x2 = xf[..., : d // 2], xf[..., d // 2:]
    out = jnp.concatenate([x1 * cos - x2 * sin, x2 * cos + x1 * sin], axis=-1)
    return out.astype(x.dtype)


def neighborhood_attention(q, k, v, rpb):
    b, s, h, d = q.shape
    rows = s // GRID_W
    wr = min(NA_WIN_R, rows)
    wc = NA_WIN_C
    qg = q.reshape(b, rows, GRID_W, h, d)
    kg = k.reshape(b, rows, GRID_W, h, d)
    vg = v.reshape(b, rows, GRID_W, h, d)
    r = jnp.arange(rows)
    row_start = jnp.clip(r - wr // 2, 0, rows - wr)
    row_idx = row_start[:, None] + jnp.arange(wr)[None, :]
    k_rows = kg[:, row_idx]
    v_rows = vg[:, row_idx]
    c = jnp.arange(GRID_W)
    col_start = jnp.clip(c - wc // 2, 0, GRID_W - wc)
    col_mask = (c[None, :] >= col_start[:, None]) & (c[None, :] < col_start[:, None] + wc)
    dr = row_idx - r[:, None] + (NA_WIN_R - 1)
    dc = jnp.clip(c[None, :] - c[:, None], -(wc - 1), wc - 1) + (NA_WIN_C - 1)
    bias = rpb[:, dr]
    bias = bias[:, :, :, dc]
    bias = jnp.transpose(bias, (0, 1, 3, 2, 4)).astype(jnp.float32)
    scale = 1.0 / math.sqrt(d)
    sc = jnp.einsum('brqhd,brwkhd->bhrqwk', qg, k_rows).astype(jnp.float32) * scale
    sc = sc + bias[None]
    sc = jnp.where(col_mask[:, None, :], sc, NEG)
    shp = sc.shape
    p = jax.nn.softmax(sc.reshape(shp[:-2] + (wr * GRID_W,)), axis=-1).reshape(shp)
    out = jnp.einsum('bhrqwk,brwkhd->brqhd', p.astype(v.dtype), v_rows)
    return out.reshape(b, s, h * d)


def windowed_gqa_sink(q, k, v, sink):
    b, s, hq, d = q.shape
    hkv = k.shape[2]
    g = hq // hkv
    nb = s // WA_BLOCK
    qb = q.reshape(b, nb, WA_BLOCK, hkv, g, d)
    pad = ((0, 0), (WA_BLOCK, WA_BLOCK), (0, 0), (0, 0))
    kp = jnp.pad(k, pad).reshape(b, nb + 2, WA_BLOCK, hkv, d)
    vp = jnp.pad(v, pad).reshape(b, nb + 2, WA_BLOCK, hkv, d)
    kw = jnp.concatenate([kp[:, :-2], kp[:, 1:-1], kp[:, 2:]], axis=2)
    vw = jnp.concatenate([vp[:, :-2], vp[:, 1:-1], vp[:, 2:]], axis=2)
    blk = jnp.arange(nb)[:, None]
    qpos = blk * WA_BLOCK + jnp.arange(WA_BLOCK)[None, :]
    kpos = (blk - 1) * WA_BLOCK + jnp.arange(3 * WA_BLOCK)[None, :]
    diff = kpos[:, None, :] - qpos[:, :, None]
    valid = (jnp.abs(diff) <= WA_WINDOW) & (kpos[:, None, :] >= 0) & (kpos[:, None, :] < s)
    scale = 1.0 / math.sqrt(d)
    sc = jnp.einsum('bnqhgd,bnkhd->bhgnqk', qb, kw).astype(jnp.float32) * scale
    sc = jnp.where(valid, sc, NEG)
    sink_l = sink.astype(jnp.float32).reshape(hkv, g)[None, :, :, None, None, None]
    m = jnp.maximum(jnp.max(sc, axis=-1, keepdims=True), sink_l)
    e = jnp.exp(sc - m)
    p = e / (jnp.sum(e, axis=-1, keepdims=True) + jnp.exp(sink_l - m))
    out = jnp.einsum('bhgnqk,bnkhd->bnqhgd', p.astype(v.dtype), vw)
    return out.reshape(b, s, hq * d)


def depthwise_conv(u, w, bias):
    up = jnp.pad(u, ((0, 0), (1, 1), (0, 0)))
    return up[:, :-2] * w[0] + up[:, 1:-1] * w[1] + up[:, 2:] * w[2] + bias


def _fwd_setup_inputs(seed: int = 0) -> dict:
    key = jax.random.key(seed)
    ks = jax.random.split(key, 20)
    f32 = jnp.float32
    nrm = lambda k, shp, sc: jax.random.normal(k, shp, f32) * sc
    centre = jnp.zeros((CONV_W, 1), f32).at[CONV_W // 2].set(1.0)
    return {
        "x": nrm(ks[0], (BATCH, SEQ, D_MODEL), 1.0),
        "positions": jnp.arange(SEQ, dtype=jnp.int32),
        "ln1_g": 1.0 + nrm(ks[1], (DEPTH, D_MODEL), 0.02),
        "w_in": nrm(ks[2], (DEPTH, D_MODEL, PROJ_WIDTH), D_MODEL ** -0.5),
        "qn_a": 1.0 + nrm(ks[3], (DEPTH, HEAD_DIM), 0.02),
        "kn_a": 1.0 + nrm(ks[4], (DEPTH, HEAD_DIM), 0.02),
        "rpb": nrm(ks[5], (DEPTH, NA_HEADS, 2 * NA_WIN_R - 1, 2 * NA_WIN_C - 1), 0.1),
        "qn_b": 1.0 + nrm(ks[6], (DEPTH, HEAD_DIM), 0.02),
        "kn_b": 1.0 + nrm(ks[7], (DEPTH, HEAD_DIM), 0.02),
        "sink": nrm(ks[8], (DEPTH, WA_HEADS), 0.5),
        "on_a": 1.0 + nrm(ks[9], (DEPTH, NA_WIDTH), 0.02),
        "on_b": 1.0 + nrm(ks[10], (DEPTH, WA_WIDTH), 0.02),
        "w_out": nrm(ks[11], (DEPTH, MIX_WIDTH, D_MODEL), 0.5 * MIX_WIDTH ** -0.5),
        "ln2_g": 1.0 + nrm(ks[12], (DEPTH, D_MODEL), 0.02),
        "w_up": nrm(ks[13], (DEPTH, D_MODEL, 2 * D_FF), D_MODEL ** -0.5),
        "conv_w": centre[None] + nrm(ks[14], (DEPTH, CONV_W, 2 * D_FF), 0.3),
        "conv_b": nrm(ks[15], (DEPTH, 2 * D_FF), 0.01),
        "w_down": nrm(ks[16], (DEPTH, D_FF, D_MODEL), 0.5 * D_FF ** -0.5),
    }


def _fwd_reference(x, positions, ln1_g, w_in, qn_a, kn_a, rpb, qn_b, kn_b, sink,
              on_a, on_b, w_out, ln2_g, w_up, conv_w, conv_b, w_down):
    b, s, _ = x.shape
    cuts = list(np.cumsum(PROJ_SPLITS)[:-1])
    for l in range(DEPTH):
        h = rms_norm(x, ln1_g[l])
        proj = h @ w_in[l]
        qa, ka, va, qb, kb, vb = jnp.split(proj, cuts, axis=-1)
        qa = rms_norm(qa.reshape(b, s, NA_HEADS, HEAD_DIM), qn_a[l])
        ka = rms_norm(ka.reshape(b, s, NA_HEADS, HEAD_DIM), kn_a[l])
        va = va.reshape(b, s, NA_HEADS, HEAD_DIM)
        oa = neighborhood_attention(qa, ka, va, rpb[l])
        qb = rope(rms_norm(qb.reshape(b, s, WA_HEADS, HEAD_DIM), qn_b[l]), positions)
        kb = rope(rms_norm(kb.reshape(b, s, WA_KV_HEADS, HEAD_DIM), kn_b[l]), positions)
        vb = vb.reshape(b, s, WA_KV_HEADS, HEAD_DIM)
        ob = windowed_gqa_sink(qb, kb, vb, sink[l])
        o = jnp.concatenate([rms_norm(oa, on_a[l]), rms_norm(ob, on_b[l])], axis=-1)
        x = x + o @ w_out[l]
        h2 = rms_norm(x, ln2_g[l])
        u = depthwise_conv(h2 @ w_up[l], conv_w[l], conv_b[l])
        gate, up = u[..., :D_FF], u[..., D_FF:]
        x = x + (jax.nn.silu(gate) * up) @ w_down[l]
    return x


import jax as _jax
import jax.numpy as _jnp

TWIN_FORMAT = 'train_step'
FWD_PARAMS = ['x', 'positions', 'ln1_g', 'w_in', 'qn_a', 'kn_a', 'rpb', 'qn_b', 'kn_b', 'sink', 'on_a', 'on_b', 'w_out', 'ln2_g', 'w_up', 'conv_w', 'conv_b', 'w_down']
TWIN_WEIGHTS = ['ln1_g', 'w_in', 'qn_a', 'kn_a', 'rpb', 'qn_b', 'kn_b', 'sink', 'on_a', 'on_b', 'w_out', 'ln2_g', 'w_up', 'conv_w', 'conv_b', 'w_down']
TWIN_DIFF_INPUT = 'x'
TWIN_INPUTS = ['x', 'positions', 'ln1_g', 'w_in', 'qn_a', 'kn_a', 'rpb', 'qn_b', 'kn_b', 'sink', 'on_a', 'on_b', 'w_out', 'ln2_g', 'w_up', 'conv_w', 'conv_b', 'w_down', 'loss_target', 'm_ln1_g', 'm_w_in', 'm_qn_a', 'm_kn_a', 'm_rpb', 'm_qn_b', 'm_kn_b', 'm_sink', 'm_on_a', 'm_on_b', 'm_w_out', 'm_ln2_g', 'm_w_up', 'm_conv_w', 'm_conv_b', 'm_w_down', 'v_ln1_g', 'v_w_in', 'v_qn_a', 'v_kn_a', 'v_rpb', 'v_qn_b', 'v_kn_b', 'v_sink', 'v_on_a', 'v_on_b', 'v_w_out', 'v_ln2_g', 'v_w_up', 'v_conv_w', 'v_conv_b', 'v_w_down']
TWIN_OUTPUTS = ['loss', 'grad_x', 'grad_ln1_g', 'grad_w_in', 'grad_qn_a', 'grad_kn_a', 'grad_rpb', 'grad_qn_b', 'grad_kn_b', 'grad_sink', 'grad_on_a', 'grad_on_b', 'grad_w_out', 'grad_ln2_g', 'grad_w_up', 'grad_conv_w', 'grad_conv_b', 'grad_w_down', 'delta_ln1_g', 'delta_w_in', 'delta_qn_a', 'delta_kn_a', 'delta_rpb', 'delta_qn_b', 'delta_kn_b', 'delta_sink', 'delta_on_a', 'delta_on_b', 'delta_w_out', 'delta_ln2_g', 'delta_w_up', 'delta_conv_w', 'delta_conv_b', 'delta_w_down', 'new_m_ln1_g', 'new_m_w_in', 'new_m_qn_a', 'new_m_kn_a', 'new_m_rpb', 'new_m_qn_b', 'new_m_kn_b', 'new_m_sink', 'new_m_on_a', 'new_m_on_b', 'new_m_w_out', 'new_m_ln2_g', 'new_m_w_up', 'new_m_conv_w', 'new_m_conv_b', 'new_m_w_down', 'new_v_ln1_g', 'new_v_w_in', 'new_v_qn_a', 'new_v_kn_a', 'new_v_rpb', 'new_v_qn_b', 'new_v_kn_b', 'new_v_sink', 'new_v_on_a', 'new_v_on_b', 'new_v_w_out', 'new_v_ln2_g', 'new_v_w_up', 'new_v_conv_w', 'new_v_conv_b', 'new_v_w_down']
TWIN_LEAF_KINDS = {'loss': 'loss', 'grad_x': 'grad_x', 'grad_ln1_g': 'grad_w', 'grad_w_in': 'grad_w', 'grad_qn_a': 'grad_w', 'grad_kn_a': 'grad_w', 'grad_rpb': 'grad_w', 'grad_qn_b': 'grad_w', 'grad_kn_b': 'grad_w', 'grad_sink': 'grad_w', 'grad_on_a': 'grad_w', 'grad_on_b': 'grad_w', 'grad_w_out': 'grad_w', 'grad_ln2_g': 'grad_w', 'grad_w_up': 'grad_w', 'grad_conv_w': 'grad_w', 'grad_conv_b': 'grad_w', 'grad_w_down': 'grad_w', 'delta_ln1_g': 'delta_w', 'delta_w_in': 'delta_w', 'delta_qn_a': 'delta_w', 'delta_kn_a': 'delta_w', 'delta_rpb': 'delta_w', 'delta_qn_b': 'delta_w', 'delta_kn_b': 'delta_w', 'delta_sink': 'delta_w', 'delta_on_a': 'delta_w', 'delta_on_b': 'delta_w', 'delta_w_out': 'delta_w', 'delta_ln2_g': 'delta_w', 'delta_w_up': 'delta_w', 'delta_conv_w': 'delta_w', 'delta_conv_b': 'delta_w', 'delta_w_down': 'delta_w', 'new_m_ln1_g': 'new_m', 'new_m_w_in': 'new_m', 'new_m_qn_a': 'new_m', 'new_m_kn_a': 'new_m', 'new_m_rpb': 'new_m', 'new_m_qn_b': 'new_m', 'new_m_kn_b': 'new_m', 'new_m_sink': 'new_m', 'new_m_on_a': 'new_m', 'new_m_on_b': 'new_m', 'new_m_w_out': 'new_m', 'new_m_ln2_g': 'new_m', 'new_m_w_up': 'new_m', 'new_m_conv_w': 'new_m', 'new_m_conv_b': 'new_m', 'new_m_w_down': 'new_m', 'new_v_ln1_g': 'new_v', 'new_v_w_in': 'new_v', 'new_v_qn_a': 'new_v', 'new_v_kn_a': 'new_v', 'new_v_rpb': 'new_v', 'new_v_qn_b': 'new_v', 'new_v_kn_b': 'new_v', 'new_v_sink': 'new_v', 'new_v_on_a': 'new_v', 'new_v_on_b': 'new_v', 'new_v_w_out': 'new_v', 'new_v_ln2_g': 'new_v', 'new_v_w_up': 'new_v', 'new_v_conv_w': 'new_v', 'new_v_conv_b': 'new_v', 'new_v_w_down': 'new_v'}


def _forward(args):
    return _fwd_reference(*[args[k] for k in FWD_PARAMS])


def _output_shape():
    def fwd():
        inp = _fwd_setup_inputs(0)
        return _fwd_reference(*[inp[k] for k in FWD_PARAMS])
    out = _jax.eval_shape(fwd)
    return out.shape, out.dtype

N_MICROBATCH = 1
ADAM_LR = 0.001
ADAM_B1 = 0.9
ADAM_B2 = 0.999
ADAM_EPS = 1e-08
ADAM_WD = 0.01
ADAM_STEP = 10
PER_EXAMPLE_BATCH_AXIS = {'x': 0, 'loss_target': 0}
SHARED_INPUTS = ['positions']
_WEIGHT_DTYPES = {'ln1_g': _jnp.float32, 'w_in': _jnp.float32, 'qn_a': _jnp.float32, 'kn_a': _jnp.float32, 'rpb': _jnp.float32, 'qn_b': _jnp.float32, 'kn_b': _jnp.float32, 'sink': _jnp.float32, 'on_a': _jnp.float32, 'on_b': _jnp.float32, 'w_out': _jnp.float32, 'ln2_g': _jnp.float32, 'w_up': _jnp.float32, 'conv_w': _jnp.float32, 'conv_b': _jnp.float32, 'w_down': _jnp.float32}
MOMENT_SCALE = {'ln1_g': 1.955375e+00, 'w_in': 1.396183e+00, 'qn_a': 5.682301e-01, 'kn_a': 5.673485e-01, 'rpb': 8.427561e-02, 'qn_b': 6.050290e-01, 'kn_b': 7.186875e-01, 'sink': 4.024236e-02, 'on_a': 7.915507e+00, 'on_b': 7.435947e+00, 'w_out': 4.320020e+00, 'ln2_g': 1.043228e+01, 'w_up': 2.317190e-01, 'conv_w': 1.112368e+00, 'conv_b': 1.287067e+00, 'w_down': 7.794233e-01}


def _to_microbatches(a, axis):
    t = _jnp.moveaxis(a, axis, 0)
    t = t.reshape((N_MICROBATCH, t.shape[0] // N_MICROBATCH) + t.shape[1:])
    return _jnp.moveaxis(t, 1, axis + 1)


def setup_inputs(seed: int = 0) -> dict:
    inp = _fwd_setup_inputs(seed)
    key = _jax.random.fold_in(_jax.random.key(seed), 7919)
    shape, _ = _output_shape()
    out = dict(inp)
    out["loss_target"] = _jax.random.normal(_jax.random.fold_in(key, 0), shape, _jnp.float32)
    for i, name in enumerate(TWIN_WEIGHTS):
        w = inp[name].astype(_jnp.float32)
        if MOMENT_SCALE is None:
            s = _jnp.sqrt(_jnp.mean(_jnp.square(w)) + 1e-30)
        else:
            s = MOMENT_SCALE[name]
        km, kv = _jax.random.split(_jax.random.fold_in(key, i + 1))
        out[name] = w
        out["m_" + name] = s * _jax.random.normal(km, w.shape, _jnp.float32)
        out["v_" + name] = (s * s) * _jax.random.uniform(kv, w.shape, _jnp.float32, 0.5, 1.5)
    if N_MICROBATCH > 1:
        for name, axis in PER_EXAMPLE_BATCH_AXIS.items():
            out[name] = _to_microbatches(out[name], axis)
    return {'x': out['x'], 'positions': out['positions'], 'ln1_g': out['ln1_g'], 'w_in': out['w_in'], 'qn_a': out['qn_a'], 'kn_a': out['kn_a'], 'rpb': out['rpb'], 'qn_b': out['qn_b'], 'kn_b': out['kn_b'], 'sink': out['sink'], 'on_a': out['on_a'], 'on_b': out['on_b'], 'w_out': out['w_out'], 'ln2_g': out['ln2_g'], 'w_up': out['w_up'], 'conv_w': out['conv_w'], 'conv_b': out['conv_b'], 'w_down': out['w_down'], 'loss_target': out['loss_target'], 'm_ln1_g': out['m_ln1_g'], 'm_w_in': out['m_w_in'], 'm_qn_a': out['m_qn_a'], 'm_kn_a': out['m_kn_a'], 'm_rpb': out['m_rpb'], 'm_qn_b': out['m_qn_b'], 'm_kn_b': out['m_kn_b'], 'm_sink': out['m_sink'], 'm_on_a': out['m_on_a'], 'm_on_b': out['m_on_b'], 'm_w_out': out['m_w_out'], 'm_ln2_g': out['m_ln2_g'], 'm_w_up': out['m_w_up'], 'm_conv_w': out['m_conv_w'], 'm_conv_b': out['m_conv_b'], 'm_w_down': out['m_w_down'], 'v_ln1_g': out['v_ln1_g'], 'v_w_in': out['v_w_in'], 'v_qn_a': out['v_qn_a'], 'v_kn_a': out['v_kn_a'], 'v_rpb': out['v_rpb'], 'v_qn_b': out['v_qn_b'], 'v_kn_b': out['v_kn_b'], 'v_sink': out['v_sink'], 'v_on_a': out['v_on_a'], 'v_on_b': out['v_on_b'], 'v_w_out': out['v_w_out'], 'v_ln2_g': out['v_ln2_g'], 'v_w_up': out['v_w_up'], 'v_conv_w': out['v_conv_w'], 'v_conv_b': out['v_conv_b'], 'v_w_down': out['v_w_down']}


def _loss(weights, diff, rest, loss_target):
    with _jax.named_scope("forward"):
        args = {**rest, TWIN_DIFF_INPUT: diff, **{k: w.astype(_WEIGHT_DTYPES[k]) for k, w in weights.items()}}
        y = _forward(args)
    with _jax.named_scope("loss_head"):
        err = _jnp.square(y.astype(_jnp.float32) - loss_target)
        return 0.5 * _jnp.sum(_jnp.mean(err, axis=-1)) if err.ndim else 0.5 * err


def _adamw(w, g, m, v):
    m = ADAM_B1 * m + (1.0 - ADAM_B1) * g
    v = ADAM_B2 * v + (1.0 - ADAM_B2) * _jnp.square(g)
    m_hat = m / (1.0 - ADAM_B1 ** ADAM_STEP)
    v_hat = v / (1.0 - ADAM_B2 ** ADAM_STEP)
    delta = -ADAM_LR * (m_hat / (_jnp.sqrt(v_hat) + ADAM_EPS) + ADAM_WD * w)
    return delta, m, v


def reference(x, positions, ln1_g, w_in, qn_a, kn_a, rpb, qn_b, kn_b, sink, on_a, on_b, w_out, ln2_g, w_up, conv_w, conv_b, w_down, loss_target, m_ln1_g, m_w_in, m_qn_a, m_kn_a, m_rpb, m_qn_b, m_kn_b, m_sink, m_on_a, m_on_b, m_w_out, m_ln2_g, m_w_up, m_conv_w, m_conv_b, m_w_down, v_ln1_g, v_w_in, v_qn_a, v_kn_a, v_rpb, v_qn_b, v_kn_b, v_sink, v_on_a, v_on_b, v_w_out, v_ln2_g, v_w_up, v_conv_w, v_conv_b, v_w_down):
    given = dict(x=x, positions=positions, ln1_g=ln1_g, w_in=w_in, qn_a=qn_a, kn_a=kn_a, rpb=rpb, qn_b=qn_b, kn_b=kn_b, sink=sink, on_a=on_a, on_b=on_b, w_out=w_out, ln2_g=ln2_g, w_up=w_up, conv_w=conv_w, conv_b=conv_b, w_down=w_down, loss_target=loss_target, m_ln1_g=m_ln1_g, m_w_in=m_w_in, m_qn_a=m_qn_a, m_kn_a=m_kn_a, m_rpb=m_rpb, m_qn_b=m_qn_b, m_kn_b=m_kn_b, m_sink=m_sink, m_on_a=m_on_a, m_on_b=m_on_b, m_w_out=m_w_out, m_ln2_g=m_ln2_g, m_w_up=m_w_up, m_conv_w=m_conv_w, m_conv_b=m_conv_b, m_w_down=m_w_down, v_ln1_g=v_ln1_g, v_w_in=v_w_in, v_qn_a=v_qn_a, v_kn_a=v_kn_a, v_rpb=v_rpb, v_qn_b=v_qn_b, v_kn_b=v_kn_b, v_sink=v_sink, v_on_a=v_on_a, v_on_b=v_on_b, v_w_out=v_w_out, v_ln2_g=v_ln2_g, v_w_up=v_w_up, v_conv_w=v_conv_w, v_conv_b=v_conv_b, v_w_down=v_w_down)
    weights = {n: given[n] for n in TWIN_WEIGHTS}
    shared = {n: given[n] for n in SHARED_INPUTS}
    per_example = {n: given[n] for n in ['x']}
    grad_fn = _jax.value_and_grad(_loss, argnums=(0, 1))

    def one_microbatch(ex, loss_target):
        ex = dict(ex)
        diff = ex.pop(TWIN_DIFF_INPUT)
        return grad_fn(weights, diff, {**shared, **ex}, loss_target)

    if N_MICROBATCH == 1:
        loss, (grad_w, grad_x) = one_microbatch(per_example, given["loss_target"])
    else:
        def body(carry, xs):
            loss_sum, grad_sum = carry
            l_k, (gw_k, gx_k) = one_microbatch(xs[0], xs[1])
            with _jax.named_scope("update"):
                return (loss_sum + l_k, _jax.tree.map(_jnp.add, grad_sum, gw_k)), gx_k

        init = (_jnp.zeros((), _jnp.float32), _jax.tree.map(_jnp.zeros_like, weights))
        (loss, grad_w), grad_x = _jax.lax.scan(body, init, (per_example, given["loss_target"]))
    with _jax.named_scope("update"):
        delta_w, new_m, new_v = {}, {}, {}
        for n in TWIN_WEIGHTS:
            delta_w[n], new_m[n], new_v[n] = _adamw(weights[n], grad_w[n], given["m_" + n], given["v_" + n])
    return (loss, grad_x, *[grad_w[n] for n in TWIN_WEIGHTS], *[delta_w[n] for n in TWIN_WEIGHTS],
            *[new_m[n] for n in TWIN_WEIGHTS], *[new_v[n] for n in TWIN_WEIGHTS])
```

```python
import functools
import math

import jax
import jax.numpy as jnp
from jax import lax
from jax.experimental import pallas as pl
from jax.experimental.pallas import tpu as pltpu

F32 = jnp.float32
BF16 = jnp.bfloat16

HEAD = 128
GRID_W = 64
WIN_R = 8
WIN_C = 16
BAND = 128
ROPE_THETA = 10000.0
EPS = 1e-6
NEG = -1e30
SCALE = 1.0 / math.sqrt(HEAD)

ADAM_LR = 0.001
ADAM_B1 = 0.9
ADAM_B2 = 0.999
ADAM_EPS = 1e-08
ADAM_WD = 0.01
ADAM_STEP = 10

N_DEV = 8
LANES = 128
VMEM_LIMIT_BYTES = 56 * 2 ** 20
MESH = pl.DeviceIdType.MESH
ANY = pl.BlockSpec(memory_space=pl.ANY)
SMEM = pl.BlockSpec(memory_space=pltpu.SMEM)


def _params():
    return pltpu.CompilerParams(vmem_limit_bytes=VMEM_LIMIT_BYTES)


def _tile(n, pref, align):
    t = min(n, pref)
    t -= t % align
    while t > 0 and n % t:
        t -= align
    return t if t > 0 else n


def _place():
    x, y, c = lax.axis_index("x"), lax.axis_index("y"), lax.axis_index("c")
    chips = [(1 - x, y), (x, 1 - y), (1 - x, 1 - y)]
    return x, y, c, chips


def _allgather(v, name):
    def body(v_ref, out_ref, send_sems, recv_sems, local_sem):
        x, y, c, chips = _place()
        me, sibling = (x, y, c), (x, y, 1 - c)

        def slot(px, py, pc):
            return out_ref.at[4 * px + 2 * py + pc]

        def copy(k, block, to, src=None):
            return pltpu.make_async_remote_copy(
                src_ref=slot(*block) if src is None else src, dst_ref=slot(*block),
                send_sem=send_sems.at[k], recv_sem=recv_sems.at[k], device_id=to, device_id_type=MESH)

        mine = pltpu.make_async_copy(v_ref, slot(*me), local_sem)
        mine.start()
        first = [copy(0, me, sibling, src=v_ref)]
        first += [copy(1 + j, me, (*chip, c), src=v_ref) for j, chip in enumerate(chips)]
        for cp in first:
            cp.start()
        passed = [copy(4 + j, (*chip, c), sibling) for j, chip in enumerate(chips)]
        for j, chip in enumerate(chips):
            copy(1 + j, (*chip, c), me).wait_recv()
            passed[j].start()
        copy(0, sibling, me).wait_recv()
        for j, chip in enumerate(chips):
            copy(4 + j, (*chip, 1 - c), me).wait_recv()
        for cp in first + passed:
            cp.wait_send()
        mine.wait()

    return pl.pallas_call(
        body, name=name,
        out_shape=jax.ShapeDtypeStruct((N_DEV,) + v.shape, v.dtype),
        in_specs=[ANY], out_specs=ANY,
        scratch_shapes=[pltpu.SemaphoreType.DMA((7,)), pltpu.SemaphoreType.DMA((7,)), pltpu.SemaphoreType.DMA],
    )(v)


def _sibling_exchange(g, name):
    def body(g_ref, out_ref, send_sems, recv_sems):
        x, y, c, _ = _place()
        sibling = (x, y, 1 - c)
        copies = []
        for j in range(4):
            copies.append(pltpu.make_async_remote_copy(
                src_ref=g_ref.at[2 * j + (1 - c)], dst_ref=out_ref.at[j],
                send_sem=send_sems.at[j], recv_sem=recv_sems.at[j], device_id=sibling, device_id_type=MESH))
        for cp in copies:
            cp.start()
        for cp in copies:
            cp.wait_recv()
        for cp in copies:
            cp.wait_send()

    return pl.pallas_call(
        body, name=name,
        out_shape=jax.ShapeDtypeStruct((4,) + g.shape[1:], g.dtype),
        in_specs=[ANY], out_specs=ANY,
        scratch_shapes=[pltpu.SemaphoreType.DMA((4,)), pltpu.SemaphoreType.DMA((4,))],
    )(g)


def _pair_sum(g, got, core, name):
    _, r, c = g.shape
    tr = _tile(r, max(16, (1 << 20) // c), 16)

    def body(core_ref, g_ref, got_ref, o_ref):
        del core_ref
        o_ref[...] = (g_ref[...].astype(F32) + got_ref[...].astype(F32)).astype(o_ref.dtype)

    return pl.pallas_call(
        body, name=name,
        out_shape=jax.ShapeDtypeStruct((4, r, c), g.dtype),
        grid_spec=pltpu.PrefetchScalarGridSpec(
            num_scalar_prefetch=1, grid=(4, r // tr),
            in_specs=[pl.BlockSpec((None, tr, c), lambda j, i, core_ref: (2 * j + core_ref[0], i, 0)),
                      pl.BlockSpec((None, tr, c), lambda j, i, core_ref: (j, i, 0))],
            out_specs=pl.BlockSpec((None, tr, c), lambda j, i, core_ref: (j, i, 0))),
        compiler_params=_params(),
    )(core, g, got)


def _chip_exchange(p, name):
    def body(p_ref, out_ref, send_sems, recv_sems, local_sem):
        x, y, c, chips = _place()
        mine = pltpu.make_async_copy(p_ref.at[2 * x + y], out_ref.at[3], local_sem)
        mine.start()
        copies = []
        for k, (px, py) in enumerate(chips):
            copies.append(pltpu.make_async_remote_copy(
                src_ref=p_ref.at[2 * px + py], dst_ref=out_ref.at[k],
                send_sem=send_sems.at[k], recv_sem=recv_sems.at[k], device_id=(px, py, c), device_id_type=MESH))
        for cp in copies:
            cp.start()
        for cp in copies:
            cp.wait_recv()
        for cp in copies:
            cp.wait_send()
        mine.wait()

    return pl.pallas_call(
        body, name=name,
        out_shape=jax.ShapeDtypeStruct(p.shape, p.dtype),
        in_specs=[ANY], out_specs=ANY,
        scratch_shapes=[pltpu.SemaphoreType.DMA((3,)), pltpu.SemaphoreType.DMA((3,)), pltpu.SemaphoreType.DMA],
    )(p)


def _reduce_scatter(g, core, name):
    got = _sibling_exchange(g, name + "_d2d")
    p = _pair_sum(g, got, core, name + "_pair")
    return _chip_exchange(p, name + "_ici")


def _adamw(parts, w, m, v, name):
    n_parts, r, c = parts.shape
    tr = _tile(r, max(8, (1 << 19) // c), 16 if parts.dtype == BF16 else 8)
    c1 = 1.0 - ADAM_B1 ** ADAM_STEP
    c2 = 1.0 - ADAM_B2 ** ADAM_STEP

    def body(p_ref, w_ref, m_ref, v_ref, g_out, d_out, m_out, v_out):
        g = p_ref[0].astype(F32)
        for k in range(1, n_parts):
            g = g + p_ref[k].astype(F32)
        m2 = ADAM_B1 * m_ref[...] + (1.0 - ADAM_B1) * g
        v2 = ADAM_B2 * v_ref[...] + (1.0 - ADAM_B2) * (g * g)
        g_out[...] = g
        m_out[...] = m2
        v_out[...] = v2
        d_out[...] = -ADAM_LR * ((m2 / c1) / (jnp.sqrt(v2 / c2) + ADAM_EPS) + ADAM_WD * w_ref[...])

    blk = pl.BlockSpec((tr, c), lambda i: (i, 0))
    out = jax.ShapeDtypeStruct((r, c), F32)
    return pl.pallas_call(
        body, name=name, grid=(r // tr,),
        in_specs=[pl.BlockSpec((n_parts, tr, c), lambda i: (0, i, 0)), blk, blk, blk],
        out_specs=[blk, blk, blk, blk], out_shape=[out, out, out, out],
        compiler_params=_params(),
    )(parts, w, m, v)


def _matmul(a, b, *, dims, ti, tj, tk, out_dtype, name, j_outer=True, resid=None, dev_major=False):
    if dims == "nn":
        (I, K), (K2, J) = a.shape, b.shape
    elif dims == "nt":
        (I, K), (J, K2) = a.shape, b.shape
    else:
        (K, I), (K2, J) = a.shape, b.shape
    assert K == K2 and I % ti == 0 and J % tj == 0 and K % tk == 0, (name, a.shape, b.shape, ti, tj, tk)
    ni, nj, nk = I // ti, J // tj, K // tk

    def ij(g0, g1):
        return (g1, g0) if j_outer else (g0, g1)

    if dims == "nn":
        a_spec = pl.BlockSpec((ti, tk), lambda g0, g1, k: (ij(g0, g1)[0], k))
        b_spec = pl.BlockSpec((tk, tj), lambda g0, g1, k: (k, ij(g0, g1)[1]))
        dn = (((1,), (0,)), ((), ()))
    elif dims == "nt":
        a_spec = pl.BlockSpec((ti, tk), lambda g0, g1, k: (ij(g0, g1)[0], k))
        b_spec = pl.BlockSpec((tj, tk), lambda g0, g1, k: (ij(g0, g1)[1], k))
        dn = (((1,), (1,)), ((), ()))
    else:
        a_spec = pl.BlockSpec((tk, ti), lambda g0, g1, k: (k, ij(g0, g1)[0]))
        b_spec = pl.BlockSpec((tk, tj), lambda g0, g1, k: (k, ij(g0, g1)[1]))
        dn = (((0,), (0,)), ((), ()))
    in_specs = [a_spec, b_spec]
    operands = [a, b]
    if resid is not None:
        in_specs.append(pl.BlockSpec((ti, tj), lambda g0, g1, k: ij(g0, g1)))
        operands.append(resid)
    if dev_major:
        out_spec = pl.BlockSpec((None, ti, tj), lambda g0, g1, k: (ij(g0, g1)[1], ij(g0, g1)[0], 0))
        out_shape = jax.ShapeDtypeStruct((nj, I, tj), out_dtype)
    else:
        out_spec = pl.BlockSpec((ti, tj), lambda g0, g1, k: ij(g0, g1))
        out_shape = jax.ShapeDtypeStruct((I, J), out_dtype)

    def body(*refs):
        a_ref, b_ref = refs[0], refs[1]
        r_ref = refs[2] if resid is not None else None
        o_ref = refs[3] if resid is not None else refs[2]
        part = lax.dot_general(a_ref[...].astype(BF16), b_ref[...].astype(BF16), dn, preferred_element_type=F32)

        def finish(acc):
            if r_ref is not None:
                acc = acc + r_ref[...]
            o_ref[...] = acc.astype(o_ref.dtype)

        if nk == 1:
            finish(part)
        else:
            acc_ref = refs[-1]
            k = pl.program_id(2)

            @pl.when(k == 0)
            def _():
                acc_ref[...] = part

            @pl.when(k > 0)
            def _():
                acc_ref[...] += part

            @pl.when(k == nk - 1)
            def _():
                finish(acc_ref[...])

    return pl.pallas_call(
        body, name=name, grid=(nj, ni, nk) if j_outer else (ni, nj, nk),
        in_specs=in_specs, out_specs=out_spec, out_shape=out_shape,
        scratch_shapes=[pltpu.VMEM((ti, tj), F32)] if nk > 1 else [],
        compiler_params=_params(),
    )(*operands)


def _rms_fwd(x, g, name):
    s, d = x.shape
    ts = _tile(s, 256, 16)

    def body(x_ref, g_ref, h_ref):
        xv = x_ref[...]
        r = lax.rsqrt(jnp.mean(xv * xv, axis=-1, keepdims=True) + EPS)
        h_ref[...] = (xv * r * g_ref[...]).astype(BF16)

    return pl.pallas_call(
        body, name=name, grid=(s // ts,),
        in_specs=[pl.BlockSpec((ts, d), lambda i: (i, 0)), pl.BlockSpec((1, d), lambda i: (0, 0))],
        out_specs=pl.BlockSpec((ts, d), lambda i: (i, 0)),
        out_shape=jax.ShapeDtypeStruct((s, d), BF16), compiler_params=_params(),
    )(x, g)


def _rms_bwd(x, g, dh, dres, name):
    s, d = x.shape
    ts = _tile(s, 256, 8)

    def body(x_ref, g_ref, dh_ref, dres_ref, dx_ref, dg_ref):
        xv = x_ref[...]
        r = lax.rsqrt(jnp.mean(xv * xv, axis=-1, keepdims=True) + EPS)
        y = xv * r
        dhv = dh_ref[...]
        gd = dhv * g_ref[...]
        dx_ref[...] = dres_ref[...] + r * (gd - y * jnp.mean(gd * y, axis=-1, keepdims=True))

        @pl.when(pl.program_id(0) == 0)
        def _():
            dg_ref[...] = jnp.zeros_like(dg_ref)

        dg_ref[0:1, :] += jnp.sum(dhv * y, axis=0, keepdims=True)

    blk = pl.BlockSpec((ts, d), lambda i: (i, 0))
    return pl.pallas_call(
        body, name=name, grid=(s // ts,),
        in_specs=[blk, pl.BlockSpec((1, d), lambda i: (0, 0)), blk, blk],
        out_specs=[blk, pl.BlockSpec((8, d), lambda i: (0, 0))],
        out_shape=[jax.ShapeDtypeStruct((s, d), F32), jax.ShapeDtypeStruct((8, d), F32)],
        compiler_params=_params(),
    )(x, g, dh, dres)


def _head_norm(t, gain):
    r = lax.rsqrt(jnp.mean(t * t, axis=-1, keepdims=True) + EPS)
    return t * r * gain


def _head_norm_bwd(t, gain, dn):
    r = lax.rsqrt(jnp.mean(t * t, axis=-1, keepdims=True) + EPS)
    y = t * r
    gd = dn * gain
    dt = r * (gd - y * jnp.mean(gd * y, axis=-1, keepdims=True))
    return dt, jnp.sum(dn * y, axis=0, keepdims=True)


def _rope(n, cos, sin):
    return n * cos + pltpu.roll(n, HEAD // 2, axis=1) * sin


def _rope_bwd(do, cos, sin):
    return do * cos + pltpu.roll(do * sin, HEAD // 2, axis=1)


def _qkv_fwd(proj, gains, cos, sin, cfg, name):
    s, pw = proj.shape
    ha, hq, hkv = cfg
    ts = _tile(s, 256, 16)

    def body(p_ref, gn_ref, cos_ref, sin_ref, qa_ref, ka_ref, va_ref, qb_ref, kb_ref, vb_ref):
        cosv, sinv = cos_ref[...], sin_ref[...]
        col = 0
        for out_ref, nh, gi, rot in ((qa_ref, ha, 0, False), (ka_ref, ha, 1, False), (va_ref, ha, None, False),
                                     (qb_ref, hq, 2, True), (kb_ref, hkv, 3, True), (vb_ref, hkv, None, False)):
            for h in range(nh):
                t = p_ref[:, col * HEAD:(col + 1) * HEAD]
                if gi is not None:
                    t = _head_norm(t, gn_ref[gi:gi + 1, :])
                if rot:
                    t = _rope(t, cosv, sinv)
                out_ref[h] = t.astype(BF16)
                col += 1

    def hm(nh):
        return pl.BlockSpec((nh, ts, HEAD), lambda i: (0, i, 0)), jax.ShapeDtypeStruct((nh, s, HEAD), BF16)

    specs, shapes = zip(hm(ha), hm(ha), hm(ha), hm(hq), hm(hkv), hm(hkv))
    tok = pl.BlockSpec((ts, HEAD), lambda i: (i, 0))
    return pl.pallas_call(
        body, name=name, grid=(s // ts,),
        in_specs=[pl.BlockSpec((ts, pw), lambda i: (i, 0)), pl.BlockSpec((8, HEAD), lambda i: (0, 0)), tok, tok],
        out_specs=list(specs), out_shape=list(shapes), compiler_params=_params(),
    )(proj, gains, cos, sin)


def _qkv_bwd(proj, gains, cos, sin, grads, cfg, name):
    s, pw = proj.shape
    ha, hq, hkv = cfg
    ts = _tile(s, 256, 16)

    def body(p_ref, gn_ref, cos_ref, sin_ref, dqa, dka, dva, dqb, dkb, dvb, dp_ref, dgn_ref):
        cosv, sinv = cos_ref[...], sin_ref[...]

        @pl.when(pl.program_id(0) == 0)
        def _():
            dgn_ref[...] = jnp.zeros_like(dgn_ref)

        col = 0
        for d_ref, nh, gi, rot in ((dqa, ha, 0, False), (dka, ha, 1, False), (dva, ha, None, False),
                                   (dqb, hq, 2, True), (dkb, hkv, 3, True), (dvb, hkv, None, False)):
            dgain = jnp.zeros((1, HEAD), F32)
            for h in range(nh):
                dt = d_ref[h]
                if rot:
                    dt = _rope_bwd(dt, cosv, sinv)
                if gi is not None:
                    dt, dg = _head_norm_bwd(p_ref[:, col * HEAD:(col + 1) * HEAD], gn_ref[gi:gi + 1, :], dt)
                    dgain = dgain + dg
                dp_ref[:, col * HEAD:(col + 1) * HEAD] = dt.astype(BF16)
                col += 1
            if gi is not None:
                dgn_ref[gi:gi + 1, :] += dgain

    def hm(nh):
        return pl.BlockSpec((nh, ts, HEAD), lambda i: (0, i, 0))

    tok = pl.BlockSpec((ts, HEAD), lambda i: (i, 0))
    small = pl.BlockSpec((8, HEAD), lambda i: (0, 0))
    return pl.pallas_call(
        body, name=name, grid=(s // ts,),
        in_specs=[pl.BlockSpec((ts, pw), lambda i: (i, 0)), small, tok, tok,
                  hm(ha), hm(ha), hm(ha), hm(hq), hm(hkv), hm(hkv)],
        out_specs=[pl.BlockSpec((ts, pw), lambda i: (i, 0)), small],
        out_shape=[jax.ShapeDtypeStruct((s, pw), BF16), jax.ShapeDtypeStruct((8, HEAD), F32)],
        compiler_params=_params(),
    )(proj, gains, cos, sin, *grads)


NA_QROWS = 8
NA_WIN_TOK = 2 * WIN_R * GRID_W
NA_KEYS = WIN_R * GRID_W
N_DR = 2 * WIN_R - 1
N_DC = 2 * WIN_C - 1


def _na_bias(rpb_flat, n_heads, name):
    def body(rpb_ref, tb_ref):
        h = pl.program_id(0)
        qi = lax.broadcasted_iota(jnp.int32, (GRID_W, LANES), 0)
        lane = lax.broadcasted_iota(jnp.int32, (GRID_W, LANES), 1)
        kk = lane & (GRID_W - 1)
        upper = lane >= GRID_W
        dcm = kk - qi + (WIN_C - 1)
        cs = jnp.clip(qi - WIN_C // 2, 0, GRID_W - WIN_C)
        valid = (kk >= cs) & (kk < cs + WIN_C)
        base = h * (N_DR * N_DC)
        for dra in range(N_DR - 1):
            def step(j, acc, dra=dra):
                va = rpb_ref[base + dra * N_DC + j]
                vb = rpb_ref[base + (dra + 1) * N_DC + j]
                return jnp.where(dcm == j, jnp.where(upper, vb, va), acc)

            pair = lax.fori_loop(0, N_DC, step, jnp.zeros((GRID_W, LANES), F32))
            pair = jnp.where(valid, pair, NEG)
            for dr0 in range(WIN_R):
                wp, odd = divmod(dra - dr0, 2)
                if odd == 0 and 0 <= wp < WIN_R // 2:
                    tb_ref[0, dr0, :, wp * LANES:(wp + 1) * LANES] = pair

    return pl.pallas_call(
        body, name=name, grid=(n_heads,),
        in_specs=[SMEM],
        out_specs=pl.BlockSpec((1, WIN_R, GRID_W, NA_KEYS), lambda h: (h, 0, 0, 0)),
        out_shape=jax.ShapeDtypeStruct((n_heads, WIN_R, GRID_W, NA_KEYS), F32),
        compiler_params=_params(),
    )(rpb_flat)


def _na_window(b, nrows):
    return jnp.clip(b * NA_QROWS - WIN_R // 2, 0, nrows - 2 * WIN_R)


def _na_row(b, i, row0, nrows):
    r = b * NA_QROWS + i
    rs = jnp.clip(r - WIN_R // 2, 0, nrows - WIN_R)
    return pl.multiple_of((rs - row0) * GRID_W, GRID_W), rs - r + (WIN_R - 1)


def _softmax(s):
    e = jnp.exp(s - jnp.max(s, axis=-1, keepdims=True))
    return e / jnp.sum(e, axis=-1, keepdims=True)


_NT = (((1,), (1,)), ((), ()))
_NN = (((1,), (0,)), ((), ()))
_TN = (((0,), (0,)), ((), ()))


def _dot(a, b, dn):
    return lax.dot_general(a, b, dn, preferred_element_type=F32)


def _na_fwd(q, k, v, tb, name):
    nh, s, _ = q.shape
    nrows = s // GRID_W
    tq = NA_QROWS * GRID_W

    def body(q_ref, k_hbm, v_hbm, tb_ref, o_ref, kbuf, vbuf, sems):
        h, b = pl.program_id(0), pl.program_id(1)
        row0 = _na_window(b, nrows)
        t0 = pl.multiple_of(row0 * GRID_W, GRID_W)
        ck = pltpu.make_async_copy(k_hbm.at[h, pl.ds(t0, NA_WIN_TOK), :], kbuf, sems.at[0])
        cv = pltpu.make_async_copy(v_hbm.at[h, pl.ds(t0, NA_WIN_TOK), :], vbuf, sems.at[1])
        ck.start()
        cv.start()
        ck.wait()
        cv.wait()
        for i in range(NA_QROWS):
            off, dr0 = _na_row(b, i, row0, nrows)
            qi = q_ref[i * GRID_W:(i + 1) * GRID_W, :]
            p = _softmax(_dot(qi, kbuf[pl.ds(off, NA_KEYS), :], _NT) * SCALE + tb_ref[0, dr0])
            o_ref[i * GRID_W:(i + 1) * GRID_W, :] = _dot(p.astype(BF16), vbuf[pl.ds(off, NA_KEYS), :], _NN)

    qspec = pl.BlockSpec((None, tq, HEAD), lambda h, b: (h, b, 0))
    return pl.pallas_call(
        body, name=name, grid=(nh, nrows // NA_QROWS),
        in_specs=[qspec, ANY, ANY, pl.BlockSpec((1, WIN_R, GRID_W, NA_KEYS), lambda h, b: (h, 0, 0, 0))],
        out_specs=qspec, out_shape=jax.ShapeDtypeStruct((nh, s, HEAD), F32),
        scratch_shapes=[pltpu.VMEM((NA_WIN_TOK, HEAD), BF16), pltpu.VMEM((NA_WIN_TOK, HEAD), BF16),
                        pltpu.SemaphoreType.DMA((2,))],
        compiler_params=_params(),
    )(q, k, v, tb)


def _na_bwd(q, k, v, tb, do, name):
    nh, s, _ = q.shape
    nrows = s // GRID_W
    tq = NA_QROWS * GRID_W

    def body(q_ref, do_ref, k_hbm, v_hbm, tb_ref, dq_ref, dk_ref, dv_ref, dtb_ref, kbuf, vbuf, sems):
        h, b = pl.program_id(0), pl.program_id(1)
        row0 = _na_window(b, nrows)
        t0 = pl.multiple_of(row0 * GRID_W, GRID_W)
        ck = pltpu.make_async_copy(k_hbm.at[h, pl.ds(t0, NA_WIN_TOK), :], kbuf, sems.at[0])
        cv = pltpu.make_async_copy(v_hbm.at[h, pl.ds(t0, NA_WIN_TOK), :], vbuf, sems.at[1])
        ck.start()
        cv.start()

        @pl.when(b == 0)
        def _():
            dk_ref[...] = jnp.zeros_like(dk_ref)
            dv_ref[...] = jnp.zeros_like(dv_ref)
            dtb_ref[...] = jnp.zeros_like(dtb_ref)

        ck.wait()
        cv.wait()
        for i in range(NA_QROWS):
            off, dr0 = _na_row(b, i, row0, nrows)
            keys = pl.ds(pl.multiple_of(t0 + off, GRID_W), NA_KEYS)
            qi = q_ref[i * GRID_W:(i + 1) * GRID_W, :]
            doi = do_ref[i * GRID_W:(i + 1) * GRID_W, :]
            ki = kbuf[pl.ds(off, NA_KEYS), :]
            vi = vbuf[pl.ds(off, NA_KEYS), :]
            p = _softmax(_dot(qi, ki, _NT) * SCALE + tb_ref[0, dr0])
            dp = _dot(doi, vi, _NT)
            dv_ref[keys, :] += _dot(p.astype(BF16), doi, _TN)
            ds = p * (dp - jnp.sum(p * dp, axis=-1, keepdims=True))
            dtb_ref[0, dr0] += ds
            dss = (ds * SCALE).astype(BF16)
            dq_ref[i * GRID_W:(i + 1) * GRID_W, :] = _dot(dss, ki, _NN)
            dk_ref[keys, :] += _dot(dss, qi, _TN)

    qspec = pl.BlockSpec((None, tq, HEAD), lambda h, b: (h, b, 0))
    full = pl.BlockSpec((None, s, HEAD), lambda h, b: (h, 0, 0))
    tbs = pl.BlockSpec((1, WIN_R, GRID_W, NA_KEYS), lambda h, b: (h, 0, 0, 0))
    hm = jax.ShapeDtypeStruct((nh, s, HEAD), F32)
    return pl.pallas_call(
        body, name=name, grid=(nh, nrows // NA_QROWS),
        in_specs=[qspec, qspec, ANY, ANY, tbs],
        out_specs=[qspec, full, full, tbs],
        out_shape=[hm, hm, hm, jax.ShapeDtypeStruct((nh, WIN_R, GRID_W, NA_KEYS), F32)],
        scratch_shapes=[pltpu.VMEM((NA_WIN_TOK, HEAD), BF16), pltpu.VMEM((NA_WIN_TOK, HEAD), BF16),
                        pltpu.SemaphoreType.DMA((2,))],
        compiler_params=_params(),
    )(q, do, k, v, tb)


def _rpb_fold(y, n_heads, name):
    def body(y_ref, o_ref):
        for h in range(n_heads):
            for dr in range(2 * WIN_R):
                acc = jnp.zeros((1, LANES), F32)
                for dr0 in range(WIN_R):
                    w = dr - dr0
                    if 0 <= w < WIN_R:
                        acc = acc + y_ref[h, dr0, w:w + 1, :]
                o_ref[h, dr:dr + 1, :] = acc

    return pl.pallas_call(
        body, name=name, out_shape=jax.ShapeDtypeStruct((n_heads, 2 * WIN_R, LANES), F32),
    )(y)


def _rpb_grad(dtb, onehot, name):
    nh = dtb.shape[0]
    rows = dtb.reshape(nh, WIN_R, GRID_W, WIN_R, GRID_W).transpose(0, 1, 3, 2, 4).reshape(nh * WIN_R * WIN_R, GRID_W * GRID_W)
    y = _matmul(rows, onehot, dims="nn", ti=rows.shape[0], tj=LANES, tk=GRID_W * GRID_W, out_dtype=F32, name=name + "_dc")
    folded = _rpb_fold(y.reshape(nh, WIN_R, WIN_R, LANES), nh, name + "_dr")
    return folded[:, :N_DR, :N_DC]


WA_WIN_TOK = 3 * BAND


def _wa_scores(q, kwin, t0, j, sink):
    s = _dot(q, kwin, _NT) * SCALE
    qpos = j * BAND + lax.broadcasted_iota(jnp.int32, (BAND, WA_WIN_TOK), 0)
    kpos = t0 + lax.broadcasted_iota(jnp.int32, (BAND, WA_WIN_TOK), 1)
    s = jnp.where(jnp.abs(kpos - qpos) <= BAND, s, NEG)
    m = jnp.maximum(jnp.max(s, axis=-1, keepdims=True), sink)
    e = jnp.exp(s - m)
    es = jnp.exp(sink - m)
    z = jnp.sum(e, axis=-1, keepdims=True) + es
    return e / z, es / z


def _wa_fwd(q, k, v, sink, name):
    hq, s, _ = q.shape
    hkv = k.shape[0]
    grp = hq // hkv

    def body(sink_ref, q_ref, k_hbm, v_hbm, o_ref, kbuf, vbuf, sems):
        kh, j = pl.program_id(0), pl.program_id(1)
        t0 = pl.multiple_of(jnp.clip((j - 1) * BAND, 0, s - WA_WIN_TOK), BAND)
        ck = pltpu.make_async_copy(k_hbm.at[kh, pl.ds(t0, WA_WIN_TOK), :], kbuf, sems.at[0])
        cv = pltpu.make_async_copy(v_hbm.at[kh, pl.ds(t0, WA_WIN_TOK), :], vbuf, sems.at[1])
        ck.start()
        cv.start()
        ck.wait()
        cv.wait()
        for g in range(grp):
            p, _ = _wa_scores(q_ref[g], kbuf[...], t0, j, sink_ref[kh * grp + g])
            o_ref[g] = _dot(p.astype(BF16), vbuf[...], _NN)

    qspec = pl.BlockSpec((grp, BAND, HEAD), lambda kh, j: (kh, j, 0))
    return pl.pallas_call(
        body, name=name, grid=(hkv, s // BAND),
        in_specs=[SMEM, qspec, ANY, ANY],
        out_specs=qspec, out_shape=jax.ShapeDtypeStruct((hq, s, HEAD), F32),
        scratch_shapes=[pltpu.VMEM((WA_WIN_TOK, HEAD), BF16), pltpu.VMEM((WA_WIN_TOK, HEAD), BF16),
                        pltpu.SemaphoreType.DMA((2,))],
        compiler_params=_params(),
    )(sink, q, k, v)


def _wa_bwd(q, k, v, sink, do, name):
    hq, s, _ = q.shape
    hkv = k.shape[0]
    grp = hq // hkv

    def body(sink_ref, q_ref, do_ref, k_hbm, v_hbm, dq_ref, dk_ref, dv_ref, dsink_ref, kbuf, vbuf, sems):
        kh, j = pl.program_id(0), pl.program_id(1)
        t0 = pl.multiple_of(jnp.clip((j - 1) * BAND, 0, s - WA_WIN_TOK), BAND)
        ck = pltpu.make_async_copy(k_hbm.at[kh, pl.ds(t0, WA_WIN_TOK), :], kbuf, sems.at[0])
        cv = pltpu.make_async_copy(v_hbm.at[kh, pl.ds(t0, WA_WIN_TOK), :], vbuf, sems.at[1])
        ck.start()
        cv.start()

        @pl.when(j == 0)
        def _():
            dk_ref[...] = jnp.zeros_like(dk_ref)
            dv_ref[...] = jnp.zeros_like(dv_ref)
            dsink_ref[...] = jnp.zeros_like(dsink_ref)

        ck.wait()
        cv.wait()
        keys = pl.ds(t0, WA_WIN_TOK)
        for g in range(grp):
            qg, dog = q_ref[g], do_ref[g]
            p, ps = _wa_scores(qg, kbuf[...], t0, j, sink_ref[kh * grp + g])
            dp = _dot(dog, vbuf[...], _NT)
            dv_ref[keys, :] += _dot(p.astype(BF16), dog, _TN)
            rowdot = jnp.sum(p * dp, axis=-1, keepdims=True)
            dsink_ref[g] += jnp.zeros((8, LANES), F32) - jnp.sum(ps * rowdot)
            dss = (p * (dp - rowdot) * SCALE).astype(BF16)
            dq_ref[g] = _dot(dss, kbuf[...], _NN)
            dk_ref[keys, :] += _dot(dss, qg, _TN)

    qspec = pl.BlockSpec((grp, BAND, HEAD), lambda kh, j: (kh, j, 0))
    full = pl.BlockSpec((None, s, HEAD), lambda kh, j: (kh, 0, 0))
    kv = jax.ShapeDtypeStruct((hkv, s, HEAD), F32)
    return pl.pallas_call(
        body, name=name, grid=(hkv, s // BAND),
        in_specs=[SMEM, qspec, qspec, ANY, ANY],
        out_specs=[qspec, full, full, pl.BlockSpec((grp, 8, LANES), lambda kh, j: (kh, 0, 0))],
        out_shape=[jax.ShapeDtypeStruct((hq, s, HEAD), F32), kv, kv, jax.ShapeDtypeStruct((hq, 8, LANES), F32)],
        scratch_shapes=[pltpu.VMEM((WA_WIN_TOK, HEAD), BF16), pltpu.VMEM((WA_WIN_TOK, HEAD), BF16),
                        pltpu.SemaphoreType.DMA((2,))],
        compiler_params=_params(),
    )(sink, q, do, k, v)


def _onorm_fwd(oa, ob, gains, name):
    ha, s, _ = oa.shape
    hq = ob.shape[0]
    ts = _tile(s, 256, 16)

    def body(oa_ref, ob_ref, g_ref, o_ref):
        col = 0
        for ref, nh in ((oa_ref, ha), (ob_ref, hq)):
            ss = sum(jnp.sum(ref[h] * ref[h], axis=-1, keepdims=True) for h in range(nh))
            r = lax.rsqrt(ss / (nh * HEAD) + EPS)
            for h in range(nh):
                o_ref[:, col * HEAD:(col + 1) * HEAD] = (ref[h] * r * g_ref[:, col * HEAD:(col + 1) * HEAD]).astype(BF16)
                col += 1

    mix = (ha + hq) * HEAD
    return pl.pallas_call(
        body, name=name, grid=(s // ts,),
        in_specs=[pl.BlockSpec((ha, ts, HEAD), lambda i: (0, i, 0)), pl.BlockSpec((hq, ts, HEAD), lambda i: (0, i, 0)),
                  pl.BlockSpec((1, mix), lambda i: (0, 0))],
        out_specs=pl.BlockSpec((ts, mix), lambda i: (i, 0)),
        out_shape=jax.ShapeDtypeStruct((s, mix), BF16), compiler_params=_params(),
    )(oa, ob, gains)


def _onorm_bwd(oa, ob, gains, don, name):
    ha, s, _ = oa.shape
    hq = ob.shape[0]
    ts = _tile(s, 256, 16)
    mix = (ha + hq) * HEAD

    def body(oa_ref, ob_ref, g_ref, don_ref, doa_ref, dob_ref, dg_ref):
        @pl.when(pl.program_id(0) == 0)
        def _():
            dg_ref[...] = jnp.zeros_like(dg_ref)

        col0 = 0
        for ref, d_ref, nh in ((oa_ref, doa_ref, ha), (ob_ref, dob_ref, hq)):
            ss = sum(jnp.sum(ref[h] * ref[h], axis=-1, keepdims=True) for h in range(nh))
            r = lax.rsqrt(ss / (nh * HEAD) + EPS)
            dot = jnp.zeros((ts, 1), F32)
            for h in range(nh):
                cols = slice((col0 + h) * HEAD, (col0 + h + 1) * HEAD)
                dot = dot + jnp.sum(don_ref[:, cols] * g_ref[:, cols] * ref[h], axis=-1, keepdims=True)
            mean = dot * r / (nh * HEAD)
            for h in range(nh):
                cols = slice((col0 + h) * HEAD, (col0 + h + 1) * HEAD)
                y = ref[h] * r
                dn = don_ref[:, cols]
                d_ref[h] = (r * (dn * g_ref[:, cols] - y * mean)).astype(BF16)
                dg_ref[0:1, cols] += jnp.sum(dn * y, axis=0, keepdims=True)
            col0 += nh

    return pl.pallas_call(
        body, name=name, grid=(s // ts,),
        in_specs=[pl.BlockSpec((ha, ts, HEAD), lambda i: (0, i, 0)), pl.BlockSpec((hq, ts, HEAD), lambda i: (0, i, 0)),
                  pl.BlockSpec((1, mix), lambda i: (0, 0)), pl.BlockSpec((ts, mix), lambda i: (i, 0))],
        out_specs=[pl.BlockSpec((ha, ts, HEAD), lambda i: (0, i, 0)), pl.BlockSpec((hq, ts, HEAD), lambda i: (0, i, 0)),
                   pl.BlockSpec((8, mix), lambda i: (0, 0))],
        out_shape=[jax.ShapeDtypeStruct((ha, s, HEAD), BF16), jax.ShapeDtypeStruct((hq, s, HEAD), BF16),
                   jax.ShapeDtypeStruct((8, mix), F32)],
        compiler_params=_params(),
    )(oa, ob, gains, don)


def _shift_rows(cur, halo_prev, halo_next, i, n):
    ts = cur.shape[0]
    row = lax.broadcasted_iota(jnp.int32, cur.shape, 0)
    first = jnp.where(i > 0, halo_prev[7:8, :], 0.0)
    last = jnp.where(i < n - 1, halo_next[0:1, :], 0.0)
    prev = jnp.where(row == 0, first, pltpu.roll(cur, 1, axis=0))
    nxt = jnp.where(row == ts - 1, last, pltpu.roll(cur, ts - 1, axis=0))
    return prev, nxt


def _halo_specs(ts, tc, col_off):
    per = ts // 8
    cur = pl.BlockSpec((ts, tc), lambda j, i: (i, j + col_off))
    prev = pl.BlockSpec((8, tc), lambda j, i: (jnp.maximum(i * per - 1, 0), j + col_off))

    def nxt_map(n_blocks):
        return pl.BlockSpec((8, tc), lambda j, i: (jnp.minimum((i + 1) * per, n_blocks - 1), j + col_off))

    return cur, prev, nxt_map


def _sigmoid(x):
    return 1.0 / (1.0 + jnp.exp(-x))


def _ffn_tiles(s, f):
    return _tile(s, 512, 16), _tile(f, 512, LANES)


def _gate_fwd(u, cw, cb, f, name):
    s = u.shape[0]
    ts, tc = _ffn_tiles(s, f)
    nj, ni = f // tc, s // ts

    def body(g_ref, gp_ref, gn_ref, u_ref, up_ref, un_ref, wg_ref, wu_ref, bg_ref, bu_ref, a_ref):
        i = pl.program_id(1)

        def conv(c_ref, p_ref, n_ref, w_ref, b_ref):
            cur = c_ref[...]
            prev, nxt = _shift_rows(cur, p_ref[...], n_ref[...], i, ni)
            return prev * w_ref[0:1, :] + cur * w_ref[1:2, :] + nxt * w_ref[2:3, :] + b_ref[...]

        gate = conv(g_ref, gp_ref, gn_ref, wg_ref, bg_ref)
        up = conv(u_ref, up_ref, un_ref, wu_ref, bu_ref)
        a_ref[...] = (gate * _sigmoid(gate) * up).astype(BF16)

    gc, gp, gn = _halo_specs(ts, tc, 0)
    uc, up_, un = _halo_specs(ts, tc, nj)
    wg = pl.BlockSpec((3, tc), lambda j, i: (0, j))
    wu = pl.BlockSpec((3, tc), lambda j, i: (0, j + nj))
    bg = pl.BlockSpec((1, tc), lambda j, i: (0, j))
    bu = pl.BlockSpec((1, tc), lambda j, i: (0, j + nj))
    return pl.pallas_call(
        body, name=name, grid=(nj, ni),
        in_specs=[gc, gp, gn(s // 8), uc, up_, un(s // 8), wg, wu, bg, bu],
        out_specs=pl.BlockSpec((ts, tc), lambda j, i: (i, j)),
        out_shape=jax.ShapeDtypeStruct((s, f), BF16), compiler_params=_params(),
    )(u, u, u, u, u, u, cw, cw, cb, cb)


def _gate_bwd(u, cw, cb, da, f, name):
    s = u.shape[0]
    ts, tc = _ffn_tiles(s, f)
    nj, ni = f // tc, s // ts

    def body(g_ref, gp_ref, gn_ref, u_ref, up_ref, un_ref, wg_ref, wu_ref, bg_ref, bu_ref, da_ref,
             dg_ref, du_ref, dcw_ref, dcb_ref):
        i = pl.program_id(1)

        @pl.when(i == 0)
        def _():
            dcw_ref[...] = jnp.zeros_like(dcw_ref)
            dcb_ref[...] = jnp.zeros_like(dcb_ref)

        def conv(c_ref, p_ref, n_ref, w_ref, b_ref):
            cur = c_ref[...]
            prev, nxt = _shift_rows(cur, p_ref[...], n_ref[...], i, ni)
            return prev * w_ref[0:1, :] + cur * w_ref[1:2, :] + nxt * w_ref[2:3, :] + b_ref[...], (prev, cur, nxt)

        gate, g_taps = conv(g_ref, gp_ref, gn_ref, wg_ref, bg_ref)
        up, u_taps = conv(u_ref, up_ref, un_ref, wu_ref, bu_ref)
        dav = da_ref[...]
        sg = _sigmoid(gate)
        d_up = dav * gate * sg
        d_gate = dav * up * (sg * (1.0 + gate * (1.0 - sg)))
        dg_ref[...] = d_gate
        du_ref[...] = d_up
        for half, (d, taps) in enumerate(((d_gate, g_taps), (d_up, u_taps))):
            dcb_ref[half, 0:1, :] += jnp.sum(d, axis=0, keepdims=True)
            for k in range(3):
                dcw_ref[half, k, 0:1, :] += jnp.sum(d * taps[k], axis=0, keepdims=True)

    gc, gp, gn = _halo_specs(ts, tc, 0)
    uc, up_, un = _halo_specs(ts, tc, nj)
    wg = pl.BlockSpec((3, tc), lambda j, i: (0, j))
    wu = pl.BlockSpec((3, tc), lambda j, i: (0, j + nj))
    bg = pl.BlockSpec((1, tc), lambda j, i: (0, j))
    bu = pl.BlockSpec((1, tc), lambda j, i: (0, j + nj))
    tile = pl.BlockSpec((ts, tc), lambda j, i: (i, j))
    return pl.pallas_call(
        body, name=name, grid=(nj, ni),
        in_specs=[gc, gp, gn(s // 8), uc, up_, un(s // 8), wg, wu, bg, bu, tile],
        out_specs=[tile, tile, pl.BlockSpec((2, 3, 8, tc), lambda j, i: (0, 0, 0, j)),
                   pl.BlockSpec((2, 8, tc), lambda j, i: (0, 0, j))],
        out_shape=[jax.ShapeDtypeStruct((s, f), F32), jax.ShapeDtypeStruct((s, f), F32),
                   jax.ShapeDtypeStruct((2, 3, 8, f), F32), jax.ShapeDtypeStruct((2, 8, f), F32)],
        compiler_params=_params(),
    )(u, u, u, u, u, u, cw, cw, cb, cb, da)


def _conv_bwd(dg, du, cw, f, name):
    s = dg.shape[0]
    ts, tc = _ffn_tiles(s, f)
    nj, ni = f // tc, s // ts

    def body(c_ref, p_ref, n_ref, w_ref, o_ref):
        i = pl.program_id(1)
        cur = c_ref[...]
        prev, nxt = _shift_rows(cur, p_ref[...], n_ref[...], i, ni)
        o_ref[...] = (prev * w_ref[2:3, :] + cur * w_ref[1:2, :] + nxt * w_ref[0:1, :]).astype(BF16)

    halves = []
    for half, d in enumerate((dg, du)):
        cur, prev, nxt = _halo_specs(ts, tc, 0)
        halves.append(pl.pallas_call(
            functools.partial(body), name=f"{name}_{half}", grid=(nj, ni),
            in_specs=[cur, prev, nxt(s // 8), pl.BlockSpec((3, tc), lambda j, i, half=half: (0, j + half * nj))],
            out_specs=pl.BlockSpec((ts, tc), lambda j, i: (i, j)),
            out_shape=jax.ShapeDtypeStruct((s, f), BF16), compiler_params=_params(),
        )(d, d, d, cw))
    return halves


def _loss_head(y, target, name):
    s, d = y.shape
    ts = _tile(s, 256, 8)

    def body(y_ref, t_ref, dy_ref, l_ref):
        @pl.when(pl.program_id(0) == 0)
        def _():
            l_ref[...] = jnp.zeros_like(l_ref)

        err = y_ref[...] - t_ref[...]
        dy_ref[...] = err / d
        l_ref[...] += jnp.zeros((8, LANES), F32) + 0.5 * jnp.sum(jnp.sum(err * err, axis=-1, keepdims=True) / d)

    blk = pl.BlockSpec((ts, d), lambda i: (i, 0))
    return pl.pallas_call(
        body, name=name, grid=(s // ts,),
        in_specs=[blk, blk], out_specs=[blk, pl.BlockSpec((8, LANES), lambda i: (0, 0))],
        out_shape=[jax.ShapeDtypeStruct((s, d), F32), jax.ShapeDtypeStruct((8, LANES), F32)],
        compiler_params=_params(),
    )(y, target)


SMALL = ("ln1_g", "qn_a", "kn_a", "rpb", "qn_b", "kn_b", "sink", "on_a", "on_b", "ln2_g", "conv_b", "conv_w")
PACK_ALIGN = 8 * LANES


def _pack(arrays):
    flat = []
    for a in arrays:
        a = a.reshape(-1)
        flat.append(jnp.pad(a, (0, -a.size % PACK_ALIGN)))
    return jnp.concatenate(flat).reshape(-1, LANES)


def _unpack(packed, like):
    out, at = [], 0
    flat = packed.reshape(-1)
    for a in like:
        out.append(flat[at:at + a.size].reshape(a.shape))
        at += a.size + (-a.size % PACK_ALIGN)
    return out


def _matmul_tiles(s, k, j):
    return dict(ti=_tile(s, 512, 16), tj=_tile(j, 1536, LANES), tk=_tile(k, 2048, LANES))


def kernel(x, positions, ln1_g, w_in, qn_a, kn_a, rpb, qn_b, kn_b, sink, on_a, on_b, w_out, ln2_g, w_up, conv_w, conv_b, w_down, loss_target, m_ln1_g, m_w_in, m_qn_a, m_kn_a, m_rpb, m_qn_b, m_kn_b, m_sink, m_on_a, m_on_b, m_w_out, m_ln2_g, m_w_up, m_conv_w, m_conv_b, m_w_down, v_ln1_g, v_w_in, v_qn_a, v_kn_a, v_rpb, v_qn_b, v_kn_b, v_sink, v_on_a, v_on_b, v_w_out, v_ln2_g, v_w_up, v_conv_w, v_conv_b, v_w_down):
    weights = dict(ln1_g=ln1_g, w_in=w_in, qn_a=qn_a, kn_a=kn_a, rpb=rpb, qn_b=qn_b, kn_b=kn_b, sink=sink, on_a=on_a,
                   on_b=on_b, w_out=w_out, ln2_g=ln2_g, w_up=w_up, conv_w=conv_w, conv_b=conv_b, w_down=w_down)
    mom1 = dict(ln1_g=m_ln1_g, w_in=m_w_in, qn_a=m_qn_a, kn_a=m_kn_a, rpb=m_rpb, qn_b=m_qn_b, kn_b=m_kn_b, sink=m_sink,
                on_a=m_on_a, on_b=m_on_b, w_out=m_w_out, ln2_g=m_ln2_g, w_up=m_w_up, conv_w=m_conv_w, conv_b=m_conv_b,
                w_down=m_w_down)
    mom2 = dict(ln1_g=v_ln1_g, w_in=v_w_in, qn_a=v_qn_a, kn_a=v_kn_a, rpb=v_rpb, qn_b=v_qn_b, kn_b=v_kn_b, sink=v_sink,
                on_a=v_on_a, on_b=v_on_b, w_out=v_w_out, ln2_g=v_ln2_g, w_up=v_w_up, conv_w=v_conv_w, conv_b=v_conv_b,
                w_down=v_w_down)
    order = ("ln1_g", "w_in", "qn_a", "kn_a", "rpb", "qn_b", "kn_b", "sink", "on_a", "on_b", "w_out", "ln2_g", "w_up",
             "conv_w", "conv_b", "w_down")

    depth, d = ln1_g.shape
    s = x.shape[1]
    ha = on_a.shape[1] // HEAD
    hq = on_b.shape[1] // HEAD
    pw = w_in.shape[2] * N_DEV
    hkv = (pw - 3 * ha * HEAD - hq * HEAD) // (2 * HEAD)
    f = w_down.shape[1] * N_DEV
    mix = (ha + hq) * HEAD
    cfg = (ha, hq, hkv)
    fs = conv_w.shape[2]
    dev = 4 * lax.axis_index("x") + 2 * lax.axis_index("y") + lax.axis_index("c")
    core = lax.axis_index("c").astype(jnp.int32).reshape(1)

    g_in = _allgather(w_in.astype(BF16), "gather_w_in")
    g_out = _allgather(w_out.astype(BF16), "gather_w_out")
    g_up = _allgather(w_up.astype(BF16), "gather_w_up")
    g_down = _allgather(w_down.astype(BF16), "gather_w_down")
    cw_rows = depth * 3
    cw_pad = jnp.pad(conv_w.reshape(cw_rows, fs), ((0, -cw_rows % 8), (0, 0)))
    g_cw = _allgather(cw_pad, "gather_conv_w")
    full_in = g_in.transpose(1, 2, 0, 3).reshape(depth, d, pw)
    full_out = g_out.transpose(1, 0, 2, 3).reshape(depth, mix, d)
    full_up = g_up.transpose(1, 2, 0, 3).reshape(depth, d, 2 * f)
    full_down = g_down.transpose(1, 0, 2, 3).reshape(depth, f, d)
    full_cw = g_cw[:, :cw_rows].reshape(N_DEV, depth, 3, fs).transpose(1, 2, 0, 3).reshape(depth, 3, 2 * f)

    inv = ROPE_THETA ** (-jnp.arange(0, HEAD, 2, dtype=F32) / HEAD)
    ang = positions.astype(F32)[:, None] * inv[None, :]
    cos = jnp.concatenate([jnp.cos(ang), jnp.cos(ang)], axis=-1)
    sin = jnp.concatenate([-jnp.sin(ang), jnp.sin(ang)], axis=-1)
    qk = jnp.arange(GRID_W * GRID_W)
    dc_of = (qk % GRID_W) - (qk // GRID_W) + (WIN_C - 1)
    onehot = (dc_of[:, None] == jnp.arange(LANES)[None, :]).astype(BF16)

    tiles_s = _tile(s, 512, 16)

    xs = x.reshape(s, d)
    saved = []
    for l in range(depth):
        gains = jnp.zeros((8, HEAD), F32).at[0].set(qn_a[l]).at[1].set(kn_a[l]).at[2].set(qn_b[l]).at[3].set(kn_b[l])
        on_g = jnp.concatenate([on_a[l], on_b[l]]).reshape(1, mix)
        h = _rms_fwd(xs, ln1_g[l].reshape(1, d), "ln1_fwd")
        proj = _matmul(h, full_in[l], dims="nn", ti=tiles_s, tj=_tile(pw, 1536, LANES), tk=d, out_dtype=F32, name="proj_fwd")
        qa, ka, va, qb, kb, vb = _qkv_fwd(proj, gains, cos, sin, cfg, "qkv_fwd")
        tb = _na_bias(rpb[l].reshape(-1), ha, "na_bias")
        oa = _na_fwd(qa, ka, va, tb, "na_fwd")
        ob = _wa_fwd(qb, kb, vb, sink[l], "wa_fwd")
        o_n = _onorm_fwd(oa, ob, on_g, "onorm_fwd")
        x1 = _matmul(o_n, full_out[l], dims="nn", ti=tiles_s, tj=_tile(d, 2048, LANES), tk=mix, out_dtype=F32,
                     name="out_fwd", resid=xs)
        h2 = _rms_fwd(x1, ln2_g[l].reshape(1, d), "ln2_fwd")
        u = _matmul(h2, full_up[l], dims="nn", ti=tiles_s, tj=fs, tk=d, out_dtype=F32, name="up_fwd")
        cb = conv_b[l].reshape(1, 2 * f)
        a = _gate_fwd(u, full_cw[l], cb, f, "gate_fwd")
        x2 = _matmul(a, full_down[l], dims="nn", ti=tiles_s, tj=_tile(d, 1024, LANES), tk=_tile(f, 1536, LANES),
                     out_dtype=F32, name="down_fwd", resid=x1)
        saved.append(dict(x=xs, h=h, proj=proj, gains=gains, on_g=on_g, qkv=(qa, ka, va, qb, kb, vb), tb=tb, oa=oa, ob=ob,
                          o_n=o_n, x1=x1, h2=h2, u=u, cb=cb, a=a))
        xs = x2

    dx, loss_part = _loss_head(xs, loss_target.reshape(s, d), "loss_head")
    loss = lax.psum(loss_part[0, 0], ("x", "y", "c"))

    small_grads = [None] * depth
    big = {n: [None] * depth for n in ("w_in", "w_out", "w_up", "w_down")}
    for l in reversed(range(depth)):
        sv = saved[l]
        qa, ka, va, qb, kb, vb = sv["qkv"]
        da = _matmul(dx, full_down[l], dims="nt", ti=tiles_s, tj=_tile(f, 1408, LANES), tk=d, out_dtype=F32, name="down_bwd_x")
        gw_down = _matmul(sv["a"], dx, dims="tn", ti=_tile(f, 1408, LANES), tj=d, tk=tiles_s, out_dtype=BF16,
                          name="down_bwd_w", j_outer=False)
        dgate, dup, dcw, dcb = _gate_bwd(sv["u"], full_cw[l], sv["cb"], da, f, "gate_bwd")
        du_g, du_u = _conv_bwd(dgate, dup, full_cw[l], f, "conv_bwd")
        du = jnp.concatenate([du_g, du_u], axis=1)
        dh2 = _matmul(du, full_up[l], dims="nt", ti=tiles_s, tj=d, tk=fs, out_dtype=F32, name="up_bwd_x")
        gw_up = _matmul(sv["h2"], du, dims="tn", ti=d, tj=fs, tk=tiles_s, out_dtype=BF16, name="up_bwd_w", dev_major=True)
        dx1, dln2 = _rms_bwd(sv["x1"], ln2_g[l].reshape(1, d), dh2, dx, "ln2_bwd")
        don = _matmul(dx1, full_out[l], dims="nt", ti=tiles_s, tj=mix, tk=d, out_dtype=F32, name="out_bwd_x")
        gw_out = _matmul(sv["o_n"], dx1, dims="tn", ti=_tile(mix, 1024, LANES), tj=d, tk=tiles_s, out_dtype=BF16,
                         name="out_bwd_w", j_outer=False)
        doa, dob, don_g = _onorm_bwd(sv["oa"], sv["ob"], sv["on_g"], don, "onorm_bwd")
        dqa, dka, dva, dtb = _na_bwd(qa, ka, va, sv["tb"], doa, "na_bwd")
        dqb, dkb, dvb, dsink = _wa_bwd(qb, kb, vb, sink[l], dob, "wa_bwd")
        drpb = _rpb_grad(dtb, onehot, "rpb_grad")
        dproj, dgains = _qkv_bwd(sv["proj"], sv["gains"], cos, sin, (dqa, dka, dva, dqb, dkb, dvb), cfg, "qkv_bwd")
        dh = _matmul(dproj, full_in[l], dims="nt", ti=tiles_s, tj=d, tk=_tile(pw, 1536, LANES), out_dtype=F32, name="proj_bwd_x")
        gw_in = _matmul(sv["h"], dproj, dims="tn", ti=d, tj=_tile(pw, 1536, LANES), tk=tiles_s, out_dtype=BF16, name="proj_bwd_w")
        dx, dln1 = _rms_bwd(sv["x"], ln1_g[l].reshape(1, d), dh, dx1, "ln1_bwd")

        small_grads[l] = dict(
            ln1_g=dln1[0], qn_a=dgains[0], kn_a=dgains[1], rpb=drpb, qn_b=dgains[2], kn_b=dgains[3], sink=dsink[:, 0, 0],
            on_a=don_g[0, :ha * HEAD], on_b=don_g[0, ha * HEAD:], ln2_g=dln2[0],
            conv_b=dcb[:, 0, :].reshape(2 * f), conv_w=dcw[:, :, 0, :].transpose(1, 0, 2).reshape(3, 2 * f))

        parts = dict(
            w_in=gw_in.reshape(d, N_DEV, pw // N_DEV).transpose(1, 0, 2),
            w_out=gw_out.reshape(N_DEV, mix // N_DEV, d),
            w_up=gw_up,
            w_down=gw_down.reshape(N_DEV, f // N_DEV, d))
        for n in big:
            summed = _reduce_scatter(parts[n], core, "rs_" + n)
            big[n][l] = _adamw(summed, weights[n][l], mom1[n][l], mom2[n][l], "adamw_" + n)

    grads_l = [small_grads[l][n] for l in range(depth) for n in SMALL]
    gathered = _allgather(_pack(grads_l), "gather_small")
    zeros_cw = jnp.zeros((3, 2 * f), F32)

    def small_state(src):
        return _pack([zeros_cw if n == "conv_w" else src[n][l] for l in range(depth) for n in SMALL])

    sm = _adamw(gathered, small_state(weights), small_state(mom1), small_state(mom2), "adamw_small")
    sm = [_unpack(t, grads_l) for t in sm]
    small_out = {n: [jnp.stack([sm[k][l * len(SMALL) + i] for l in range(depth)]) for k in range(4)]
                 for i, n in enumerate(SMALL)}
    cw_grad = lax.dynamic_slice_in_dim(small_out["conv_w"][0], dev * fs, fs, axis=2)
    cw_rows_pad = cw_rows + (-cw_rows % 8)

    def rows8(a):
        return jnp.pad(a.reshape(cw_rows, fs), ((0, cw_rows_pad - cw_rows), (0, 0)))

    cw_res = _adamw(rows8(cw_grad)[None], rows8(conv_w), rows8(m_conv_w), rows8(v_conv_w), "adamw_conv_w")
    small_out["conv_w"] = [t[:cw_rows].reshape(depth, 3, fs) for t in cw_res]

    results = {n: ([jnp.stack([big[n][l][k] for l in range(depth)]) for k in range(4)] if n in big else small_out[n])
               for n in order}
    grad_x = dx.reshape(1, s, d)
    return (loss, grad_x, *[results[n][0] for n in order], *[results[n][1] for n in order],
            *[results[n][2] for n in order], *[results[n][3] for n in order])
```

```python
import functools
import math

import jax
import jax.numpy as jnp
from jax import lax
from jax.experimental import pallas as pl
from jax.experimental.pallas import tpu as pltpu

F32 = jnp.float32
BF16 = jnp.bfloat16

HEAD = 128
GRID_W = 64
WIN_R = 8
WIN_C = 16
BAND = 128
ROPE_THETA = 10000.0
EPS = 1e-6
NEG = -1e30
SCALE = 1.0 / math.sqrt(HEAD)

ADAM_LR = 0.001
ADAM_B1 = 0.9
ADAM_B2 = 0.999
ADAM_EPS = 1e-08
ADAM_WD = 0.01
ADAM_STEP = 10

N_DEV = 8
LANES = 128
VMEM_LIMIT_BYTES = 56 * 2 ** 20
MESH = pl.DeviceIdType.MESH
ANY = pl.BlockSpec(memory_space=pl.ANY)
SMEM = pl.BlockSpec(memory_space=pltpu.SMEM)


def _params():
    return pltpu.CompilerParams(vmem_limit_bytes=VMEM_LIMIT_BYTES)


def _tile(n, pref, align):
    t = min(n, pref)
    t -= t % align
    while t > 0 and n % t:
        t -= align
    return t if t > 0 else n


def _place():
    x, y, c = lax.axis_index("x"), lax.axis_index("y"), lax.axis_index("c")
    chips = [(1 - x, y), (x, 1 - y), (1 - x, 1 - y)]
    return x, y, c, chips


COPIES_PER_ARRAY = N_DEV - 1


def _comm_scratch(n_arrays):
    return [pltpu.SemaphoreType.DMA((COPIES_PER_ARRAY * n_arrays,)), pltpu.SemaphoreType.DMA((COPIES_PER_ARRAY * n_arrays,)),
            pltpu.SemaphoreType.DMA((n_arrays,))]


def _gather_plan(src_refs, out_refs, send_sems, recv_sems, local_sems):
    x, y, c, chips = _place()
    me, sibling = (x, y, c), (x, y, 1 - c)

    def slot(a, px, py, pc):
        return out_refs[a].at[4 * px + 2 * py + pc]

    def copy(a, k, block, to, src=None):
        return pltpu.make_async_remote_copy(
            src_ref=slot(a, *block) if src is None else src, dst_ref=slot(a, *block),
            send_sem=send_sems.at[COPIES_PER_ARRAY * a + k], recv_sem=recv_sems.at[COPIES_PER_ARRAY * a + k],
            device_id=to, device_id_type=MESH)

    def mine(a):
        return pltpu.make_async_copy(src_refs[a], slot(a, *me), local_sems.at[a])

    def first(a):
        return [copy(a, 0, me, sibling, src=src_refs[a])] + [
            copy(a, 1 + j, me, (*chip, c), src=src_refs[a]) for j, chip in enumerate(chips)]

    def passed(a):
        return [copy(a, 4 + j, (*chip, c), sibling) for j, chip in enumerate(chips)]

    def start():
        for a in range(len(src_refs)):
            mine(a).start()
            for cp in first(a):
                cp.start()

    def finish():
        for a in range(len(src_refs)):
            for j, chip in enumerate(chips):
                copy(a, 1 + j, (*chip, c), me).wait_recv()
                passed(a)[j].start()
        for a in range(len(src_refs)):
            copy(a, 0, sibling, me).wait_recv()
            for j, chip in enumerate(chips):
                copy(a, 4 + j, (*chip, 1 - c), me).wait_recv()
            for cp in first(a) + passed(a):
                cp.wait_send()
            mine(a).wait()

    return start, finish


def _scatter_plan(src_refs, out_refs, send_sems, recv_sems, local_sems):
    x, y, c, _ = _place()
    me = 4 * x + 2 * y + c

    def peer(k):
        px = 1 - x if k & 4 else x
        py = 1 - y if k & 2 else y
        pc = 1 - c if k & 1 else c
        return (px, py, pc), 4 * px + 2 * py + pc

    def copy(a, k, outgoing):
        to, idx = peer(k)
        return pltpu.make_async_remote_copy(
            src_ref=src_refs[a].at[idx], dst_ref=out_refs[a].at[me if outgoing else idx],
            send_sem=send_sems.at[COPIES_PER_ARRAY * a + k - 1], recv_sem=recv_sems.at[COPIES_PER_ARRAY * a + k - 1],
            device_id=to, device_id_type=MESH)

    def mine(a):
        return pltpu.make_async_copy(src_refs[a].at[me], out_refs[a].at[me], local_sems.at[a])

    def start():
        for a in range(len(src_refs)):
            mine(a).start()
            for k in range(1, N_DEV):
                copy(a, k, True).start()

    def finish():
        for a in range(len(src_refs)):
            for k in range(1, N_DEV):
                copy(a, k, False).wait_recv()
            for k in range(1, N_DEV):
                copy(a, k, True).wait_send()
            mine(a).wait()

    return start, finish


def _allgather(v, name):
    def body(v_ref, out_ref, send_sems, recv_sems, local_sems):
        start, finish = _gather_plan([v_ref], [out_ref], send_sems, recv_sems, local_sems)
        start()
        finish()

    return pl.pallas_call(
        body, name=name,
        out_shape=jax.ShapeDtypeStruct((N_DEV,) + v.shape, v.dtype),
        in_specs=[ANY], out_specs=ANY, scratch_shapes=_comm_scratch(1),
    )(v)


def _sibling_exchange(g, name):
    def body(g_ref, out_ref, send_sems, recv_sems):
        x, y, c, _ = _place()
        sibling = (x, y, 1 - c)
        copies = []
        for j in range(4):
            copies.append(pltpu.make_async_remote_copy(
                src_ref=g_ref.at[2 * j + (1 - c)], dst_ref=out_ref.at[j],
                send_sem=send_sems.at[j], recv_sem=recv_sems.at[j], device_id=sibling, device_id_type=MESH))
        for cp in copies:
            cp.start()
        for cp in copies:
            cp.wait_recv()
        for cp in copies:
            cp.wait_send()

    return pl.pallas_call(
        body, name=name,
        out_shape=jax.ShapeDtypeStruct((4,) + g.shape[1:], g.dtype),
        in_specs=[ANY], out_specs=ANY,
        scratch_shapes=[pltpu.SemaphoreType.DMA((4,)), pltpu.SemaphoreType.DMA((4,))],
    )(g)


def _pair_sum(g, got, core, name):
    _, r, c = g.shape
    tr = _tile(r, max(16, (1 << 20) // c), 16)

    def body(core_ref, g_ref, got_ref, o_ref):
        del core_ref
        o_ref[...] = (g_ref[...].astype(F32) + got_ref[...].astype(F32)).astype(o_ref.dtype)

    return pl.pallas_call(
        body, name=name,
        out_shape=jax.ShapeDtypeStruct((4, r, c), g.dtype),
        grid_spec=pltpu.PrefetchScalarGridSpec(
            num_scalar_prefetch=1, grid=(4, r // tr),
            in_specs=[pl.BlockSpec((None, tr, c), lambda j, i, core_ref: (2 * j + core_ref[0], i, 0)),
                      pl.BlockSpec((None, tr, c), lambda j, i, core_ref: (j, i, 0))],
            out_specs=pl.BlockSpec((None, tr, c), lambda j, i, core_ref: (j, i, 0))),
        compiler_params=_params(),
    )(core, g, got)


def _chip_exchange(p, name):
    def body(p_ref, out_ref, send_sems, recv_sems, local_sem):
        x, y, c, chips = _place()
        mine = pltpu.make_async_copy(p_ref.at[2 * x + y], out_ref.at[3], local_sem)
        mine.start()
        copies = []
        for k, (px, py) in enumerate(chips):
            copies.append(pltpu.make_async_remote_copy(
                src_ref=p_ref.at[2 * px + py], dst_ref=out_ref.at[k],
                send_sem=send_sems.at[k], recv_sem=recv_sems.at[k], device_id=(px, py, c), device_id_type=MESH))
        for cp in copies:
            cp.start()
        for cp in copies:
            cp.wait_recv()
        for cp in copies:
            cp.wait_send()
        mine.wait()

    return pl.pallas_call(
        body, name=name,
        out_shape=jax.ShapeDtypeStruct(p.shape, p.dtype),
        in_specs=[ANY], out_specs=ANY,
        scratch_shapes=[pltpu.SemaphoreType.DMA((3,)), pltpu.SemaphoreType.DMA((3,)), pltpu.SemaphoreType.DMA],
    )(p)


def _reduce_scatter(g, core, name):
    got = _sibling_exchange(g, name + "_d2d")
    p = _pair_sum(g, got, core, name + "_pair")
    return _chip_exchange(p, name + "_ici")


def _adamw(parts, w, m, v, name):
    n_parts, r, c = parts.shape
    tr = _tile(r, max(8, (1 << 19) // c), 16 if parts.dtype == BF16 else 8)
    c1 = 1.0 - ADAM_B1 ** ADAM_STEP
    c2 = 1.0 - ADAM_B2 ** ADAM_STEP

    def body(p_ref, w_ref, m_ref, v_ref, g_out, d_out, m_out, v_out):
        g = p_ref[0].astype(F32)
        for k in range(1, n_parts):
            g = g + p_ref[k].astype(F32)
        m2 = ADAM_B1 * m_ref[...] + (1.0 - ADAM_B1) * g
        v2 = ADAM_B2 * v_ref[...] + (1.0 - ADAM_B2) * (g * g)
        g_out[...] = g
        m_out[...] = m2
        v_out[...] = v2
        d_out[...] = -ADAM_LR * ((m2 / c1) / (jnp.sqrt(v2 / c2) + ADAM_EPS) + ADAM_WD * w_ref[...])

    blk = pl.BlockSpec((tr, c), lambda i: (i, 0))
    out = jax.ShapeDtypeStruct((r, c), F32)
    return pl.pallas_call(
        body, name=name, grid=(r // tr,),
        in_specs=[pl.BlockSpec((n_parts, tr, c), lambda i: (0, i, 0)), blk, blk, blk],
        out_specs=[blk, blk, blk, blk], out_shape=[out, out, out, out],
        compiler_params=_params(),
    )(parts, w, m, v)


def _gather_comm(shards):
    return _gather_plan, shards, [jax.ShapeDtypeStruct((N_DEV,) + v.shape, v.dtype) for v in shards]


def _scatter_comm(blocks):
    return _scatter_plan, blocks, [jax.ShapeDtypeStruct(g.shape, g.dtype) for g in blocks]


def _matmul(a, b, *, dims, ti, tj, tk, out_dtype, name, j_outer=True, resid=None, dev_major=False, comm=None):
    if dims == "nn":
        (I, K), (K2, J) = a.shape, b.shape
    elif dims == "nt":
        (I, K), (J, K2) = a.shape, b.shape
    else:
        (K, I), (K2, J) = a.shape, b.shape
    assert K == K2 and I % ti == 0 and J % tj == 0 and K % tk == 0, (name, a.shape, b.shape, ti, tj, tk)
    ni, nj, nk = I // ti, J // tj, K // tk

    def ij(g0, g1):
        return (g1, g0) if j_outer else (g0, g1)

    if dims == "nn":
        a_spec = pl.BlockSpec((ti, tk), lambda g0, g1, k: (ij(g0, g1)[0], k))
        b_spec = pl.BlockSpec((tk, tj), lambda g0, g1, k: (k, ij(g0, g1)[1]))
        dn = (((1,), (0,)), ((), ()))
    elif dims == "nt":
        a_spec = pl.BlockSpec((ti, tk), lambda g0, g1, k: (ij(g0, g1)[0], k))
        b_spec = pl.BlockSpec((tj, tk), lambda g0, g1, k: (ij(g0, g1)[1], k))
        dn = (((1,), (1,)), ((), ()))
    else:
        a_spec = pl.BlockSpec((tk, ti), lambda g0, g1, k: (k, ij(g0, g1)[0]))
        b_spec = pl.BlockSpec((tk, tj), lambda g0, g1, k: (k, ij(g0, g1)[1]))
        dn = (((0,), (0,)), ((), ()))
    in_specs = [a_spec, b_spec]
    operands = [a, b]
    if resid is not None:
        in_specs.append(pl.BlockSpec((ti, tj), lambda g0, g1, k: ij(g0, g1)))
        operands.append(resid)
    if dev_major:
        out_spec = pl.BlockSpec((None, ti, tj), lambda g0, g1, k: (ij(g0, g1)[1], ij(g0, g1)[0], 0))
        out_shape = jax.ShapeDtypeStruct((nj, I, tj), out_dtype)
    else:
        out_spec = pl.BlockSpec((ti, tj), lambda g0, g1, k: ij(g0, g1))
        out_shape = jax.ShapeDtypeStruct((I, J), out_dtype)

    grid = (nj, ni, nk) if j_outer else (ni, nj, nk)
    n_in = len(operands)
    n_comm = 0
    out_specs, out_shapes = [out_spec], [out_shape]
    scratch = [pltpu.VMEM((ti, tj), F32)] if nk > 1 else []
    if comm is not None:
        plan, comm_in, comm_out = comm
        n_comm = len(comm_in)
        operands += list(comm_in)
        in_specs += [ANY] * n_comm
        out_specs += [ANY] * n_comm
        out_shapes += list(comm_out)
        scratch += _comm_scratch(n_comm)

    def body(*refs):
        a_ref, b_ref = refs[0], refs[1]
        r_ref = refs[2] if resid is not None else None
        o_ref = refs[n_in + n_comm]
        if comm is not None:
            start, finish_comm = plan(refs[n_in:n_in + n_comm], refs[n_in + n_comm + 1:n_in + 2 * n_comm + 1], *refs[-3:])
            steps = [pl.program_id(axis) for axis in range(3)]

            @pl.when((steps[0] == 0) & (steps[1] == 0) & (steps[2] == 0))
            def _():
                start()

        part = lax.dot_general(a_ref[...].astype(BF16), b_ref[...].astype(BF16), dn, preferred_element_type=F32)

        def finish(acc):
            if r_ref is not None:
                acc = acc + r_ref[...]
            o_ref[...] = acc.astype(o_ref.dtype)

        if nk == 1:
            finish(part)
        else:
            acc_ref = refs[n_in + 2 * n_comm + 1]
            k = pl.program_id(2)

            @pl.when(k == 0)
            def _():
                acc_ref[...] = part

            @pl.when(k > 0)
            def _():
                acc_ref[...] += part

            @pl.when(k == nk - 1)
            def _():
                finish(acc_ref[...])

        if comm is not None:
            @pl.when((steps[0] == grid[0] - 1) & (steps[1] == grid[1] - 1) & (steps[2] == grid[2] - 1))
            def _():
                finish_comm()

    res = pl.pallas_call(
        body, name=name, grid=grid,
        in_specs=in_specs, out_specs=out_specs, out_shape=out_shapes,
        scratch_shapes=scratch, compiler_params=_params(),
    )(*operands)
    return res[0] if comm is None else res


def _rms_fwd(x, g, name):
    s, d = x.shape
    ts = _tile(s, 256, 16)

    def body(x_ref, g_ref, h_ref):
        xv = x_ref[...]
        r = lax.rsqrt(jnp.mean(xv * xv, axis=-1, keepdims=True) + EPS)
        h_ref[...] = (xv * r * g_ref[...]).astype(BF16)

    return pl.pallas_call(
        body, name=name, grid=(s // ts,),
        in_specs=[pl.BlockSpec((ts, d), lambda i: (i, 0)), pl.BlockSpec((1, d), lambda i: (0, 0))],
        out_specs=pl.BlockSpec((ts, d), lambda i: (i, 0)),
        out_shape=jax.ShapeDtypeStruct((s, d), BF16), compiler_params=_params(),
    )(x, g)


def _rms_bwd(x, g, dh, dres, name):
    s, d = x.shape
    ts = _tile(s, 256, 8)

    def body(x_ref, g_ref, dh_ref, dres_ref, dx_ref, dg_ref):
        xv = x_ref[...]
        r = lax.rsqrt(jnp.mean(xv * xv, axis=-1, keepdims=True) + EPS)
        y = xv * r
        dhv = dh_ref[...]
        gd = dhv * g_ref[...]
        dx_ref[...] = dres_ref[...] + r * (gd - y * jnp.mean(gd * y, axis=-1, keepdims=True))

        @pl.when(pl.program_id(0) == 0)
        def _():
            dg_ref[...] = jnp.zeros_like(dg_ref)

        dg_ref[0:1, :] += jnp.sum(dhv * y, axis=0, keepdims=True)

    blk = pl.BlockSpec((ts, d), lambda i: (i, 0))
    return pl.pallas_call(
        body, name=name, grid=(s // ts,),
        in_specs=[blk, pl.BlockSpec((1, d), lambda i: (0, 0)), blk, blk],
        out_specs=[blk, pl.BlockSpec((8, d), lambda i: (0, 0))],
        out_shape=[jax.ShapeDtypeStruct((s, d), F32), jax.ShapeDtypeStruct((8, d), F32)],
        compiler_params=_params(),
    )(x, g, dh, dres)


def _head_norm(t, gain):
    r = lax.rsqrt(jnp.mean(t * t, axis=-1, keepdims=True) + EPS)
    return t * r * gain


def _head_norm_bwd(t, gain, dn):
    r = lax.rsqrt(jnp.mean(t * t, axis=-1, keepdims=True) + EPS)
    y = t * r
    gd = dn * gain
    dt = r * (gd - y * jnp.mean(gd * y, axis=-1, keepdims=True))
    return dt, jnp.sum(dn * y, axis=0, keepdims=True)


def _rope(n, cos, sin):
    return n * cos + pltpu.roll(n, HEAD // 2, axis=1) * sin


def _rope_bwd(do, cos, sin):
    return do * cos + pltpu.roll(do * sin, HEAD // 2, axis=1)


def _qkv_fwd(proj, gains, cos, sin, cfg, name):
    s, pw = proj.shape
    ha, hq, hkv = cfg
    ts = _tile(s, 256, 16)

    def body(p_ref, gn_ref, cos_ref, sin_ref, qa_ref, ka_ref, va_ref, qb_ref, kb_ref, vb_ref):
        cosv, sinv = cos_ref[...], sin_ref[...]
        col = 0
        for out_ref, nh, gi, rot in ((qa_ref, ha, 0, False), (ka_ref, ha, 1, False), (va_ref, ha, None, False),
                                     (qb_ref, hq, 2, True), (kb_ref, hkv, 3, True), (vb_ref, hkv, None, False)):
            for h in range(nh):
                t = p_ref[:, col * HEAD:(col + 1) * HEAD]
                if gi is not None:
                    t = _head_norm(t, gn_ref[gi:gi + 1, :])
                if rot:
                    t = _rope(t, cosv, sinv)
                out_ref[h] = t.astype(BF16)
                col += 1

    def hm(nh):
        return pl.BlockSpec((nh, ts, HEAD), lambda i: (0, i, 0)), jax.ShapeDtypeStruct((nh, s, HEAD), BF16)

    specs, shapes = zip(hm(ha), hm(ha), hm(ha), hm(hq), hm(hkv), hm(hkv))
    tok = pl.BlockSpec((ts, HEAD), lambda i: (i, 0))
    return pl.pallas_call(
        body, name=name, grid=(s // ts,),
        in_specs=[pl.BlockSpec((ts, pw), lambda i: (i, 0)), pl.BlockSpec((8, HEAD), lambda i: (0, 0)), tok, tok],
        out_specs=list(specs), out_shape=list(shapes), compiler_params=_params(),
    )(proj, gains, cos, sin)


def _qkv_bwd(proj, gains, cos, sin, grads, cfg, name):
    s, pw = proj.shape
    ha, hq, hkv = cfg
    ts = _tile(s, 256, 16)

    def body(p_ref, gn_ref, cos_ref, sin_ref, dqa, dka, dva, dqb, dkb, dvb, dp_ref, dgn_ref):
        cosv, sinv = cos_ref[...], sin_ref[...]

        @pl.when(pl.program_id(0) == 0)
        def _():
            dgn_ref[...] = jnp.zeros_like(dgn_ref)

        col = 0
        for d_ref, nh, gi, rot in ((dqa, ha, 0, False), (dka, ha, 1, False), (dva, ha, None, False),
                                   (dqb, hq, 2, True), (dkb, hkv, 3, True), (dvb, hkv, None, False)):
            dgain = jnp.zeros((1, HEAD), F32)
            for h in range(nh):
                dt = d_ref[h]
                if rot:
                    dt = _rope_bwd(dt, cosv, sinv)
                if gi is not None:
                    dt, dg = _head_norm_bwd(p_ref[:, col * HEAD:(col + 1) * HEAD], gn_ref[gi:gi + 1, :], dt)
                    dgain = dgain + dg
                dp_ref[:, col * HEAD:(col + 1) * HEAD] = dt.astype(BF16)
                col += 1
            if gi is not None:
                dgn_ref[gi:gi + 1, :] += dgain

    def hm(nh):
        return pl.BlockSpec((nh, ts, HEAD), lambda i: (0, i, 0))

    tok = pl.BlockSpec((ts, HEAD), lambda i: (i, 0))
    small = pl.BlockSpec((8, HEAD), lambda i: (0, 0))
    return pl.pallas_call(
        body, name=name, grid=(s // ts,),
        in_specs=[pl.BlockSpec((ts, pw), lambda i: (i, 0)), small, tok, tok,
                  hm(ha), hm(ha), hm(ha), hm(hq), hm(hkv), hm(hkv)],
        out_specs=[pl.BlockSpec((ts, pw), lambda i: (i, 0)), small],
        out_shape=[jax.ShapeDtypeStruct((s, pw), BF16), jax.ShapeDtypeStruct((8, HEAD), F32)],
        compiler_params=_params(),
    )(proj, gains, cos, sin, *grads)


NA_QROWS = 8
NA_KEYS = WIN_R * GRID_W
N_DR = 2 * WIN_R - 1
N_DC = 2 * WIN_C - 1


def _na_bias(rpb_flat, n_heads, name):
    def body(rpb_ref, tb_ref):
        h = pl.program_id(0)
        qi = lax.broadcasted_iota(jnp.int32, (GRID_W, LANES), 0)
        lane = lax.broadcasted_iota(jnp.int32, (GRID_W, LANES), 1)
        kk = lane & (GRID_W - 1)
        upper = lane >= GRID_W
        dcm = kk - qi + (WIN_C - 1)
        cs = jnp.clip(qi - WIN_C // 2, 0, GRID_W - WIN_C)
        valid = (kk >= cs) & (kk < cs + WIN_C)
        base = h * (N_DR * N_DC)
        for dra in range(N_DR - 1):
            def step(j, acc, dra=dra):
                va = rpb_ref[base + dra * N_DC + j]
                vb = rpb_ref[base + (dra + 1) * N_DC + j]
                return jnp.where(dcm == j, jnp.where(upper, vb, va), acc)

            pair = lax.fori_loop(0, N_DC, step, jnp.zeros((GRID_W, LANES), F32))
            pair = jnp.where(valid, pair, NEG)
            for dr0 in range(WIN_R):
                wp, odd = divmod(dra - dr0, 2)
                if odd == 0 and 0 <= wp < WIN_R // 2:
                    tb_ref[0, dr0, :, wp * LANES:(wp + 1) * LANES] = pair

    return pl.pallas_call(
        body, name=name, grid=(n_heads,),
        in_specs=[SMEM],
        out_specs=pl.BlockSpec((1, WIN_R, GRID_W, NA_KEYS), lambda h: (h, 0, 0, 0)),
        out_shape=jax.ShapeDtypeStruct((n_heads, WIN_R, GRID_W, NA_KEYS), F32),
        compiler_params=_params(),
    )(rpb_flat)


def _na_row(b, i, nrows):
    r = b * NA_QROWS + i
    rs = jnp.clip(r - WIN_R // 2, 0, nrows - WIN_R)
    return pl.ds(pl.multiple_of(rs * GRID_W, GRID_W), NA_KEYS), rs - r + (WIN_R - 1)


def _softmax(s):
    e = jnp.exp(s - jnp.max(s, axis=-1, keepdims=True))
    return e * (1.0 / jnp.sum(e, axis=-1, keepdims=True))


_NT = (((1,), (1,)), ((), ()))
_NN = (((1,), (0,)), ((), ()))
_TN = (((0,), (0,)), ((), ()))


def _dot(a, b, dn):
    return lax.dot_general(a, b, dn, preferred_element_type=F32)


def _na_fwd(q, k, v, tb, name):
    nh, s, _ = q.shape
    nrows = s // GRID_W
    tq = NA_QROWS * GRID_W

    def body(q_ref, k_ref, v_ref, tb_ref, o_ref, s_scr, p_scr):
        b = pl.program_id(1)
        rows = [slice(i * GRID_W, (i + 1) * GRID_W) for i in range(NA_QROWS)]
        at = [_na_row(b, i, nrows) for i in range(NA_QROWS)]
        for i, (keys, dr0) in enumerate(at):
            s_scr[i] = _dot(q_ref[rows[i], :], k_ref[keys, :], _NT) * SCALE + tb_ref[0, dr0]
        for i in range(NA_QROWS):
            p_scr[i] = _softmax(s_scr[i]).astype(BF16)
        for i, (keys, _) in enumerate(at):
            o_ref[rows[i], :] = _dot(p_scr[i], v_ref[keys, :], _NN)

    qspec = pl.BlockSpec((None, tq, HEAD), lambda h, b: (h, b, 0))
    full = pl.BlockSpec((None, s, HEAD), lambda h, b: (h, 0, 0))
    return pl.pallas_call(
        body, name=name, grid=(nh, nrows // NA_QROWS),
        in_specs=[qspec, full, full, pl.BlockSpec((1, WIN_R, GRID_W, NA_KEYS), lambda h, b: (h, 0, 0, 0))],
        out_specs=qspec, out_shape=jax.ShapeDtypeStruct((nh, s, HEAD), F32),
        scratch_shapes=[pltpu.VMEM((NA_QROWS, GRID_W, NA_KEYS), F32), pltpu.VMEM((NA_QROWS, GRID_W, NA_KEYS), BF16)],
        compiler_params=_params(),
    )(q, k, v, tb)


def _na_bwd(q, k, v, tb, do, name):
    nh, s, _ = q.shape
    nrows = s // GRID_W
    tq = NA_QROWS * GRID_W

    def body(q_ref, do_ref, k_ref, v_ref, tb_ref, dq_ref, dk_ref, dv_ref, dtb_ref, s_scr, dp_scr, p_scr, ds_scr):
        b = pl.program_id(1)

        @pl.when(b == 0)
        def _():
            dk_ref[...] = jnp.zeros_like(dk_ref)
            dv_ref[...] = jnp.zeros_like(dv_ref)
            dtb_ref[...] = jnp.zeros_like(dtb_ref)

        rows = [slice(i * GRID_W, (i + 1) * GRID_W) for i in range(NA_QROWS)]
        at = [_na_row(b, i, nrows) for i in range(NA_QROWS)]
        for i, (keys, dr0) in enumerate(at):
            s_scr[i] = _dot(q_ref[rows[i], :], k_ref[keys, :], _NT) * SCALE + tb_ref[0, dr0]
            dp_scr[i] = _dot(do_ref[rows[i], :], v_ref[keys, :], _NT)
        for i in range(NA_QROWS):
            p = _softmax(s_scr[i])
            dp = dp_scr[i]
            ds = p * (dp - jnp.sum(p * dp, axis=-1, keepdims=True))
            p_scr[i] = p.astype(BF16)
            s_scr[i] = ds
            ds_scr[i] = (ds * SCALE).astype(BF16)
        for i, (keys, _) in enumerate(at):
            dq_ref[rows[i], :] = _dot(ds_scr[i], k_ref[keys, :], _NN)
        for i, (keys, dr0) in enumerate(at):
            dv_ref[keys, :] += _dot(p_scr[i], do_ref[rows[i], :], _TN)
            dk_ref[keys, :] += _dot(ds_scr[i], q_ref[rows[i], :], _TN)
            dtb_ref[0, dr0] += s_scr[i]

    qspec = pl.BlockSpec((None, tq, HEAD), lambda h, b: (h, b, 0))
    full = pl.BlockSpec((None, s, HEAD), lambda h, b: (h, 0, 0))
    tbs = pl.BlockSpec((1, WIN_R, GRID_W, NA_KEYS), lambda h, b: (h, 0, 0, 0))
    hm = jax.ShapeDtypeStruct((nh, s, HEAD), F32)
    tile = (NA_QROWS, GRID_W, NA_KEYS)
    return pl.pallas_call(
        body, name=name, grid=(nh, nrows // NA_QROWS),
        in_specs=[qspec, qspec, full, full, tbs],
        out_specs=[qspec, full, full, tbs],
        out_shape=[hm, hm, hm, jax.ShapeDtypeStruct((nh, WIN_R, GRID_W, NA_KEYS), F32)],
        scratch_shapes=[pltpu.VMEM(tile, F32), pltpu.VMEM(tile, F32), pltpu.VMEM(tile, BF16), pltpu.VMEM(tile, BF16)],
        compiler_params=_params(),
    )(q, do, k, v, tb)


def _rpb_fold(y, n_heads, name):
    def body(y_ref, o_ref):
        for h in range(n_heads):
            for dr in range(2 * WIN_R):
                acc = jnp.zeros((1, LANES), F32)
                for dr0 in range(WIN_R):
                    w = dr - dr0
                    if 0 <= w < WIN_R:
                        acc = acc + y_ref[h, dr0, w:w + 1, :]
                o_ref[h, dr:dr + 1, :] = acc

    return pl.pallas_call(
        body, name=name, out_shape=jax.ShapeDtypeStruct((n_heads, 2 * WIN_R, LANES), F32),
    )(y)


def _rpb_grad(dtb, onehot, name):
    nh = dtb.shape[0]
    rows = dtb.reshape(nh, WIN_R, GRID_W, WIN_R, GRID_W).transpose(0, 1, 3, 2, 4).reshape(nh * WIN_R * WIN_R, GRID_W * GRID_W)
    y = _matmul(rows, onehot, dims="nn", ti=rows.shape[0], tj=LANES, tk=GRID_W * GRID_W, out_dtype=F32, name=name + "_dc")
    folded = _rpb_fold(y.reshape(nh, WIN_R, WIN_R, LANES), nh, name + "_dr")
    return folded[:, :N_DR, :N_DC]


WA_WIN_TOK = 3 * BAND


def _wa_scores(q, kwin, t0, j, sink_ref, head0, grp):
    rows = grp * BAND
    s = _dot(q, kwin, _NT) * SCALE
    row = lax.broadcasted_iota(jnp.int32, (rows, WA_WIN_TOK), 0)
    qpos = j * BAND + (row & (BAND - 1))
    kpos = t0 + lax.broadcasted_iota(jnp.int32, (rows, WA_WIN_TOK), 1)
    s = jnp.where(jnp.abs(kpos - qpos) <= BAND, s, NEG)
    head = lax.broadcasted_iota(jnp.int32, (rows, 1), 0) // BAND
    sink = jnp.zeros((rows, 1), F32) + sink_ref[head0]
    for g in range(1, grp):
        sink = jnp.where(head == g, sink_ref[head0 + g], sink)
    m = jnp.maximum(jnp.max(s, axis=-1, keepdims=True), sink)
    e = jnp.exp(s - m)
    es = jnp.exp(sink - m)
    rz = 1.0 / (jnp.sum(e, axis=-1, keepdims=True) + es)
    return e * rz, es * rz


def _wa_window(j, s):
    return pl.multiple_of(jnp.clip((j - 1) * BAND, 0, s - WA_WIN_TOK), BAND)


def _wa_fwd(q, k, v, sink, name):
    hq, s, _ = q.shape
    hkv = k.shape[0]
    grp = hq // hkv

    def body(sink_ref, q_ref, k_ref, v_ref, o_ref):
        kh, j = pl.program_id(0), pl.program_id(1)
        t0 = _wa_window(j, s)
        keys = pl.ds(t0, WA_WIN_TOK)
        p, _ = _wa_scores(q_ref[...].reshape(grp * BAND, HEAD), k_ref[keys, :], t0, j, sink_ref, kh * grp, grp)
        o_ref[...] = _dot(p.astype(BF16), v_ref[keys, :], _NN).reshape(grp, BAND, HEAD)

    qspec = pl.BlockSpec((grp, BAND, HEAD), lambda kh, j: (kh, j, 0))
    full = pl.BlockSpec((None, s, HEAD), lambda kh, j: (kh, 0, 0))
    return pl.pallas_call(
        body, name=name, grid=(hkv, s // BAND),
        in_specs=[SMEM, qspec, full, full],
        out_specs=qspec, out_shape=jax.ShapeDtypeStruct((hq, s, HEAD), F32),
        compiler_params=_params(),
    )(sink, q, k, v)


def _wa_bwd(q, k, v, sink, do, name):
    hq, s, _ = q.shape
    hkv = k.shape[0]
    grp = hq // hkv

    def body(sink_ref, q_ref, do_ref, k_ref, v_ref, dq_ref, dk_ref, dv_ref, dsink_ref):
        kh, j = pl.program_id(0), pl.program_id(1)

        @pl.when(j == 0)
        def _():
            dk_ref[...] = jnp.zeros_like(dk_ref)
            dv_ref[...] = jnp.zeros_like(dv_ref)
            dsink_ref[...] = jnp.zeros_like(dsink_ref)

        t0 = _wa_window(j, s)
        keys = pl.ds(t0, WA_WIN_TOK)
        qs = q_ref[...].reshape(grp * BAND, HEAD)
        dos = do_ref[...].reshape(grp * BAND, HEAD)
        kwin, vwin = k_ref[keys, :], v_ref[keys, :]
        p, ps = _wa_scores(qs, kwin, t0, j, sink_ref, kh * grp, grp)
        dp = _dot(dos, vwin, _NT)
        dv_ref[keys, :] += _dot(p.astype(BF16), dos, _TN)
        rowdot = jnp.sum(p * dp, axis=-1, keepdims=True)
        to_sink = ps * rowdot
        for g in range(grp):
            dsink_ref[g] += jnp.zeros((8, LANES), F32) - jnp.sum(to_sink[g * BAND:(g + 1) * BAND])
        dss = (p * (dp - rowdot) * SCALE).astype(BF16)
        dq_ref[...] = _dot(dss, kwin, _NN).reshape(grp, BAND, HEAD)
        dk_ref[keys, :] += _dot(dss, qs, _TN)

    qspec = pl.BlockSpec((grp, BAND, HEAD), lambda kh, j: (kh, j, 0))
    full = pl.BlockSpec((None, s, HEAD), lambda kh, j: (kh, 0, 0))
    kv = jax.ShapeDtypeStruct((hkv, s, HEAD), F32)
    return pl.pallas_call(
        body, name=name, grid=(hkv, s // BAND),
        in_specs=[SMEM, qspec, qspec, full, full],
        out_specs=[qspec, full, full, pl.BlockSpec((grp, 8, LANES), lambda kh, j: (kh, 0, 0))],
        out_shape=[jax.ShapeDtypeStruct((hq, s, HEAD), F32), kv, kv, jax.ShapeDtypeStruct((hq, 8, LANES), F32)],
        compiler_params=_params(),
    )(sink, q, do, k, v)


def _onorm_fwd(oa, ob, gains, name):
    ha, s, _ = oa.shape
    hq = ob.shape[0]
    ts = _tile(s, 256, 16)

    def body(oa_ref, ob_ref, g_ref, o_ref):
        col = 0
        for ref, nh in ((oa_ref, ha), (ob_ref, hq)):
            ss = sum(jnp.sum(ref[h] * ref[h], axis=-1, keepdims=True) for h in range(nh))
            r = lax.rsqrt(ss / (nh * HEAD) + EPS)
            for h in range(nh):
                o_ref[:, col * HEAD:(col + 1) * HEAD] = (ref[h] * r * g_ref[:, col * HEAD:(col + 1) * HEAD]).astype(BF16)
                col += 1

    mix = (ha + hq) * HEAD
    return pl.pallas_call(
        body, name=name, grid=(s // ts,),
        in_specs=[pl.BlockSpec((ha, ts, HEAD), lambda i: (0, i, 0)), pl.BlockSpec((hq, ts, HEAD), lambda i: (0, i, 0)),
                  pl.BlockSpec((1, mix), lambda i: (0, 0))],
        out_specs=pl.BlockSpec((ts, mix), lambda i: (i, 0)),
        out_shape=jax.ShapeDtypeStruct((s, mix), BF16), compiler_params=_params(),
    )(oa, ob, gains)


def _onorm_bwd(oa, ob, gains, don, name):
    ha, s, _ = oa.shape
    hq = ob.shape[0]
    ts = _tile(s, 256, 16)
    mix = (ha + hq) * HEAD

    def body(oa_ref, ob_ref, g_ref, don_ref, doa_ref, dob_ref, dg_ref):
        @pl.when(pl.program_id(0) == 0)
        def _():
            dg_ref[...] = jnp.zeros_like(dg_ref)

        col0 = 0
        for ref, d_ref, nh in ((oa_ref, doa_ref, ha), (ob_ref, dob_ref, hq)):
            ss = sum(jnp.sum(ref[h] * ref[h], axis=-1, keepdims=True) for h in range(nh))
            r = lax.rsqrt(ss / (nh * HEAD) + EPS)
            dot = jnp.zeros((ts, 1), F32)
            for h in range(nh):
                cols = slice((col0 + h) * HEAD, (col0 + h + 1) * HEAD)
                dot = dot + jnp.sum(don_ref[:, cols] * g_ref[:, cols] * ref[h], axis=-1, keepdims=True)
            mean = dot * r / (nh * HEAD)
            for h in range(nh):
                cols = slice((col0 + h) * HEAD, (col0 + h + 1) * HEAD)
                y = ref[h] * r
                dn = don_ref[:, cols]
                d_ref[h] = (r * (dn * g_ref[:, cols] - y * mean)).astype(BF16)
                dg_ref[0:1, cols] += jnp.sum(dn * y, axis=0, keepdims=True)
            col0 += nh

    return pl.pallas_call(
        body, name=name, grid=(s // ts,),
        in_specs=[pl.BlockSpec((ha, ts, HEAD), lambda i: (0, i, 0)), pl.BlockSpec((hq, ts, HEAD), lambda i: (0, i, 0)),
                  pl.BlockSpec((1, mix), lambda i: (0, 0)), pl.BlockSpec((ts, mix), lambda i: (i, 0))],
        out_specs=[pl.BlockSpec((ha, ts, HEAD), lambda i: (0, i, 0)), pl.BlockSpec((hq, ts, HEAD), lambda i: (0, i, 0)),
                   pl.BlockSpec((8, mix), lambda i: (0, 0))],
        out_shape=[jax.ShapeDtypeStruct((ha, s, HEAD), BF16), jax.ShapeDtypeStruct((hq, s, HEAD), BF16),
                   jax.ShapeDtypeStruct((8, mix), F32)],
        compiler_params=_params(),
    )(oa, ob, gains, don)


def _shift_rows(cur, halo_prev, halo_next, i, n):
    ts = cur.shape[0]
    row = lax.broadcasted_iota(jnp.int32, cur.shape, 0)
    first = jnp.where(i > 0, halo_prev[7:8, :], 0.0)
    last = jnp.where(i < n - 1, halo_next[0:1, :], 0.0)
    prev = jnp.where(row == 0, first, pltpu.roll(cur, 1, axis=0))
    nxt = jnp.where(row == ts - 1, last, pltpu.roll(cur, ts - 1, axis=0))
    return prev, nxt


def _halo_specs(ts, tc, col_off):
    per = ts // 8
    cur = pl.BlockSpec((ts, tc), lambda j, i: (i, j + col_off))
    prev = pl.BlockSpec((8, tc), lambda j, i: (jnp.maximum(i * per - 1, 0), j + col_off))

    def nxt_map(n_blocks):
        return pl.BlockSpec((8, tc), lambda j, i: (jnp.minimum((i + 1) * per, n_blocks - 1), j + col_off))

    return cur, prev, nxt_map


def _sigmoid(x):
    return 1.0 / (1.0 + jnp.exp(-x))


def _ffn_tiles(s, f):
    return _tile(s, 512, 16), _tile(f, 512, LANES)


def _gate_fwd(u, cw, cb, f, name):
    s = u.shape[0]
    ts, tc = _ffn_tiles(s, f)
    nj, ni = f // tc, s // ts

    def body(g_ref, gp_ref, gn_ref, u_ref, up_ref, un_ref, wg_ref, wu_ref, bg_ref, bu_ref, a_ref):
        i = pl.program_id(1)

        def conv(c_ref, p_ref, n_ref, w_ref, b_ref):
            cur = c_ref[...]
            prev, nxt = _shift_rows(cur, p_ref[...], n_ref[...], i, ni)
            return prev * w_ref[0:1, :] + cur * w_ref[1:2, :] + nxt * w_ref[2:3, :] + b_ref[...]

        gate = conv(g_ref, gp_ref, gn_ref, wg_ref, bg_ref)
        up = conv(u_ref, up_ref, un_ref, wu_ref, bu_ref)
        a_ref[...] = (gate * _sigmoid(gate) * up).astype(BF16)

    gc, gp, gn = _halo_specs(ts, tc, 0)
    uc, up_, un = _halo_specs(ts, tc, nj)
    wg = pl.BlockSpec((3, tc), lambda j, i: (0, j))
    wu = pl.BlockSpec((3, tc), lambda j, i: (0, j + nj))
    bg = pl.BlockSpec((1, tc), lambda j, i: (0, j))
    bu = pl.BlockSpec((1, tc), lambda j, i: (0, j + nj))
    return pl.pallas_call(
        body, name=name, grid=(nj, ni),
        in_specs=[gc, gp, gn(s // 8), uc, up_, un(s // 8), wg, wu, bg, bu],
        out_specs=pl.BlockSpec((ts, tc), lambda j, i: (i, j)),
        out_shape=jax.ShapeDtypeStruct((s, f), BF16), compiler_params=_params(),
    )(u, u, u, u, u, u, cw, cw, cb, cb)


def _gate_bwd(u, cw, cb, da, f, name):
    s = u.shape[0]
    ts, tc = _ffn_tiles(s, f)
    nj, ni = f // tc, s // ts

    def body(g_ref, gp_ref, gn_ref, u_ref, up_ref, un_ref, wg_ref, wu_ref, bg_ref, bu_ref, da_ref,
             dgu_ref, dcw_ref, dcb_ref):
        i = pl.program_id(1)

        @pl.when(i == 0)
        def _():
            dcw_ref[...] = jnp.zeros_like(dcw_ref)
            dcb_ref[...] = jnp.zeros_like(dcb_ref)

        def conv(c_ref, p_ref, n_ref, w_ref, b_ref):
            cur = c_ref[...]
            prev, nxt = _shift_rows(cur, p_ref[...], n_ref[...], i, ni)
            return prev * w_ref[0:1, :] + cur * w_ref[1:2, :] + nxt * w_ref[2:3, :] + b_ref[...], (prev, cur, nxt)

        gate, g_taps = conv(g_ref, gp_ref, gn_ref, wg_ref, bg_ref)
        up, u_taps = conv(u_ref, up_ref, un_ref, wu_ref, bu_ref)
        dav = da_ref[...]
        sg = _sigmoid(gate)
        d_up = dav * gate * sg
        d_gate = dav * up * (sg * (1.0 + gate * (1.0 - sg)))
        dgu_ref[0] = d_gate
        dgu_ref[1] = d_up
        for half, (d, taps) in enumerate(((d_gate, g_taps), (d_up, u_taps))):
            dcb_ref[half, 0:1, :] += jnp.sum(d, axis=0, keepdims=True)
            for k in range(3):
                dcw_ref[half, k, 0:1, :] += jnp.sum(d * taps[k], axis=0, keepdims=True)

    gc, gp, gn = _halo_specs(ts, tc, 0)
    uc, up_, un = _halo_specs(ts, tc, nj)
    wg = pl.BlockSpec((3, tc), lambda j, i: (0, j))
    wu = pl.BlockSpec((3, tc), lambda j, i: (0, j + nj))
    bg = pl.BlockSpec((1, tc), lambda j, i: (0, j))
    bu = pl.BlockSpec((1, tc), lambda j, i: (0, j + nj))
    tile = pl.BlockSpec((ts, tc), lambda j, i: (i, j))
    return pl.pallas_call(
        body, name=name, grid=(nj, ni),
        in_specs=[gc, gp, gn(s // 8), uc, up_, un(s // 8), wg, wu, bg, bu, tile],
        out_specs=[pl.BlockSpec((2, ts, tc), lambda j, i: (0, i, j)),
                   pl.BlockSpec((2, 3, 8, tc), lambda j, i: (0, 0, 0, j)),
                   pl.BlockSpec((2, 8, tc), lambda j, i: (0, 0, j))],
        out_shape=[jax.ShapeDtypeStruct((2, s, f), F32),
                   jax.ShapeDtypeStruct((2, 3, 8, f), F32), jax.ShapeDtypeStruct((2, 8, f), F32)],
        compiler_params=_params(),
    )(u, u, u, u, u, u, cw, cw, cb, cb, da)


def _conv_bwd(dgu, cw, name):
    _, s, f = dgu.shape
    ts, tc = _ffn_tiles(s, f)
    nj, ni = f // tc, s // ts
    per = ts // 8

    def body(c_ref, p_ref, n_ref, w_ref, o_ref):
        i = pl.program_id(2)
        cur = c_ref[...]
        prev, nxt = _shift_rows(cur, p_ref[...], n_ref[...], i, ni)
        o_ref[...] = (prev * w_ref[2:3, :] + cur * w_ref[1:2, :] + nxt * w_ref[0:1, :]).astype(BF16)

    return pl.pallas_call(
        body, name=name, grid=(2, nj, ni),
        in_specs=[pl.BlockSpec((None, ts, tc), lambda h, j, i: (h, i, j)),
                  pl.BlockSpec((None, 8, tc), lambda h, j, i: (h, jnp.maximum(i * per - 1, 0), j)),
                  pl.BlockSpec((None, 8, tc), lambda h, j, i: (h, jnp.minimum((i + 1) * per, s // 8 - 1), j)),
                  pl.BlockSpec((3, tc), lambda h, j, i: (0, h * nj + j))],
        out_specs=pl.BlockSpec((ts, tc), lambda h, j, i: (i, h * nj + j)),
        out_shape=jax.ShapeDtypeStruct((s, 2 * f), BF16), compiler_params=_params(),
    )(dgu, dgu, dgu, cw)


def _loss_head(y, target, name):
    s, d = y.shape
    ts = _tile(s, 256, 8)

    def body(y_ref, t_ref, dy_ref, l_ref):
        @pl.when(pl.program_id(0) == 0)
        def _():
            l_ref[...] = jnp.zeros_like(l_ref)

        err = y_ref[...] - t_ref[...]
        dy_ref[...] = err / d
        l_ref[...] += jnp.zeros((8, LANES), F32) + 0.5 * jnp.sum(jnp.sum(err * err, axis=-1, keepdims=True) / d)

    blk = pl.BlockSpec((ts, d), lambda i: (i, 0))
    return pl.pallas_call(
        body, name=name, grid=(s // ts,),
        in_specs=[blk, blk], out_specs=[blk, pl.BlockSpec((8, LANES), lambda i: (0, 0))],
        out_shape=[jax.ShapeDtypeStruct((s, d), F32), jax.ShapeDtypeStruct((8, LANES), F32)],
        compiler_params=_params(),
    )(y, target)


SMALL = ("ln1_g", "qn_a", "kn_a", "rpb", "qn_b", "kn_b", "sink", "on_a", "on_b", "ln2_g", "conv_b", "conv_w")
PACK_ALIGN = 8 * LANES


def _pack(arrays):
    flat = []
    for a in arrays:
        a = a.reshape(-1)
        flat.append(jnp.pad(a, (0, -a.size % PACK_ALIGN)))
    return jnp.concatenate(flat).reshape(-1, LANES)


def _unpack(packed, like):
    out, at = [], 0
    flat = packed.reshape(-1)
    for a in like:
        out.append(flat[at:at + a.size].reshape(a.shape))
        at += a.size + (-a.size % PACK_ALIGN)
    return out


def _matmul_tiles(s, k, j):
    return dict(ti=_tile(s, 512, 16), tj=_tile(j, 1536, LANES), tk=_tile(k, 2048, LANES))


def kernel(x, positions, ln1_g, w_in, qn_a, kn_a, rpb, qn_b, kn_b, sink, on_a, on_b, w_out, ln2_g, w_up, conv_w, conv_b, w_down, loss_target, m_ln1_g, m_w_in, m_qn_a, m_kn_a, m_rpb, m_qn_b, m_kn_b, m_sink, m_on_a, m_on_b, m_w_out, m_ln2_g, m_w_up, m_conv_w, m_conv_b, m_w_down, v_ln1_g, v_w_in, v_qn_a, v_kn_a, v_rpb, v_qn_b, v_kn_b, v_sink, v_on_a, v_on_b, v_w_out, v_ln2_g, v_w_up, v_conv_w, v_conv_b, v_w_down):
    weights = dict(ln1_g=ln1_g, w_in=w_in, qn_a=qn_a, kn_a=kn_a, rpb=rpb, qn_b=qn_b, kn_b=kn_b, sink=sink, on_a=on_a,
                   on_b=on_b, w_out=w_out, ln2_g=ln2_g, w_up=w_up, conv_w=conv_w, conv_b=conv_b, w_down=w_down)
    mom1 = dict(ln1_g=m_ln1_g, w_in=m_w_in, qn_a=m_qn_a, kn_a=m_kn_a, rpb=m_rpb, qn_b=m_qn_b, kn_b=m_kn_b, sink=m_sink,
                on_a=m_on_a, on_b=m_on_b, w_out=m_w_out, ln2_g=m_ln2_g, w_up=m_w_up, conv_w=m_conv_w, conv_b=m_conv_b,
                w_down=m_w_down)
    mom2 = dict(ln1_g=v_ln1_g, w_in=v_w_in, qn_a=v_qn_a, kn_a=v_kn_a, rpb=v_rpb, qn_b=v_qn_b, kn_b=v_kn_b, sink=v_sink,
                on_a=v_on_a, on_b=v_on_b, w_out=v_w_out, ln2_g=v_ln2_g, w_up=v_w_up, conv_w=v_conv_w, conv_b=v_conv_b,
                w_down=v_w_down)
    order = ("ln1_g", "w_in", "qn_a", "kn_a", "rpb", "qn_b", "kn_b", "sink", "on_a", "on_b", "w_out", "ln2_g", "w_up",
             "conv_w", "conv_b", "w_down")

    depth, d = ln1_g.shape
    s = x.shape[1]
    ha = on_a.shape[1] // HEAD
    hq = on_b.shape[1] // HEAD
    pw = w_in.shape[2] * N_DEV
    hkv = (pw - 3 * ha * HEAD - hq * HEAD) // (2 * HEAD)
    f = w_down.shape[1] * N_DEV
    mix = (ha + hq) * HEAD
    cfg = (ha, hq, hkv)
    fs = conv_w.shape[2]
    dev = 4 * lax.axis_index("x") + 2 * lax.axis_index("y") + lax.axis_index("c")
    core = lax.axis_index("c").astype(jnp.int32).reshape(1)

    shard = {n: weights[n].astype(BF16) for n in ("w_in", "w_out", "w_up", "w_down")}

    def unshard(n, g):
        if n in ("w_in", "w_up"):
            return g.transpose(1, 0, 2).reshape(g.shape[1], N_DEV * g.shape[2])
        return g.reshape(N_DEV * g.shape[1], g.shape[2])

    full = {n: [None] * depth for n in shard}
    for n in shard:
        full[n][0] = unshard(n, _allgather(shard[n][0], "gather0_" + n))
    cw_rows = depth * 3
    cw_pad = jnp.pad(conv_w.reshape(cw_rows, fs), ((0, -cw_rows % 8), (0, 0)))
    g_cw = _allgather(cw_pad, "gather_conv_w")
    full_cw = g_cw[:, :cw_rows].reshape(N_DEV, depth, 3, fs).transpose(1, 2, 0, 3).reshape(depth, 3, 2 * f)

    inv = ROPE_THETA ** (-jnp.arange(0, HEAD, 2, dtype=F32) / HEAD)
    ang = positions.astype(F32)[:, None] * inv[None, :]
    cos = jnp.concatenate([jnp.cos(ang), jnp.cos(ang)], axis=-1)
    sin = jnp.concatenate([-jnp.sin(ang), jnp.sin(ang)], axis=-1)
    qk = jnp.arange(GRID_W * GRID_W)
    dc_of = (qk % GRID_W) - (qk // GRID_W) + (WIN_C - 1)
    onehot = (dc_of[:, None] == jnp.arange(LANES)[None, :]).astype(BF16)

    tiles_s = _tile(s, 512, 16)

    xs = x.reshape(s, d)
    saved = []
    for l in range(depth):
        more = l + 1 < depth

        def fwd_matmul(n, a_op, name, **kw):
            if not more:
                return _matmul(a_op, full[n][l], dims="nn", out_dtype=F32, name=name + "_last", **kw)
            out, got = _matmul(a_op, full[n][l], dims="nn", out_dtype=F32, name=name, comm=_gather_comm([shard[n][l + 1]]), **kw)
            full[n][l + 1] = unshard(n, got)
            return out

        gains = jnp.zeros((8, HEAD), F32).at[0].set(qn_a[l]).at[1].set(kn_a[l]).at[2].set(qn_b[l]).at[3].set(kn_b[l])
        on_g = jnp.concatenate([on_a[l], on_b[l]]).reshape(1, mix)
        h = _rms_fwd(xs, ln1_g[l].reshape(1, d), "ln1_fwd")
        proj = fwd_matmul("w_in", h, "proj_fwd", ti=tiles_s, tj=_tile(pw, 1536, LANES), tk=d)
        qa, ka, va, qb, kb, vb = _qkv_fwd(proj, gains, cos, sin, cfg, "qkv_fwd")
        tb = _na_bias(rpb[l].reshape(-1), ha, "na_bias")
        oa = _na_fwd(qa, ka, va, tb, "na_fwd")
        ob = _wa_fwd(qb, kb, vb, sink[l], "wa_fwd")
        o_n = _onorm_fwd(oa, ob, on_g, "onorm_fwd")
        x1 = fwd_matmul("w_out", o_n, "out_fwd", ti=tiles_s, tj=_tile(d, 2048, LANES), tk=mix, resid=xs)
        h2 = _rms_fwd(x1, ln2_g[l].reshape(1, d), "ln2_fwd")
        u = fwd_matmul("w_up", h2, "up_fwd", ti=tiles_s, tj=fs, tk=d)
        cb = conv_b[l].reshape(1, 2 * f)
        a = _gate_fwd(u, full_cw[l], cb, f, "gate_fwd")
        x2 = fwd_matmul("w_down", a, "down_fwd", ti=tiles_s, tj=_tile(d, 1024, LANES), tk=_tile(f, 1536, LANES), resid=x1)
        saved.append(dict(x=xs, h=h, proj=proj, gains=gains, on_g=on_g, qkv=(qa, ka, va, qb, kb, vb), tb=tb, oa=oa, ob=ob,
                          o_n=o_n, x1=x1, h2=h2, u=u, cb=cb, a=a))
        xs = x2

    dx, loss_part = _loss_head(xs, loss_target.reshape(s, d), "loss_head")
    loss = lax.psum(loss_part[0, 0], ("x", "y", "c"))

    small_grads = [None] * depth
    big = {n: [None] * depth for n in ("w_in", "w_out", "w_up", "w_down")}
    pending = None
    for l in reversed(range(depth)):
        sv = saved[l]
        qa, ka, va, qb, kb, vb = sv["qkv"]

        def grad_matmul(n, a_op, b_op, name, **kw):
            if pending is None:
                return _matmul(a_op, b_op, dims="tn", out_dtype=BF16, name=name + "_first", **kw)
            out, got = _matmul(a_op, b_op, dims="tn", out_dtype=BF16, name=name, comm=_scatter_comm([pending[n]]), **kw)
            big[n][l + 1] = _adamw(got, weights[n][l + 1], mom1[n][l + 1], mom2[n][l + 1], "adamw_" + n)
            return out

        da = _matmul(dx, full["w_down"][l], dims="nt", ti=tiles_s, tj=_tile(f, 1408, LANES), tk=d, out_dtype=F32, name="down_bwd_x")
        gw_down = grad_matmul("w_down", sv["a"], dx, "down_bwd_w", ti=_tile(f, 1408, LANES), tj=d, tk=tiles_s, j_outer=False)
        dgu, dcw, dcb = _gate_bwd(sv["u"], full_cw[l], sv["cb"], da, f, "gate_bwd")
        du = _conv_bwd(dgu, full_cw[l], "conv_bwd")
        dh2 = _matmul(du, full["w_up"][l], dims="nt", ti=tiles_s, tj=d, tk=fs, out_dtype=F32, name="up_bwd_x")
        gw_up = grad_matmul("w_up", sv["h2"], du, "up_bwd_w", ti=d, tj=fs, tk=tiles_s, dev_major=True)
        dx1, dln2 = _rms_bwd(sv["x1"], ln2_g[l].reshape(1, d), dh2, dx, "ln2_bwd")
        don = _matmul(dx1, full["w_out"][l], dims="nt", ti=tiles_s, tj=mix, tk=d, out_dtype=F32, name="out_bwd_x")
        gw_out = grad_matmul("w_out", sv["o_n"], dx1, "out_bwd_w", ti=_tile(mix, 1024, LANES), tj=d, tk=tiles_s, j_outer=False)
        doa, dob, don_g = _onorm_bwd(sv["oa"], sv["ob"], sv["on_g"], don, "onorm_bwd")
        dqa, dka, dva, dtb = _na_bwd(qa, ka, va, sv["tb"], doa, "na_bwd")
        dqb, dkb, dvb, dsink = _wa_bwd(qb, kb, vb, sink[l], dob, "wa_bwd")
        drpb = _rpb_grad(dtb, onehot, "rpb_grad")
        dproj, dgains = _qkv_bwd(sv["proj"], sv["gains"], cos, sin, (dqa, dka, dva, dqb, dkb, dvb), cfg, "qkv_bwd")
        dh = _matmul(dproj, full["w_in"][l], dims="nt", ti=tiles_s, tj=d, tk=_tile(pw, 1536, LANES), out_dtype=F32, name="proj_bwd_x")
        gw_in = grad_matmul("w_in", sv["h"], dproj, "proj_bwd_w", ti=d, tj=_tile(pw, 1536, LANES), tk=tiles_s)
        dx, dln1 = _rms_bwd(sv["x"], ln1_g[l].reshape(1, d), dh, dx1, "ln1_bwd")

        small_grads[l] = dict(
            ln1_g=dln1[0], qn_a=dgains[0], kn_a=dgains[1], rpb=drpb, qn_b=dgains[2], kn_b=dgains[3], sink=dsink[:, 0, 0],
            on_a=don_g[0, :ha * HEAD], on_b=don_g[0, ha * HEAD:], ln2_g=dln2[0],
            conv_b=dcb[:, 0, :].reshape(2 * f), conv_w=dcw[:, :, 0, :].transpose(1, 0, 2).reshape(3, 2 * f))

        pending = dict(
            w_in=gw_in.reshape(d, N_DEV, pw // N_DEV).transpose(1, 0, 2),
            w_out=gw_out.reshape(N_DEV, mix // N_DEV, d),
            w_up=gw_up,
            w_down=gw_down.reshape(N_DEV, f // N_DEV, d))

    for n in big:
        summed = _reduce_scatter(pending[n], core, "rs0_" + n)
        big[n][0] = _adamw(summed, weights[n][0], mom1[n][0], mom2[n][0], "adamw0_" + n)

    grads_l = [small_grads[l][n] for l in range(depth) for n in SMALL]
    gathered = _allgather(_pack(grads_l), "gather_small")
    zeros_cw = jnp.zeros((3, 2 * f), F32)

    def small_state(src):
        return _pack([zeros_cw if n == "conv_w" else src[n][l] for l in range(depth) for n in SMALL])

    sm = _adamw(gathered, small_state(weights), small_state(mom1), small_state(mom2), "adamw_small")
    sm = [_unpack(t, grads_l) for t in sm]
    small_out = {n: [jnp.stack([sm[k][l * len(SMALL) + i] for l in range(depth)]) for k in range(4)]
                 for i, n in enumerate(SMALL)}
    cw_grad = lax.dynamic_slice_in_dim(small_out["conv_w"][0], dev * fs, fs, axis=2)
    cw_rows_pad = cw_rows + (-cw_rows % 8)

    def rows8(a):
        return jnp.pad(a.reshape(cw_rows, fs), ((0, cw_rows_pad - cw_rows), (0, 0)))

    cw_res = _adamw(rows8(cw_grad)[None], rows8(conv_w), rows8(m_conv_w), rows8(v_conv_w), "adamw_conv_w")
    small_out["conv_w"] = [t[:cw_rows].reshape(depth, 3, fs) for t in cw_res]

    results = {n: ([jnp.stack([big[n][l][k] for l in range(depth)]) for k in range(4)] if n in big else small_out[n])
               for n in order}
    grad_x = dx.reshape(1, s, d)
    return (loss, grad_x, *[results[n][0] for n in order], *[results[n][1] for n in order],
            *[results[n][2] for n in order], *[results[n][3] for n in order])
```

```python
import functools
import math

import jax
import jax.numpy as jnp
from jax import lax
from jax.experimental import pallas as pl
from jax.experimental.pallas import tpu as pltpu

F32 = jnp.float32
BF16 = jnp.bfloat16

HEAD = 128
GRID_W = 64
WIN_R = 8
WIN_C = 16
BAND = 128
ROPE_THETA = 10000.0
EPS = 1e-6
NEG = -1e30
SCALE = 1.0 / math.sqrt(HEAD)

ADAM_LR = 0.001
ADAM_B1 = 0.9
ADAM_B2 = 0.999
ADAM_EPS = 1e-08
ADAM_WD = 0.01
ADAM_STEP = 10

N_DEV = 8
LANES = 128
VMEM_LIMIT_BYTES = 56 * 2 ** 20
MESH = pl.DeviceIdType.MESH
ANY = pl.BlockSpec(memory_space=pl.ANY)
SMEM = pl.BlockSpec(memory_space=pltpu.SMEM)


def _params():
    return pltpu.CompilerParams(vmem_limit_bytes=VMEM_LIMIT_BYTES)


def _tile(n, pref, align):
    t = min(n, pref)
    t -= t % align
    while t > 0 and n % t:
        t -= align
    return t if t > 0 else n


def _place():
    x, y, c = lax.axis_index("x"), lax.axis_index("y"), lax.axis_index("c")
    chips = [(1 - x, y), (x, 1 - y), (1 - x, 1 - y)]
    return x, y, c, chips


COPIES_PER_ARRAY = N_DEV - 1


def _comm_scratch(n_arrays):
    return [pltpu.SemaphoreType.DMA((COPIES_PER_ARRAY * n_arrays,)), pltpu.SemaphoreType.DMA((COPIES_PER_ARRAY * n_arrays,)),
            pltpu.SemaphoreType.DMA((n_arrays,))]


def _gather_plan(src_refs, out_refs, send_sems, recv_sems, local_sems):
    x, y, c, chips = _place()
    me, sibling = (x, y, c), (x, y, 1 - c)

    def slot(a, px, py, pc):
        return out_refs[a].at[4 * px + 2 * py + pc]

    def copy(a, k, block, to, src=None):
        return pltpu.make_async_remote_copy(
            src_ref=slot(a, *block) if src is None else src, dst_ref=slot(a, *block),
            send_sem=send_sems.at[COPIES_PER_ARRAY * a + k], recv_sem=recv_sems.at[COPIES_PER_ARRAY * a + k],
            device_id=to, device_id_type=MESH)

    def mine(a):
        return pltpu.make_async_copy(src_refs[a], slot(a, *me), local_sems.at[a])

    def first(a):
        return [copy(a, 0, me, sibling, src=src_refs[a])] + [
            copy(a, 1 + j, me, (*chip, c), src=src_refs[a]) for j, chip in enumerate(chips)]

    def passed(a):
        return [copy(a, 4 + j, (*chip, c), sibling) for j, chip in enumerate(chips)]

    def start():
        for a in range(len(src_refs)):
            mine(a).start()
            for cp in first(a):
                cp.start()

    def finish():
        forwards = [passed(a) for a in range(len(src_refs))]
        for a in range(len(src_refs)):
            for j, chip in enumerate(chips):
                copy(a, 1 + j, (*chip, c), me).wait_recv()
                forwards[a][j].start()
        for a in range(len(src_refs)):
            copy(a, 0, sibling, me).wait_recv()
            for j, chip in enumerate(chips):
                copy(a, 4 + j, (*chip, 1 - c), me).wait_recv()
            for cp in first(a) + forwards[a]:
                cp.wait_send()
            mine(a).wait()

    return start, finish


def _scatter_plan(src_refs, out_refs, send_sems, recv_sems, local_sems):
    x, y, c, _ = _place()
    me = 4 * x + 2 * y + c

    def peer(k):
        px = 1 - x if k & 4 else x
        py = 1 - y if k & 2 else y
        pc = 1 - c if k & 1 else c
        return (px, py, pc), 4 * px + 2 * py + pc

    def copy(a, k, outgoing):
        to, idx = peer(k)
        return pltpu.make_async_remote_copy(
            src_ref=src_refs[a].at[idx], dst_ref=out_refs[a].at[me if outgoing else idx],
            send_sem=send_sems.at[COPIES_PER_ARRAY * a + k - 1], recv_sem=recv_sems.at[COPIES_PER_ARRAY * a + k - 1],
            device_id=to, device_id_type=MESH)

    def mine(a):
        return pltpu.make_async_copy(src_refs[a].at[me], out_refs[a].at[me], local_sems.at[a])

    def start():
        for a in range(len(src_refs)):
            mine(a).start()
            for k in range(1, N_DEV):
                copy(a, k, True).start()

    def finish():
        for a in range(len(src_refs)):
            for k in range(1, N_DEV):
                copy(a, k, False).wait_recv()
            for k in range(1, N_DEV):
                copy(a, k, True).wait_send()
            mine(a).wait()

    return start, finish


def _allgather(v, name):
    def body(v_ref, out_ref, send_sems, recv_sems, local_sems):
        start, finish = _gather_plan([v_ref], [out_ref], send_sems, recv_sems, local_sems)
        start()
        finish()

    return pl.pallas_call(
        body, name=name,
        out_shape=jax.ShapeDtypeStruct((N_DEV,) + v.shape, v.dtype),
        in_specs=[ANY], out_specs=ANY, scratch_shapes=_comm_scratch(1),
    )(v)


def _sibling_exchange(g, name):
    def body(g_ref, out_ref, send_sems, recv_sems):
        x, y, c, _ = _place()
        sibling = (x, y, 1 - c)
        copies = []
        for j in range(4):
            copies.append(pltpu.make_async_remote_copy(
                src_ref=g_ref.at[2 * j + (1 - c)], dst_ref=out_ref.at[j],
                send_sem=send_sems.at[j], recv_sem=recv_sems.at[j], device_id=sibling, device_id_type=MESH))
        for cp in copies:
            cp.start()
        for cp in copies:
            cp.wait_recv()
        for cp in copies:
            cp.wait_send()

    return pl.pallas_call(
        body, name=name,
        out_shape=jax.ShapeDtypeStruct((4,) + g.shape[1:], g.dtype),
        in_specs=[ANY], out_specs=ANY,
        scratch_shapes=[pltpu.SemaphoreType.DMA((4,)), pltpu.SemaphoreType.DMA((4,))],
    )(g)


def _pair_sum(g, got, core, name):
    _, r, c = g.shape
    tr = _tile(r, max(16, (1 << 20) // c), 16)

    def body(core_ref, g_ref, got_ref, o_ref):
        del core_ref
        o_ref[...] = (g_ref[...].astype(F32) + got_ref[...].astype(F32)).astype(o_ref.dtype)

    return pl.pallas_call(
        body, name=name,
        out_shape=jax.ShapeDtypeStruct((4, r, c), g.dtype),
        grid_spec=pltpu.PrefetchScalarGridSpec(
            num_scalar_prefetch=1, grid=(4, r // tr),
            in_specs=[pl.BlockSpec((None, tr, c), lambda j, i, core_ref: (2 * j + core_ref[0], i, 0)),
                      pl.BlockSpec((None, tr, c), lambda j, i, core_ref: (j, i, 0))],
            out_specs=pl.BlockSpec((None, tr, c), lambda j, i, core_ref: (j, i, 0))),
        compiler_params=_params(),
    )(core, g, got)


def _chip_exchange(p, name):
    def body(p_ref, out_ref, send_sems, recv_sems, local_sem):
        x, y, c, chips = _place()
        mine = pltpu.make_async_copy(p_ref.at[2 * x + y], out_ref.at[3], local_sem)
        mine.start()
        copies = []
        for k, (px, py) in enumerate(chips):
            copies.append(pltpu.make_async_remote_copy(
                src_ref=p_ref.at[2 * px + py], dst_ref=out_ref.at[k],
                send_sem=send_sems.at[k], recv_sem=recv_sems.at[k], device_id=(px, py, c), device_id_type=MESH))
        for cp in copies:
            cp.start()
        for cp in copies:
            cp.wait_recv()
        for cp in copies:
            cp.wait_send()
        mine.wait()

    return pl.pallas_call(
        body, name=name,
        out_shape=jax.ShapeDtypeStruct(p.shape, p.dtype),
        in_specs=[ANY], out_specs=ANY,
        scratch_shapes=[pltpu.SemaphoreType.DMA((3,)), pltpu.SemaphoreType.DMA((3,)), pltpu.SemaphoreType.DMA],
    )(p)


def _reduce_scatter(g, core, name):
    got = _sibling_exchange(g, name + "_d2d")
    p = _pair_sum(g, got, core, name + "_pair")
    return _chip_exchange(p, name + "_ici")


def _adamw(parts, w, m, v, name, layer=None, into=None):
    n_parts, r, c = parts.shape
    tr = _tile(r, max(8, (1 << 19) // c), 16 if parts.dtype == BF16 else 8)
    c1 = 1.0 - ADAM_B1 ** ADAM_STEP
    c2 = 1.0 - ADAM_B2 ** ADAM_STEP
    n_into = 0 if into is None else len(into)

    def body(p_ref, w_ref, m_ref, v_ref, *rest):
        g_out, d_out, m_out, v_out = rest[n_into:]
        g = p_ref[0].astype(F32)
        for k in range(1, n_parts):
            g = g + p_ref[k].astype(F32)
        m2 = ADAM_B1 * m_ref[...] + (1.0 - ADAM_B1) * g
        v2 = ADAM_B2 * v_ref[...] + (1.0 - ADAM_B2) * (g * g)
        g_out[...] = g
        m_out[...] = m2
        v_out[...] = v2
        d_out[...] = -ADAM_LR * ((m2 / c1) / (jnp.sqrt(v2 / c2) + ADAM_EPS) + ADAM_WD * w_ref[...])

    if layer is None:
        blk = pl.BlockSpec((tr, c), lambda i: (i, 0))
        out = jax.ShapeDtypeStruct((r, c), F32)
    else:
        blk = pl.BlockSpec((None, tr, c), lambda i: (layer, i, 0))
        out = jax.ShapeDtypeStruct(w.shape, F32)
    return pl.pallas_call(
        body, name=name, grid=(r // tr,),
        in_specs=[pl.BlockSpec((n_parts, tr, c), lambda i: (0, i, 0)), blk, blk, blk] + [ANY] * n_into,
        out_specs=[blk, blk, blk, blk], out_shape=[out, out, out, out],
        input_output_aliases={4 + k: k for k in range(n_into)},
        compiler_params=_params(),
    )(parts, w, m, v, *(into or ()))


def _gather_comm(shards):
    return _gather_plan, shards, [jax.ShapeDtypeStruct((N_DEV,) + v.shape, v.dtype) for v in shards]


def _scatter_comm(blocks):
    return _scatter_plan, blocks, [jax.ShapeDtypeStruct(g.shape, g.dtype) for g in blocks]


def _matmul(a, b, *, dims, ti, tj, tk, out_dtype, name, j_outer=True, resid=None, dev_major=False, comm=None,
            k_blocks=None):
    if dims == "nn":
        (I, K), (K2, J) = a.shape, b.shape
    elif dims == "nt":
        (I, K), (J, K2) = a.shape, b.shape
    else:
        (K, I), (K2, J) = a.shape, b.shape
    assert K == K2 and I % ti == 0 and J % tj == 0 and K % tk == 0, (name, a.shape, b.shape, ti, tj, tk)
    k0, nk = k_blocks if k_blocks is not None else (0, K // tk)
    ni, nj = I // ti, J // tj

    def ij(g0, g1):
        return (g1, g0) if j_outer else (g0, g1)

    if dims == "nn":
        a_spec = pl.BlockSpec((ti, tk), lambda g0, g1, k: (ij(g0, g1)[0], k0 + k))
        b_spec = pl.BlockSpec((tk, tj), lambda g0, g1, k: (k0 + k, ij(g0, g1)[1]))
        dn = (((1,), (0,)), ((), ()))
    elif dims == "nt":
        a_spec = pl.BlockSpec((ti, tk), lambda g0, g1, k: (ij(g0, g1)[0], k0 + k))
        b_spec = pl.BlockSpec((tj, tk), lambda g0, g1, k: (ij(g0, g1)[1], k0 + k))
        dn = (((1,), (1,)), ((), ()))
    else:
        a_spec = pl.BlockSpec((tk, ti), lambda g0, g1, k: (k0 + k, ij(g0, g1)[0]))
        b_spec = pl.BlockSpec((tk, tj), lambda g0, g1, k: (k0 + k, ij(g0, g1)[1]))
        dn = (((0,), (0,)), ((), ()))
    in_specs = [a_spec, b_spec]
    operands = [a, b]
    if resid is not None:
        in_specs.append(pl.BlockSpec((ti, tj), lambda g0, g1, k: ij(g0, g1)))
        operands.append(resid)
    if dev_major:
        out_spec = pl.BlockSpec((None, ti, tj), lambda g0, g1, k: (ij(g0, g1)[1], ij(g0, g1)[0], 0))
        out_shape = jax.ShapeDtypeStruct((nj, I, tj), out_dtype)
    else:
        out_spec = pl.BlockSpec((ti, tj), lambda g0, g1, k: ij(g0, g1))
        out_shape = jax.ShapeDtypeStruct((I, J), out_dtype)

    grid = (nj, ni, nk) if j_outer else (ni, nj, nk)
    n_in = len(operands)
    n_comm = 0
    out_specs, out_shapes = [out_spec], [out_shape]
    scratch = [pltpu.VMEM((ti, tj), F32)] if nk > 1 else []
    if comm is not None:
        plan, comm_in, comm_out = comm
        n_comm = len(comm_in)
        operands += list(comm_in)
        in_specs += [ANY] * n_comm
        out_specs += [ANY] * n_comm
        out_shapes += list(comm_out)
        scratch += _comm_scratch(n_comm)

    def body(*refs):
        a_ref, b_ref = refs[0], refs[1]
        r_ref = refs[2] if resid is not None else None
        o_ref = refs[n_in + n_comm]
        if comm is not None:
            start, finish_comm = plan(refs[n_in:n_in + n_comm], refs[n_in + n_comm + 1:n_in + 2 * n_comm + 1], *refs[-3:])
            steps = [pl.program_id(axis) for axis in range(3)]

            @pl.when((steps[0] == 0) & (steps[1] == 0) & (steps[2] == 0))
            def _():
                start()

        part = lax.dot_general(a_ref[...].astype(BF16), b_ref[...].astype(BF16), dn, preferred_element_type=F32)

        def finish(acc):
            if r_ref is not None:
                acc = acc + r_ref[...]
            o_ref[...] = acc.astype(o_ref.dtype)

        if nk == 1:
            finish(part)
        else:
            acc_ref = refs[n_in + 2 * n_comm + 1]
            k = pl.program_id(2)

            @pl.when(k == 0)
            def _():
                acc_ref[...] = part

            @pl.when(k > 0)
            def _():
                acc_ref[...] += part

            @pl.when(k == nk - 1)
            def _():
                finish(acc_ref[...])

        if comm is not None:
            @pl.when((steps[0] == grid[0] - 1) & (steps[1] == grid[1] - 1) & (steps[2] == grid[2] - 1))
            def _():
                finish_comm()

    res = pl.pallas_call(
        body, name=name, grid=grid,
        in_specs=in_specs, out_specs=out_specs, out_shape=out_shapes,
        scratch_shapes=scratch, compiler_params=_params(),
    )(*operands)
    return res[0] if comm is None else res


def _rms_fwd(x, g, name):
    s, d = x.shape
    ts = _tile(s, 256, 16)

    def body(x_ref, g_ref, h_ref):
        xv = x_ref[...]
        r = lax.rsqrt(jnp.mean(xv * xv, axis=-1, keepdims=True) + EPS)
        h_ref[...] = (xv * r * g_ref[...]).astype(BF16)

    return pl.pallas_call(
        body, name=name, grid=(s // ts,),
        in_specs=[pl.BlockSpec((ts, d), lambda i: (i, 0)), pl.BlockSpec((1, d), lambda i: (0, 0))],
        out_specs=pl.BlockSpec((ts, d), lambda i: (i, 0)),
        out_shape=jax.ShapeDtypeStruct((s, d), BF16), compiler_params=_params(),
    )(x, g)


def _rms_bwd(x, g, dh, dres, name):
    s, d = x.shape
    ts = _tile(s, 256, 16)

    def body(x_ref, g_ref, dh_ref, dres_ref, dx_ref, dxb_ref, dg_ref):
        xv = x_ref[...]
        r = lax.rsqrt(jnp.mean(xv * xv, axis=-1, keepdims=True) + EPS)
        y = xv * r
        dhv = dh_ref[...]
        gd = dhv * g_ref[...]
        dxv = dres_ref[...] + r * (gd - y * jnp.mean(gd * y, axis=-1, keepdims=True))
        dx_ref[...] = dxv
        dxb_ref[...] = dxv.astype(BF16)

        @pl.when(pl.program_id(0) == 0)
        def _():
            dg_ref[...] = jnp.zeros_like(dg_ref)

        dg_ref[0:1, :] += jnp.sum(dhv * y, axis=0, keepdims=True)

    blk = pl.BlockSpec((ts, d), lambda i: (i, 0))
    return pl.pallas_call(
        body, name=name, grid=(s // ts,),
        in_specs=[blk, pl.BlockSpec((1, d), lambda i: (0, 0)), blk, blk],
        out_specs=[blk, blk, pl.BlockSpec((8, d), lambda i: (0, 0))],
        out_shape=[jax.ShapeDtypeStruct((s, d), F32), jax.ShapeDtypeStruct((s, d), BF16), jax.ShapeDtypeStruct((8, d), F32)],
        compiler_params=_params(),
    )(x, g, dh, dres)


def _head_norm(t, gain):
    r = lax.rsqrt(jnp.mean(t * t, axis=-1, keepdims=True) + EPS)
    return t * r * gain


def _head_norm_bwd(t, gain, dn):
    r = lax.rsqrt(jnp.mean(t * t, axis=-1, keepdims=True) + EPS)
    y = t * r
    gd = dn * gain
    dt = r * (gd - y * jnp.mean(gd * y, axis=-1, keepdims=True))
    return dt, jnp.sum(dn * y, axis=0, keepdims=True)


def _rope(n, cos, sin):
    return n * cos + pltpu.roll(n, HEAD // 2, axis=1) * sin


def _rope_bwd(do, cos, sin):
    return do * cos + pltpu.roll(do * sin, HEAD // 2, axis=1)


def _qkv_fwd(proj, gains, cos, sin, cfg, name):
    s, pw = proj.shape
    ha, hq, hkv = cfg
    ts = _tile(s, 256, 16)

    def body(p_ref, gn_ref, cos_ref, sin_ref, qa_ref, ka_ref, va_ref, qb_ref, kb_ref, vb_ref):
        cosv, sinv = cos_ref[...], sin_ref[...]
        col = 0
        for out_ref, nh, gi, rot in ((qa_ref, ha, 0, False), (ka_ref, ha, 1, False), (va_ref, ha, None, False),
                                     (qb_ref, hq, 2, True), (kb_ref, hkv, 3, True), (vb_ref, hkv, None, False)):
            for h in range(nh):
                t = p_ref[:, col * HEAD:(col + 1) * HEAD]
                if gi is not None:
                    t = _head_norm(t, gn_ref[gi:gi + 1, :])
                if rot:
                    t = _rope(t, cosv, sinv)
                out_ref[h] = t.astype(BF16)
                col += 1

    def hm(nh):
        return pl.BlockSpec((nh, ts, HEAD), lambda i: (0, i, 0)), jax.ShapeDtypeStruct((nh, s, HEAD), BF16)

    specs, shapes = zip(hm(ha), hm(ha), hm(ha), hm(hq), hm(hkv), hm(hkv))
    tok = pl.BlockSpec((ts, HEAD), lambda i: (i, 0))
    return pl.pallas_call(
        body, name=name, grid=(s // ts,),
        in_specs=[pl.BlockSpec((ts, pw), lambda i: (i, 0)), pl.BlockSpec((8, HEAD), lambda i: (0, 0)), tok, tok],
        out_specs=list(specs), out_shape=list(shapes), compiler_params=_params(),
    )(proj, gains, cos, sin)


def _qkv_bwd(proj, gains, cos, sin, grads, cfg, name):
    s, pw = proj.shape
    ha, hq, hkv = cfg
    ts = _tile(s, 256, 16)

    def body(p_ref, gn_ref, cos_ref, sin_ref, dqa, dka, dva, dqb, dkb, dvb, dp_ref, dgn_ref):
        cosv, sinv = cos_ref[...], sin_ref[...]

        @pl.when(pl.program_id(0) == 0)
        def _():
            dgn_ref[...] = jnp.zeros_like(dgn_ref)

        col = 0
        for d_ref, nh, gi, rot in ((dqa, ha, 0, False), (dka, ha, 1, False), (dva, ha, None, False),
                                   (dqb, hq, 2, True), (dkb, hkv, 3, True), (dvb, hkv, None, False)):
            dgain = jnp.zeros((1, HEAD), F32)
            for h in range(nh):
                dt = d_ref[h]
                if rot:
                    dt = _rope_bwd(dt, cosv, sinv)
                if gi is not None:
                    dt, dg = _head_norm_bwd(p_ref[:, col * HEAD:(col + 1) * HEAD], gn_ref[gi:gi + 1, :], dt)
                    dgain = dgain + dg
                dp_ref[:, col * HEAD:(col + 1) * HEAD] = dt.astype(BF16)
                col += 1
            if gi is not None:
                dgn_ref[gi:gi + 1, :] += dgain

    def hm(nh):
        return pl.BlockSpec((nh, ts, HEAD), lambda i: (0, i, 0))

    tok = pl.BlockSpec((ts, HEAD), lambda i: (i, 0))
    small = pl.BlockSpec((8, HEAD), lambda i: (0, 0))
    return pl.pallas_call(
        body, name=name, grid=(s // ts,),
        in_specs=[pl.BlockSpec((ts, pw), lambda i: (i, 0)), small, tok, tok,
                  hm(ha), hm(ha), hm(ha), hm(hq), hm(hkv), hm(hkv)],
        out_specs=[pl.BlockSpec((ts, pw), lambda i: (i, 0)), small],
        out_shape=[jax.ShapeDtypeStruct((s, pw), BF16), jax.ShapeDtypeStruct((8, HEAD), F32)],
        compiler_params=_params(),
    )(proj, gains, cos, sin, *grads)


NA_QROWS = 8
NA_KEYS = WIN_R * GRID_W
N_DR = 2 * WIN_R - 1
N_DC = 2 * WIN_C - 1


def _na_bias(rpb_flat, n_heads, name):
    def body(rpb_ref, tb_ref):
        h = pl.program_id(0)
        qi = lax.broadcasted_iota(jnp.int32, (GRID_W, LANES), 0)
        lane = lax.broadcasted_iota(jnp.int32, (GRID_W, LANES), 1)
        kk = lane & (GRID_W - 1)
        upper = lane >= GRID_W
        dcm = kk - qi + (WIN_C - 1)
        cs = jnp.clip(qi - WIN_C // 2, 0, GRID_W - WIN_C)
        valid = (kk >= cs) & (kk < cs + WIN_C)
        base = h * (N_DR * N_DC)
        for dra in range(N_DR - 1):
            def step(j, acc, dra=dra):
                va = rpb_ref[base + dra * N_DC + j]
                vb = rpb_ref[base + (dra + 1) * N_DC + j]
                return jnp.where(dcm == j, jnp.where(upper, vb, va), acc)

            pair = lax.fori_loop(0, N_DC, step, jnp.zeros((GRID_W, LANES), F32))
            pair = jnp.where(valid, pair, NEG)
            for dr0 in range(WIN_R):
                wp, odd = divmod(dra - dr0, 2)
                if odd == 0 and 0 <= wp < WIN_R // 2:
                    tb_ref[0, dr0, :, wp * LANES:(wp + 1) * LANES] = pair

    return pl.pallas_call(
        body, name=name, grid=(n_heads,),
        in_specs=[SMEM],
        out_specs=pl.BlockSpec((1, WIN_R, GRID_W, NA_KEYS), lambda h: (h, 0, 0, 0)),
        out_shape=jax.ShapeDtypeStruct((n_heads, WIN_R, GRID_W, NA_KEYS), F32),
        compiler_params=_params(),
    )(rpb_flat)


def _na_row(b, i, nrows):
    r = b * NA_QROWS + i
    rs = jnp.clip(r - WIN_R // 2, 0, nrows - WIN_R)
    return pl.ds(pl.multiple_of(rs * GRID_W, GRID_W), NA_KEYS), rs - r + (WIN_R - 1)


def _softmax(s):
    e = jnp.exp(s - jnp.max(s, axis=-1, keepdims=True))
    return e * (1.0 / jnp.sum(e, axis=-1, keepdims=True))


_NT = (((1,), (1,)), ((), ()))
_NN = (((1,), (0,)), ((), ()))
_TN = (((0,), (0,)), ((), ()))


def _dot(a, b, dn):
    return lax.dot_general(a, b, dn, preferred_element_type=F32)


def _na_fwd(q, k, v, tb, name):
    nh, s, _ = q.shape
    nrows = s // GRID_W
    tq = NA_QROWS * GRID_W

    def body(q_ref, k_ref, v_ref, tb_ref, o_ref, s_scr, p_scr):
        b = pl.program_id(1)
        rows = [slice(i * GRID_W, (i + 1) * GRID_W) for i in range(NA_QROWS)]
        at = [_na_row(b, i, nrows) for i in range(NA_QROWS)]
        for i, (keys, dr0) in enumerate(at):
            s_scr[i] = _dot(q_ref[rows[i], :], k_ref[keys, :], _NT) * SCALE + tb_ref[0, dr0]
        for i in range(NA_QROWS):
            p_scr[i] = _softmax(s_scr[i]).astype(BF16)
        for i, (keys, _) in enumerate(at):
            o_ref[rows[i], :] = _dot(p_scr[i], v_ref[keys, :], _NN)

    qspec = pl.BlockSpec((None, tq, HEAD), lambda h, b: (h, b, 0))
    full = pl.BlockSpec((None, s, HEAD), lambda h, b: (h, 0, 0))
    return pl.pallas_call(
        body, name=name, grid=(nh, nrows // NA_QROWS),
        in_specs=[qspec, full, full, pl.BlockSpec((1, WIN_R, GRID_W, NA_KEYS), lambda h, b: (h, 0, 0, 0))],
        out_specs=qspec, out_shape=jax.ShapeDtypeStruct((nh, s, HEAD), F32),
        scratch_shapes=[pltpu.VMEM((NA_QROWS, GRID_W, NA_KEYS), F32), pltpu.VMEM((NA_QROWS, GRID_W, NA_KEYS), BF16)],
        compiler_params=_params(),
    )(q, k, v, tb)


def _na_bwd(q, k, v, tb, do, name):
    nh, s, _ = q.shape
    nrows = s // GRID_W
    tq = NA_QROWS * GRID_W

    def body(q_ref, do_ref, k_ref, v_ref, tb_ref, dq_ref, dk_ref, dv_ref, dtb_ref, s_scr, dp_scr, p_scr, ds_scr):
        b = pl.program_id(1)

        @pl.when(b == 0)
        def _():
            dk_ref[...] = jnp.zeros_like(dk_ref)
            dv_ref[...] = jnp.zeros_like(dv_ref)
            dtb_ref[...] = jnp.zeros_like(dtb_ref)

        rows = [slice(i * GRID_W, (i + 1) * GRID_W) for i in range(NA_QROWS)]
        at = [_na_row(b, i, nrows) for i in range(NA_QROWS)]
        for i, (keys, dr0) in enumerate(at):
            s_scr[i] = _dot(q_ref[rows[i], :], k_ref[keys, :], _NT) * SCALE + tb_ref[0, dr0]
            dp_scr[i] = _dot(do_ref[rows[i], :], v_ref[keys, :], _NT)
        for i in range(NA_QROWS):
            p = _softmax(s_scr[i])
            dp = dp_scr[i]
            ds = p * (dp - jnp.sum(p * dp, axis=-1, keepdims=True))
            p_scr[i] = p.astype(BF16)
            s_scr[i] = ds
            ds_scr[i] = (ds * SCALE).astype(BF16)
        for i, (keys, _) in enumerate(at):
            dq_ref[rows[i], :] = _dot(ds_scr[i], k_ref[keys, :], _NN)
        for i, (keys, dr0) in enumerate(at):
            dv_ref[keys, :] += _dot(p_scr[i], do_ref[rows[i], :], _TN)
            dk_ref[keys, :] += _dot(ds_scr[i], q_ref[rows[i], :], _TN)
            dtb_ref[0, dr0] += s_scr[i]

    qspec = pl.BlockSpec((None, tq, HEAD), lambda h, b: (h, b, 0))
    full = pl.BlockSpec((None, s, HEAD), lambda h, b: (h, 0, 0))
    tbs = pl.BlockSpec((1, WIN_R, GRID_W, NA_KEYS), lambda h, b: (h, 0, 0, 0))
    hm = jax.ShapeDtypeStruct((nh, s, HEAD), F32)
    tile = (NA_QROWS, GRID_W, NA_KEYS)
    return pl.pallas_call(
        body, name=name, grid=(nh, nrows // NA_QROWS),
        in_specs=[qspec, qspec, full, full, tbs],
        out_specs=[qspec, full, full, tbs],
        out_shape=[hm, hm, hm, jax.ShapeDtypeStruct((nh, WIN_R, GRID_W, NA_KEYS), F32)],
        scratch_shapes=[pltpu.VMEM(tile, F32), pltpu.VMEM(tile, F32), pltpu.VMEM(tile, BF16), pltpu.VMEM(tile, BF16)],
        compiler_params=_params(),
    )(q, do, k, v, tb)


def _rpb_fold(y, n_heads, name):
    def body(y_ref, o_ref):
        for h in range(n_heads):
            for dr in range(2 * WIN_R):
                acc = jnp.zeros((1, LANES), F32)
                for dr0 in range(WIN_R):
                    w = dr - dr0
                    if 0 <= w < WIN_R:
                        acc = acc + y_ref[h, dr0, w:w + 1, :]
                o_ref[h, dr:dr + 1, :] = acc

    return pl.pallas_call(
        body, name=name, out_shape=jax.ShapeDtypeStruct((n_heads, 2 * WIN_R, LANES), F32),
    )(y)


def _rpb_grad(dtb, onehot, name):
    nh = dtb.shape[0]
    rows = dtb.reshape(nh, WIN_R, GRID_W, WIN_R, GRID_W).transpose(0, 1, 3, 2, 4).reshape(nh * WIN_R * WIN_R, GRID_W * GRID_W)
    y = _matmul(rows, onehot, dims="nn", ti=rows.shape[0], tj=LANES, tk=GRID_W * GRID_W, out_dtype=F32, name=name + "_dc")
    folded = _rpb_fold(y.reshape(nh, WIN_R, WIN_R, LANES), nh, name + "_dr")
    return folded[:, :N_DR, :N_DC]


WA_WIN_TOK = 3 * BAND


def _wa_scores(q, kwin, t0, j, sink_ref, head0, grp):
    rows = grp * BAND
    s = _dot(q, kwin, _NT) * SCALE
    row = lax.broadcasted_iota(jnp.int32, (rows, WA_WIN_TOK), 0)
    qpos = j * BAND + (row & (BAND - 1))
    kpos = t0 + lax.broadcasted_iota(jnp.int32, (rows, WA_WIN_TOK), 1)
    s = jnp.where(jnp.abs(kpos - qpos) <= BAND, s, NEG)
    head = lax.broadcasted_iota(jnp.int32, (rows, 1), 0) // BAND
    sink = jnp.zeros((rows, 1), F32) + sink_ref[head0]
    for g in range(1, grp):
        sink = jnp.where(head == g, sink_ref[head0 + g], sink)
    m = jnp.maximum(jnp.max(s, axis=-1, keepdims=True), sink)
    e = jnp.exp(s - m)
    es = jnp.exp(sink - m)
    rz = 1.0 / (jnp.sum(e, axis=-1, keepdims=True) + es)
    return e * rz, es * rz


def _wa_window(j, s):
    return pl.multiple_of(jnp.clip((j - 1) * BAND, 0, s - WA_WIN_TOK), BAND)


def _wa_fwd(q, k, v, sink, name):
    hq, s, _ = q.shape
    hkv = k.shape[0]
    grp = hq // hkv

    def body(sink_ref, q_ref, k_ref, v_ref, o_ref):
        kh, j = pl.program_id(0), pl.program_id(1)
        t0 = _wa_window(j, s)
        keys = pl.ds(t0, WA_WIN_TOK)
        p, _ = _wa_scores(q_ref[...].reshape(grp * BAND, HEAD), k_ref[keys, :], t0, j, sink_ref, kh * grp, grp)
        o_ref[...] = _dot(p.astype(BF16), v_ref[keys, :], _NN).reshape(grp, BAND, HEAD)

    qspec = pl.BlockSpec((grp, BAND, HEAD), lambda kh, j: (kh, j, 0))
    full = pl.BlockSpec((None, s, HEAD), lambda kh, j: (kh, 0, 0))
    return pl.pallas_call(
        body, name=name, grid=(hkv, s // BAND),
        in_specs=[SMEM, qspec, full, full],
        out_specs=qspec, out_shape=jax.ShapeDtypeStruct((hq, s, HEAD), F32),
        compiler_params=_params(),
    )(sink, q, k, v)


def _wa_bwd(q, k, v, sink, do, name):
    hq, s, _ = q.shape
    hkv = k.shape[0]
    grp = hq // hkv

    def body(sink_ref, q_ref, do_ref, k_ref, v_ref, dq_ref, dk_ref, dv_ref, dsink_ref):
        kh, j = pl.program_id(0), pl.program_id(1)

        @pl.when(j == 0)
        def _():
            dk_ref[...] = jnp.zeros_like(dk_ref)
            dv_ref[...] = jnp.zeros_like(dv_ref)
            dsink_ref[...] = jnp.zeros_like(dsink_ref)

        t0 = _wa_window(j, s)
        keys = pl.ds(t0, WA_WIN_TOK)
        qs = q_ref[...].reshape(grp * BAND, HEAD)
        dos = do_ref[...].reshape(grp * BAND, HEAD)
        kwin, vwin = k_ref[keys, :], v_ref[keys, :]
        p, ps = _wa_scores(qs, kwin, t0, j, sink_ref, kh * grp, grp)
        dp = _dot(dos, vwin, _NT)
        dv_ref[keys, :] += _dot(p.astype(BF16), dos, _TN)
        rowdot = jnp.sum(p * dp, axis=-1, keepdims=True)
        to_sink = ps * rowdot
        for g in range(grp):
            dsink_ref[g] += jnp.zeros((8, LANES), F32) - jnp.sum(to_sink[g * BAND:(g + 1) * BAND])
        dss = (p * (dp - rowdot) * SCALE).astype(BF16)
        dq_ref[...] = _dot(dss, kwin, _NN).reshape(grp, BAND, HEAD)
        dk_ref[keys, :] += _dot(dss, qs, _TN)

    qspec = pl.BlockSpec((grp, BAND, HEAD), lambda kh, j: (kh, j, 0))
    full = pl.BlockSpec((None, s, HEAD), lambda kh, j: (kh, 0, 0))
    kv = jax.ShapeDtypeStruct((hkv, s, HEAD), F32)
    return pl.pallas_call(
        body, name=name, grid=(hkv, s // BAND),
        in_specs=[SMEM, qspec, qspec, full, full],
        out_specs=[qspec, full, full, pl.BlockSpec((grp, 8, LANES), lambda kh, j: (kh, 0, 0))],
        out_shape=[jax.ShapeDtypeStruct((hq, s, HEAD), F32), kv, kv, jax.ShapeDtypeStruct((hq, 8, LANES), F32)],
        compiler_params=_params(),
    )(sink, q, do, k, v)


def _onorm_fwd(oa, ob, gains, name):
    ha, s, _ = oa.shape
    hq = ob.shape[0]
    ts = _tile(s, 256, 16)

    def body(oa_ref, ob_ref, g_ref, o_ref):
        col = 0
        for ref, nh in ((oa_ref, ha), (ob_ref, hq)):
            ss = sum(jnp.sum(ref[h] * ref[h], axis=-1, keepdims=True) for h in range(nh))
            r = lax.rsqrt(ss / (nh * HEAD) + EPS)
            for h in range(nh):
                o_ref[:, col * HEAD:(col + 1) * HEAD] = (ref[h] * r * g_ref[:, col * HEAD:(col + 1) * HEAD]).astype(BF16)
                col += 1

    mix = (ha + hq) * HEAD
    return pl.pallas_call(
        body, name=name, grid=(s // ts,),
        in_specs=[pl.BlockSpec((ha, ts, HEAD), lambda i: (0, i, 0)), pl.BlockSpec((hq, ts, HEAD), lambda i: (0, i, 0)),
                  pl.BlockSpec((1, mix), lambda i: (0, 0))],
        out_specs=pl.BlockSpec((ts, mix), lambda i: (i, 0)),
        out_shape=jax.ShapeDtypeStruct((s, mix), BF16), compiler_params=_params(),
    )(oa, ob, gains)


def _onorm_bwd(oa, ob, gains, don, name):
    ha, s, _ = oa.shape
    hq = ob.shape[0]
    ts = _tile(s, 256, 16)
    mix = (ha + hq) * HEAD

    def body(oa_ref, ob_ref, g_ref, don_ref, doa_ref, dob_ref, dg_ref):
        @pl.when(pl.program_id(0) == 0)
        def _():
            dg_ref[...] = jnp.zeros_like(dg_ref)

        col0 = 0
        for ref, d_ref, nh in ((oa_ref, doa_ref, ha), (ob_ref, dob_ref, hq)):
            ss = sum(jnp.sum(ref[h] * ref[h], axis=-1, keepdims=True) for h in range(nh))
            r = lax.rsqrt(ss / (nh * HEAD) + EPS)
            dot = jnp.zeros((ts, 1), F32)
            for h in range(nh):
                cols = slice((col0 + h) * HEAD, (col0 + h + 1) * HEAD)
                dot = dot + jnp.sum(don_ref[:, cols] * g_ref[:, cols] * ref[h], axis=-1, keepdims=True)
            mean = dot * r / (nh * HEAD)
            for h in range(nh):
                cols = slice((col0 + h) * HEAD, (col0 + h + 1) * HEAD)
                y = ref[h] * r
                dn = don_ref[:, cols]
                d_ref[h] = (r * (dn * g_ref[:, cols] - y * mean)).astype(BF16)
                dg_ref[0:1, cols] += jnp.sum(dn * y, axis=0, keepdims=True)
            col0 += nh

    return pl.pallas_call(
        body, name=name, grid=(s // ts,),
        in_specs=[pl.BlockSpec((ha, ts, HEAD), lambda i: (0, i, 0)), pl.BlockSpec((hq, ts, HEAD), lambda i: (0, i, 0)),
                  pl.BlockSpec((1, mix), lambda i: (0, 0)), pl.BlockSpec((ts, mix), lambda i: (i, 0))],
        out_specs=[pl.BlockSpec((ha, ts, HEAD), lambda i: (0, i, 0)), pl.BlockSpec((hq, ts, HEAD), lambda i: (0, i, 0)),
                   pl.BlockSpec((8, mix), lambda i: (0, 0))],
        out_shape=[jax.ShapeDtypeStruct((ha, s, HEAD), BF16), jax.ShapeDtypeStruct((hq, s, HEAD), BF16),
                   jax.ShapeDtypeStruct((8, mix), F32)],
        compiler_params=_params(),
    )(oa, ob, gains, don)


def _shift_rows(cur, halo_prev, halo_next, i, n):
    ts = cur.shape[0]
    row = lax.broadcasted_iota(jnp.int32, cur.shape, 0)
    first = jnp.where(i > 0, halo_prev[7:8, :], 0.0)
    last = jnp.where(i < n - 1, halo_next[0:1, :], 0.0)
    prev = jnp.where(row == 0, first, pltpu.roll(cur, 1, axis=0))
    nxt = jnp.where(row == ts - 1, last, pltpu.roll(cur, ts - 1, axis=0))
    return prev, nxt


def _halo_specs(ts, tc, col_off):
    per = ts // 8
    cur = pl.BlockSpec((ts, tc), lambda j, i: (i, j + col_off))
    prev = pl.BlockSpec((8, tc), lambda j, i: (jnp.maximum(i * per - 1, 0), j + col_off))

    def nxt_map(n_blocks):
        return pl.BlockSpec((8, tc), lambda j, i: (jnp.minimum((i + 1) * per, n_blocks - 1), j + col_off))

    return cur, prev, nxt_map


def _sigmoid(x):
    return 1.0 / (1.0 + jnp.exp(-x))


def _ffn_tiles(s, f):
    return _tile(s, 512, 16), _tile(f, 512, LANES)


def _gate_fwd(u, cw, cb, f, name):
    s = u.shape[0]
    ts, tc = _ffn_tiles(s, f)
    nj, ni = f // tc, s // ts

    def body(g_ref, gp_ref, gn_ref, u_ref, up_ref, un_ref, wg_ref, wu_ref, bg_ref, bu_ref, a_ref):
        i = pl.program_id(1)

        def conv(c_ref, p_ref, n_ref, w_ref, b_ref):
            cur = c_ref[...]
            prev, nxt = _shift_rows(cur, p_ref[...], n_ref[...], i, ni)
            return prev * w_ref[0:1, :] + cur * w_ref[1:2, :] + nxt * w_ref[2:3, :] + b_ref[...]

        gate = conv(g_ref, gp_ref, gn_ref, wg_ref, bg_ref)
        up = conv(u_ref, up_ref, un_ref, wu_ref, bu_ref)
        a_ref[...] = (gate * _sigmoid(gate) * up).astype(BF16)

    gc, gp, gn = _halo_specs(ts, tc, 0)
    uc, up_, un = _halo_specs(ts, tc, nj)
    wg = pl.BlockSpec((3, tc), lambda j, i: (0, j))
    wu = pl.BlockSpec((3, tc), lambda j, i: (0, j + nj))
    bg = pl.BlockSpec((1, tc), lambda j, i: (0, j))
    bu = pl.BlockSpec((1, tc), lambda j, i: (0, j + nj))
    return pl.pallas_call(
        body, name=name, grid=(nj, ni),
        in_specs=[gc, gp, gn(s // 8), uc, up_, un(s // 8), wg, wu, bg, bu],
        out_specs=pl.BlockSpec((ts, tc), lambda j, i: (i, j)),
        out_shape=jax.ShapeDtypeStruct((s, f), BF16), compiler_params=_params(),
    )(u, u, u, u, u, u, cw, cw, cb, cb)


def _gate_bwd(u, cw, cb, da, f, name):
    s = u.shape[0]
    ts, tc = _ffn_tiles(s, f)
    nj, ni = f // tc, s // ts

    def body(g_ref, gp_ref, gn_ref, u_ref, up_ref, un_ref, wg_ref, wu_ref, bg_ref, bu_ref, da_ref,
             dgu_ref, dcw_ref, dcb_ref):
        i = pl.program_id(1)

        @pl.when(i == 0)
        def _():
            dcw_ref[...] = jnp.zeros_like(dcw_ref)
            dcb_ref[...] = jnp.zeros_like(dcb_ref)

        def conv(c_ref, p_ref, n_ref, w_ref, b_ref):
            cur = c_ref[...]
            prev, nxt = _shift_rows(cur, p_ref[...], n_ref[...], i, ni)
            return prev * w_ref[0:1, :] + cur * w_ref[1:2, :] + nxt * w_ref[2:3, :] + b_ref[...], (prev, cur, nxt)

        gate, g_taps = conv(g_ref, gp_ref, gn_ref, wg_ref, bg_ref)
        up, u_taps = conv(u_ref, up_ref, un_ref, wu_ref, bu_ref)
        dav = da_ref[...]
        sg = _sigmoid(gate)
        d_up = dav * gate * sg
        d_gate = dav * up * (sg * (1.0 + gate * (1.0 - sg)))
        dgu_ref[0] = d_gate
        dgu_ref[1] = d_up
        for half, (d, taps) in enumerate(((d_gate, g_taps), (d_up, u_taps))):
            dcb_ref[half, 0:1, :] += jnp.sum(d, axis=0, keepdims=True)
            for k in range(3):
                dcw_ref[half, k, 0:1, :] += jnp.sum(d * taps[k], axis=0, keepdims=True)

    gc, gp, gn = _halo_specs(ts, tc, 0)
    uc, up_, un = _halo_specs(ts, tc, nj)
    wg = pl.BlockSpec((3, tc), lambda j, i: (0, j))
    wu = pl.BlockSpec((3, tc), lambda j, i: (0, j + nj))
    bg = pl.BlockSpec((1, tc), lambda j, i: (0, j))
    bu = pl.BlockSpec((1, tc), lambda j, i: (0, j + nj))
    tile = pl.BlockSpec((ts, tc), lambda j, i: (i, j))
    return pl.pallas_call(
        body, name=name, grid=(nj, ni),
        in_specs=[gc, gp, gn(s // 8), uc, up_, un(s // 8), wg, wu, bg, bu, tile],
        out_specs=[pl.BlockSpec((2, ts, tc), lambda j, i: (0, i, j)),
                   pl.BlockSpec((2, 3, 8, tc), lambda j, i: (0, 0, 0, j)),
                   pl.BlockSpec((2, 8, tc), lambda j, i: (0, 0, j))],
        out_shape=[jax.ShapeDtypeStruct((2, s, f), F32),
                   jax.ShapeDtypeStruct((2, 3, 8, f), F32), jax.ShapeDtypeStruct((2, 8, f), F32)],
        compiler_params=_params(),
    )(u, u, u, u, u, u, cw, cw, cb, cb, da)


def _conv_bwd(dgu, cw, name):
    _, s, f = dgu.shape
    ts, tc = _ffn_tiles(s, f)
    nj, ni = f // tc, s // ts
    per = ts // 8

    def body(c_ref, p_ref, n_ref, w_ref, o_ref):
        i = pl.program_id(2)
        cur = c_ref[...]
        prev, nxt = _shift_rows(cur, p_ref[...], n_ref[...], i, ni)
        o_ref[...] = (prev * w_ref[2:3, :] + cur * w_ref[1:2, :] + nxt * w_ref[0:1, :]).astype(BF16)

    return pl.pallas_call(
        body, name=name, grid=(2, nj, ni),
        in_specs=[pl.BlockSpec((None, ts, tc), lambda h, j, i: (h, i, j)),
                  pl.BlockSpec((None, 8, tc), lambda h, j, i: (h, jnp.maximum(i * per - 1, 0), j)),
                  pl.BlockSpec((None, 8, tc), lambda h, j, i: (h, jnp.minimum((i + 1) * per, s // 8 - 1), j)),
                  pl.BlockSpec((3, tc), lambda h, j, i: (0, h * nj + j))],
        out_specs=pl.BlockSpec((ts, tc), lambda h, j, i: (i, h * nj + j)),
        out_shape=jax.ShapeDtypeStruct((s, 2 * f), BF16), compiler_params=_params(),
    )(dgu, dgu, dgu, cw)


def _loss_head(y, target, name):
    s, d = y.shape
    ts = _tile(s, 256, 16)

    def body(y_ref, t_ref, dy_ref, dyb_ref, l_ref):
        @pl.when(pl.program_id(0) == 0)
        def _():
            l_ref[...] = jnp.zeros_like(l_ref)

        err = y_ref[...] - t_ref[...]
        dy = err / d
        dy_ref[...] = dy
        dyb_ref[...] = dy.astype(BF16)
        l_ref[...] += jnp.zeros((8, LANES), F32) + 0.5 * jnp.sum(jnp.sum(err * err, axis=-1, keepdims=True) / d)

    blk = pl.BlockSpec((ts, d), lambda i: (i, 0))
    return pl.pallas_call(
        body, name=name, grid=(s // ts,),
        in_specs=[blk, blk], out_specs=[blk, blk, pl.BlockSpec((8, LANES), lambda i: (0, 0))],
        out_shape=[jax.ShapeDtypeStruct((s, d), F32), jax.ShapeDtypeStruct((s, d), BF16), jax.ShapeDtypeStruct((8, LANES), F32)],
        compiler_params=_params(),
    )(y, target)


SMALL = ("ln1_g", "qn_a", "kn_a", "rpb", "qn_b", "kn_b", "sink", "on_a", "on_b", "ln2_g", "conv_b", "conv_w")
PACK_ALIGN = 8 * LANES


def _pack(arrays):
    flat = []
    for a in arrays:
        a = a.reshape(-1)
        flat.append(jnp.pad(a, (0, -a.size % PACK_ALIGN)))
    return jnp.concatenate(flat).reshape(-1, LANES)


def _unpack(packed, like):
    out, at = [], 0
    flat = packed.reshape(-1)
    for a in like:
        out.append(flat[at:at + a.size].reshape(a.shape))
        at += a.size + (-a.size % PACK_ALIGN)
    return out


def _matmul_tiles(s, k, j):
    return dict(ti=_tile(s, 512, 16), tj=_tile(j, 1536, LANES), tk=_tile(k, 2048, LANES))


def kernel(x, positions, ln1_g, w_in, qn_a, kn_a, rpb, qn_b, kn_b, sink, on_a, on_b, w_out, ln2_g, w_up, conv_w, conv_b, w_down, loss_target, m_ln1_g, m_w_in, m_qn_a, m_kn_a, m_rpb, m_qn_b, m_kn_b, m_sink, m_on_a, m_on_b, m_w_out, m_ln2_g, m_w_up, m_conv_w, m_conv_b, m_w_down, v_ln1_g, v_w_in, v_qn_a, v_kn_a, v_rpb, v_qn_b, v_kn_b, v_sink, v_on_a, v_on_b, v_w_out, v_ln2_g, v_w_up, v_conv_w, v_conv_b, v_w_down):
    weights = dict(ln1_g=ln1_g, w_in=w_in, qn_a=qn_a, kn_a=kn_a, rpb=rpb, qn_b=qn_b, kn_b=kn_b, sink=sink, on_a=on_a,
                   on_b=on_b, w_out=w_out, ln2_g=ln2_g, w_up=w_up, conv_w=conv_w, conv_b=conv_b, w_down=w_down)
    mom1 = dict(ln1_g=m_ln1_g, w_in=m_w_in, qn_a=m_qn_a, kn_a=m_kn_a, rpb=m_rpb, qn_b=m_qn_b, kn_b=m_kn_b, sink=m_sink,
                on_a=m_on_a, on_b=m_on_b, w_out=m_w_out, ln2_g=m_ln2_g, w_up=m_w_up, conv_w=m_conv_w, conv_b=m_conv_b,
                w_down=m_w_down)
    mom2 = dict(ln1_g=v_ln1_g, w_in=v_w_in, qn_a=v_qn_a, kn_a=v_kn_a, rpb=v_rpb, qn_b=v_qn_b, kn_b=v_kn_b, sink=v_sink,
                on_a=v_on_a, on_b=v_on_b, w_out=v_w_out, ln2_g=v_ln2_g, w_up=v_w_up, conv_w=v_conv_w, conv_b=v_conv_b,
                w_down=v_w_down)
    order = ("ln1_g", "w_in", "qn_a", "kn_a", "rpb", "qn_b", "kn_b", "sink", "on_a", "on_b", "w_out", "ln2_g", "w_up",
             "conv_w", "conv_b", "w_down")

    depth, d = ln1_g.shape
    s = x.shape[1]
    ha = on_a.shape[1] // HEAD
    hq = on_b.shape[1] // HEAD
    pw = w_in.shape[2] * N_DEV
    hkv = (pw - 3 * ha * HEAD - hq * HEAD) // (2 * HEAD)
    f = w_down.shape[1] * N_DEV
    mix = (ha + hq) * HEAD
    cfg = (ha, hq, hkv)
    fs = conv_w.shape[2]
    dev = 4 * lax.axis_index("x") + 2 * lax.axis_index("y") + lax.axis_index("c")
    core = lax.axis_index("c").astype(jnp.int32).reshape(1)

    shard = {n: weights[n].astype(BF16) for n in ("w_in", "w_out", "w_up", "w_down")}

    def unshard(n, g):
        if n in ("w_in", "w_up"):
            return g.transpose(1, 0, 2).reshape(g.shape[1], N_DEV * g.shape[2])
        return g.reshape(N_DEV * g.shape[1], g.shape[2])

    full = {n: [None] * depth for n in shard}
    for n in shard:
        full[n][0] = unshard(n, _allgather(shard[n][0], "gather0_" + n))
    cw_rows = depth * 3
    cw_pad = jnp.pad(conv_w.reshape(cw_rows, fs), ((0, -cw_rows % 8), (0, 0)))
    g_cw = _allgather(cw_pad, "gather_conv_w")
    full_cw = g_cw[:, :cw_rows].reshape(N_DEV, depth, 3, fs).transpose(1, 2, 0, 3).reshape(depth, 3, 2 * f)

    inv = ROPE_THETA ** (-jnp.arange(0, HEAD, 2, dtype=F32) / HEAD)
    ang = positions.astype(F32)[:, None] * inv[None, :]
    cos = jnp.concatenate([jnp.cos(ang), jnp.cos(ang)], axis=-1)
    sin = jnp.concatenate([-jnp.sin(ang), jnp.sin(ang)], axis=-1)
    qk = jnp.arange(GRID_W * GRID_W)
    dc_of = (qk % GRID_W) - (qk // GRID_W) + (WIN_C - 1)
    onehot = (dc_of[:, None] == jnp.arange(LANES)[None, :]).astype(BF16)

    tiles_s = _tile(s, 512, 16)

    xs = x.reshape(s, d)
    saved = []
    for l in range(depth):
        more = l + 1 < depth

        def fwd_matmul(n, a_op, name, **kw):
            if not more:
                return _matmul(a_op, full[n][l], dims="nn", out_dtype=F32, name=name + "_last", **kw)
            out, got = _matmul(a_op, full[n][l], dims="nn", out_dtype=F32, name=name, comm=_gather_comm([shard[n][l + 1]]), **kw)
            full[n][l + 1] = unshard(n, got)
            return out

        gains = jnp.zeros((8, HEAD), F32).at[0].set(qn_a[l]).at[1].set(kn_a[l]).at[2].set(qn_b[l]).at[3].set(kn_b[l])
        on_g = jnp.concatenate([on_a[l], on_b[l]]).reshape(1, mix)
        h = _rms_fwd(xs, ln1_g[l].reshape(1, d), "ln1_fwd")
        proj = fwd_matmul("w_in", h, "proj_fwd", ti=tiles_s, tj=_tile(pw, 1536, LANES), tk=d)
        qa, ka, va, qb, kb, vb = _qkv_fwd(proj, gains, cos, sin, cfg, "qkv_fwd")
        tb = _na_bias(rpb[l].reshape(-1), ha, "na_bias")
        oa = _na_fwd(qa, ka, va, tb, "na_fwd")
        ob = _wa_fwd(qb, kb, vb, sink[l], "wa_fwd")
        o_n = _onorm_fwd(oa, ob, on_g, "onorm_fwd")
        x1 = fwd_matmul("w_out", o_n, "out_fwd", ti=tiles_s, tj=_tile(d, 2048, LANES), tk=mix, resid=xs)
        h2 = _rms_fwd(x1, ln2_g[l].reshape(1, d), "ln2_fwd")
        u = fwd_matmul("w_up", h2, "up_fwd", ti=tiles_s, tj=fs, tk=d)
        cb = conv_b[l].reshape(1, 2 * f)
        a = _gate_fwd(u, full_cw[l], cb, f, "gate_fwd")
        x2 = fwd_matmul("w_down", a, "down_fwd", ti=tiles_s, tj=_tile(d, 512, LANES), tk=f, resid=x1)
        saved.append(dict(x=xs, h=h, proj=proj, gains=gains, on_g=on_g, qkv=(qa, ka, va, qb, kb, vb), tb=tb, oa=oa, ob=ob,
                          o_n=o_n, x1=x1, h2=h2, u=u, cb=cb, a=a))
        xs = x2

    dx, dx_b, loss_part = _loss_head(xs, loss_target.reshape(s, d), "loss_head")
    tile_c = _tile(s, 2048, 16)
    half_k = _tile(2 * f, f, fs)
    loss = lax.psum(loss_part[0, 0], ("x", "y", "c"))

    small_grads = [None] * depth
    big = {n: None for n in ("w_in", "w_out", "w_up", "w_down")}
    pending = None
    for l in reversed(range(depth)):
        sv = saved[l]
        qa, ka, va, qb, kb, vb = sv["qkv"]

        def grad_matmul(n, a_op, b_op, name, **kw):
            if pending is None:
                return _matmul(a_op, b_op, dims="tn", out_dtype=BF16, name=name + "_first", **kw)
            out, got = _matmul(a_op, b_op, dims="tn", out_dtype=BF16, name=name, comm=_scatter_comm([pending[n]]), **kw)
            big[n] = _adamw(got, weights[n], mom1[n], mom2[n], "adamw_" + n, layer=l + 1, into=big[n])
            return out

        da = _matmul(dx_b, full["w_down"][l], dims="nt", ti=tiles_s, tj=_tile(f, 1408, LANES), tk=d, out_dtype=F32, name="down_bwd_x")
        gw_down = grad_matmul("w_down", sv["a"], dx_b, "down_bwd_w", ti=_tile(f, 1408, LANES), tj=_tile(d, 1024, LANES),
                              tk=tile_c, j_outer=False)
        dgu, dcw, dcb = _gate_bwd(sv["u"], full_cw[l], sv["cb"], da, f, "gate_bwd")
        du = _conv_bwd(dgu, full_cw[l], "conv_bwd")
        dh2 = None
        for part in range(2 * f // half_k):
            dh2 = _matmul(du, full["w_up"][l], dims="nt", ti=tiles_s, tj=_tile(d, 1024, LANES), tk=half_k, out_dtype=F32,
                          name=f"up_bwd_x{part}", k_blocks=(part, 1), resid=dh2)
        gw_up = grad_matmul("w_up", sv["h2"], du, "up_bwd_w", ti=_tile(d, 1024, LANES), tj=fs, tk=tile_c, dev_major=True)
        dx1, dx1_b, dln2 = _rms_bwd(sv["x1"], ln2_g[l].reshape(1, d), dh2, dx, "ln2_bwd")
        don = _matmul(dx1_b, full["w_out"][l], dims="nt", ti=tiles_s, tj=mix, tk=d, out_dtype=F32, name="out_bwd_x")
        gw_out = grad_matmul("w_out", sv["o_n"], dx1_b, "out_bwd_w", ti=_tile(mix, 1024, LANES), tj=_tile(d, 1024, LANES),
                             tk=tile_c, j_outer=False)
        doa, dob, don_g = _onorm_bwd(sv["oa"], sv["ob"], sv["on_g"], don, "onorm_bwd")
        dqa, dka, dva, dtb = _na_bwd(qa, ka, va, sv["tb"], doa, "na_bwd")
        dqb, dkb, dvb, dsink = _wa_bwd(qb, kb, vb, sink[l], dob, "wa_bwd")
        drpb = _rpb_grad(dtb, onehot, "rpb_grad")
        dproj, dgains = _qkv_bwd(sv["proj"], sv["gains"], cos, sin, (dqa, dka, dva, dqb, dkb, dvb), cfg, "qkv_bwd")
        dh = _matmul(dproj, full["w_in"][l], dims="nt", ti=tiles_s, tj=_tile(d, 1024, LANES), tk=pw, out_dtype=F32, name="proj_bwd_x")
        gw_in = grad_matmul("w_in", sv["h"], dproj, "proj_bwd_w", ti=_tile(d, 1024, LANES), tj=_tile(pw, 1536, LANES), tk=tile_c)
        dx, dx_b, dln1 = _rms_bwd(sv["x"], ln1_g[l].reshape(1, d), dh, dx1, "ln1_bwd")

        small_grads[l] = dict(
            ln1_g=dln1[0], qn_a=dgains[0], kn_a=dgains[1], rpb=drpb, qn_b=dgains[2], kn_b=dgains[3], sink=dsink[:, 0, 0],
            on_a=don_g[0, :ha * HEAD], on_b=don_g[0, ha * HEAD:], ln2_g=dln2[0],
            conv_b=dcb[:, 0, :].reshape(2 * f), conv_w=dcw[:, :, 0, :].transpose(1, 0, 2).reshape(3, 2 * f))

        pending = dict(
            w_in=gw_in.reshape(d, N_DEV, pw // N_DEV).transpose(1, 0, 2),
            w_out=gw_out.reshape(N_DEV, mix // N_DEV, d),
            w_up=gw_up,
            w_down=gw_down.reshape(N_DEV, f // N_DEV, d))

    for n in big:
        summed = _reduce_scatter(pending[n], core, "rs0_" + n)
        big[n] = _adamw(summed, weights[n], mom1[n], mom2[n], "adamw0_" + n, layer=0, into=big[n])

    grads_l = [small_grads[l][n] for l in range(depth) for n in SMALL]
    gathered = _allgather(_pack(grads_l), "gather_small")
    zeros_cw = jnp.zeros((3, 2 * f), F32)

    def small_state(src):
        return _pack([zeros_cw if n == "conv_w" else src[n][l] for l in range(depth) for n in SMALL])

    sm = _adamw(gathered, small_state(weights), small_state(mom1), small_state(mom2), "adamw_small")
    sm = [_unpack(t, grads_l) for t in sm]
    small_out = {n: [jnp.stack([sm[k][l * len(SMALL) + i] for l in range(depth)]) for k in range(4)]
                 for i, n in enumerate(SMALL)}
    cw_grad = lax.dynamic_slice_in_dim(small_out["conv_w"][0], dev * fs, fs, axis=2)
    cw_rows_pad = cw_rows + (-cw_rows % 8)

    def rows8(a):
        return jnp.pad(a.reshape(cw_rows, fs), ((0, cw_rows_pad - cw_rows), (0, 0)))

    cw_res = _adamw(rows8(cw_grad)[None], rows8(conv_w), rows8(m_conv_w), rows8(v_conv_w), "adamw_conv_w")
    small_out["conv_w"] = [t[:cw_rows].reshape(depth, 3, fs) for t in cw_res]

    results = {n: (big[n] if n in big else small_out[n]) for n in order}
    grad_x = dx.reshape(1, s, d)
    return (loss, grad_x, *[results[n][0] for n in order], *[results[n][1] for n in order],
            *[results[n][2] for n in order], *[results[n][3] for n in order])
```

```python
import functools
import math

import jax
import jax.numpy as jnp
from jax import lax
from jax.experimental import pallas as pl
from jax.experimental.pallas import tpu as pltpu

F32 = jnp.float32
BF16 = jnp.bfloat16

HEAD = 128
GRID_W = 64
WIN_R = 8
WIN_C = 16
BAND = 128
ROPE_THETA = 10000.0
EPS = 1e-6
NEG = -1e30
SCALE = 1.0 / math.sqrt(HEAD)

ADAM_LR = 0.001
ADAM_B1 = 0.9
ADAM_B2 = 0.999
ADAM_EPS = 1e-08
ADAM_WD = 0.01
ADAM_STEP = 10

N_DEV = 8
LANES = 128
VMEM_LIMIT_BYTES = 56 * 2 ** 20
MESH = pl.DeviceIdType.MESH
ANY = pl.BlockSpec(memory_space=pl.ANY)
SMEM = pl.BlockSpec(memory_space=pltpu.SMEM)


def _params():
    return pltpu.CompilerParams(vmem_limit_bytes=VMEM_LIMIT_BYTES)


def _tile(n, pref, align):
    t = min(n, pref)
    t -= t % align
    while t > 0 and n % t:
        t -= align
    return t if t > 0 else n


def _place():
    x, y, c = lax.axis_index("x"), lax.axis_index("y"), lax.axis_index("c")
    chips = [(1 - x, y), (x, 1 - y), (1 - x, 1 - y)]
    return x, y, c, chips


COPIES_PER_ARRAY = N_DEV - 1


def _comm_scratch(n_arrays):
    return [pltpu.SemaphoreType.DMA((COPIES_PER_ARRAY * n_arrays,)), pltpu.SemaphoreType.DMA((COPIES_PER_ARRAY * n_arrays,)),
            pltpu.SemaphoreType.DMA((n_arrays,))]


def _gather_plan(src_refs, out_refs, send_sems, recv_sems, local_sems):
    x, y, c, chips = _place()
    me, sibling = (x, y, c), (x, y, 1 - c)

    def slot(a, px, py, pc):
        return out_refs[a].at[4 * px + 2 * py + pc]

    def copy(a, k, block, to, src=None):
        return pltpu.make_async_remote_copy(
            src_ref=slot(a, *block) if src is None else src, dst_ref=slot(a, *block),
            send_sem=send_sems.at[COPIES_PER_ARRAY * a + k], recv_sem=recv_sems.at[COPIES_PER_ARRAY * a + k],
            device_id=to, device_id_type=MESH)

    def mine(a):
        return pltpu.make_async_copy(src_refs[a], slot(a, *me), local_sems.at[a])

    def first(a):
        return [copy(a, 0, me, sibling, src=src_refs[a])] + [
            copy(a, 1 + j, me, (*chip, c), src=src_refs[a]) for j, chip in enumerate(chips)]

    def passed(a):
        return [copy(a, 4 + j, (*chip, c), sibling) for j, chip in enumerate(chips)]

    def start():
        for a in range(len(src_refs)):
            mine(a).start()
            for cp in first(a):
                cp.start()

    def finish():
        forwards = [passed(a) for a in range(len(src_refs))]
        for a in range(len(src_refs)):
            for j, chip in enumerate(chips):
                copy(a, 1 + j, (*chip, c), me).wait_recv()
                forwards[a][j].start()
        for a in range(len(src_refs)):
            copy(a, 0, sibling, me).wait_recv()
            for j, chip in enumerate(chips):
                copy(a, 4 + j, (*chip, 1 - c), me).wait_recv()
            for cp in first(a) + forwards[a]:
                cp.wait_send()
            mine(a).wait()

    return start, finish


def _scatter_plan(src_refs, out_refs, send_sems, recv_sems, local_sems):
    x, y, c, _ = _place()
    me = 4 * x + 2 * y + c

    def peer(k):
        px = 1 - x if k & 4 else x
        py = 1 - y if k & 2 else y
        pc = 1 - c if k & 1 else c
        return (px, py, pc), 4 * px + 2 * py + pc

    def copy(a, k, outgoing):
        to, idx = peer(k)
        return pltpu.make_async_remote_copy(
            src_ref=src_refs[a].at[idx], dst_ref=out_refs[a].at[me if outgoing else idx],
            send_sem=send_sems.at[COPIES_PER_ARRAY * a + k - 1], recv_sem=recv_sems.at[COPIES_PER_ARRAY * a + k - 1],
            device_id=to, device_id_type=MESH)

    def mine(a):
        return pltpu.make_async_copy(src_refs[a].at[me], out_refs[a].at[me], local_sems.at[a])

    def start():
        for a in range(len(src_refs)):
            mine(a).start()
            for k in range(1, N_DEV):
                copy(a, k, True).start()

    def finish():
        for a in range(len(src_refs)):
            for k in range(1, N_DEV):
                copy(a, k, False).wait_recv()
            for k in range(1, N_DEV):
                copy(a, k, True).wait_send()
            mine(a).wait()

    return start, finish


def _allgather(v, name):
    def body(v_ref, out_ref, send_sems, recv_sems, local_sems):
        start, finish = _gather_plan([v_ref], [out_ref], send_sems, recv_sems, local_sems)
        start()
        finish()

    return pl.pallas_call(
        body, name=name,
        out_shape=jax.ShapeDtypeStruct((N_DEV,) + v.shape, v.dtype),
        in_specs=[ANY], out_specs=ANY, scratch_shapes=_comm_scratch(1),
    )(v)


def _sibling_exchange(g, name):
    def body(g_ref, out_ref, send_sems, recv_sems):
        x, y, c, _ = _place()
        sibling = (x, y, 1 - c)
        copies = []
        for j in range(4):
            copies.append(pltpu.make_async_remote_copy(
                src_ref=g_ref.at[2 * j + (1 - c)], dst_ref=out_ref.at[j],
                send_sem=send_sems.at[j], recv_sem=recv_sems.at[j], device_id=sibling, device_id_type=MESH))
        for cp in copies:
            cp.start()
        for cp in copies:
            cp.wait_recv()
        for cp in copies:
            cp.wait_send()

    return pl.pallas_call(
        body, name=name,
        out_shape=jax.ShapeDtypeStruct((4,) + g.shape[1:], g.dtype),
        in_specs=[ANY], out_specs=ANY,
        scratch_shapes=[pltpu.SemaphoreType.DMA((4,)), pltpu.SemaphoreType.DMA((4,))],
    )(g)


def _pair_sum(g, got, core, name):
    _, r, c = g.shape
    tr = _tile(r, max(16, (1 << 20) // c), 16)

    def body(core_ref, g_ref, got_ref, o_ref):
        del core_ref
        o_ref[...] = (g_ref[...].astype(F32) + got_ref[...].astype(F32)).astype(o_ref.dtype)

    return pl.pallas_call(
        body, name=name,
        out_shape=jax.ShapeDtypeStruct((4, r, c), g.dtype),
        grid_spec=pltpu.PrefetchScalarGridSpec(
            num_scalar_prefetch=1, grid=(4, r // tr),
            in_specs=[pl.BlockSpec((None, tr, c), lambda j, i, core_ref: (2 * j + core_ref[0], i, 0)),
                      pl.BlockSpec((None, tr, c), lambda j, i, core_ref: (j, i, 0))],
            out_specs=pl.BlockSpec((None, tr, c), lambda j, i, core_ref: (j, i, 0))),
        compiler_params=_params(),
    )(core, g, got)


def _chip_exchange(p, name):
    def body(p_ref, out_ref, send_sems, recv_sems, local_sem):
        x, y, c, chips = _place()
        mine = pltpu.make_async_copy(p_ref.at[2 * x + y], out_ref.at[3], local_sem)
        mine.start()
        copies = []
        for k, (px, py) in enumerate(chips):
            copies.append(pltpu.make_async_remote_copy(
                src_ref=p_ref.at[2 * px + py], dst_ref=out_ref.at[k],
                send_sem=send_sems.at[k], recv_sem=recv_sems.at[k], device_id=(px, py, c), device_id_type=MESH))
        for cp in copies:
            cp.start()
        for cp in copies:
            cp.wait_recv()
        for cp in copies:
            cp.wait_send()
        mine.wait()

    return pl.pallas_call(
        body, name=name,
        out_shape=jax.ShapeDtypeStruct(p.shape, p.dtype),
        in_specs=[ANY], out_specs=ANY,
        scratch_shapes=[pltpu.SemaphoreType.DMA((3,)), pltpu.SemaphoreType.DMA((3,)), pltpu.SemaphoreType.DMA],
    )(p)


def _reduce_scatter(g, core, name):
    got = _sibling_exchange(g, name + "_d2d")
    p = _pair_sum(g, got, core, name + "_pair")
    return _chip_exchange(p, name + "_ici")


def _adamw(parts, w, m, v, name, layer=None, into=None):
    n_parts, r, c = parts.shape
    tr = _tile(r, max(8, (1 << 19) // c), 16 if parts.dtype == BF16 else 8)
    c1 = 1.0 - ADAM_B1 ** ADAM_STEP
    c2 = 1.0 - ADAM_B2 ** ADAM_STEP
    n_into = 0 if into is None else len(into)

    def body(p_ref, w_ref, m_ref, v_ref, *rest):
        g_out, d_out, m_out, v_out = rest[n_into:]
        g = p_ref[0].astype(F32)
        for k in range(1, n_parts):
            g = g + p_ref[k].astype(F32)
        m2 = ADAM_B1 * m_ref[...] + (1.0 - ADAM_B1) * g
        v2 = ADAM_B2 * v_ref[...] + (1.0 - ADAM_B2) * (g * g)
        g_out[...] = g
        m_out[...] = m2
        v_out[...] = v2
        d_out[...] = -ADAM_LR * ((m2 / c1) / (jnp.sqrt(v2 / c2) + ADAM_EPS) + ADAM_WD * w_ref[...])

    if layer is None:
        blk = pl.BlockSpec((tr, c), lambda i: (i, 0))
        out = jax.ShapeDtypeStruct((r, c), F32)
    else:
        blk = pl.BlockSpec((None, tr, c), lambda i: (layer, i, 0))
        out = jax.ShapeDtypeStruct(w.shape, F32)
    return pl.pallas_call(
        body, name=name, grid=(r // tr,),
        in_specs=[pl.BlockSpec((n_parts, tr, c), lambda i: (0, i, 0)), blk, blk, blk] + [ANY] * n_into,
        out_specs=[blk, blk, blk, blk], out_shape=[out, out, out, out],
        input_output_aliases={4 + k: k for k in range(n_into)},
        compiler_params=_params(),
    )(parts, w, m, v, *(into or ()))


def _host_exchange(body, comm, grid, n_in, n_out):
    plan, comm_in, comm_out = comm
    n = len(comm_in)

    def wrapped(*refs):
        ins, cin = refs[:n_in], refs[n_in:n_in + n]
        outs, cout = refs[n_in + n:n_in + n + n_out], refs[n_in + n + n_out:n_in + 2 * n + n_out]
        rest = refs[n_in + 2 * n + n_out:]
        start, finish = plan(cin, cout, *rest[len(rest) - 3:])
        steps = [pl.program_id(axis) for axis in range(len(grid))]
        first, last = steps[0] == 0, steps[0] == grid[0] - 1
        for axis in range(1, len(grid)):
            first, last = first & (steps[axis] == 0), last & (steps[axis] == grid[axis] - 1)

        @pl.when(first)
        def _():
            start()

        body(*ins, *outs, *rest[:len(rest) - 3])

        @pl.when(last)
        def _():
            finish()

    return wrapped, list(comm_in), [ANY] * n, list(comm_out), _comm_scratch(n)


def _gather_comm(shards):
    return _gather_plan, shards, [jax.ShapeDtypeStruct((N_DEV,) + v.shape, v.dtype) for v in shards]


def _scatter_comm(blocks):
    return _scatter_plan, blocks, [jax.ShapeDtypeStruct(g.shape, g.dtype) for g in blocks]


def _matmul(a, b, *, dims, ti, tj, tk, out_dtype, name, j_outer=True, resid=None, dev_major=False, comm=None,
            k_blocks=None, halved=None):
    a_shape = (a.shape[1], 2 * a.shape[2]) if halved == "a" else a.shape
    b_shape = (b.shape[1], 2 * b.shape[2]) if halved == "b" else b.shape
    if dims == "nn":
        (I, K), (K2, J) = a_shape, b_shape
    elif dims == "nt":
        (I, K), (J, K2) = a_shape, b_shape
    else:
        (K, I), (K2, J) = a_shape, b_shape
    assert K == K2 and I % ti == 0 and J % tj == 0 and K % tk == 0, (name, a.shape, b.shape, ti, tj, tk)
    assert halved is None or (halved, dims) in (("a", "nt"), ("b", "tn")), (name, halved, dims)
    k0, nk = k_blocks if k_blocks is not None else (0, K // tk)
    ni, nj = I // ti, J // tj

    def ij(g0, g1):
        return (g1, g0) if j_outer else (g0, g1)

    if dims == "nn":
        a_spec = pl.BlockSpec((ti, tk), lambda g0, g1, k: (ij(g0, g1)[0], k0 + k))
        b_spec = pl.BlockSpec((tk, tj), lambda g0, g1, k: (k0 + k, ij(g0, g1)[1]))
        dn = (((1,), (0,)), ((), ()))
    elif dims == "nt":
        a_spec = pl.BlockSpec((ti, tk), lambda g0, g1, k: (ij(g0, g1)[0], k0 + k))
        if halved == "a":
            per = K // 2 // tk
            a_spec = pl.BlockSpec((None, ti, tk), lambda g0, g1, k: ((k0 + k) // per, ij(g0, g1)[0], (k0 + k) % per))
        b_spec = pl.BlockSpec((tj, tk), lambda g0, g1, k: (ij(g0, g1)[1], k0 + k))
        dn = (((1,), (1,)), ((), ()))
    else:
        a_spec = pl.BlockSpec((tk, ti), lambda g0, g1, k: (k0 + k, ij(g0, g1)[0]))
        b_spec = pl.BlockSpec((tk, tj), lambda g0, g1, k: (k0 + k, ij(g0, g1)[1]))
        if halved == "b":
            per = J // 2 // tj
            b_spec = pl.BlockSpec((None, tk, tj), lambda g0, g1, k: (ij(g0, g1)[1] // per, k0 + k, ij(g0, g1)[1] % per))
        dn = (((0,), (0,)), ((), ()))
    in_specs = [a_spec, b_spec]
    operands = [a, b]
    if resid is not None:
        in_specs.append(pl.BlockSpec((ti, tj), lambda g0, g1, k: ij(g0, g1)))
        operands.append(resid)
    if dev_major:
        out_spec = pl.BlockSpec((None, ti, tj), lambda g0, g1, k: (ij(g0, g1)[1], ij(g0, g1)[0], 0))
        out_shape = jax.ShapeDtypeStruct((nj, I, tj), out_dtype)
    else:
        out_spec = pl.BlockSpec((ti, tj), lambda g0, g1, k: ij(g0, g1))
        out_shape = jax.ShapeDtypeStruct((I, J), out_dtype)

    grid = (nj, ni, nk) if j_outer else (ni, nj, nk)
    n_in = len(operands)
    n_comm = 0
    out_specs, out_shapes = [out_spec], [out_shape]
    scratch = [pltpu.VMEM((ti, tj), F32)] if nk > 1 else []
    if comm is not None:
        plan, comm_in, comm_out = comm
        n_comm = len(comm_in)
        operands += list(comm_in)
        in_specs += [ANY] * n_comm
        out_specs += [ANY] * n_comm
        out_shapes += list(comm_out)
        scratch += _comm_scratch(n_comm)

    def body(*refs):
        a_ref, b_ref = refs[0], refs[1]
        r_ref = refs[2] if resid is not None else None
        o_ref = refs[n_in + n_comm]
        if comm is not None:
            start, finish_comm = plan(refs[n_in:n_in + n_comm], refs[n_in + n_comm + 1:n_in + 2 * n_comm + 1], *refs[-3:])
            steps = [pl.program_id(axis) for axis in range(3)]

            @pl.when((steps[0] == 0) & (steps[1] == 0) & (steps[2] == 0))
            def _():
                start()

        part = lax.dot_general(a_ref[...].astype(BF16), b_ref[...].astype(BF16), dn, preferred_element_type=F32)

        def finish(acc):
            if r_ref is not None:
                acc = acc + r_ref[...]
            o_ref[...] = acc.astype(o_ref.dtype)

        if nk == 1:
            finish(part)
        else:
            acc_ref = refs[n_in + 2 * n_comm + 1]
            k = pl.program_id(2)

            @pl.when(k == 0)
            def _():
                acc_ref[...] = part

            @pl.when(k > 0)
            def _():
                acc_ref[...] += part

            @pl.when(k == nk - 1)
            def _():
                finish(acc_ref[...])

        if comm is not None:
            @pl.when((steps[0] == grid[0] - 1) & (steps[1] == grid[1] - 1) & (steps[2] == grid[2] - 1))
            def _():
                finish_comm()

    res = pl.pallas_call(
        body, name=name, grid=grid,
        in_specs=in_specs, out_specs=out_specs, out_shape=out_shapes,
        scratch_shapes=scratch, compiler_params=_params(),
    )(*operands)
    return res[0] if comm is None else res


def _rms_fwd(x, g, name):
    s, d = x.shape
    ts = _tile(s, 256, 16)

    def body(x_ref, g_ref, h_ref):
        xv = x_ref[...]
        r = lax.rsqrt(jnp.mean(xv * xv, axis=-1, keepdims=True) + EPS)
        h_ref[...] = (xv * r * g_ref[...]).astype(BF16)

    return pl.pallas_call(
        body, name=name, grid=(s // ts,),
        in_specs=[pl.BlockSpec((ts, d), lambda i: (i, 0)), pl.BlockSpec((1, d), lambda i: (0, 0))],
        out_specs=pl.BlockSpec((ts, d), lambda i: (i, 0)),
        out_shape=jax.ShapeDtypeStruct((s, d), BF16), compiler_params=_params(),
    )(x, g)


def _rms_bwd(x, g, dh, dres, name):
    s, d = x.shape
    ts = _tile(s, 256, 16)

    def body(x_ref, g_ref, dh_ref, dres_ref, dx_ref, dxb_ref, dg_ref):
        xv = x_ref[...]
        r = lax.rsqrt(jnp.mean(xv * xv, axis=-1, keepdims=True) + EPS)
        y = xv * r
        dhv = dh_ref[...]
        gd = dhv * g_ref[...]
        dxv = dres_ref[...] + r * (gd - y * jnp.mean(gd * y, axis=-1, keepdims=True))
        dx_ref[...] = dxv
        dxb_ref[...] = dxv.astype(BF16)

        @pl.when(pl.program_id(0) == 0)
        def _():
            dg_ref[...] = jnp.zeros_like(dg_ref)

        dg_ref[0:1, :] += jnp.sum(dhv * y, axis=0, keepdims=True)

    blk = pl.BlockSpec((ts, d), lambda i: (i, 0))
    return pl.pallas_call(
        body, name=name, grid=(s // ts,),
        in_specs=[blk, pl.BlockSpec((1, d), lambda i: (0, 0)), blk, blk],
        out_specs=[blk, blk, pl.BlockSpec((8, d), lambda i: (0, 0))],
        out_shape=[jax.ShapeDtypeStruct((s, d), F32), jax.ShapeDtypeStruct((s, d), BF16), jax.ShapeDtypeStruct((8, d), F32)],
        compiler_params=_params(),
    )(x, g, dh, dres)


def _head_norm(t, gain):
    r = lax.rsqrt(jnp.mean(t * t, axis=-1, keepdims=True) + EPS)
    return t * r * gain


def _head_norm_bwd(t, gain, dn):
    r = lax.rsqrt(jnp.mean(t * t, axis=-1, keepdims=True) + EPS)
    y = t * r
    gd = dn * gain
    dt = r * (gd - y * jnp.mean(gd * y, axis=-1, keepdims=True))
    return dt, jnp.sum(dn * y, axis=0, keepdims=True)


def _rope(n, cos, sin):
    return n * cos + pltpu.roll(n, HEAD // 2, axis=1) * sin


def _rope_bwd(do, cos, sin):
    return do * cos + pltpu.roll(do * sin, HEAD // 2, axis=1)


def _qkv_fwd(proj, gains, cos, sin, cfg, name):
    s, pw = proj.shape
    ha, hq, hkv = cfg
    ts = _tile(s, 256, 16)

    def body(p_ref, gn_ref, cos_ref, sin_ref, qa_ref, ka_ref, va_ref, qb_ref, kb_ref, vb_ref):
        cosv, sinv = cos_ref[...], sin_ref[...]
        col = 0
        for out_ref, nh, gi, rot in ((qa_ref, ha, 0, False), (ka_ref, ha, 1, False), (va_ref, ha, None, False),
                                     (qb_ref, hq, 2, True), (kb_ref, hkv, 3, True), (vb_ref, hkv, None, False)):
            for h in range(nh):
                t = p_ref[:, col * HEAD:(col + 1) * HEAD]
                if gi is not None:
                    t = _head_norm(t, gn_ref[gi:gi + 1, :])
                if rot:
                    t = _rope(t, cosv, sinv)
                out_ref[h] = t.astype(BF16)
                col += 1

    def hm(nh):
        return pl.BlockSpec((nh, ts, HEAD), lambda i: (0, i, 0)), jax.ShapeDtypeStruct((nh, s, HEAD), BF16)

    specs, shapes = zip(hm(ha), hm(ha), hm(ha), hm(hq), hm(hkv), hm(hkv))
    tok = pl.BlockSpec((ts, HEAD), lambda i: (i, 0))
    return pl.pallas_call(
        body, name=name, grid=(s // ts,),
        in_specs=[pl.BlockSpec((ts, pw), lambda i: (i, 0)), pl.BlockSpec((8, HEAD), lambda i: (0, 0)), tok, tok],
        out_specs=list(specs), out_shape=list(shapes), compiler_params=_params(),
    )(proj, gains, cos, sin)


def _qkv_bwd(proj, gains, cos, sin, grads, cfg, name):
    s, pw = proj.shape
    ha, hq, hkv = cfg
    ts = _tile(s, 256, 16)

    def body(p_ref, gn_ref, cos_ref, sin_ref, dqa, dka, dva, dqb, dkb, dvb, dp_ref, dgn_ref):
        cosv, sinv = cos_ref[...], sin_ref[...]

        @pl.when(pl.program_id(0) == 0)
        def _():
            dgn_ref[...] = jnp.zeros_like(dgn_ref)

        col = 0
        for d_ref, nh, gi, rot in ((dqa, ha, 0, False), (dka, ha, 1, False), (dva, ha, None, False),
                                   (dqb, hq, 2, True), (dkb, hkv, 3, True), (dvb, hkv, None, False)):
            dgain = jnp.zeros((1, HEAD), F32)
            for h in range(nh):
                dt = d_ref[h]
                if rot:
                    dt = _rope_bwd(dt, cosv, sinv)
                if gi is not None:
                    dt, dg = _head_norm_bwd(p_ref[:, col * HEAD:(col + 1) * HEAD], gn_ref[gi:gi + 1, :], dt)
                    dgain = dgain + dg
                dp_ref[:, col * HEAD:(col + 1) * HEAD] = dt.astype(BF16)
                col += 1
            if gi is not None:
                dgn_ref[gi:gi + 1, :] += dgain

    def hm(nh):
        return pl.BlockSpec((nh, ts, HEAD), lambda i: (0, i, 0))

    tok = pl.BlockSpec((ts, HEAD), lambda i: (i, 0))
    small = pl.BlockSpec((8, HEAD), lambda i: (0, 0))
    return pl.pallas_call(
        body, name=name, grid=(s // ts,),
        in_specs=[pl.BlockSpec((ts, pw), lambda i: (i, 0)), small, tok, tok,
                  hm(ha), hm(ha), hm(ha), hm(hq), hm(hkv), hm(hkv)],
        out_specs=[pl.BlockSpec((ts, pw), lambda i: (i, 0)), small],
        out_shape=[jax.ShapeDtypeStruct((s, pw), BF16), jax.ShapeDtypeStruct((8, HEAD), F32)],
        compiler_params=_params(),
    )(proj, gains, cos, sin, *grads)


NA_QROWS = 8
NA_KEYS = WIN_R * GRID_W
N_DR = 2 * WIN_R - 1
N_DC = 2 * WIN_C - 1


def _na_bias(rpb_flat, n_heads, name):
    def body(rpb_ref, tb_ref):
        h = pl.program_id(0)
        qi = lax.broadcasted_iota(jnp.int32, (GRID_W, LANES), 0)
        lane = lax.broadcasted_iota(jnp.int32, (GRID_W, LANES), 1)
        kk = lane & (GRID_W - 1)
        upper = lane >= GRID_W
        dcm = kk - qi + (WIN_C - 1)
        cs = jnp.clip(qi - WIN_C // 2, 0, GRID_W - WIN_C)
        valid = (kk >= cs) & (kk < cs + WIN_C)
        base = h * (N_DR * N_DC)
        for dra in range(N_DR - 1):
            def step(j, acc, dra=dra):
                va = rpb_ref[base + dra * N_DC + j]
                vb = rpb_ref[base + (dra + 1) * N_DC + j]
                return jnp.where(dcm == j, jnp.where(upper, vb, va), acc)

            pair = lax.fori_loop(0, N_DC, step, jnp.zeros((GRID_W, LANES), F32))
            pair = jnp.where(valid, pair, NEG)
            for dr0 in range(WIN_R):
                wp, odd = divmod(dra - dr0, 2)
                if odd == 0 and 0 <= wp < WIN_R // 2:
                    tb_ref[0, dr0, :, wp * LANES:(wp + 1) * LANES] = pair

    return pl.pallas_call(
        body, name=name, grid=(n_heads,),
        in_specs=[SMEM],
        out_specs=pl.BlockSpec((1, WIN_R, GRID_W, NA_KEYS), lambda h: (h, 0, 0, 0)),
        out_shape=jax.ShapeDtypeStruct((n_heads, WIN_R, GRID_W, NA_KEYS), F32),
        compiler_params=_params(),
    )(rpb_flat)


def _na_row(b, i, nrows):
    r = b * NA_QROWS + i
    rs = jnp.clip(r - WIN_R // 2, 0, nrows - WIN_R)
    return pl.ds(pl.multiple_of(rs * GRID_W, GRID_W), NA_KEYS), rs - r + (WIN_R - 1)


def _softmax(s):
    e = jnp.exp(s - jnp.max(s, axis=-1, keepdims=True))
    return e * (1.0 / jnp.sum(e, axis=-1, keepdims=True))


_NT = (((1,), (1,)), ((), ()))
_NN = (((1,), (0,)), ((), ()))
_TN = (((0,), (0,)), ((), ()))


def _dot(a, b, dn):
    return lax.dot_general(a, b, dn, preferred_element_type=F32)


def _call(body, comm, *, name, grid, operands, in_specs, out_specs, out_shape, scratch_shapes):
    if comm is not None:
        body, more_operands, more_specs, more_shapes, sems = _host_exchange(body, comm, grid, len(operands), len(out_shape))
        operands = operands + more_operands
        in_specs = in_specs + more_specs
        out_specs = out_specs + more_specs
        out_shape = out_shape + more_shapes
        scratch_shapes = scratch_shapes + sems
    res = pl.pallas_call(body, name=name, grid=grid, in_specs=in_specs, out_specs=out_specs, out_shape=out_shape,
                         scratch_shapes=scratch_shapes, compiler_params=_params())(*operands)
    return res[0] if len(res) == 1 else res


def _na_fwd(q, k, v, tb, name, comm=None):
    nh, s, _ = q.shape
    nrows = s // GRID_W
    tq = NA_QROWS * GRID_W

    def body(q_ref, k_ref, v_ref, tb_ref, o_ref, s_scr, p_scr):
        b = pl.program_id(1)
        rows = [slice(i * GRID_W, (i + 1) * GRID_W) for i in range(NA_QROWS)]
        at = [_na_row(b, i, nrows) for i in range(NA_QROWS)]
        for i, (keys, dr0) in enumerate(at):
            s_scr[i] = _dot(q_ref[rows[i], :], k_ref[keys, :], _NT) * SCALE + tb_ref[0, dr0]
        for i in range(NA_QROWS):
            p_scr[i] = _softmax(s_scr[i]).astype(BF16)
        for i, (keys, _) in enumerate(at):
            o_ref[rows[i], :] = _dot(p_scr[i], v_ref[keys, :], _NN)

    qspec = pl.BlockSpec((None, tq, HEAD), lambda h, b: (h, b, 0))
    full = pl.BlockSpec((None, s, HEAD), lambda h, b: (h, 0, 0))
    return _call(
        body, comm, name=name, grid=(nh, nrows // NA_QROWS), operands=[q, k, v, tb],
        in_specs=[qspec, full, full, pl.BlockSpec((1, WIN_R, GRID_W, NA_KEYS), lambda h, b: (h, 0, 0, 0))],
        out_specs=[qspec], out_shape=[jax.ShapeDtypeStruct((nh, s, HEAD), F32)],
        scratch_shapes=[pltpu.VMEM((NA_QROWS, GRID_W, NA_KEYS), F32), pltpu.VMEM((NA_QROWS, GRID_W, NA_KEYS), BF16)])


def _na_bwd(q, k, v, tb, do, name, comm=None):
    nh, s, _ = q.shape
    nrows = s // GRID_W
    tq = NA_QROWS * GRID_W

    def body(q_ref, do_ref, k_ref, v_ref, tb_ref, dq_ref, dk_ref, dv_ref, dtb_ref, s_scr, dp_scr, p_scr, ds_scr):
        b = pl.program_id(1)

        @pl.when(b == 0)
        def _():
            dk_ref[...] = jnp.zeros_like(dk_ref)
            dv_ref[...] = jnp.zeros_like(dv_ref)
            dtb_ref[...] = jnp.zeros_like(dtb_ref)

        rows = [slice(i * GRID_W, (i + 1) * GRID_W) for i in range(NA_QROWS)]
        at = [_na_row(b, i, nrows) for i in range(NA_QROWS)]
        for i, (keys, dr0) in enumerate(at):
            s_scr[i] = _dot(q_ref[rows[i], :], k_ref[keys, :], _NT) * SCALE + tb_ref[0, dr0]
            dp_scr[i] = _dot(do_ref[rows[i], :], v_ref[keys, :], _NT)
        for i in range(NA_QROWS):
            p = _softmax(s_scr[i])
            dp = dp_scr[i]
            ds = p * (dp - jnp.sum(p * dp, axis=-1, keepdims=True))
            p_scr[i] = p.astype(BF16)
            s_scr[i] = ds
            ds_scr[i] = (ds * SCALE).astype(BF16)
        for i, (keys, _) in enumerate(at):
            dq_ref[rows[i], :] = _dot(ds_scr[i], k_ref[keys, :], _NN)
        for i, (keys, dr0) in enumerate(at):
            dv_ref[keys, :] += _dot(p_scr[i], do_ref[rows[i], :], _TN)
            dk_ref[keys, :] += _dot(ds_scr[i], q_ref[rows[i], :], _TN)
            dtb_ref[0, dr0] += s_scr[i]

    qspec = pl.BlockSpec((None, tq, HEAD), lambda h, b: (h, b, 0))
    full = pl.BlockSpec((None, s, HEAD), lambda h, b: (h, 0, 0))
    tbs = pl.BlockSpec((1, WIN_R, GRID_W, NA_KEYS), lambda h, b: (h, 0, 0, 0))
    hm = jax.ShapeDtypeStruct((nh, s, HEAD), F32)
    tile = (NA_QROWS, GRID_W, NA_KEYS)
    return _call(
        body, comm, name=name, grid=(nh, nrows // NA_QROWS), operands=[q, do, k, v, tb],
        in_specs=[qspec, qspec, full, full, tbs],
        out_specs=[qspec, full, full, tbs],
        out_shape=[hm, hm, hm, jax.ShapeDtypeStruct((nh, WIN_R, GRID_W, NA_KEYS), F32)],
        scratch_shapes=[pltpu.VMEM(tile, F32), pltpu.VMEM(tile, F32), pltpu.VMEM(tile, BF16), pltpu.VMEM(tile, BF16)])


def _rpb_fold(y, n_heads, name):
    def body(y_ref, o_ref):
        for h in range(n_heads):
            for dr in range(2 * WIN_R):
                acc = jnp.zeros((1, LANES), F32)
                for dr0 in range(WIN_R):
                    w = dr - dr0
                    if 0 <= w < WIN_R:
                        acc = acc + y_ref[h, dr0, w:w + 1, :]
                o_ref[h, dr:dr + 1, :] = acc

    return pl.pallas_call(
        body, name=name, out_shape=jax.ShapeDtypeStruct((n_heads, 2 * WIN_R, LANES), F32),
    )(y)


def _rpb_grad(dtb, onehot, name):
    nh = dtb.shape[0]
    rows = dtb.reshape(nh, WIN_R, GRID_W, WIN_R, GRID_W).transpose(0, 1, 3, 2, 4).reshape(nh * WIN_R * WIN_R, GRID_W * GRID_W)
    y = _matmul(rows, onehot, dims="nn", ti=rows.shape[0], tj=LANES, tk=GRID_W * GRID_W, out_dtype=F32, name=name + "_dc")
    folded = _rpb_fold(y.reshape(nh, WIN_R, WIN_R, LANES), nh, name + "_dr")
    return folded[:, :N_DR, :N_DC]


WA_WIN_TOK = 3 * BAND


def _wa_scores(q, kwin, t0, j, sink_ref, head0, grp):
    rows = grp * BAND
    s = _dot(q, kwin, _NT) * SCALE
    row = lax.broadcasted_iota(jnp.int32, (rows, WA_WIN_TOK), 0)
    qpos = j * BAND + (row & (BAND - 1))
    kpos = t0 + lax.broadcasted_iota(jnp.int32, (rows, WA_WIN_TOK), 1)
    s = jnp.where(jnp.abs(kpos - qpos) <= BAND, s, NEG)
    head = lax.broadcasted_iota(jnp.int32, (rows, 1), 0) // BAND
    sink = jnp.zeros((rows, 1), F32) + sink_ref[head0]
    for g in range(1, grp):
        sink = jnp.where(head == g, sink_ref[head0 + g], sink)
    m = jnp.maximum(jnp.max(s, axis=-1, keepdims=True), sink)
    e = jnp.exp(s - m)
    es = jnp.exp(sink - m)
    rz = 1.0 / (jnp.sum(e, axis=-1, keepdims=True) + es)
    return e * rz, es * rz


def _wa_window(j, s):
    return pl.multiple_of(jnp.clip((j - 1) * BAND, 0, s - WA_WIN_TOK), BAND)


def _wa_fwd(q, k, v, sink, name, comm=None):
    hq, s, _ = q.shape
    hkv = k.shape[0]
    grp = hq // hkv

    def body(sink_ref, q_ref, k_ref, v_ref, o_ref):
        kh, j = pl.program_id(0), pl.program_id(1)
        t0 = _wa_window(j, s)
        keys = pl.ds(t0, WA_WIN_TOK)
        p, _ = _wa_scores(q_ref[...].reshape(grp * BAND, HEAD), k_ref[keys, :], t0, j, sink_ref, kh * grp, grp)
        o_ref[...] = _dot(p.astype(BF16), v_ref[keys, :], _NN).reshape(grp, BAND, HEAD)

    qspec = pl.BlockSpec((grp, BAND, HEAD), lambda kh, j: (kh, j, 0))
    full = pl.BlockSpec((None, s, HEAD), lambda kh, j: (kh, 0, 0))
    return _call(
        body, comm, name=name, grid=(hkv, s // BAND), operands=[sink, q, k, v],
        in_specs=[SMEM, qspec, full, full],
        out_specs=[qspec], out_shape=[jax.ShapeDtypeStruct((hq, s, HEAD), F32)], scratch_shapes=[])


def _wa_bwd(q, k, v, sink, do, name, comm=None):
    hq, s, _ = q.shape
    hkv = k.shape[0]
    grp = hq // hkv

    def body(sink_ref, q_ref, do_ref, k_ref, v_ref, dq_ref, dk_ref, dv_ref, dsink_ref):
        kh, j = pl.program_id(0), pl.program_id(1)

        @pl.when(j == 0)
        def _():
            dk_ref[...] = jnp.zeros_like(dk_ref)
            dv_ref[...] = jnp.zeros_like(dv_ref)
            dsink_ref[...] = jnp.zeros_like(dsink_ref)

        t0 = _wa_window(j, s)
        keys = pl.ds(t0, WA_WIN_TOK)
        qs = q_ref[...].reshape(grp * BAND, HEAD)
        dos = do_ref[...].reshape(grp * BAND, HEAD)
        kwin, vwin = k_ref[keys, :], v_ref[keys, :]
        p, ps = _wa_scores(qs, kwin, t0, j, sink_ref, kh * grp, grp)
        dp = _dot(dos, vwin, _NT)
        dv_ref[keys, :] += _dot(p.astype(BF16), dos, _TN)
        rowdot = jnp.sum(p * dp, axis=-1, keepdims=True)
        to_sink = ps * rowdot
        for g in range(grp):
            dsink_ref[g] += jnp.zeros((8, LANES), F32) - jnp.sum(to_sink[g * BAND:(g + 1) * BAND])
        dss = (p * (dp - rowdot) * SCALE).astype(BF16)
        dq_ref[...] = _dot(dss, kwin, _NN).reshape(grp, BAND, HEAD)
        dk_ref[keys, :] += _dot(dss, qs, _TN)

    qspec = pl.BlockSpec((grp, BAND, HEAD), lambda kh, j: (kh, j, 0))
    full = pl.BlockSpec((None, s, HEAD), lambda kh, j: (kh, 0, 0))
    kv = jax.ShapeDtypeStruct((hkv, s, HEAD), F32)
    return _call(
        body, comm, name=name, grid=(hkv, s // BAND), operands=[sink, q, do, k, v],
        in_specs=[SMEM, qspec, qspec, full, full],
        out_specs=[qspec, full, full, pl.BlockSpec((grp, 8, LANES), lambda kh, j: (kh, 0, 0))],
        out_shape=[jax.ShapeDtypeStruct((hq, s, HEAD), F32), kv, kv, jax.ShapeDtypeStruct((hq, 8, LANES), F32)],
        scratch_shapes=[])


def _onorm_fwd(oa, ob, gains, name):
    ha, s, _ = oa.shape
    hq = ob.shape[0]
    ts = _tile(s, 256, 16)

    def body(oa_ref, ob_ref, g_ref, o_ref):
        col = 0
        for ref, nh in ((oa_ref, ha), (ob_ref, hq)):
            ss = sum(jnp.sum(ref[h] * ref[h], axis=-1, keepdims=True) for h in range(nh))
            r = lax.rsqrt(ss / (nh * HEAD) + EPS)
            for h in range(nh):
                o_ref[:, col * HEAD:(col + 1) * HEAD] = (ref[h] * r * g_ref[:, col * HEAD:(col + 1) * HEAD]).astype(BF16)
                col += 1

    mix = (ha + hq) * HEAD
    return pl.pallas_call(
        body, name=name, grid=(s // ts,),
        in_specs=[pl.BlockSpec((ha, ts, HEAD), lambda i: (0, i, 0)), pl.BlockSpec((hq, ts, HEAD), lambda i: (0, i, 0)),
                  pl.BlockSpec((1, mix), lambda i: (0, 0))],
        out_specs=pl.BlockSpec((ts, mix), lambda i: (i, 0)),
        out_shape=jax.ShapeDtypeStruct((s, mix), BF16), compiler_params=_params(),
    )(oa, ob, gains)


def _onorm_bwd(oa, ob, gains, don, name):
    ha, s, _ = oa.shape
    hq = ob.shape[0]
    ts = _tile(s, 256, 16)
    mix = (ha + hq) * HEAD

    def body(oa_ref, ob_ref, g_ref, don_ref, doa_ref, dob_ref, dg_ref):
        @pl.when(pl.program_id(0) == 0)
        def _():
            dg_ref[...] = jnp.zeros_like(dg_ref)

        col0 = 0
        for ref, d_ref, nh in ((oa_ref, doa_ref, ha), (ob_ref, dob_ref, hq)):
            ss = sum(jnp.sum(ref[h] * ref[h], axis=-1, keepdims=True) for h in range(nh))
            r = lax.rsqrt(ss / (nh * HEAD) + EPS)
            dot = jnp.zeros((ts, 1), F32)
            for h in range(nh):
                cols = slice((col0 + h) * HEAD, (col0 + h + 1) * HEAD)
                dot = dot + jnp.sum(don_ref[:, cols] * g_ref[:, cols] * ref[h], axis=-1, keepdims=True)
            mean = dot * r / (nh * HEAD)
            for h in range(nh):
                cols = slice((col0 + h) * HEAD, (col0 + h + 1) * HEAD)
                y = ref[h] * r
                dn = don_ref[:, cols]
                d_ref[h] = (r * (dn * g_ref[:, cols] - y * mean)).astype(BF16)
                dg_ref[0:1, cols] += jnp.sum(dn * y, axis=0, keepdims=True)
            col0 += nh

    return pl.pallas_call(
        body, name=name, grid=(s // ts,),
        in_specs=[pl.BlockSpec((ha, ts, HEAD), lambda i: (0, i, 0)), pl.BlockSpec((hq, ts, HEAD), lambda i: (0, i, 0)),
                  pl.BlockSpec((1, mix), lambda i: (0, 0)), pl.BlockSpec((ts, mix), lambda i: (i, 0))],
        out_specs=[pl.BlockSpec((ha, ts, HEAD), lambda i: (0, i, 0)), pl.BlockSpec((hq, ts, HEAD), lambda i: (0, i, 0)),
                   pl.BlockSpec((8, mix), lambda i: (0, 0))],
        out_shape=[jax.ShapeDtypeStruct((ha, s, HEAD), BF16), jax.ShapeDtypeStruct((hq, s, HEAD), BF16),
                   jax.ShapeDtypeStruct((8, mix), F32)],
        compiler_params=_params(),
    )(oa, ob, gains, don)


HALO = 8


def _shift_rows(cur, halo_prev, halo_next, i, n):
    ts = cur.shape[0]
    row = lax.broadcasted_iota(jnp.int32, cur.shape, 0)
    first = jnp.where(i > 0, halo_prev[HALO - 1:HALO, :], 0.0)
    last = jnp.where(i < n - 1, halo_next[0:1, :], 0.0)
    prev = jnp.where(row == 0, first, pltpu.roll(cur, 1, axis=0))
    nxt = jnp.where(row == ts - 1, last, pltpu.roll(cur, ts - 1, axis=0))
    return prev, nxt


def _halo_specs(ts, tc, col_off):
    per = ts // HALO
    cur = pl.BlockSpec((ts, tc), lambda j, i: (i, j + col_off))
    prev = pl.BlockSpec((HALO, tc), lambda j, i: (jnp.maximum(i * per - 1, 0), j + col_off))

    def nxt_map(n_blocks):
        return pl.BlockSpec((HALO, tc), lambda j, i: (jnp.minimum((i + 1) * per, n_blocks - 1), j + col_off))

    return cur, prev, nxt_map


def _sigmoid(x):
    return 1.0 / (1.0 + jnp.exp(-x))


def _ffn_tiles(s, f):
    return _tile(s, 512, 16), _tile(f, 512, LANES)


def _gate_fwd(u, cw, cb, f, name):
    s = u.shape[0]
    ts, tc = _ffn_tiles(s, f)
    nj, ni = f // tc, s // ts

    def body(g_ref, gp_ref, gn_ref, u_ref, up_ref, un_ref, wg_ref, wu_ref, bg_ref, bu_ref, a_ref, gu_ref):
        i = pl.program_id(1)

        def conv(c_ref, p_ref, n_ref, w_ref, b_ref):
            cur = c_ref[...]
            prev, nxt = _shift_rows(cur, p_ref[...], n_ref[...], i, ni)
            return prev * w_ref[0:1, :] + cur * w_ref[1:2, :] + nxt * w_ref[2:3, :] + b_ref[...]

        gate = conv(g_ref, gp_ref, gn_ref, wg_ref, bg_ref)
        up = conv(u_ref, up_ref, un_ref, wu_ref, bu_ref)
        gu_ref[0] = gate
        gu_ref[1] = up
        a_ref[...] = (gate * _sigmoid(gate) * up).astype(BF16)

    gc, gp, gn = _halo_specs(ts, tc, 0)
    uc, up_, un = _halo_specs(ts, tc, nj)
    wg = pl.BlockSpec((3, tc), lambda j, i: (0, j))
    wu = pl.BlockSpec((3, tc), lambda j, i: (0, j + nj))
    bg = pl.BlockSpec((1, tc), lambda j, i: (0, j))
    bu = pl.BlockSpec((1, tc), lambda j, i: (0, j + nj))
    return pl.pallas_call(
        body, name=name, grid=(nj, ni),
        in_specs=[gc, gp, gn(s // HALO), uc, up_, un(s // HALO), wg, wu, bg, bu],
        out_specs=[pl.BlockSpec((ts, tc), lambda j, i: (i, j)), pl.BlockSpec((2, ts, tc), lambda j, i: (0, i, j))],
        out_shape=[jax.ShapeDtypeStruct((s, f), BF16), jax.ShapeDtypeStruct((2, s, f), F32)], compiler_params=_params(),
    )(u, u, u, u, u, u, cw, cw, cb, cb)


def _ffn_bwd(gu, u, da, cw, name):
    _, s, f = gu.shape
    ts, tc = _ffn_tiles(s, f)
    nj, ni = f // tc, s // ts

    def body(gu_ref, gup_ref, gun_ref, da_ref, dap_ref, dan_ref, xg_ref, xu_ref, wg_ref, wu_ref,
             du_ref, dcw_ref, dcb_ref):
        i = pl.program_id(1)

        @pl.when(i == 0)
        def _():
            dcw_ref[...] = jnp.zeros_like(dcw_ref)
            dcb_ref[...] = jnp.zeros_like(dcb_ref)

        rows = ts + 2 * HALO
        mid = slice(HALO, HALO + ts)
        da = jnp.concatenate([jnp.where(i > 0, dap_ref[...], 0.0), da_ref[...], jnp.where(i < ni - 1, dan_ref[...], 0.0)], axis=0)
        gate = jnp.concatenate([gup_ref[0], gu_ref[0], gun_ref[0]], axis=0)
        up = jnp.concatenate([gup_ref[1], gu_ref[1], gun_ref[1]], axis=0)
        sg = _sigmoid(gate)
        d_up = da * gate * sg
        d_gate = da * up * (sg * (1.0 + gate * (1.0 - sg)))
        for half, (dd, x_ref, w_ref) in enumerate(((d_gate, xg_ref, wg_ref), (d_up, xu_ref, wu_ref))):
            before = pltpu.roll(dd, 1, axis=0)
            after = pltpu.roll(dd, rows - 1, axis=0)
            du_ref[half] = (before * w_ref[2:3, :] + dd * w_ref[1:2, :] + after * w_ref[0:1, :])[mid].astype(BF16)
            x = x_ref[...]
            dcb_ref[half, 0:1, :] += jnp.sum(dd[mid], axis=0, keepdims=True)
            for k, shifted in enumerate((after, dd, before)):
                dcw_ref[half, k, 0:1, :] += jnp.sum(shifted[mid] * x, axis=0, keepdims=True)

    per = ts // HALO
    cur3 = pl.BlockSpec((2, ts, tc), lambda j, i: (0, i, j))
    prev3 = pl.BlockSpec((2, HALO, tc), lambda j, i: (0, jnp.maximum(i * per - 1, 0), j))
    next3 = pl.BlockSpec((2, HALO, tc), lambda j, i: (0, jnp.minimum((i + 1) * per, s // HALO - 1), j))
    cur, prev, nxt = _halo_specs(ts, tc, 0)
    return pl.pallas_call(
        body, name=name, grid=(nj, ni),
        in_specs=[cur3, prev3, next3, cur, prev, nxt(s // HALO),
                  pl.BlockSpec((ts, tc), lambda j, i: (i, j)), pl.BlockSpec((ts, tc), lambda j, i: (i, j + nj)),
                  pl.BlockSpec((3, tc), lambda j, i: (0, j)), pl.BlockSpec((3, tc), lambda j, i: (0, j + nj))],
        out_specs=[cur3, pl.BlockSpec((2, 3, 8, tc), lambda j, i: (0, 0, 0, j)),
                   pl.BlockSpec((2, 8, tc), lambda j, i: (0, 0, j))],
        out_shape=[jax.ShapeDtypeStruct((2, s, f), BF16),
                   jax.ShapeDtypeStruct((2, 3, 8, f), F32), jax.ShapeDtypeStruct((2, 8, f), F32)],
        compiler_params=_params(),
    )(gu, gu, gu, da, da, da, u, u, cw, cw)


def _loss_head(y, target, name):
    s, d = y.shape
    ts = _tile(s, 256, 16)

    def body(y_ref, t_ref, dy_ref, dyb_ref, l_ref):
        @pl.when(pl.program_id(0) == 0)
        def _():
            l_ref[...] = jnp.zeros_like(l_ref)

        err = y_ref[...] - t_ref[...]
        dy = err / d
        dy_ref[...] = dy
        dyb_ref[...] = dy.astype(BF16)
        l_ref[...] += jnp.zeros((8, LANES), F32) + 0.5 * jnp.sum(jnp.sum(err * err, axis=-1, keepdims=True) / d)

    blk = pl.BlockSpec((ts, d), lambda i: (i, 0))
    return pl.pallas_call(
        body, name=name, grid=(s // ts,),
        in_specs=[blk, blk], out_specs=[blk, blk, pl.BlockSpec((8, LANES), lambda i: (0, 0))],
        out_shape=[jax.ShapeDtypeStruct((s, d), F32), jax.ShapeDtypeStruct((s, d), BF16), jax.ShapeDtypeStruct((8, LANES), F32)],
        compiler_params=_params(),
    )(y, target)


SMALL = ("ln1_g", "qn_a", "kn_a", "rpb", "qn_b", "kn_b", "sink", "on_a", "on_b", "ln2_g", "conv_b", "conv_w")
PACK_ALIGN = 8 * LANES


def _pack(arrays):
    flat = []
    for a in arrays:
        a = a.reshape(-1)
        flat.append(jnp.pad(a, (0, -a.size % PACK_ALIGN)))
    return jnp.concatenate(flat).reshape(-1, LANES)


def _unpack(packed, like):
    out, at = [], 0
    flat = packed.reshape(-1)
    for a in like:
        out.append(flat[at:at + a.size].reshape(a.shape))
        at += a.size + (-a.size % PACK_ALIGN)
    return out


def _matmul_tiles(s, k, j):
    return dict(ti=_tile(s, 512, 16), tj=_tile(j, 1536, LANES), tk=_tile(k, 2048, LANES))


def kernel(x, positions, ln1_g, w_in, qn_a, kn_a, rpb, qn_b, kn_b, sink, on_a, on_b, w_out, ln2_g, w_up, conv_w, conv_b, w_down, loss_target, m_ln1_g, m_w_in, m_qn_a, m_kn_a, m_rpb, m_qn_b, m_kn_b, m_sink, m_on_a, m_on_b, m_w_out, m_ln2_g, m_w_up, m_conv_w, m_conv_b, m_w_down, v_ln1_g, v_w_in, v_qn_a, v_kn_a, v_rpb, v_qn_b, v_kn_b, v_sink, v_on_a, v_on_b, v_w_out, v_ln2_g, v_w_up, v_conv_w, v_conv_b, v_w_down):
    weights = dict(ln1_g=ln1_g, w_in=w_in, qn_a=qn_a, kn_a=kn_a, rpb=rpb, qn_b=qn_b, kn_b=kn_b, sink=sink, on_a=on_a,
                   on_b=on_b, w_out=w_out, ln2_g=ln2_g, w_up=w_up, conv_w=conv_w, conv_b=conv_b, w_down=w_down)
    mom1 = dict(ln1_g=m_ln1_g, w_in=m_w_in, qn_a=m_qn_a, kn_a=m_kn_a, rpb=m_rpb, qn_b=m_qn_b, kn_b=m_kn_b, sink=m_sink,
                on_a=m_on_a, on_b=m_on_b, w_out=m_w_out, ln2_g=m_ln2_g, w_up=m_w_up, conv_w=m_conv_w, conv_b=m_conv_b,
                w_down=m_w_down)
    mom2 = dict(ln1_g=v_ln1_g, w_in=v_w_in, qn_a=v_qn_a, kn_a=v_kn_a, rpb=v_rpb, qn_b=v_qn_b, kn_b=v_kn_b, sink=v_sink,
                on_a=v_on_a, on_b=v_on_b, w_out=v_w_out, ln2_g=v_ln2_g, w_up=v_w_up, conv_w=v_conv_w, conv_b=v_conv_b,
                w_down=v_w_down)
    order = ("ln1_g", "w_in", "qn_a", "kn_a", "rpb", "qn_b", "kn_b", "sink", "on_a", "on_b", "w_out", "ln2_g", "w_up",
             "conv_w", "conv_b", "w_down")

    depth, d = ln1_g.shape
    s = x.shape[1]
    ha = on_a.shape[1] // HEAD
    hq = on_b.shape[1] // HEAD
    pw = w_in.shape[2] * N_DEV
    hkv = (pw - 3 * ha * HEAD - hq * HEAD) // (2 * HEAD)
    f = w_down.shape[1] * N_DEV
    mix = (ha + hq) * HEAD
    cfg = (ha, hq, hkv)
    fs = conv_w.shape[2]
    dev = 4 * lax.axis_index("x") + 2 * lax.axis_index("y") + lax.axis_index("c")
    core = lax.axis_index("c").astype(jnp.int32).reshape(1)

    shard = {n: weights[n].astype(BF16) for n in ("w_in", "w_out", "w_up", "w_down")}

    def unshard(n, g):
        if n in ("w_in", "w_up"):
            return g.transpose(1, 0, 2).reshape(g.shape[1], N_DEV * g.shape[2])
        return g.reshape(N_DEV * g.shape[1], g.shape[2])

    full = {n: [None] * depth for n in shard}
    full["w_in"][0] = unshard("w_in", _allgather(shard["w_in"][0], "gather0_w_in"))
    cw_rows = depth * 3
    cw_pad = jnp.pad(conv_w.reshape(cw_rows, fs), ((0, -cw_rows % 8), (0, 0)))
    g_cw = _allgather(cw_pad, "gather_conv_w")
    full_cw = g_cw[:, :cw_rows].reshape(N_DEV, depth, 3, fs).transpose(1, 2, 0, 3).reshape(depth, 3, 2 * f)

    inv = ROPE_THETA ** (-jnp.arange(0, HEAD, 2, dtype=F32) / HEAD)
    ang = positions.astype(F32)[:, None] * inv[None, :]
    cos = jnp.concatenate([jnp.cos(ang), jnp.cos(ang)], axis=-1)
    sin = jnp.concatenate([-jnp.sin(ang), jnp.sin(ang)], axis=-1)
    qk = jnp.arange(GRID_W * GRID_W)
    dc_of = (qk % GRID_W) - (qk // GRID_W) + (WIN_C - 1)
    onehot = (dc_of[:, None] == jnp.arange(LANES)[None, :]).astype(BF16)

    tiles_s = _tile(s, 512, 16)

    xs = x.reshape(s, d)
    saved = []
    for l in range(depth):
        more = l + 1 < depth

        def fwd_matmul(n, a_op, name, also=(), **kw):
            wanted = ([(n, l + 1)] if more else []) + list(also)
            if not wanted:
                return _matmul(a_op, full[n][l], dims="nn", out_dtype=F32, name=name + "_last", **kw)
            out, *got = _matmul(a_op, full[n][l], dims="nn", out_dtype=F32, name=name + "_also" * bool(also),
                                comm=_gather_comm([shard[m][k] for m, k in wanted]), **kw)
            for (m, k), g in zip(wanted, got):
                full[m][k] = unshard(m, g)
            return out

        gains = jnp.zeros((8, HEAD), F32).at[0].set(qn_a[l]).at[1].set(kn_a[l]).at[2].set(qn_b[l]).at[3].set(kn_b[l])
        on_g = jnp.concatenate([on_a[l], on_b[l]]).reshape(1, mix)
        h = _rms_fwd(xs, ln1_g[l].reshape(1, d), "ln1_fwd")
        proj = fwd_matmul("w_in", h, "proj_fwd", also=[("w_out", 0)] if l == 0 else (), ti=tiles_s, tj=_tile(pw, 1536, LANES), tk=d)
        qa, ka, va, qb, kb, vb = _qkv_fwd(proj, gains, cos, sin, cfg, "qkv_fwd")
        tb = _na_bias(rpb[l].reshape(-1), ha, "na_bias")
        if l == 0:
            oa, got = _na_fwd(qa, ka, va, tb, "na_fwd_also", comm=_gather_comm([shard["w_up"][0]]))
            full["w_up"][0] = unshard("w_up", got)
            ob, got = _wa_fwd(qb, kb, vb, sink[l], "wa_fwd_also", comm=_gather_comm([shard["w_down"][0]]))
            full["w_down"][0] = unshard("w_down", got)
        else:
            oa = _na_fwd(qa, ka, va, tb, "na_fwd")
            ob = _wa_fwd(qb, kb, vb, sink[l], "wa_fwd")
        o_n = _onorm_fwd(oa, ob, on_g, "onorm_fwd")
        x1 = fwd_matmul("w_out", o_n, "out_fwd", ti=tiles_s, tj=_tile(d, 2048, LANES), tk=mix, resid=xs)
        h2 = _rms_fwd(x1, ln2_g[l].reshape(1, d), "ln2_fwd")
        u = fwd_matmul("w_up", h2, "up_fwd", ti=tiles_s, tj=_tile(2 * f, 1024, 2 * LANES), tk=d)
        a, gu = _gate_fwd(u, full_cw[l], conv_b[l].reshape(1, 2 * f), f, "gate_fwd")
        x2 = fwd_matmul("w_down", a, "down_fwd", ti=tiles_s, tj=_tile(d, 512, LANES), tk=f, resid=x1)
        saved.append(dict(x=xs, h=h, proj=proj, gains=gains, on_g=on_g, qkv=(qa, ka, va, qb, kb, vb), tb=tb, oa=oa, ob=ob,
                          o_n=o_n, x1=x1, h2=h2, u=u, gu=gu, a=a))
        xs = x2

    dx, dx_b, loss_part = _loss_head(xs, loss_target.reshape(s, d), "loss_head")
    tile_c = _tile(s, 2048, 16)
    half_k = _tile(2 * f, f, fs)
    loss = lax.psum(loss_part[0, 0], ("x", "y", "c"))

    small_grads = [None] * depth
    big = {n: None for n in ("w_in", "w_out", "w_up", "w_down")}
    pending = None
    for l in reversed(range(depth)):
        sv = saved[l]
        qa, ka, va, qb, kb, vb = sv["qkv"]

        def update(n, layer, got):
            big[n] = _adamw(got, weights[n], mom1[n], mom2[n], "adamw_" + n, layer=layer, into=big[n])

        def grad_matmul(n, a_op, b_op, name, **kw):
            if pending is None:
                return _matmul(a_op, b_op, dims="tn", out_dtype=BF16, name=name + "_first", **kw)
            out, got = _matmul(a_op, b_op, dims="tn", out_dtype=BF16, name=name, comm=_scatter_comm([pending[n]]), **kw)
            update(n, l + 1, got)
            return out

        def own(blocks):
            return _scatter_comm([blocks]) if l == 0 else None

        gw_down = grad_matmul("w_down", sv["a"], dx_b, "down_bwd_w", ti=_tile(f, 1408, LANES), tj=_tile(d, 1024, LANES),
                              tk=tile_c, j_outer=False)
        da = _matmul(dx_b, full["w_down"][l], dims="nt", ti=tiles_s, tj=_tile(f, 2816, 2 * LANES), tk=d, out_dtype=F32,
                     name="down_bwd_x")
        du, dcw, dcb = _ffn_bwd(sv["gu"], sv["u"], da, full_cw[l], "ffn_bwd")
        gw_down = gw_down.reshape(N_DEV, f // N_DEV, d)
        dh2 = None
        for part in range(2 * f // half_k):
            comm = own(gw_down) if part == 0 else None
            dh2 = _matmul(du, full["w_up"][l], dims="nt", ti=tiles_s, tj=_tile(d, 1024, LANES), tk=half_k, out_dtype=F32,
                          name=f"up_bwd_x{part}" + "_own" * bool(comm), k_blocks=(part, 1), resid=dh2, halved="a", comm=comm)
            if comm:
                dh2, got = dh2
                update("w_down", 0, got)
        gw_up = grad_matmul("w_up", sv["h2"], du, "up_bwd_w", ti=_tile(d, 1024, LANES), tj=fs, tk=tile_c, dev_major=True,
                            halved="b")
        dx1, dx1_b, dln2 = _rms_bwd(sv["x1"], ln2_g[l].reshape(1, d), dh2, dx, "ln2_bwd")
        don = _matmul(dx1_b, full["w_out"][l], dims="nt", ti=tiles_s, tj=mix, tk=d, out_dtype=F32, name="out_bwd_x")
        gw_out = grad_matmul("w_out", sv["o_n"], dx1_b, "out_bwd_w", ti=_tile(mix, 1024, LANES), tj=_tile(d, 1024, LANES),
                             tk=tile_c, j_outer=False)
        gw_out = gw_out.reshape(N_DEV, mix // N_DEV, d)
        doa, dob, don_g = _onorm_bwd(sv["oa"], sv["ob"], sv["on_g"], don, "onorm_bwd")
        dqa, dka, dva, dtb, *got = _na_bwd(qa, ka, va, sv["tb"], doa, "na_bwd" + "_own" * (l == 0), comm=own(gw_up))
        if got:
            update("w_up", 0, got[0])
        dqb, dkb, dvb, dsink, *got = _wa_bwd(qb, kb, vb, sink[l], dob, "wa_bwd" + "_own" * (l == 0), comm=own(gw_out))
        if got:
            update("w_out", 0, got[0])
        drpb = _rpb_grad(dtb, onehot, "rpb_grad")
        dproj, dgains = _qkv_bwd(sv["proj"], sv["gains"], cos, sin, (dqa, dka, dva, dqb, dkb, dvb), cfg, "qkv_bwd")
        dh = _matmul(dproj, full["w_in"][l], dims="nt", ti=tiles_s, tj=_tile(d, 1024, LANES), tk=pw, out_dtype=F32, name="proj_bwd_x")
        gw_in = grad_matmul("w_in", sv["h"], dproj, "proj_bwd_w", ti=_tile(d, 1024, LANES), tj=_tile(pw, 1536, LANES), tk=tile_c)
        dx, dx_b, dln1 = _rms_bwd(sv["x"], ln1_g[l].reshape(1, d), dh, dx1, "ln1_bwd")

        small_grads[l] = dict(
            ln1_g=dln1[0], qn_a=dgains[0], kn_a=dgains[1], rpb=drpb, qn_b=dgains[2], kn_b=dgains[3], sink=dsink[:, 0, 0],
            on_a=don_g[0, :ha * HEAD], on_b=don_g[0, ha * HEAD:], ln2_g=dln2[0],
            conv_b=dcb[:, 0, :].reshape(2 * f), conv_w=dcw[:, :, 0, :].transpose(1, 0, 2).reshape(3, 2 * f))

        pending = dict(w_in=gw_in.reshape(d, N_DEV, pw // N_DEV).transpose(1, 0, 2), w_out=gw_out, w_up=gw_up, w_down=gw_down)

    summed = _reduce_scatter(pending["w_in"], core, "rs0_w_in")
    big["w_in"] = _adamw(summed, weights["w_in"], mom1["w_in"], mom2["w_in"], "adamw0_w_in", layer=0, into=big["w_in"])

    grads_l = [small_grads[l][n] for l in range(depth) for n in SMALL]
    gathered = _allgather(_pack(grads_l), "gather_small")
    zeros_cw = jnp.zeros((3, 2 * f), F32)

    def small_state(src):
        return _pack([zeros_cw if n == "conv_w" else src[n][l] for l in range(depth) for n in SMALL])

    sm = _adamw(gathered, small_state(weights), small_state(mom1), small_state(mom2), "adamw_small")
    sm = [_unpack(t, grads_l) for t in sm]
    small_out = {n: [jnp.stack([sm[k][l * len(SMALL) + i] for l in range(depth)]) for k in range(4)]
                 for i, n in enumerate(SMALL)}
    cw_grad = lax.dynamic_slice_in_dim(small_out["conv_w"][0], dev * fs, fs, axis=2)
    cw_rows_pad = cw_rows + (-cw_rows % 8)

    def rows8(a):
        return jnp.pad(a.reshape(cw_rows, fs), ((0, cw_rows_pad - cw_rows), (0, 0)))

    cw_res = _adamw(rows8(cw_grad)[None], rows8(conv_w), rows8(m_conv_w), rows8(v_conv_w), "adamw_conv_w")
    small_out["conv_w"] = [t[:cw_rows].reshape(depth, 3, fs) for t in cw_res]

    results = {n: (big[n] if n in big else small_out[n]) for n in order}
    grad_x = dx.reshape(1, s, d)
    return (loss, grad_x, *[results[n][0] for n in order], *[results[n][1] for n in order],
            *[results[n][2] for n in order], *[results[n][3] for n in order])
```

```python
import functools
import math

import jax
import jax.numpy as jnp
from jax import lax
from jax.experimental import pallas as pl
from jax.experimental.pallas import tpu as pltpu

F32 = jnp.float32
BF16 = jnp.bfloat16

HEAD = 128
GRID_W = 64
WIN_R = 8
WIN_C = 16
BAND = 128
ROPE_THETA = 10000.0
EPS = 1e-6
NEG = -1e30
SCALE = 1.0 / math.sqrt(HEAD)

ADAM_LR = 0.001
ADAM_B1 = 0.9
ADAM_B2 = 0.999
ADAM_EPS = 1e-08
ADAM_WD = 0.01
ADAM_STEP = 10

N_DEV = 8
LANES = 128
VMEM_LIMIT_BYTES = 56 * 2 ** 20
MESH = pl.DeviceIdType.MESH
ANY = pl.BlockSpec(memory_space=pl.ANY)
SMEM = pl.BlockSpec(memory_space=pltpu.SMEM)


def _params():
    return pltpu.CompilerParams(vmem_limit_bytes=VMEM_LIMIT_BYTES)


def _tile(n, pref, align):
    t = min(n, pref)
    t -= t % align
    while t > 0 and n % t:
        t -= align
    return t if t > 0 else n


def _place():
    x, y, c = lax.axis_index("x"), lax.axis_index("y"), lax.axis_index("c")
    chips = [(1 - x, y), (x, 1 - y), (1 - x, 1 - y)]
    return x, y, c, chips


COPIES_PER_ARRAY = N_DEV - 1


def _comm_scratch(n_arrays):
    return [pltpu.SemaphoreType.DMA((COPIES_PER_ARRAY * n_arrays,)), pltpu.SemaphoreType.DMA((COPIES_PER_ARRAY * n_arrays,)),
            pltpu.SemaphoreType.DMA((n_arrays,))]


def _gather_plan(src_refs, out_refs, send_sems, recv_sems, local_sems):
    x, y, c, chips = _place()
    me, sibling = (x, y, c), (x, y, 1 - c)

    def slot(a, px, py, pc):
        return out_refs[a].at[4 * px + 2 * py + pc]

    def copy(a, k, block, to, src=None):
        return pltpu.make_async_remote_copy(
            src_ref=slot(a, *block) if src is None else src, dst_ref=slot(a, *block),
            send_sem=send_sems.at[COPIES_PER_ARRAY * a + k], recv_sem=recv_sems.at[COPIES_PER_ARRAY * a + k],
            device_id=to, device_id_type=MESH)

    def mine(a):
        return pltpu.make_async_copy(src_refs[a], slot(a, *me), local_sems.at[a])

    def first(a):
        return [copy(a, 0, me, sibling, src=src_refs[a])] + [
            copy(a, 1 + j, me, (*chip, c), src=src_refs[a]) for j, chip in enumerate(chips)]

    def passed(a):
        return [copy(a, 4 + j, (*chip, c), sibling) for j, chip in enumerate(chips)]

    def start():
        for a in range(len(src_refs)):
            mine(a).start()
            for cp in first(a):
                cp.start()

    def finish():
        forwards = [passed(a) for a in range(len(src_refs))]
        for a in range(len(src_refs)):
            for j, chip in enumerate(chips):
                copy(a, 1 + j, (*chip, c), me).wait_recv()
                forwards[a][j].start()
        for a in range(len(src_refs)):
            copy(a, 0, sibling, me).wait_recv()
            for j, chip in enumerate(chips):
                copy(a, 4 + j, (*chip, 1 - c), me).wait_recv()
            for cp in first(a) + forwards[a]:
                cp.wait_send()
            mine(a).wait()

    return start, finish


def _scatter_plan(src_refs, out_refs, send_sems, recv_sems, local_sems):
    x, y, c, _ = _place()
    me = 4 * x + 2 * y + c

    def peer(k):
        px = 1 - x if k & 4 else x
        py = 1 - y if k & 2 else y
        pc = 1 - c if k & 1 else c
        return (px, py, pc), 4 * px + 2 * py + pc

    def copy(a, k, outgoing):
        to, idx = peer(k)
        return pltpu.make_async_remote_copy(
            src_ref=src_refs[a].at[idx], dst_ref=out_refs[a].at[me if outgoing else idx],
            send_sem=send_sems.at[COPIES_PER_ARRAY * a + k - 1], recv_sem=recv_sems.at[COPIES_PER_ARRAY * a + k - 1],
            device_id=to, device_id_type=MESH)

    def mine(a):
        return pltpu.make_async_copy(src_refs[a].at[me], out_refs[a].at[me], local_sems.at[a])

    def start():
        for a in range(len(src_refs)):
            mine(a).start()
            for k in range(1, N_DEV):
                copy(a, k, True).start()

    def finish():
        for a in range(len(src_refs)):
            for k in range(1, N_DEV):
                copy(a, k, False).wait_recv()
            for k in range(1, N_DEV):
                copy(a, k, True).wait_send()
            mine(a).wait()

    return start, finish


def _allgather(v, name):
    def body(v_ref, out_ref, send_sems, recv_sems, local_sems):
        start, finish = _gather_plan([v_ref], [out_ref], send_sems, recv_sems, local_sems)
        start()
        finish()

    return pl.pallas_call(
        body, name=name,
        out_shape=jax.ShapeDtypeStruct((N_DEV,) + v.shape, v.dtype),
        in_specs=[ANY], out_specs=ANY, scratch_shapes=_comm_scratch(1),
    )(v)


def _sibling_exchange(g, name):
    def body(g_ref, out_ref, send_sems, recv_sems):
        x, y, c, _ = _place()
        sibling = (x, y, 1 - c)
        copies = []
        for j in range(4):
            copies.append(pltpu.make_async_remote_copy(
                src_ref=g_ref.at[2 * j + (1 - c)], dst_ref=out_ref.at[j],
                send_sem=send_sems.at[j], recv_sem=recv_sems.at[j], device_id=sibling, device_id_type=MESH))
        for cp in copies:
            cp.start()
        for cp in copies:
            cp.wait_recv()
        for cp in copies:
            cp.wait_send()

    return pl.pallas_call(
        body, name=name,
        out_shape=jax.ShapeDtypeStruct((4,) + g.shape[1:], g.dtype),
        in_specs=[ANY], out_specs=ANY,
        scratch_shapes=[pltpu.SemaphoreType.DMA((4,)), pltpu.SemaphoreType.DMA((4,))],
    )(g)


def _pair_sum(g, got, core, name):
    _, r, c = g.shape
    tr = _tile(r, max(16, (1 << 20) // c), 16)

    def body(core_ref, g_ref, got_ref, o_ref):
        del core_ref
        o_ref[...] = (g_ref[...].astype(F32) + got_ref[...].astype(F32)).astype(o_ref.dtype)

    return pl.pallas_call(
        body, name=name,
        out_shape=jax.ShapeDtypeStruct((4, r, c), g.dtype),
        grid_spec=pltpu.PrefetchScalarGridSpec(
            num_scalar_prefetch=1, grid=(4, r // tr),
            in_specs=[pl.BlockSpec((None, tr, c), lambda j, i, core_ref: (2 * j + core_ref[0], i, 0)),
                      pl.BlockSpec((None, tr, c), lambda j, i, core_ref: (j, i, 0))],
            out_specs=pl.BlockSpec((None, tr, c), lambda j, i, core_ref: (j, i, 0))),
        compiler_params=_params(),
    )(core, g, got)


def _chip_plan(src_refs, out_refs, send_sems, recv_sems, local_sems):
    x, y, c, chips = _place()

    def copies(a):
        return [pltpu.make_async_remote_copy(
            src_ref=src_refs[a].at[2 * px + py], dst_ref=out_refs[a].at[k],
            send_sem=send_sems.at[COPIES_PER_ARRAY * a + k], recv_sem=recv_sems.at[COPIES_PER_ARRAY * a + k],
            device_id=(px, py, c), device_id_type=MESH) for k, (px, py) in enumerate(chips)]

    def mine(a):
        return pltpu.make_async_copy(src_refs[a].at[2 * x + y], out_refs[a].at[3], local_sems.at[a])

    def start():
        for a in range(len(src_refs)):
            mine(a).start()
            for cp in copies(a):
                cp.start()

    def finish():
        for a in range(len(src_refs)):
            for cp in copies(a):
                cp.wait_recv()
            for cp in copies(a):
                cp.wait_send()
            mine(a).wait()

    return start, finish


def _chip_comm(blocks):
    return _chip_plan, blocks, [jax.ShapeDtypeStruct(p.shape, p.dtype) for p in blocks]


def _chip_exchange(p, name):
    def body(p_ref, out_ref, send_sems, recv_sems, local_sems):
        start, finish = _chip_plan([p_ref], [out_ref], send_sems, recv_sems, local_sems)
        start()
        finish()

    return pl.pallas_call(
        body, name=name,
        out_shape=jax.ShapeDtypeStruct(p.shape, p.dtype),
        in_specs=[ANY], out_specs=ANY, scratch_shapes=_comm_scratch(1),
    )(p)


def _pair_sums(g, core, name):
    got = _sibling_exchange(g, name + "_d2d")
    return _pair_sum(g, got, core, name + "_pair")


def _adamw(parts, w, m, v, name, layer=None, into=None):
    n_parts, r, c = parts.shape
    tr = _tile(r, max(8, (1 << 19) // c), 16 if parts.dtype == BF16 else 8)
    c1 = 1.0 - ADAM_B1 ** ADAM_STEP
    c2 = 1.0 - ADAM_B2 ** ADAM_STEP
    n_into = 0 if into is None else len(into)

    def body(p_ref, w_ref, m_ref, v_ref, *rest):
        g_out, d_out, m_out, v_out = rest[n_into:]
        g = p_ref[0].astype(F32)
        for k in range(1, n_parts):
            g = g + p_ref[k].astype(F32)
        m2 = ADAM_B1 * m_ref[...] + (1.0 - ADAM_B1) * g
        v2 = ADAM_B2 * v_ref[...] + (1.0 - ADAM_B2) * (g * g)
        g_out[...] = g
        m_out[...] = m2
        v_out[...] = v2
        d_out[...] = -ADAM_LR * ((m2 / c1) / (jnp.sqrt(v2 / c2) + ADAM_EPS) + ADAM_WD * w_ref[...])

    if layer is None:
        blk = pl.BlockSpec((tr, c), lambda i: (i, 0))
        out = jax.ShapeDtypeStruct((r, c), F32)
    else:
        blk = pl.BlockSpec((None, tr, c), lambda i: (layer, i, 0))
        out = jax.ShapeDtypeStruct(w.shape, F32)
    return pl.pallas_call(
        body, name=name, grid=(r // tr,),
        in_specs=[pl.BlockSpec((n_parts, tr, c), lambda i: (0, i, 0)), blk, blk, blk] + [ANY] * n_into,
        out_specs=[blk, blk, blk, blk], out_shape=[out, out, out, out],
        input_output_aliases={4 + k: k for k in range(n_into)},
        compiler_params=_params(),
    )(parts, w, m, v, *(into or ()))


def _host_exchange(body, comm, grid, n_in, n_out):
    plan, comm_in, comm_out = comm
    n = len(comm_in)

    def wrapped(*refs):
        ins, cin = refs[:n_in], refs[n_in:n_in + n]
        outs, cout = refs[n_in + n:n_in + n + n_out], refs[n_in + n + n_out:n_in + 2 * n + n_out]
        rest = refs[n_in + 2 * n + n_out:]
        start, finish = plan(cin, cout, *rest[len(rest) - 3:])
        steps = [pl.program_id(axis) for axis in range(len(grid))]
        first, last = steps[0] == 0, steps[0] == grid[0] - 1
        for axis in range(1, len(grid)):
            first, last = first & (steps[axis] == 0), last & (steps[axis] == grid[axis] - 1)

        @pl.when(first)
        def _():
            start()

        body(*ins, *outs, *rest[:len(rest) - 3])

        @pl.when(last)
        def _():
            finish()

    return wrapped, list(comm_in), [ANY] * n, list(comm_out), _comm_scratch(n)


def _gather_comm(shards):
    return _gather_plan, shards, [jax.ShapeDtypeStruct((N_DEV,) + v.shape, v.dtype) for v in shards]


def _scatter_comm(blocks):
    return _scatter_plan, blocks, [jax.ShapeDtypeStruct(g.shape, g.dtype) for g in blocks]


def _matmul(a, b, *, dims, ti, tj, tk, out_dtype, name, j_outer=True, resid=None, dev_major=False, comm=None,
            k_blocks=None, halved=None):
    a_shape = (a.shape[1], 2 * a.shape[2]) if halved == "a" else a.shape
    b_shape = (b.shape[1], 2 * b.shape[2]) if halved == "b" else b.shape
    if dims == "nn":
        (I, K), (K2, J) = a_shape, b_shape
    elif dims == "nt":
        (I, K), (J, K2) = a_shape, b_shape
    else:
        (K, I), (K2, J) = a_shape, b_shape
    assert K == K2 and I % ti == 0 and J % tj == 0 and K % tk == 0, (name, a.shape, b.shape, ti, tj, tk)
    assert halved is None or (halved, dims) in (("a", "nt"), ("b", "tn")), (name, halved, dims)
    k0, nk = k_blocks if k_blocks is not None else (0, K // tk)
    ni, nj = I // ti, J // tj

    def ij(g0, g1):
        return (g1, g0) if j_outer else (g0, g1)

    if dims == "nn":
        a_spec = pl.BlockSpec((ti, tk), lambda g0, g1, k: (ij(g0, g1)[0], k0 + k))
        b_spec = pl.BlockSpec((tk, tj), lambda g0, g1, k: (k0 + k, ij(g0, g1)[1]))
        dn = (((1,), (0,)), ((), ()))
    elif dims == "nt":
        a_spec = pl.BlockSpec((ti, tk), lambda g0, g1, k: (ij(g0, g1)[0], k0 + k))
        if halved == "a":
            per = K // 2 // tk
            a_spec = pl.BlockSpec((None, ti, tk), lambda g0, g1, k: ((k0 + k) // per, ij(g0, g1)[0], (k0 + k) % per))
        b_spec = pl.BlockSpec((tj, tk), lambda g0, g1, k: (ij(g0, g1)[1], k0 + k))
        dn = (((1,), (1,)), ((), ()))
    else:
        a_spec = pl.BlockSpec((tk, ti), lambda g0, g1, k: (k0 + k, ij(g0, g1)[0]))
        b_spec = pl.BlockSpec((tk, tj), lambda g0, g1, k: (k0 + k, ij(g0, g1)[1]))
        if halved == "b":
            per = J // 2 // tj
            b_spec = pl.BlockSpec((None, tk, tj), lambda g0, g1, k: (ij(g0, g1)[1] // per, k0 + k, ij(g0, g1)[1] % per))
        dn = (((0,), (0,)), ((), ()))
    in_specs = [a_spec, b_spec]
    operands = [a, b]
    if resid is not None:
        in_specs.append(pl.BlockSpec((ti, tj), lambda g0, g1, k: ij(g0, g1)))
        operands.append(resid)
    if dev_major:
        out_spec = pl.BlockSpec((None, ti, tj), lambda g0, g1, k: (ij(g0, g1)[1], ij(g0, g1)[0], 0))
        out_shape = jax.ShapeDtypeStruct((nj, I, tj), out_dtype)
    else:
        out_spec = pl.BlockSpec((ti, tj), lambda g0, g1, k: ij(g0, g1))
        out_shape = jax.ShapeDtypeStruct((I, J), out_dtype)

    grid = (nj, ni, nk) if j_outer else (ni, nj, nk)
    n_in = len(operands)
    n_comm = 0
    out_specs, out_shapes = [out_spec], [out_shape]
    scratch = [pltpu.VMEM((ti, tj), F32)] if nk > 1 else []
    if comm is not None:
        plan, comm_in, comm_out = comm
        n_comm = len(comm_in)
        operands += list(comm_in)
        in_specs += [ANY] * n_comm
        out_specs += [ANY] * n_comm
        out_shapes += list(comm_out)
        scratch += _comm_scratch(n_comm)

    def body(*refs):
        a_ref, b_ref = refs[0], refs[1]
        r_ref = refs[2] if resid is not None else None
        o_ref = refs[n_in + n_comm]
        if comm is not None:
            start, finish_comm = plan(refs[n_in:n_in + n_comm], refs[n_in + n_comm + 1:n_in + 2 * n_comm + 1], *refs[-3:])
            steps = [pl.program_id(axis) for axis in range(3)]

            @pl.when((steps[0] == 0) & (steps[1] == 0) & (steps[2] == 0))
            def _():
                start()

        part = lax.dot_general(a_ref[...].astype(BF16), b_ref[...].astype(BF16), dn, preferred_element_type=F32)

        def finish(acc):
            if r_ref is not None:
                acc = acc + r_ref[...]
            o_ref[...] = acc.astype(o_ref.dtype)

        if nk == 1:
            finish(part)
        else:
            acc_ref = refs[n_in + 2 * n_comm + 1]
            k = pl.program_id(2)

            @pl.when(k == 0)
            def _():
                acc_ref[...] = part

            @pl.when(k > 0)
            def _():
                acc_ref[...] += part

            @pl.when(k == nk - 1)
            def _():
                finish(acc_ref[...])

        if comm is not None:
            @pl.when((steps[0] == grid[0] - 1) & (steps[1] == grid[1] - 1) & (steps[2] == grid[2] - 1))
            def _():
                finish_comm()

    res = pl.pallas_call(
        body, name=name, grid=grid,
        in_specs=in_specs, out_specs=out_specs, out_shape=out_shapes,
        scratch_shapes=scratch, compiler_params=_params(),
    )(*operands)
    return res[0] if comm is None else res


def _rms_fwd(x, g, name):
    s, d = x.shape
    ts = _tile(s, 256, 16)

    def body(x_ref, g_ref, h_ref):
        xv = x_ref[...]
        r = lax.rsqrt(jnp.mean(xv * xv, axis=-1, keepdims=True) + EPS)
        h_ref[...] = (xv * r * g_ref[...]).astype(BF16)

    return pl.pallas_call(
        body, name=name, grid=(s // ts,),
        in_specs=[pl.BlockSpec((ts, d), lambda i: (i, 0)), pl.BlockSpec((1, d), lambda i: (0, 0))],
        out_specs=pl.BlockSpec((ts, d), lambda i: (i, 0)),
        out_shape=jax.ShapeDtypeStruct((s, d), BF16), compiler_params=_params(),
    )(x, g)


def _rms_bwd(x, g, dh, dres, name):
    s, d = x.shape
    ts = _tile(s, 256, 16)

    def body(x_ref, g_ref, dh_ref, dres_ref, dx_ref, dxb_ref, dg_ref):
        xv = x_ref[...]
        r = lax.rsqrt(jnp.mean(xv * xv, axis=-1, keepdims=True) + EPS)
        y = xv * r
        dhv = dh_ref[...]
        gd = dhv * g_ref[...]
        dxv = dres_ref[...] + r * (gd - y * jnp.mean(gd * y, axis=-1, keepdims=True))
        dx_ref[...] = dxv
        dxb_ref[...] = dxv.astype(BF16)

        @pl.when(pl.program_id(0) == 0)
        def _():
            dg_ref[...] = jnp.zeros_like(dg_ref)

        dg_ref[0:1, :] += jnp.sum(dhv * y, axis=0, keepdims=True)

    blk = pl.BlockSpec((ts, d), lambda i: (i, 0))
    return pl.pallas_call(
        body, name=name, grid=(s // ts,),
        in_specs=[blk, pl.BlockSpec((1, d), lambda i: (0, 0)), blk, blk],
        out_specs=[blk, blk, pl.BlockSpec((8, d), lambda i: (0, 0))],
        out_shape=[jax.ShapeDtypeStruct((s, d), F32), jax.ShapeDtypeStruct((s, d), BF16), jax.ShapeDtypeStruct((8, d), F32)],
        compiler_params=_params(),
    )(x, g, dh, dres)


def _head_norm(t, gain):
    r = lax.rsqrt(jnp.mean(t * t, axis=-1, keepdims=True) + EPS)
    return t * r * gain


def _head_norm_bwd(t, gain, dn):
    r = lax.rsqrt(jnp.mean(t * t, axis=-1, keepdims=True) + EPS)
    y = t * r
    gd = dn * gain
    dt = r * (gd - y * jnp.mean(gd * y, axis=-1, keepdims=True))
    return dt, jnp.sum(dn * y, axis=0, keepdims=True)


def _rope(n, cos, sin):
    return n * cos + pltpu.roll(n, HEAD // 2, axis=1) * sin


def _rope_bwd(do, cos, sin):
    return do * cos + pltpu.roll(do * sin, HEAD // 2, axis=1)


def _qkv_fwd(proj, gains, cos, sin, cfg, name):
    s, pw = proj.shape
    ha, hq, hkv = cfg
    ts = _tile(s, 256, 16)

    def body(p_ref, gn_ref, cos_ref, sin_ref, qa_ref, ka_ref, va_ref, qb_ref, kb_ref, vb_ref):
        cosv, sinv = cos_ref[...], sin_ref[...]
        col = 0
        for out_ref, nh, gi, rot in ((qa_ref, ha, 0, False), (ka_ref, ha, 1, False), (va_ref, ha, None, False),
                                     (qb_ref, hq, 2, True), (kb_ref, hkv, 3, True), (vb_ref, hkv, None, False)):
            for h in range(nh):
                t = p_ref[:, col * HEAD:(col + 1) * HEAD]
                if gi is not None:
                    t = _head_norm(t, gn_ref[gi:gi + 1, :])
                if rot:
                    t = _rope(t, cosv, sinv)
                out_ref[h] = t.astype(BF16)
                col += 1

    def hm(nh):
        return pl.BlockSpec((nh, ts, HEAD), lambda i: (0, i, 0)), jax.ShapeDtypeStruct((nh, s, HEAD), BF16)

    specs, shapes = zip(hm(ha), hm(ha), hm(ha), hm(hq), hm(hkv), hm(hkv))
    tok = pl.BlockSpec((ts, HEAD), lambda i: (i, 0))
    return pl.pallas_call(
        body, name=name, grid=(s // ts,),
        in_specs=[pl.BlockSpec((ts, pw), lambda i: (i, 0)), pl.BlockSpec((8, HEAD), lambda i: (0, 0)), tok, tok],
        out_specs=list(specs), out_shape=list(shapes), compiler_params=_params(),
    )(proj, gains, cos, sin)


def _qkv_bwd(proj, gains, cos, sin, grads, cfg, name):
    s, pw = proj.shape
    ha, hq, hkv = cfg
    ts = _tile(s, 256, 16)

    def body(p_ref, gn_ref, cos_ref, sin_ref, dqa, dka, dva, dqb, dkb, dvb, dp_ref, dgn_ref):
        cosv, sinv = cos_ref[...], sin_ref[...]

        @pl.when(pl.program_id(0) == 0)
        def _():
            dgn_ref[...] = jnp.zeros_like(dgn_ref)

        col = 0
        for d_ref, nh, gi, rot in ((dqa, ha, 0, False), (dka, ha, 1, False), (dva, ha, None, False),
                                   (dqb, hq, 2, True), (dkb, hkv, 3, True), (dvb, hkv, None, False)):
            dgain = jnp.zeros((1, HEAD), F32)
            for h in range(nh):
                dt = d_ref[h]
                if rot:
                    dt = _rope_bwd(dt, cosv, sinv)
                if gi is not None:
                    dt, dg = _head_norm_bwd(p_ref[:, col * HEAD:(col + 1) * HEAD], gn_ref[gi:gi + 1, :], dt)
                    dgain = dgain + dg
                dp_ref[:, col * HEAD:(col + 1) * HEAD] = dt.astype(BF16)
                col += 1
            if gi is not None:
                dgn_ref[gi:gi + 1, :] += dgain

    def hm(nh):
        return pl.BlockSpec((nh, ts, HEAD), lambda i: (0, i, 0))

    tok = pl.BlockSpec((ts, HEAD), lambda i: (i, 0))
    small = pl.BlockSpec((8, HEAD), lambda i: (0, 0))
    return pl.pallas_call(
        body, name=name, grid=(s // ts,),
        in_specs=[pl.BlockSpec((ts, pw), lambda i: (i, 0)), small, tok, tok,
                  hm(ha), hm(ha), hm(ha), hm(hq), hm(hkv), hm(hkv)],
        out_specs=[pl.BlockSpec((ts, pw), lambda i: (i, 0)), small],
        out_shape=[jax.ShapeDtypeStruct((s, pw), BF16), jax.ShapeDtypeStruct((8, HEAD), F32)],
        compiler_params=_params(),
    )(proj, gains, cos, sin, *grads)


NA_QROWS = 16
NA_KEYS = WIN_R * GRID_W
N_DR = 2 * WIN_R - 1
N_DC = 2 * WIN_C - 1


def _na_bias(rpb_flat, n_heads, name):
    def body(rpb_ref, tb_ref):
        h = pl.program_id(0)
        qi = lax.broadcasted_iota(jnp.int32, (GRID_W, LANES), 0)
        lane = lax.broadcasted_iota(jnp.int32, (GRID_W, LANES), 1)
        kk = lane & (GRID_W - 1)
        upper = lane >= GRID_W
        dcm = kk - qi + (WIN_C - 1)
        cs = jnp.clip(qi - WIN_C // 2, 0, GRID_W - WIN_C)
        valid = (kk >= cs) & (kk < cs + WIN_C)
        base = h * (N_DR * N_DC)
        for dra in range(N_DR - 1):
            def step(j, acc, dra=dra):
                va = rpb_ref[base + dra * N_DC + j]
                vb = rpb_ref[base + (dra + 1) * N_DC + j]
                return jnp.where(dcm == j, jnp.where(upper, vb, va), acc)

            pair = lax.fori_loop(0, N_DC, step, jnp.zeros((GRID_W, LANES), F32))
            pair = jnp.where(valid, pair, NEG)
            for dr0 in range(WIN_R):
                wp, odd = divmod(dra - dr0, 2)
                if odd == 0 and 0 <= wp < WIN_R // 2:
                    tb_ref[0, dr0, :, wp * LANES:(wp + 1) * LANES] = pair

    return pl.pallas_call(
        body, name=name, grid=(n_heads,),
        in_specs=[SMEM],
        out_specs=pl.BlockSpec((1, WIN_R, GRID_W, NA_KEYS), lambda h: (h, 0, 0, 0)),
        out_shape=jax.ShapeDtypeStruct((n_heads, WIN_R, GRID_W, NA_KEYS), F32),
        compiler_params=_params(),
    )(rpb_flat)


def _na_row(b, i, nrows):
    r = b * NA_QROWS + i
    rs = jnp.clip(r - WIN_R // 2, 0, nrows - WIN_R)
    return pl.ds(pl.multiple_of(rs * GRID_W, GRID_W), NA_KEYS), rs - r + (WIN_R - 1)


def _softmax(s):
    e = jnp.exp(s - jnp.max(s, axis=-1, keepdims=True))
    return e * (1.0 / jnp.sum(e, axis=-1, keepdims=True))


_NT = (((1,), (1,)), ((), ()))
_NN = (((1,), (0,)), ((), ()))
_TN = (((0,), (0,)), ((), ()))


def _dot(a, b, dn):
    return lax.dot_general(a, b, dn, preferred_element_type=F32)


def _call(body, comm, *, name, grid, operands, in_specs, out_specs, out_shape, scratch_shapes):
    if comm is not None:
        body, more_operands, more_specs, more_shapes, sems = _host_exchange(body, comm, grid, len(operands), len(out_shape))
        operands = operands + more_operands
        in_specs = in_specs + more_specs
        out_specs = out_specs + more_specs
        out_shape = out_shape + more_shapes
        scratch_shapes = scratch_shapes + sems
    res = pl.pallas_call(body, name=name, grid=grid, in_specs=in_specs, out_specs=out_specs, out_shape=out_shape,
                         scratch_shapes=scratch_shapes, compiler_params=_params())(*operands)
    return res[0] if len(res) == 1 else res


def _na_fwd(q, k, v, tb, name, comm=None):
    nh, s, _ = q.shape
    nrows = s // GRID_W
    tq = NA_QROWS * GRID_W

    def body(q_ref, k_ref, v_ref, tb_ref, o_ref, s_scr, p_scr):
        b = pl.program_id(1)
        rows = [slice(i * GRID_W, (i + 1) * GRID_W) for i in range(NA_QROWS)]
        at = [_na_row(b, i, nrows) for i in range(NA_QROWS)]
        for i, (keys, dr0) in enumerate(at):
            s_scr[i] = _dot(q_ref[rows[i], :], k_ref[keys, :], _NT) * SCALE + tb_ref[0, dr0]
        for i in range(NA_QROWS):
            p_scr[i] = _softmax(s_scr[i]).astype(BF16)
        for i, (keys, _) in enumerate(at):
            o_ref[rows[i], :] = _dot(p_scr[i], v_ref[keys, :], _NN)

    qspec = pl.BlockSpec((None, tq, HEAD), lambda h, b: (h, b, 0))
    full = pl.BlockSpec((None, s, HEAD), lambda h, b: (h, 0, 0))
    return _call(
        body, comm, name=name, grid=(nh, nrows // NA_QROWS), operands=[q, k, v, tb],
        in_specs=[qspec, full, full, pl.BlockSpec((1, WIN_R, GRID_W, NA_KEYS), lambda h, b: (h, 0, 0, 0))],
        out_specs=[qspec], out_shape=[jax.ShapeDtypeStruct((nh, s, HEAD), F32)],
        scratch_shapes=[pltpu.VMEM((NA_QROWS, GRID_W, NA_KEYS), F32), pltpu.VMEM((NA_QROWS, GRID_W, NA_KEYS), BF16)])


def _na_bwd(q, k, v, tb, do, name, comm=None):
    nh, s, _ = q.shape
    nrows = s // GRID_W
    tq = NA_QROWS * GRID_W

    def body(q_ref, do_ref, k_ref, v_ref, tb_ref, dq_ref, dk_ref, dv_ref, dtb_ref, s_scr, dp_scr, p_scr, ds_scr):
        b = pl.program_id(1)

        @pl.when(b == 0)
        def _():
            dk_ref[...] = jnp.zeros_like(dk_ref)
            dv_ref[...] = jnp.zeros_like(dv_ref)
            dtb_ref[...] = jnp.zeros_like(dtb_ref)

        rows = [slice(i * GRID_W, (i + 1) * GRID_W) for i in range(NA_QROWS)]
        at = [_na_row(b, i, nrows) for i in range(NA_QROWS)]
        for i, (keys, dr0) in enumerate(at):
            s_scr[i] = _dot(q_ref[rows[i], :], k_ref[keys, :], _NT) * SCALE + tb_ref[0, dr0]
            dp_scr[i] = _dot(do_ref[rows[i], :], v_ref[keys, :], _NT)
        for i in range(NA_QROWS):
            p = _softmax(s_scr[i])
            dp = dp_scr[i]
            ds = p * (dp - jnp.sum(p * dp, axis=-1, keepdims=True))
            p_scr[i] = p.astype(BF16)
            s_scr[i] = ds
            ds_scr[i] = (ds * SCALE).astype(BF16)
        for i, (keys, _) in enumerate(at):
            dq_ref[rows[i], :] = _dot(ds_scr[i], k_ref[keys, :], _NN)
        for i, (keys, dr0) in enumerate(at):
            dv_ref[keys, :] += _dot(p_scr[i], do_ref[rows[i], :], _TN)
            dk_ref[keys, :] += _dot(ds_scr[i], q_ref[rows[i], :], _TN)
            dtb_ref[0, dr0] += s_scr[i]

    qspec = pl.BlockSpec((None, tq, HEAD), lambda h, b: (h, b, 0))
    full = pl.BlockSpec((None, s, HEAD), lambda h, b: (h, 0, 0))
    tbs = pl.BlockSpec((1, WIN_R, GRID_W, NA_KEYS), lambda h, b: (h, 0, 0, 0))
    hm = jax.ShapeDtypeStruct((nh, s, HEAD), F32)
    tile = (NA_QROWS, GRID_W, NA_KEYS)
    return _call(
        body, comm, name=name, grid=(nh, nrows // NA_QROWS), operands=[q, do, k, v, tb],
        in_specs=[qspec, qspec, full, full, tbs],
        out_specs=[qspec, full, full, tbs],
        out_shape=[hm, hm, hm, jax.ShapeDtypeStruct((nh, WIN_R, GRID_W, NA_KEYS), F32)],
        scratch_shapes=[pltpu.VMEM(tile, F32), pltpu.VMEM(tile, F32), pltpu.VMEM(tile, BF16), pltpu.VMEM(tile, BF16)])


def _rpb_fold(y, n_heads, name):
    def body(y_ref, o_ref):
        for h in range(n_heads):
            for dr in range(2 * WIN_R):
                acc = jnp.zeros((1, LANES), F32)
                for dr0 in range(WIN_R):
                    w = dr - dr0
                    if 0 <= w < WIN_R:
                        acc = acc + y_ref[h, dr0, w:w + 1, :]
                o_ref[h, dr:dr + 1, :] = acc

    return pl.pallas_call(
        body, name=name, out_shape=jax.ShapeDtypeStruct((n_heads, 2 * WIN_R, LANES), F32),
    )(y)


def _rpb_grad(dtb, onehot, name):
    nh = dtb.shape[0]
    rows = dtb.reshape(nh, WIN_R, GRID_W, WIN_R, GRID_W).transpose(0, 1, 3, 2, 4).reshape(nh * WIN_R * WIN_R, GRID_W * GRID_W)
    y = _matmul(rows, onehot, dims="nn", ti=rows.shape[0], tj=LANES, tk=GRID_W * GRID_W, out_dtype=F32, name=name + "_dc")
    folded = _rpb_fold(y.reshape(nh, WIN_R, WIN_R, LANES), nh, name + "_dr")
    return folded[:, :N_DR, :N_DC]


WA_WIN_TOK = 3 * BAND


def _wa_scores(q, kwin, t0, j, sink_ref, head0, grp):
    rows = grp * BAND
    s = _dot(q, kwin, _NT) * SCALE
    row = lax.broadcasted_iota(jnp.int32, (rows, WA_WIN_TOK), 0)
    qpos = j * BAND + (row & (BAND - 1))
    kpos = t0 + lax.broadcasted_iota(jnp.int32, (rows, WA_WIN_TOK), 1)
    s = jnp.where(jnp.abs(kpos - qpos) <= BAND, s, NEG)
    head = lax.broadcasted_iota(jnp.int32, (rows, 1), 0) // BAND
    sink = jnp.zeros((rows, 1), F32) + sink_ref[head0]
    for g in range(1, grp):
        sink = jnp.where(head == g, sink_ref[head0 + g], sink)
    m = jnp.maximum(jnp.max(s, axis=-1, keepdims=True), sink)
    e = jnp.exp(s - m)
    es = jnp.exp(sink - m)
    rz = 1.0 / (jnp.sum(e, axis=-1, keepdims=True) + es)
    return e * rz, es * rz


def _wa_window(j, s):
    return pl.multiple_of(jnp.clip((j - 1) * BAND, 0, s - WA_WIN_TOK), BAND)


def _wa_fwd(q, k, v, sink, name, comm=None):
    hq, s, _ = q.shape
    hkv = k.shape[0]
    grp = hq // hkv

    def body(sink_ref, q_ref, k_ref, v_ref, o_ref):
        kh, j = pl.program_id(0), pl.program_id(1)
        t0 = _wa_window(j, s)
        keys = pl.ds(t0, WA_WIN_TOK)
        p, _ = _wa_scores(q_ref[...].reshape(grp * BAND, HEAD), k_ref[keys, :], t0, j, sink_ref, kh * grp, grp)
        o_ref[...] = _dot(p.astype(BF16), v_ref[keys, :], _NN).reshape(grp, BAND, HEAD)

    qspec = pl.BlockSpec((grp, BAND, HEAD), lambda kh, j: (kh, j, 0))
    full = pl.BlockSpec((None, s, HEAD), lambda kh, j: (kh, 0, 0))
    return _call(
        body, comm, name=name, grid=(hkv, s // BAND), operands=[sink, q, k, v],
        in_specs=[SMEM, qspec, full, full],
        out_specs=[qspec], out_shape=[jax.ShapeDtypeStruct((hq, s, HEAD), F32)], scratch_shapes=[])


def _wa_bwd(q, k, v, sink, do, name, comm=None):
    hq, s, _ = q.shape
    hkv = k.shape[0]
    grp = hq // hkv

    def body(sink_ref, q_ref, do_ref, k_ref, v_ref, dq_ref, dk_ref, dv_ref, dsink_ref):
        kh, j = pl.program_id(0), pl.program_id(1)

        @pl.when(j == 0)
        def _():
            dk_ref[...] = jnp.zeros_like(dk_ref)
            dv_ref[...] = jnp.zeros_like(dv_ref)
            dsink_ref[...] = jnp.zeros_like(dsink_ref)

        t0 = _wa_window(j, s)
        keys = pl.ds(t0, WA_WIN_TOK)
        qs = q_ref[...].reshape(grp * BAND, HEAD)
        dos = do_ref[...].reshape(grp * BAND, HEAD)
        kwin, vwin = k_ref[keys, :], v_ref[keys, :]
        p, ps = _wa_scores(qs, kwin, t0, j, sink_ref, kh * grp, grp)
        dp = _dot(dos, vwin, _NT)
        dv_ref[keys, :] += _dot(p.astype(BF16), dos, _TN)
        rowdot = jnp.sum(p * dp, axis=-1, keepdims=True)
        to_sink = ps * rowdot
        for g in range(grp):
            dsink_ref[g] += jnp.zeros((8, LANES), F32) - jnp.sum(to_sink[g * BAND:(g + 1) * BAND])
        dss = (p * (dp - rowdot) * SCALE).astype(BF16)
        dq_ref[...] = _dot(dss, kwin, _NN).reshape(grp, BAND, HEAD)
        dk_ref[keys, :] += _dot(dss, qs, _TN)

    qspec = pl.BlockSpec((grp, BAND, HEAD), lambda kh, j: (kh, j, 0))
    full = pl.BlockSpec((None, s, HEAD), lambda kh, j: (kh, 0, 0))
    kv = jax.ShapeDtypeStruct((hkv, s, HEAD), F32)
    return _call(
        body, comm, name=name, grid=(hkv, s // BAND), operands=[sink, q, do, k, v],
        in_specs=[SMEM, qspec, qspec, full, full],
        out_specs=[qspec, full, full, pl.BlockSpec((grp, 8, LANES), lambda kh, j: (kh, 0, 0))],
        out_shape=[jax.ShapeDtypeStruct((hq, s, HEAD), F32), kv, kv, jax.ShapeDtypeStruct((hq, 8, LANES), F32)],
        scratch_shapes=[])


def _onorm_fwd(oa, ob, gains, name):
    ha, s, _ = oa.shape
    hq = ob.shape[0]
    ts = _tile(s, 256, 16)

    def body(oa_ref, ob_ref, g_ref, o_ref):
        col = 0
        for ref, nh in ((oa_ref, ha), (ob_ref, hq)):
            ss = sum(jnp.sum(ref[h] * ref[h], axis=-1, keepdims=True) for h in range(nh))
            r = lax.rsqrt(ss / (nh * HEAD) + EPS)
            for h in range(nh):
                o_ref[:, col * HEAD:(col + 1) * HEAD] = (ref[h] * r * g_ref[:, col * HEAD:(col + 1) * HEAD]).astype(BF16)
                col += 1

    mix = (ha + hq) * HEAD
    return pl.pallas_call(
        body, name=name, grid=(s // ts,),
        in_specs=[pl.BlockSpec((ha, ts, HEAD), lambda i: (0, i, 0)), pl.BlockSpec((hq, ts, HEAD), lambda i: (0, i, 0)),
                  pl.BlockSpec((1, mix), lambda i: (0, 0))],
        out_specs=pl.BlockSpec((ts, mix), lambda i: (i, 0)),
        out_shape=jax.ShapeDtypeStruct((s, mix), BF16), compiler_params=_params(),
    )(oa, ob, gains)


def _onorm_bwd(oa, ob, gains, don, name):
    ha, s, _ = oa.shape
    hq = ob.shape[0]
    ts = _tile(s, 256, 16)
    mix = (ha + hq) * HEAD

    def body(oa_ref, ob_ref, g_ref, don_ref, doa_ref, dob_ref, dg_ref):
        @pl.when(pl.program_id(0) == 0)
        def _():
            dg_ref[...] = jnp.zeros_like(dg_ref)

        col0 = 0
        for ref, d_ref, nh in ((oa_ref, doa_ref, ha), (ob_ref, dob_ref, hq)):
            ss = sum(jnp.sum(ref[h] * ref[h], axis=-1, keepdims=True) for h in range(nh))
            r = lax.rsqrt(ss / (nh * HEAD) + EPS)
            dot = jnp.zeros((ts, 1), F32)
            for h in range(nh):
                cols = slice((col0 + h) * HEAD, (col0 + h + 1) * HEAD)
                dot = dot + jnp.sum(don_ref[:, cols] * g_ref[:, cols] * ref[h], axis=-1, keepdims=True)
            mean = dot * r / (nh * HEAD)
            for h in range(nh):
                cols = slice((col0 + h) * HEAD, (col0 + h + 1) * HEAD)
                y = ref[h] * r
                dn = don_ref[:, cols]
                d_ref[h] = (r * (dn * g_ref[:, cols] - y * mean)).astype(BF16)
                dg_ref[0:1, cols] += jnp.sum(dn * y, axis=0, keepdims=True)
            col0 += nh

    return pl.pallas_call(
        body, name=name, grid=(s // ts,),
        in_specs=[pl.BlockSpec((ha, ts, HEAD), lambda i: (0, i, 0)), pl.BlockSpec((hq, ts, HEAD), lambda i: (0, i, 0)),
                  pl.BlockSpec((1, mix), lambda i: (0, 0)), pl.BlockSpec((ts, mix), lambda i: (i, 0))],
        out_specs=[pl.BlockSpec((ha, ts, HEAD), lambda i: (0, i, 0)), pl.BlockSpec((hq, ts, HEAD), lambda i: (0, i, 0)),
                   pl.BlockSpec((8, mix), lambda i: (0, 0))],
        out_shape=[jax.ShapeDtypeStruct((ha, s, HEAD), BF16), jax.ShapeDtypeStruct((hq, s, HEAD), BF16),
                   jax.ShapeDtypeStruct((8, mix), F32)],
        compiler_params=_params(),
    )(oa, ob, gains, don)


HALO = 8
PACKED = 16


def _shift_rows(cur, halo_prev, halo_next, i, n):
    ts = cur.shape[0]
    row = lax.broadcasted_iota(jnp.int32, cur.shape, 0)
    first = jnp.where(i > 0, halo_prev[HALO - 1:HALO, :], 0.0)
    last = jnp.where(i < n - 1, halo_next[0:1, :], 0.0)
    prev = jnp.where(row == 0, first, pltpu.roll(cur, 1, axis=0))
    nxt = jnp.where(row == ts - 1, last, pltpu.roll(cur, ts - 1, axis=0))
    return prev, nxt


def _halo_specs(ts, tc, col_off):
    per = ts // HALO
    cur = pl.BlockSpec((ts, tc), lambda j, i: (i, j + col_off))
    prev = pl.BlockSpec((HALO, tc), lambda j, i: (jnp.maximum(i * per - 1, 0), j + col_off))

    def nxt_map(n_blocks):
        return pl.BlockSpec((HALO, tc), lambda j, i: (jnp.minimum((i + 1) * per, n_blocks - 1), j + col_off))

    return cur, prev, nxt_map


def _sigmoid(x):
    return 1.0 / (1.0 + jnp.exp(-x))


def _ffn_tiles(s, f):
    return _tile(s, 512, 16), _tile(f, 512, LANES)


def _gate_fwd(u, cw, cb, f, name):
    s = u.shape[0]
    ts, tc = _ffn_tiles(s, f)
    nj, ni = f // tc, s // ts

    def body(g_ref, gp_ref, gn_ref, u_ref, up_ref, un_ref, wg_ref, wu_ref, bg_ref, bu_ref, a_ref, gu_ref):
        i = pl.program_id(1)

        def conv(c_ref, p_ref, n_ref, w_ref, b_ref):
            cur = c_ref[...]
            prev, nxt = _shift_rows(cur, p_ref[...], n_ref[...], i, ni)
            return prev * w_ref[0:1, :] + cur * w_ref[1:2, :] + nxt * w_ref[2:3, :] + b_ref[...]

        gate = conv(g_ref, gp_ref, gn_ref, wg_ref, bg_ref)
        up = conv(u_ref, up_ref, un_ref, wu_ref, bu_ref)
        gu_ref[0] = gate.astype(BF16)
        gu_ref[1] = up.astype(BF16)
        a_ref[...] = (gate * _sigmoid(gate) * up).astype(BF16)

    gc, gp, gn = _halo_specs(ts, tc, 0)
    uc, up_, un = _halo_specs(ts, tc, nj)
    wg = pl.BlockSpec((3, tc), lambda j, i: (0, j))
    wu = pl.BlockSpec((3, tc), lambda j, i: (0, j + nj))
    bg = pl.BlockSpec((1, tc), lambda j, i: (0, j))
    bu = pl.BlockSpec((1, tc), lambda j, i: (0, j + nj))
    return pl.pallas_call(
        body, name=name, grid=(nj, ni),
        in_specs=[gc, gp, gn(s // HALO), uc, up_, un(s // HALO), wg, wu, bg, bu],
        out_specs=[pl.BlockSpec((ts, tc), lambda j, i: (i, j)), pl.BlockSpec((2, ts, tc), lambda j, i: (0, i, j))],
        out_shape=[jax.ShapeDtypeStruct((s, f), BF16), jax.ShapeDtypeStruct((2, s, f), BF16)], compiler_params=_params(),
    )(u, u, u, u, u, u, cw, cw, cb, cb)


def _ffn_bwd(gu, u, da, cw, name):
    _, s, f = gu.shape
    ts, tc = _ffn_tiles(s, f)
    nj, ni = f // tc, s // ts

    def body(gu_ref, gup_ref, gun_ref, da_ref, dap_ref, dan_ref, xg_ref, xu_ref, wg_ref, wu_ref,
             du_ref, dcw_ref, dcb_ref):
        i = pl.program_id(1)

        @pl.when(i == 0)
        def _():
            dcw_ref[...] = jnp.zeros_like(dcw_ref)
            dcb_ref[...] = jnp.zeros_like(dcb_ref)

        rows = ts + 2 * HALO
        mid = slice(HALO, HALO + ts)
        da = jnp.concatenate([jnp.where(i > 0, dap_ref[...], 0.0), da_ref[...], jnp.where(i < ni - 1, dan_ref[...], 0.0)], axis=0)
        def rows_of(half):
            before = gup_ref[half].astype(F32)[PACKED - HALO:]
            after = gun_ref[half].astype(F32)[:HALO]
            return jnp.concatenate([before, gu_ref[half].astype(F32), after], axis=0)

        gate, up = rows_of(0), rows_of(1)
        sg = _sigmoid(gate)
        d_up = da * gate * sg
        d_gate = da * up * (sg * (1.0 + gate * (1.0 - sg)))
        for half, (dd, x_ref, w_ref) in enumerate(((d_gate, xg_ref, wg_ref), (d_up, xu_ref, wu_ref))):
            before = pltpu.roll(dd, 1, axis=0)
            after = pltpu.roll(dd, rows - 1, axis=0)
            du_ref[half] = (before * w_ref[2:3, :] + dd * w_ref[1:2, :] + after * w_ref[0:1, :])[mid].astype(BF16)
            x = x_ref[...]
            dcb_ref[half, 0:1, :] += jnp.sum(dd[mid], axis=0, keepdims=True)
            for k, shifted in enumerate((after, dd, before)):
                dcw_ref[half, k, 0:1, :] += jnp.sum(shifted[mid] * x, axis=0, keepdims=True)

    per = ts // PACKED
    cur3 = pl.BlockSpec((2, ts, tc), lambda j, i: (0, i, j))
    prev3 = pl.BlockSpec((2, PACKED, tc), lambda j, i: (0, jnp.maximum(i * per - 1, 0), j))
    next3 = pl.BlockSpec((2, PACKED, tc), lambda j, i: (0, jnp.minimum((i + 1) * per, s // PACKED - 1), j))
    cur, prev, nxt = _halo_specs(ts, tc, 0)
    return pl.pallas_call(
        body, name=name, grid=(nj, ni),
        in_specs=[cur3, prev3, next3, cur, prev, nxt(s // HALO),
                  pl.BlockSpec((ts, tc), lambda j, i: (i, j)), pl.BlockSpec((ts, tc), lambda j, i: (i, j + nj)),
                  pl.BlockSpec((3, tc), lambda j, i: (0, j)), pl.BlockSpec((3, tc), lambda j, i: (0, j + nj))],
        out_specs=[cur3, pl.BlockSpec((2, 3, 8, tc), lambda j, i: (0, 0, 0, j)),
                   pl.BlockSpec((2, 8, tc), lambda j, i: (0, 0, j))],
        out_shape=[jax.ShapeDtypeStruct((2, s, f), BF16),
                   jax.ShapeDtypeStruct((2, 3, 8, f), F32), jax.ShapeDtypeStruct((2, 8, f), F32)],
        compiler_params=_params(),
    )(gu, gu, gu, da, da, da, u, u, cw, cw)


def _loss_head(y, target, name):
    s, d = y.shape
    ts = _tile(s, 256, 16)

    def body(y_ref, t_ref, dy_ref, dyb_ref, l_ref):
        @pl.when(pl.program_id(0) == 0)
        def _():
            l_ref[...] = jnp.zeros_like(l_ref)

        err = y_ref[...] - t_ref[...]
        dy = err / d
        dy_ref[...] = dy
        dyb_ref[...] = dy.astype(BF16)
        l_ref[...] += jnp.zeros((8, LANES), F32) + 0.5 * jnp.sum(jnp.sum(err * err, axis=-1, keepdims=True) / d)

    blk = pl.BlockSpec((ts, d), lambda i: (i, 0))
    return pl.pallas_call(
        body, name=name, grid=(s // ts,),
        in_specs=[blk, blk], out_specs=[blk, blk, pl.BlockSpec((8, LANES), lambda i: (0, 0))],
        out_shape=[jax.ShapeDtypeStruct((s, d), F32), jax.ShapeDtypeStruct((s, d), BF16), jax.ShapeDtypeStruct((8, LANES), F32)],
        compiler_params=_params(),
    )(y, target)


SMALL = ("ln1_g", "qn_a", "kn_a", "rpb", "qn_b", "kn_b", "sink", "on_a", "on_b", "ln2_g", "conv_b", "conv_w")
PACK_ALIGN = 8 * LANES


def _pack(arrays):
    flat = []
    for a in arrays:
        a = a.reshape(-1)
        flat.append(jnp.pad(a, (0, -a.size % PACK_ALIGN)))
    return jnp.concatenate(flat).reshape(-1, LANES)


def _unpack(packed, like):
    out, at = [], 0
    flat = packed.reshape(-1)
    for a in like:
        out.append(flat[at:at + a.size].reshape(a.shape))
        at += a.size + (-a.size % PACK_ALIGN)
    return out


def _matmul_tiles(s, k, j):
    return dict(ti=_tile(s, 512, 16), tj=_tile(j, 1536, LANES), tk=_tile(k, 2048, LANES))


def kernel(x, positions, ln1_g, w_in, qn_a, kn_a, rpb, qn_b, kn_b, sink, on_a, on_b, w_out, ln2_g, w_up, conv_w, conv_b, w_down, loss_target, m_ln1_g, m_w_in, m_qn_a, m_kn_a, m_rpb, m_qn_b, m_kn_b, m_sink, m_on_a, m_on_b, m_w_out, m_ln2_g, m_w_up, m_conv_w, m_conv_b, m_w_down, v_ln1_g, v_w_in, v_qn_a, v_kn_a, v_rpb, v_qn_b, v_kn_b, v_sink, v_on_a, v_on_b, v_w_out, v_ln2_g, v_w_up, v_conv_w, v_conv_b, v_w_down):
    weights = dict(ln1_g=ln1_g, w_in=w_in, qn_a=qn_a, kn_a=kn_a, rpb=rpb, qn_b=qn_b, kn_b=kn_b, sink=sink, on_a=on_a,
                   on_b=on_b, w_out=w_out, ln2_g=ln2_g, w_up=w_up, conv_w=conv_w, conv_b=conv_b, w_down=w_down)
    mom1 = dict(ln1_g=m_ln1_g, w_in=m_w_in, qn_a=m_qn_a, kn_a=m_kn_a, rpb=m_rpb, qn_b=m_qn_b, kn_b=m_kn_b, sink=m_sink,
                on_a=m_on_a, on_b=m_on_b, w_out=m_w_out, ln2_g=m_ln2_g, w_up=m_w_up, conv_w=m_conv_w, conv_b=m_conv_b,
                w_down=m_w_down)
    mom2 = dict(ln1_g=v_ln1_g, w_in=v_w_in, qn_a=v_qn_a, kn_a=v_kn_a, rpb=v_rpb, qn_b=v_qn_b, kn_b=v_kn_b, sink=v_sink,
                on_a=v_on_a, on_b=v_on_b, w_out=v_w_out, ln2_g=v_ln2_g, w_up=v_w_up, conv_w=v_conv_w, conv_b=v_conv_b,
                w_down=v_w_down)
    order = ("ln1_g", "w_in", "qn_a", "kn_a", "rpb", "qn_b", "kn_b", "sink", "on_a", "on_b", "w_out", "ln2_g", "w_up",
             "conv_w", "conv_b", "w_down")

    depth, d = ln1_g.shape
    s = x.shape[1]
    ha = on_a.shape[1] // HEAD
    hq = on_b.shape[1] // HEAD
    pw = w_in.shape[2] * N_DEV
    hkv = (pw - 3 * ha * HEAD - hq * HEAD) // (2 * HEAD)
    f = w_down.shape[1] * N_DEV
    mix = (ha + hq) * HEAD
    cfg = (ha, hq, hkv)
    fs = conv_w.shape[2]
    dev = 4 * lax.axis_index("x") + 2 * lax.axis_index("y") + lax.axis_index("c")
    core = lax.axis_index("c").astype(jnp.int32).reshape(1)

    shard = {n: weights[n].astype(BF16) for n in ("w_in", "w_out", "w_up", "w_down")}

    def unshard(n, g):
        if n in ("w_in", "w_up"):
            return g.transpose(1, 0, 2).reshape(g.shape[1], N_DEV * g.shape[2])
        return g.reshape(N_DEV * g.shape[1], g.shape[2])

    full = {n: [None] * depth for n in shard}
    full["w_in"][0] = unshard("w_in", _allgather(shard["w_in"][0], "gather0_w_in"))
    cw_rows = depth * 3
    cw_pad = jnp.pad(conv_w.reshape(cw_rows, fs), ((0, -cw_rows % 8), (0, 0)))
    g_cw = _allgather(cw_pad, "gather_conv_w")
    full_cw = g_cw[:, :cw_rows].reshape(N_DEV, depth, 3, fs).transpose(1, 2, 0, 3).reshape(depth, 3, 2 * f)

    inv = ROPE_THETA ** (-jnp.arange(0, HEAD, 2, dtype=F32) / HEAD)
    ang = positions.astype(F32)[:, None] * inv[None, :]
    cos = jnp.concatenate([jnp.cos(ang), jnp.cos(ang)], axis=-1)
    sin = jnp.concatenate([-jnp.sin(ang), jnp.sin(ang)], axis=-1)
    qk = jnp.arange(GRID_W * GRID_W)
    dc_of = (qk % GRID_W) - (qk // GRID_W) + (WIN_C - 1)
    onehot = (dc_of[:, None] == jnp.arange(LANES)[None, :]).astype(BF16)

    tiles_s = _tile(s, 512, 16)
    tiles_l = _tile(s, 1024, 16)

    xs = x.reshape(s, d)
    saved = []
    for l in range(depth):
        more = l + 1 < depth

        def fwd_matmul(n, a_op, name, also=(), **kw):
            wanted = ([(n, l + 1)] if more else []) + list(also)
            if not wanted:
                return _matmul(a_op, full[n][l], dims="nn", out_dtype=F32, name=name + "_last", **kw)
            out, *got = _matmul(a_op, full[n][l], dims="nn", out_dtype=F32, name=name + "_also" * bool(also),
                                comm=_gather_comm([shard[m][k] for m, k in wanted]), **kw)
            for (m, k), g in zip(wanted, got):
                full[m][k] = unshard(m, g)
            return out

        gains = jnp.zeros((8, HEAD), F32).at[0].set(qn_a[l]).at[1].set(kn_a[l]).at[2].set(qn_b[l]).at[3].set(kn_b[l])
        on_g = jnp.concatenate([on_a[l], on_b[l]]).reshape(1, mix)
        h = _rms_fwd(xs, ln1_g[l].reshape(1, d), "ln1_fwd")
        proj = fwd_matmul("w_in", h, "proj_fwd", also=[("w_out", 0)] if l == 0 else (), ti=tiles_l, tj=_tile(pw, 1536, LANES), tk=d)
        qa, ka, va, qb, kb, vb = _qkv_fwd(proj, gains, cos, sin, cfg, "qkv_fwd")
        tb = _na_bias(rpb[l].reshape(-1), ha, "na_bias")
        if l == 0:
            oa, got = _na_fwd(qa, ka, va, tb, "na_fwd_also", comm=_gather_comm([shard["w_up"][0]]))
            full["w_up"][0] = unshard("w_up", got)
            ob, got = _wa_fwd(qb, kb, vb, sink[l], "wa_fwd_also", comm=_gather_comm([shard["w_down"][0]]))
            full["w_down"][0] = unshard("w_down", got)
        else:
            oa = _na_fwd(qa, ka, va, tb, "na_fwd")
            ob = _wa_fwd(qb, kb, vb, sink[l], "wa_fwd")
        o_n = _onorm_fwd(oa, ob, on_g, "onorm_fwd")
        x1 = fwd_matmul("w_out", o_n, "out_fwd", ti=tiles_l, tj=_tile(d, 1024, LANES), tk=mix, resid=xs)
        h2 = _rms_fwd(x1, ln2_g[l].reshape(1, d), "ln2_fwd")
        u = fwd_matmul("w_up", h2, "up_fwd", ti=tiles_l, tj=_tile(2 * f, 1024, 2 * LANES), tk=d)
        a, gu = _gate_fwd(u, full_cw[l], conv_b[l].reshape(1, 2 * f), f, "gate_fwd")
        x2 = fwd_matmul("w_down", a, "down_fwd", ti=tiles_s, tj=_tile(d, 512, LANES), tk=f, resid=x1)
        saved.append(dict(x=xs, h=h, proj=proj, gains=gains, on_g=on_g, qkv=(qa, ka, va, qb, kb, vb), tb=tb, oa=oa, ob=ob,
                          o_n=o_n, x1=x1, h2=h2, u=u, gu=gu, a=a))
        xs = x2

    dx, dx_b, loss_part = _loss_head(xs, loss_target.reshape(s, d), "loss_head")
    tile_c = _tile(s, 2048, 16)
    half_k = _tile(2 * f, f, fs)
    loss = lax.psum(loss_part[0, 0], ("x", "y", "c"))

    small_grads = [None] * depth
    big = {n: None for n in ("w_in", "w_out", "w_up", "w_down")}
    pending = None
    for l in reversed(range(depth)):
        sv = saved[l]
        qa, ka, va, qb, kb, vb = sv["qkv"]

        def update(n, layer, got):
            big[n] = _adamw(got, weights[n], mom1[n], mom2[n], "adamw_" + n, layer=layer, into=big[n])

        def grad_matmul(n, a_op, b_op, name, **kw):
            if pending is None:
                return _matmul(a_op, b_op, dims="tn", out_dtype=BF16, name=name + "_first", **kw)
            out, got = _matmul(a_op, b_op, dims="tn", out_dtype=BF16, name=name, comm=_scatter_comm([pending[n]]), **kw)
            update(n, l + 1, got)
            return out

        def own(blocks):
            return _scatter_comm([blocks]) if l == 0 else None

        gw_down = grad_matmul("w_down", sv["a"], dx_b, "down_bwd_w", ti=_tile(f, 1408, LANES), tj=_tile(d, 1024, LANES),
                              tk=tile_c, j_outer=False)
        da = _matmul(dx_b, full["w_down"][l], dims="nt", ti=tiles_s, tj=_tile(f, 2816, 2 * LANES), tk=d, out_dtype=F32,
                     name="down_bwd_x")
        du, dcw, dcb = _ffn_bwd(sv["gu"], sv["u"], da, full_cw[l], "ffn_bwd")
        gw_down = gw_down.reshape(N_DEV, f // N_DEV, d)
        dh2 = None
        for part in range(2 * f // half_k):
            comm = own(gw_down) if part == 0 else None
            dh2 = _matmul(du, full["w_up"][l], dims="nt", ti=tiles_s, tj=_tile(d, 1024, LANES), tk=half_k, out_dtype=F32,
                          name=f"up_bwd_x{part}" + "_own" * bool(comm), k_blocks=(part, 1), resid=dh2, halved="a", comm=comm)
            if comm:
                dh2, got = dh2
                update("w_down", 0, got)
        gw_up = grad_matmul("w_up", sv["h2"], du, "up_bwd_w", ti=_tile(d, 1024, LANES), tj=fs, tk=tile_c, dev_major=True,
                            halved="b")
        dx1, dx1_b, dln2 = _rms_bwd(sv["x1"], ln2_g[l].reshape(1, d), dh2, dx, "ln2_bwd")
        don = _matmul(dx1_b, full["w_out"][l], dims="nt", ti=tiles_l, tj=mix, tk=d, out_dtype=F32, name="out_bwd_x")
        gw_out = grad_matmul("w_out", sv["o_n"], dx1_b, "out_bwd_w", ti=_tile(mix, 1024, LANES), tj=_tile(d, 1024, LANES),
                             tk=tile_c, j_outer=False)
        gw_out = gw_out.reshape(N_DEV, mix // N_DEV, d)
        doa, dob, don_g = _onorm_bwd(sv["oa"], sv["ob"], sv["on_g"], don, "onorm_bwd")
        comm = _chip_comm([_pair_sums(gw_up, core, "rs0_w_up")]) if l == 0 else None
        dqa, dka, dva, dtb, *got = _na_bwd(qa, ka, va, sv["tb"], doa, "na_bwd" + "_own" * (l == 0), comm=comm)
        if got:
            update("w_up", 0, got[0])
        dqb, dkb, dvb, dsink, *got = _wa_bwd(qb, kb, vb, sink[l], dob, "wa_bwd" + "_own" * (l == 0), comm=own(gw_out))
        if got:
            update("w_out", 0, got[0])
        drpb = _rpb_grad(dtb, onehot, "rpb_grad")
        dproj, dgains = _qkv_bwd(sv["proj"], sv["gains"], cos, sin, (dqa, dka, dva, dqb, dkb, dvb), cfg, "qkv_bwd")
        dh = _matmul(dproj, full["w_in"][l], dims="nt", ti=tiles_s, tj=_tile(d, 1024, LANES), tk=pw, out_dtype=F32, name="proj_bwd_x")
        gw_in = grad_matmul("w_in", sv["h"], dproj, "proj_bwd_w", ti=_tile(d, 1024, LANES), tj=_tile(pw, 1536, LANES), tk=tile_c)
        dx, dx_b, dln1 = _rms_bwd(sv["x"], ln1_g[l].reshape(1, d), dh, dx1, "ln1_bwd")

        small_grads[l] = dict(
            ln1_g=dln1[0], qn_a=dgains[0], kn_a=dgains[1], rpb=drpb, qn_b=dgains[2], kn_b=dgains[3], sink=dsink[:, 0, 0],
            on_a=don_g[0, :ha * HEAD], on_b=don_g[0, ha * HEAD:], ln2_g=dln2[0],
            conv_b=dcb[:, 0, :].reshape(2 * f), conv_w=dcw[:, :, 0, :].transpose(1, 0, 2).reshape(3, 2 * f))

        pending = dict(w_in=gw_in.reshape(d, N_DEV, pw // N_DEV).transpose(1, 0, 2), w_out=gw_out, w_up=gw_up, w_down=gw_down)

    summed = _chip_exchange(_pair_sums(pending["w_in"], core, "rs0_w_in"), "rs0_w_in_ici")
    big["w_in"] = _adamw(summed, weights["w_in"], mom1["w_in"], mom2["w_in"], "adamw0_w_in", layer=0, into=big["w_in"])

    grads_l = [small_grads[l][n] for l in range(depth) for n in SMALL]
    gathered = _allgather(_pack(grads_l), "gather_small")
    zeros_cw = jnp.zeros((3, 2 * f), F32)

    def small_state(src):
        return _pack([zeros_cw if n == "conv_w" else src[n][l] for l in range(depth) for n in SMALL])

    sm = _adamw(gathered, small_state(weights), small_state(mom1), small_state(mom2), "adamw_small")
    sm = [_unpack(t, grads_l) for t in sm]
    small_out = {n: [jnp.stack([sm[k][l * len(SMALL) + i] for l in range(depth)]) for k in range(4)]
                 for i, n in enumerate(SMALL)}
    cw_grad = lax.dynamic_slice_in_dim(small_out["conv_w"][0], dev * fs, fs, axis=2)
    cw_rows_pad = cw_rows + (-cw_rows % 8)

    def rows8(a):
        return jnp.pad(a.reshape(cw_rows, fs), ((0, cw_rows_pad - cw_rows), (0, 0)))

    cw_res = _adamw(rows8(cw_grad)[None], rows8(conv_w), rows8(m_conv_w), rows8(v_conv_w), "adamw_conv_w")
    small_out["conv_w"] = [t[:cw_rows].reshape(depth, 3, fs) for t in cw_res]

    results = {n: (big[n] if n in big else small_out[n]) for n in order}
    grad_x = dx.reshape(1, s, d)
    return (loss, grad_x, *[results[n][0] for n in order], *[results[n][1] for n in order],
            *[results[n][2] for n in order], *[results[n][3] for n in order])
```

```python
import functools
import math

import jax
import jax.numpy as jnp
from jax import lax
from jax.experimental import pallas as pl
from jax.experimental.pallas import tpu as pltpu

F32 = jnp.float32
BF16 = jnp.bfloat16

HEAD = 128
GRID_W = 64
WIN_R = 8
WIN_C = 16
BAND = 128
ROPE_THETA = 10000.0
EPS = 1e-6
NEG = -1e30
SCALE = 1.0 / math.sqrt(HEAD)

ADAM_LR = 0.001
ADAM_B1 = 0.9
ADAM_B2 = 0.999
ADAM_EPS = 1e-08
ADAM_WD = 0.01
ADAM_STEP = 10

N_DEV = 8
LANES = 128
VMEM_LIMIT_BYTES = 56 * 2 ** 20
MESH = pl.DeviceIdType.MESH
ANY = pl.BlockSpec(memory_space=pl.ANY)
SMEM = pl.BlockSpec(memory_space=pltpu.SMEM)


def _params():
    return pltpu.CompilerParams(vmem_limit_bytes=VMEM_LIMIT_BYTES)


def _tile(n, pref, align):
    t = min(n, pref)
    t -= t % align
    while t > 0 and n % t:
        t -= align
    return t if t > 0 else n


def _place():
    x, y, c = lax.axis_index("x"), lax.axis_index("y"), lax.axis_index("c")
    chips = [(1 - x, y), (x, 1 - y), (1 - x, 1 - y)]
    return x, y, c, chips


COPIES_PER_ARRAY = N_DEV - 1


def _comm_scratch(n_arrays):
    return [pltpu.SemaphoreType.DMA((COPIES_PER_ARRAY * n_arrays,)), pltpu.SemaphoreType.DMA((COPIES_PER_ARRAY * n_arrays,)),
            pltpu.SemaphoreType.DMA((n_arrays,))]


def _gather_plan(src_refs, out_refs, send_sems, recv_sems, local_sems):
    x, y, c, chips = _place()
    me, sibling = (x, y, c), (x, y, 1 - c)

    def slot(a, px, py, pc):
        return out_refs[a].at[4 * px + 2 * py + pc]

    def copy(a, k, block, to, src=None):
        return pltpu.make_async_remote_copy(
            src_ref=slot(a, *block) if src is None else src, dst_ref=slot(a, *block),
            send_sem=send_sems.at[COPIES_PER_ARRAY * a + k], recv_sem=recv_sems.at[COPIES_PER_ARRAY * a + k],
            device_id=to, device_id_type=MESH)

    def mine(a):
        return pltpu.make_async_copy(src_refs[a], slot(a, *me), local_sems.at[a])

    def first(a):
        return [copy(a, 0, me, sibling, src=src_refs[a])] + [
            copy(a, 1 + j, me, (*chip, c), src=src_refs[a]) for j, chip in enumerate(chips)]

    def passed(a):
        return [copy(a, 4 + j, (*chip, c), sibling) for j, chip in enumerate(chips)]

    def start():
        for a in range(len(src_refs)):
            mine(a).start()
            for cp in first(a):
                cp.start()

    def finish():
        forwards = [passed(a) for a in range(len(src_refs))]
        for a in range(len(src_refs)):
            for j, chip in enumerate(chips):
                copy(a, 1 + j, (*chip, c), me).wait_recv()
                forwards[a][j].start()
        for a in range(len(src_refs)):
            copy(a, 0, sibling, me).wait_recv()
            for j, chip in enumerate(chips):
                copy(a, 4 + j, (*chip, 1 - c), me).wait_recv()
            for cp in first(a) + forwards[a]:
                cp.wait_send()
            mine(a).wait()

    return start, finish


def _scatter_plan(src_refs, out_refs, send_sems, recv_sems, local_sems):
    x, y, c, _ = _place()
    me = 4 * x + 2 * y + c

    def peer(k):
        px = 1 - x if k & 4 else x
        py = 1 - y if k & 2 else y
        pc = 1 - c if k & 1 else c
        return (px, py, pc), 4 * px + 2 * py + pc

    def copy(a, k, outgoing):
        to, idx = peer(k)
        return pltpu.make_async_remote_copy(
            src_ref=src_refs[a].at[idx], dst_ref=out_refs[a].at[me if outgoing else idx],
            send_sem=send_sems.at[COPIES_PER_ARRAY * a + k - 1], recv_sem=recv_sems.at[COPIES_PER_ARRAY * a + k - 1],
            device_id=to, device_id_type=MESH)

    def mine(a):
        return pltpu.make_async_copy(src_refs[a].at[me], out_refs[a].at[me], local_sems.at[a])

    def start():
        for a in range(len(src_refs)):
            mine(a).start()
            for k in range(1, N_DEV):
                copy(a, k, True).start()

    def finish():
        for a in range(len(src_refs)):
            for k in range(1, N_DEV):
                copy(a, k, False).wait_recv()
            for k in range(1, N_DEV):
                copy(a, k, True).wait_send()
            mine(a).wait()

    return start, finish


def _allgather(v, name):
    def body(v_ref, out_ref, send_sems, recv_sems, local_sems):
        start, finish = _gather_plan([v_ref], [out_ref], send_sems, recv_sems, local_sems)
        start()
        finish()

    return pl.pallas_call(
        body, name=name,
        out_shape=jax.ShapeDtypeStruct((N_DEV,) + v.shape, v.dtype),
        in_specs=[ANY], out_specs=ANY, scratch_shapes=_comm_scratch(1),
    )(v)


def _sibling_exchange(g, name):
    def body(g_ref, out_ref, send_sems, recv_sems):
        x, y, c, _ = _place()
        sibling = (x, y, 1 - c)
        copies = []
        for j in range(4):
            copies.append(pltpu.make_async_remote_copy(
                src_ref=g_ref.at[2 * j + (1 - c)], dst_ref=out_ref.at[j],
                send_sem=send_sems.at[j], recv_sem=recv_sems.at[j], device_id=sibling, device_id_type=MESH))
        for cp in copies:
            cp.start()
        for cp in copies:
            cp.wait_recv()
        for cp in copies:
            cp.wait_send()

    return pl.pallas_call(
        body, name=name,
        out_shape=jax.ShapeDtypeStruct((4,) + g.shape[1:], g.dtype),
        in_specs=[ANY], out_specs=ANY,
        scratch_shapes=[pltpu.SemaphoreType.DMA((4,)), pltpu.SemaphoreType.DMA((4,))],
    )(g)


def _pair_sum(g, got, core, name):
    _, r, c = g.shape
    tr = _tile(r, max(16, (1 << 20) // c), 16)

    def body(core_ref, g_ref, got_ref, o_ref):
        del core_ref
        o_ref[...] = (g_ref[...].astype(F32) + got_ref[...].astype(F32)).astype(o_ref.dtype)

    return pl.pallas_call(
        body, name=name,
        out_shape=jax.ShapeDtypeStruct((4, r, c), g.dtype),
        grid_spec=pltpu.PrefetchScalarGridSpec(
            num_scalar_prefetch=1, grid=(4, r // tr),
            in_specs=[pl.BlockSpec((None, tr, c), lambda j, i, core_ref: (2 * j + core_ref[0], i, 0)),
                      pl.BlockSpec((None, tr, c), lambda j, i, core_ref: (j, i, 0))],
            out_specs=pl.BlockSpec((None, tr, c), lambda j, i, core_ref: (j, i, 0))),
        compiler_params=_params(),
    )(core, g, got)


def _chip_plan(src_refs, out_refs, send_sems, recv_sems, local_sems):
    x, y, c, chips = _place()

    def copies(a):
        return [pltpu.make_async_remote_copy(
            src_ref=src_refs[a].at[2 * px + py], dst_ref=out_refs[a].at[k],
            send_sem=send_sems.at[COPIES_PER_ARRAY * a + k], recv_sem=recv_sems.at[COPIES_PER_ARRAY * a + k],
            device_id=(px, py, c), device_id_type=MESH) for k, (px, py) in enumerate(chips)]

    def mine(a):
        return pltpu.make_async_copy(src_refs[a].at[2 * x + y], out_refs[a].at[3], local_sems.at[a])

    def start():
        for a in range(len(src_refs)):
            mine(a).start()
            for cp in copies(a):
                cp.start()

    def finish():
        for a in range(len(src_refs)):
            for cp in copies(a):
                cp.wait_recv()
            for cp in copies(a):
                cp.wait_send()
            mine(a).wait()

    return start, finish


def _chip_comm(blocks):
    return _chip_plan, blocks, [jax.ShapeDtypeStruct(p.shape, p.dtype) for p in blocks]


def _chip_exchange(p, name):
    def body(p_ref, out_ref, send_sems, recv_sems, local_sems):
        start, finish = _chip_plan([p_ref], [out_ref], send_sems, recv_sems, local_sems)
        start()
        finish()

    return pl.pallas_call(
        body, name=name,
        out_shape=jax.ShapeDtypeStruct(p.shape, p.dtype),
        in_specs=[ANY], out_specs=ANY, scratch_shapes=_comm_scratch(1),
    )(p)


def _pair_sums(g, core, name):
    got = _sibling_exchange(g, name + "_d2d")
    return _pair_sum(g, got, core, name + "_pair")


def _adamw(parts, w, m, v, name, layer=None, into=None):
    n_parts, r, c = parts.shape
    tr = _tile(r, max(8, (1 << 19) // c), 16 if parts.dtype == BF16 else 8)
    c1 = 1.0 - ADAM_B1 ** ADAM_STEP
    c2 = 1.0 - ADAM_B2 ** ADAM_STEP
    n_into = 0 if into is None else len(into)

    def body(p_ref, w_ref, m_ref, v_ref, *rest):
        g_out, d_out, m_out, v_out = rest[n_into:]
        g = p_ref[0].astype(F32)
        for k in range(1, n_parts):
            g = g + p_ref[k].astype(F32)
        m2 = ADAM_B1 * m_ref[...] + (1.0 - ADAM_B1) * g
        v2 = ADAM_B2 * v_ref[...] + (1.0 - ADAM_B2) * (g * g)
        g_out[...] = g
        m_out[...] = m2
        v_out[...] = v2
        d_out[...] = -ADAM_LR * ((m2 / c1) / (jnp.sqrt(v2 / c2) + ADAM_EPS) + ADAM_WD * w_ref[...])

    if layer is None:
        blk = pl.BlockSpec((tr, c), lambda i: (i, 0))
        out = jax.ShapeDtypeStruct((r, c), F32)
    else:
        blk = pl.BlockSpec((None, tr, c), lambda i: (layer, i, 0))
        out = jax.ShapeDtypeStruct(w.shape, F32)
    return pl.pallas_call(
        body, name=name, grid=(r // tr,),
        in_specs=[pl.BlockSpec((n_parts, tr, c), lambda i: (0, i, 0)), blk, blk, blk] + [ANY] * n_into,
        out_specs=[blk, blk, blk, blk], out_shape=[out, out, out, out],
        input_output_aliases={4 + k: k for k in range(n_into)},
        compiler_params=_params(),
    )(parts, w, m, v, *(into or ()))


def _host_exchange(body, comm, grid, n_in, n_out):
    plan, comm_in, comm_out = comm
    n = len(comm_in)

    def wrapped(*refs):
        ins, cin = refs[:n_in], refs[n_in:n_in + n]
        outs, cout = refs[n_in + n:n_in + n + n_out], refs[n_in + n + n_out:n_in + 2 * n + n_out]
        rest = refs[n_in + 2 * n + n_out:]
        start, finish = plan(cin, cout, *rest[len(rest) - 3:])
        steps = [pl.program_id(axis) for axis in range(len(grid))]
        first, last = steps[0] == 0, steps[0] == grid[0] - 1
        for axis in range(1, len(grid)):
            first, last = first & (steps[axis] == 0), last & (steps[axis] == grid[axis] - 1)

        @pl.when(first)
        def _():
            start()

        body(*ins, *outs, *rest[:len(rest) - 3])

        @pl.when(last)
        def _():
            finish()

    return wrapped, list(comm_in), [ANY] * n, list(comm_out), _comm_scratch(n)


def _gather_comm(shards):
    return _gather_plan, shards, [jax.ShapeDtypeStruct((N_DEV,) + v.shape, v.dtype) for v in shards]


def _scatter_comm(blocks):
    return _scatter_plan, blocks, [jax.ShapeDtypeStruct(g.shape, g.dtype) for g in blocks]


def _matmul(a, b, *, dims, ti, tj, tk, out_dtype, name, j_outer=True, resid=None, dev_major=False, comm=None,
            k_blocks=None, halved=None):
    a_shape = (a.shape[1], 2 * a.shape[2]) if halved == "a" else a.shape
    b_shape = (b.shape[1], 2 * b.shape[2]) if halved == "b" else b.shape
    if dims == "nn":
        (I, K), (K2, J) = a_shape, b_shape
    elif dims == "nt":
        (I, K), (J, K2) = a_shape, b_shape
    else:
        (K, I), (K2, J) = a_shape, b_shape
    assert K == K2 and I % ti == 0 and J % tj == 0 and K % tk == 0, (name, a.shape, b.shape, ti, tj, tk)
    assert halved is None or (halved, dims) in (("a", "nt"), ("b", "tn")), (name, halved, dims)
    k0, nk = k_blocks if k_blocks is not None else (0, K // tk)
    ni, nj = I // ti, J // tj

    def ij(g0, g1):
        return (g1, g0) if j_outer else (g0, g1)

    if dims == "nn":
        a_spec = pl.BlockSpec((ti, tk), lambda g0, g1, k: (ij(g0, g1)[0], k0 + k))
        b_spec = pl.BlockSpec((tk, tj), lambda g0, g1, k: (k0 + k, ij(g0, g1)[1]))
        dn = (((1,), (0,)), ((), ()))
    elif dims == "nt":
        a_spec = pl.BlockSpec((ti, tk), lambda g0, g1, k: (ij(g0, g1)[0], k0 + k))
        if halved == "a":
            per = K // 2 // tk
            a_spec = pl.BlockSpec((None, ti, tk), lambda g0, g1, k: ((k0 + k) // per, ij(g0, g1)[0], (k0 + k) % per))
        b_spec = pl.BlockSpec((tj, tk), lambda g0, g1, k: (ij(g0, g1)[1], k0 + k))
        dn = (((1,), (1,)), ((), ()))
    else:
        a_spec = pl.BlockSpec((tk, ti), lambda g0, g1, k: (k0 + k, ij(g0, g1)[0]))
        b_spec = pl.BlockSpec((tk, tj), lambda g0, g1, k: (k0 + k, ij(g0, g1)[1]))
        if halved == "b":
            per = J // 2 // tj
            b_spec = pl.BlockSpec((None, tk, tj), lambda g0, g1, k: (ij(g0, g1)[1] // per, k0 + k, ij(g0, g1)[1] % per))
        dn = (((0,), (0,)), ((), ()))
    in_specs = [a_spec, b_spec]
    operands = [a, b]
    if resid is not None:
        in_specs.append(pl.BlockSpec((ti, tj), lambda g0, g1, k: ij(g0, g1)))
        operands.append(resid)
    if dev_major:
        out_spec = pl.BlockSpec((None, ti, tj), lambda g0, g1, k: (ij(g0, g1)[1], ij(g0, g1)[0], 0))
        out_shape = jax.ShapeDtypeStruct((nj, I, tj), out_dtype)
    else:
        out_spec = pl.BlockSpec((ti, tj), lambda g0, g1, k: ij(g0, g1))
        out_shape = jax.ShapeDtypeStruct((I, J), out_dtype)

    grid = (nj, ni, nk) if j_outer else (ni, nj, nk)
    n_in = len(operands)
    n_comm = 0
    out_specs, out_shapes = [out_spec], [out_shape]
    scratch = [pltpu.VMEM((ti, tj), F32)] if nk > 1 else []
    if comm is not None:
        plan, comm_in, comm_out = comm
        n_comm = len(comm_in)
        operands += list(comm_in)
        in_specs += [ANY] * n_comm
        out_specs += [ANY] * n_comm
        out_shapes += list(comm_out)
        scratch += _comm_scratch(n_comm)

    def body(*refs):
        a_ref, b_ref = refs[0], refs[1]
        r_ref = refs[2] if resid is not None else None
        o_ref = refs[n_in + n_comm]
        if comm is not None:
            start, finish_comm = plan(refs[n_in:n_in + n_comm], refs[n_in + n_comm + 1:n_in + 2 * n_comm + 1], *refs[-3:])
            steps = [pl.program_id(axis) for axis in range(3)]

            @pl.when((steps[0] == 0) & (steps[1] == 0) & (steps[2] == 0))
            def _():
                start()

        part = lax.dot_general(a_ref[...].astype(BF16), b_ref[...].astype(BF16), dn, preferred_element_type=F32)

        def finish(acc):
            if r_ref is not None:
                acc = acc + r_ref[...]
            o_ref[...] = acc.astype(o_ref.dtype)

        if nk == 1:
            finish(part)
        else:
            acc_ref = refs[n_in + 2 * n_comm + 1]
            k = pl.program_id(2)

            @pl.when(k == 0)
            def _():
                acc_ref[...] = part

            @pl.when(k > 0)
            def _():
                acc_ref[...] += part

            @pl.when(k == nk - 1)
            def _():
                finish(acc_ref[...])

        if comm is not None:
            @pl.when((steps[0] == grid[0] - 1) & (steps[1] == grid[1] - 1) & (steps[2] == grid[2] - 1))
            def _():
                finish_comm()

    res = pl.pallas_call(
        body, name=name, grid=grid,
        in_specs=in_specs, out_specs=out_specs, out_shape=out_shapes,
        scratch_shapes=scratch, compiler_params=_params(),
    )(*operands)
    return res[0] if comm is None else res


def _rms_fwd(x, g, name):
    s, d = x.shape
    ts = _tile(s, 256, 16)

    def body(x_ref, g_ref, h_ref):
        xv = x_ref[...]
        r = lax.rsqrt(jnp.mean(xv * xv, axis=-1, keepdims=True) + EPS)
        h_ref[...] = (xv * r * g_ref[...]).astype(BF16)

    return pl.pallas_call(
        body, name=name, grid=(s // ts,),
        in_specs=[pl.BlockSpec((ts, d), lambda i: (i, 0)), pl.BlockSpec((1, d), lambda i: (0, 0))],
        out_specs=pl.BlockSpec((ts, d), lambda i: (i, 0)),
        out_shape=jax.ShapeDtypeStruct((s, d), BF16), compiler_params=_params(),
    )(x, g)


def _rms_bwd(x, g, dh, dres, name):
    s, d = x.shape
    ts = _tile(s, 256, 16)

    def body(x_ref, g_ref, dh_ref, dres_ref, dx_ref, dxb_ref, dg_ref):
        xv = x_ref[...]
        r = lax.rsqrt(jnp.mean(xv * xv, axis=-1, keepdims=True) + EPS)
        y = xv * r
        dhv = dh_ref[...]
        gd = dhv * g_ref[...]
        dxv = dres_ref[...] + r * (gd - y * jnp.mean(gd * y, axis=-1, keepdims=True))
        dx_ref[...] = dxv
        dxb_ref[...] = dxv.astype(BF16)

        @pl.when(pl.program_id(0) == 0)
        def _():
            dg_ref[...] = jnp.zeros_like(dg_ref)

        dg_ref[0:1, :] += jnp.sum(dhv * y, axis=0, keepdims=True)

    blk = pl.BlockSpec((ts, d), lambda i: (i, 0))
    return pl.pallas_call(
        body, name=name, grid=(s // ts,),
        in_specs=[blk, pl.BlockSpec((1, d), lambda i: (0, 0)), blk, blk],
        out_specs=[blk, blk, pl.BlockSpec((8, d), lambda i: (0, 0))],
        out_shape=[jax.ShapeDtypeStruct((s, d), F32), jax.ShapeDtypeStruct((s, d), BF16), jax.ShapeDtypeStruct((8, d), F32)],
        compiler_params=_params(),
    )(x, g, dh, dres)


def _head_norm(t, gain):
    r = lax.rsqrt(jnp.mean(t * t, axis=-1, keepdims=True) + EPS)
    return t * r * gain


def _head_norm_bwd(t, gain, dn):
    r = lax.rsqrt(jnp.mean(t * t, axis=-1, keepdims=True) + EPS)
    y = t * r
    gd = dn * gain
    dt = r * (gd - y * jnp.mean(gd * y, axis=-1, keepdims=True))
    return dt, jnp.sum(dn * y, axis=0, keepdims=True)


def _rope(n, cos, sin):
    return n * cos + pltpu.roll(n, HEAD // 2, axis=1) * sin


def _rope_bwd(do, cos, sin):
    return do * cos + pltpu.roll(do * sin, HEAD // 2, axis=1)


def _qkv_fwd(proj, gains, cos, sin, cfg, name):
    s, pw = proj.shape
    ha, hq, hkv = cfg
    ts = _tile(s, 256, 16)

    def body(p_ref, gn_ref, cos_ref, sin_ref, qa_ref, ka_ref, va_ref, qb_ref, kb_ref, vb_ref):
        cosv, sinv = cos_ref[...], sin_ref[...]
        col = 0
        for out_ref, nh, gi, rot in ((qa_ref, ha, 0, False), (ka_ref, ha, 1, False), (va_ref, ha, None, False),
                                     (qb_ref, hq, 2, True), (kb_ref, hkv, 3, True), (vb_ref, hkv, None, False)):
            for h in range(nh):
                t = p_ref[:, col * HEAD:(col + 1) * HEAD]
                if gi is not None:
                    t = _head_norm(t, gn_ref[gi:gi + 1, :])
                if rot:
                    t = _rope(t, cosv, sinv)
                out_ref[h] = t.astype(BF16)
                col += 1

    def hm(nh):
        return pl.BlockSpec((nh, ts, HEAD), lambda i: (0, i, 0)), jax.ShapeDtypeStruct((nh, s, HEAD), BF16)

    specs, shapes = zip(hm(ha), hm(ha), hm(ha), hm(hq), hm(hkv), hm(hkv))
    tok = pl.BlockSpec((ts, HEAD), lambda i: (i, 0))
    return pl.pallas_call(
        body, name=name, grid=(s // ts,),
        in_specs=[pl.BlockSpec((ts, pw), lambda i: (i, 0)), pl.BlockSpec((8, HEAD), lambda i: (0, 0)), tok, tok],
        out_specs=list(specs), out_shape=list(shapes), compiler_params=_params(),
    )(proj, gains, cos, sin)


def _qkv_bwd(proj, gains, cos, sin, grads, cfg, name):
    s, pw = proj.shape
    ha, hq, hkv = cfg
    ts = _tile(s, 256, 16)

    def body(p_ref, gn_ref, cos_ref, sin_ref, dqa, dka, dva, dqb, dkb, dvb, dp_ref, dgn_ref):
        cosv, sinv = cos_ref[...], sin_ref[...]

        @pl.when(pl.program_id(0) == 0)
        def _():
            dgn_ref[...] = jnp.zeros_like(dgn_ref)

        col = 0
        for d_ref, nh, gi, rot in ((dqa, ha, 0, False), (dka, ha, 1, False), (dva, ha, None, False),
                                   (dqb, hq, 2, True), (dkb, hkv, 3, True), (dvb, hkv, None, False)):
            dgain = jnp.zeros((1, HEAD), F32)
            for h in range(nh):
                dt = d_ref[h]
                if rot:
                    dt = _rope_bwd(dt, cosv, sinv)
                if gi is not None:
                    dt, dg = _head_norm_bwd(p_ref[:, col * HEAD:(col + 1) * HEAD], gn_ref[gi:gi + 1, :], dt)
                    dgain = dgain + dg
                dp_ref[:, col * HEAD:(col + 1) * HEAD] = dt.astype(BF16)
                col += 1
            if gi is not None:
                dgn_ref[gi:gi + 1, :] += dgain

    def hm(nh):
        return pl.BlockSpec((nh, ts, HEAD), lambda i: (0, i, 0))

    tok = pl.BlockSpec((ts, HEAD), lambda i: (i, 0))
    small = pl.BlockSpec((8, HEAD), lambda i: (0, 0))
    return pl.pallas_call(
        body, name=name, grid=(s // ts,),
        in_specs=[pl.BlockSpec((ts, pw), lambda i: (i, 0)), small, tok, tok,
                  hm(ha), hm(ha), hm(ha), hm(hq), hm(hkv), hm(hkv)],
        out_specs=[pl.BlockSpec((ts, pw), lambda i: (i, 0)), small],
        out_shape=[jax.ShapeDtypeStruct((s, pw), BF16), jax.ShapeDtypeStruct((8, HEAD), F32)],
        compiler_params=_params(),
    )(proj, gains, cos, sin, *grads)


NA_QROWS = 32
NA_KEYS = WIN_R * GRID_W
N_DR = 2 * WIN_R - 1
N_DC = 2 * WIN_C - 1


def _na_bias(rpb_flat, n_heads, name):
    def body(rpb_ref, tb_ref):
        h = pl.program_id(0)
        qi = lax.broadcasted_iota(jnp.int32, (GRID_W, LANES), 0)
        lane = lax.broadcasted_iota(jnp.int32, (GRID_W, LANES), 1)
        kk = lane & (GRID_W - 1)
        upper = lane >= GRID_W
        dcm = kk - qi + (WIN_C - 1)
        cs = jnp.clip(qi - WIN_C // 2, 0, GRID_W - WIN_C)
        valid = (kk >= cs) & (kk < cs + WIN_C)
        base = h * (N_DR * N_DC)
        for dra in range(N_DR - 1):
            def step(j, acc, dra=dra):
                va = rpb_ref[base + dra * N_DC + j]
                vb = rpb_ref[base + (dra + 1) * N_DC + j]
                return jnp.where(dcm == j, jnp.where(upper, vb, va), acc)

            pair = lax.fori_loop(0, N_DC, step, jnp.zeros((GRID_W, LANES), F32))
            pair = jnp.where(valid, pair, NEG)
            for dr0 in range(WIN_R):
                wp, odd = divmod(dra - dr0, 2)
                if odd == 0 and 0 <= wp < WIN_R // 2:
                    tb_ref[0, dr0, :, wp * LANES:(wp + 1) * LANES] = pair

    return pl.pallas_call(
        body, name=name, grid=(n_heads,),
        in_specs=[SMEM],
        out_specs=pl.BlockSpec((1, WIN_R, GRID_W, NA_KEYS), lambda h: (h, 0, 0, 0)),
        out_shape=jax.ShapeDtypeStruct((n_heads, WIN_R, GRID_W, NA_KEYS), F32),
        compiler_params=_params(),
    )(rpb_flat)


def _na_row(b, i, nrows, qrows):
    r = b * qrows + i
    rs = jnp.clip(r - WIN_R // 2, 0, nrows - WIN_R)
    return pl.ds(pl.multiple_of(rs * GRID_W, GRID_W), NA_KEYS), rs - r + (WIN_R - 1)


def _softmax(s):
    e = jnp.exp(s - jnp.max(s, axis=-1, keepdims=True))
    return e * (1.0 / jnp.sum(e, axis=-1, keepdims=True))


_NT = (((1,), (1,)), ((), ()))
_NN = (((1,), (0,)), ((), ()))
_TN = (((0,), (0,)), ((), ()))


def _dot(a, b, dn):
    return lax.dot_general(a, b, dn, preferred_element_type=F32)


def _call(body, comm, *, name, grid, operands, in_specs, out_specs, out_shape, scratch_shapes):
    if comm is not None:
        body, more_operands, more_specs, more_shapes, sems = _host_exchange(body, comm, grid, len(operands), len(out_shape))
        operands = operands + more_operands
        in_specs = in_specs + more_specs
        out_specs = out_specs + more_specs
        out_shape = out_shape + more_shapes
        scratch_shapes = scratch_shapes + sems
    res = pl.pallas_call(body, name=name, grid=grid, in_specs=in_specs, out_specs=out_specs, out_shape=out_shape,
                         scratch_shapes=scratch_shapes, compiler_params=_params())(*operands)
    return res[0] if len(res) == 1 else res


def _na_fwd(q, k, v, tb, name, comm=None):
    nh, s, _ = q.shape
    nrows = s // GRID_W
    qrows = _tile(nrows, NA_QROWS, WIN_R)
    tq = qrows * GRID_W

    def body(q_ref, k_ref, v_ref, tb_ref, o_ref, s_scr, p_scr):
        b = pl.program_id(1)
        rows = [slice(i * GRID_W, (i + 1) * GRID_W) for i in range(qrows)]
        at = [_na_row(b, i, nrows, qrows) for i in range(qrows)]
        for i, (keys, dr0) in enumerate(at):
            s_scr[i] = _dot(q_ref[rows[i], :], k_ref[keys, :], _NT) * SCALE + tb_ref[0, dr0]
        for i in range(qrows):
            p_scr[i] = _softmax(s_scr[i]).astype(BF16)
        for i, (keys, _) in enumerate(at):
            o_ref[rows[i], :] = _dot(p_scr[i], v_ref[keys, :], _NN)

    qspec = pl.BlockSpec((None, tq, HEAD), lambda h, b: (h, b, 0))
    full = pl.BlockSpec((None, s, HEAD), lambda h, b: (h, 0, 0))
    return _call(
        body, comm, name=name, grid=(nh, nrows // qrows), operands=[q, k, v, tb],
        in_specs=[qspec, full, full, pl.BlockSpec((1, WIN_R, GRID_W, NA_KEYS), lambda h, b: (h, 0, 0, 0))],
        out_specs=[qspec], out_shape=[jax.ShapeDtypeStruct((nh, s, HEAD), F32)],
        scratch_shapes=[pltpu.VMEM((qrows, GRID_W, NA_KEYS), F32), pltpu.VMEM((qrows, GRID_W, NA_KEYS), BF16)])


def _na_bwd(q, k, v, tb, do, name, comm=None):
    nh, s, _ = q.shape
    nrows = s // GRID_W
    qrows = _tile(nrows, NA_QROWS, WIN_R)
    tq = qrows * GRID_W

    def body(q_ref, do_ref, k_ref, v_ref, tb_ref, dq_ref, dk_ref, dv_ref, dtb_ref, s_scr, dp_scr, p_scr, ds_scr):
        b = pl.program_id(1)

        @pl.when(b == 0)
        def _():
            dk_ref[...] = jnp.zeros_like(dk_ref)
            dv_ref[...] = jnp.zeros_like(dv_ref)
            dtb_ref[...] = jnp.zeros_like(dtb_ref)

        rows = [slice(i * GRID_W, (i + 1) * GRID_W) for i in range(qrows)]
        at = [_na_row(b, i, nrows, qrows) for i in range(qrows)]
        for i, (keys, dr0) in enumerate(at):
            s_scr[i] = _dot(q_ref[rows[i], :], k_ref[keys, :], _NT) * SCALE + tb_ref[0, dr0]
            dp_scr[i] = _dot(do_ref[rows[i], :], v_ref[keys, :], _NT)
        for i in range(qrows):
            p = _softmax(s_scr[i])
            dp = dp_scr[i]
            ds = p * (dp - jnp.sum(p * dp, axis=-1, keepdims=True))
            p_scr[i] = p.astype(BF16)
            s_scr[i] = ds
            ds_scr[i] = (ds * SCALE).astype(BF16)
        for i, (keys, _) in enumerate(at):
            dq_ref[rows[i], :] = _dot(ds_scr[i], k_ref[keys, :], _NN)
        for i, (keys, dr0) in enumerate(at):
            dv_ref[keys, :] += _dot(p_scr[i], do_ref[rows[i], :], _TN)
            dk_ref[keys, :] += _dot(ds_scr[i], q_ref[rows[i], :], _TN)
            dtb_ref[0, dr0] += s_scr[i]

    qspec = pl.BlockSpec((None, tq, HEAD), lambda h, b: (h, b, 0))
    full = pl.BlockSpec((None, s, HEAD), lambda h, b: (h, 0, 0))
    tbs = pl.BlockSpec((1, WIN_R, GRID_W, NA_KEYS), lambda h, b: (h, 0, 0, 0))
    hm = jax.ShapeDtypeStruct((nh, s, HEAD), F32)
    tile = (qrows, GRID_W, NA_KEYS)
    return _call(
        body, comm, name=name, grid=(nh, nrows // qrows), operands=[q, do, k, v, tb],
        in_specs=[qspec, qspec, full, full, tbs],
        out_specs=[qspec, full, full, tbs],
        out_shape=[hm, hm, hm, jax.ShapeDtypeStruct((nh, WIN_R, GRID_W, NA_KEYS), F32)],
        scratch_shapes=[pltpu.VMEM(tile, F32), pltpu.VMEM(tile, F32), pltpu.VMEM(tile, BF16), pltpu.VMEM(tile, BF16)])


def _rpb_fold(y, n_heads, name):
    def body(y_ref, o_ref):
        for h in range(n_heads):
            for dr in range(2 * WIN_R):
                acc = jnp.zeros((1, LANES), F32)
                for dr0 in range(WIN_R):
                    w = dr - dr0
                    if 0 <= w < WIN_R:
                        acc = acc + y_ref[h, dr0, w:w + 1, :]
                o_ref[h, dr:dr + 1, :] = acc

    return pl.pallas_call(
        body, name=name, out_shape=jax.ShapeDtypeStruct((n_heads, 2 * WIN_R, LANES), F32),
    )(y)


def _rpb_grad(dtb, onehot, name):
    nh = dtb.shape[0]
    rows = dtb.reshape(nh, WIN_R, GRID_W, WIN_R, GRID_W).transpose(0, 1, 3, 2, 4).reshape(nh * WIN_R * WIN_R, GRID_W * GRID_W)
    y = _matmul(rows, onehot, dims="nn", ti=rows.shape[0], tj=LANES, tk=GRID_W * GRID_W, out_dtype=F32, name=name + "_dc")
    folded = _rpb_fold(y.reshape(nh, WIN_R, WIN_R, LANES), nh, name + "_dr")
    return folded[:, :N_DR, :N_DC]


WA_WIN_TOK = 3 * BAND
WA_QBLOCKS = 2


def _wa_scores(q, kwin, t0, j, sink_ref, head0, grp):
    rows = grp * BAND
    s = _dot(q, kwin, _NT) * SCALE
    row = lax.broadcasted_iota(jnp.int32, (rows, WA_WIN_TOK), 0)
    qpos = j * BAND + (row & (BAND - 1))
    kpos = t0 + lax.broadcasted_iota(jnp.int32, (rows, WA_WIN_TOK), 1)
    s = jnp.where(jnp.abs(kpos - qpos) <= BAND, s, NEG)
    head = lax.broadcasted_iota(jnp.int32, (rows, 1), 0) // BAND
    sink = jnp.zeros((rows, 1), F32) + sink_ref[head0]
    for g in range(1, grp):
        sink = jnp.where(head == g, sink_ref[head0 + g], sink)
    m = jnp.maximum(jnp.max(s, axis=-1, keepdims=True), sink)
    e = jnp.exp(s - m)
    es = jnp.exp(sink - m)
    rz = 1.0 / (jnp.sum(e, axis=-1, keepdims=True) + es)
    return e * rz, es * rz


def _wa_window(j, s):
    return pl.multiple_of(jnp.clip((j - 1) * BAND, 0, s - WA_WIN_TOK), BAND)


def _wa_fwd(q, k, v, sink, name, comm=None):
    hq, s, _ = q.shape
    hkv = k.shape[0]
    grp = hq // hkv

    def body(sink_ref, q_ref, k_ref, v_ref, o_ref):
        kh, step = pl.program_id(0), pl.program_id(1)
        for sub in range(WA_QBLOCKS):
            j = step * WA_QBLOCKS + sub
            rows = slice(sub * BAND, (sub + 1) * BAND)
            t0 = _wa_window(j, s)
            keys = pl.ds(t0, WA_WIN_TOK)
            p, _ = _wa_scores(q_ref[:, rows, :].reshape(grp * BAND, HEAD), k_ref[keys, :], t0, j, sink_ref, kh * grp, grp)
            o_ref[:, rows, :] = _dot(p.astype(BF16), v_ref[keys, :], _NN).reshape(grp, BAND, HEAD)

    qspec = pl.BlockSpec((grp, WA_QBLOCKS * BAND, HEAD), lambda kh, j: (kh, j, 0))
    full = pl.BlockSpec((None, s, HEAD), lambda kh, j: (kh, 0, 0))
    return _call(
        body, comm, name=name, grid=(hkv, s // (WA_QBLOCKS * BAND)), operands=[sink, q, k, v],
        in_specs=[SMEM, qspec, full, full],
        out_specs=[qspec], out_shape=[jax.ShapeDtypeStruct((hq, s, HEAD), F32)], scratch_shapes=[])


def _wa_bwd(q, k, v, sink, do, name, comm=None):
    hq, s, _ = q.shape
    hkv = k.shape[0]
    grp = hq // hkv

    def body(sink_ref, q_ref, do_ref, k_ref, v_ref, dq_ref, dk_ref, dv_ref, dsink_ref):
        kh, step = pl.program_id(0), pl.program_id(1)

        @pl.when(step == 0)
        def _():
            dk_ref[...] = jnp.zeros_like(dk_ref)
            dv_ref[...] = jnp.zeros_like(dv_ref)
            dsink_ref[...] = jnp.zeros_like(dsink_ref)

        for sub in range(WA_QBLOCKS):
            j = step * WA_QBLOCKS + sub
            rows = slice(sub * BAND, (sub + 1) * BAND)
            t0 = _wa_window(j, s)
            keys = pl.ds(t0, WA_WIN_TOK)
            qs = q_ref[:, rows, :].reshape(grp * BAND, HEAD)
            dos = do_ref[:, rows, :].reshape(grp * BAND, HEAD)
            kwin, vwin = k_ref[keys, :], v_ref[keys, :]
            p, ps = _wa_scores(qs, kwin, t0, j, sink_ref, kh * grp, grp)
            dp = _dot(dos, vwin, _NT)
            dv_ref[keys, :] += _dot(p.astype(BF16), dos, _TN)
            rowdot = jnp.sum(p * dp, axis=-1, keepdims=True)
            to_sink = ps * rowdot
            for g in range(grp):
                dsink_ref[g] += jnp.zeros((8, LANES), F32) - jnp.sum(to_sink[g * BAND:(g + 1) * BAND])
            dss = (p * (dp - rowdot) * SCALE).astype(BF16)
            dq_ref[:, rows, :] = _dot(dss, kwin, _NN).reshape(grp, BAND, HEAD)
            dk_ref[keys, :] += _dot(dss, qs, _TN)

    qspec = pl.BlockSpec((grp, WA_QBLOCKS * BAND, HEAD), lambda kh, j: (kh, j, 0))
    full = pl.BlockSpec((None, s, HEAD), lambda kh, j: (kh, 0, 0))
    kv = jax.ShapeDtypeStruct((hkv, s, HEAD), F32)
    return _call(
        body, comm, name=name, grid=(hkv, s // (WA_QBLOCKS * BAND)), operands=[sink, q, do, k, v],
        in_specs=[SMEM, qspec, qspec, full, full],
        out_specs=[qspec, full, full, pl.BlockSpec((grp, 8, LANES), lambda kh, j: (kh, 0, 0))],
        out_shape=[jax.ShapeDtypeStruct((hq, s, HEAD), F32), kv, kv, jax.ShapeDtypeStruct((hq, 8, LANES), F32)],
        scratch_shapes=[])


def _onorm_fwd(oa, ob, gains, name):
    ha, s, _ = oa.shape
    hq = ob.shape[0]
    ts = _tile(s, 256, 16)

    def body(oa_ref, ob_ref, g_ref, o_ref):
        col = 0
        for ref, nh in ((oa_ref, ha), (ob_ref, hq)):
            ss = sum(jnp.sum(ref[h] * ref[h], axis=-1, keepdims=True) for h in range(nh))
            r = lax.rsqrt(ss / (nh * HEAD) + EPS)
            for h in range(nh):
                o_ref[:, col * HEAD:(col + 1) * HEAD] = (ref[h] * r * g_ref[:, col * HEAD:(col + 1) * HEAD]).astype(BF16)
                col += 1

    mix = (ha + hq) * HEAD
    return pl.pallas_call(
        body, name=name, grid=(s // ts,),
        in_specs=[pl.BlockSpec((ha, ts, HEAD), lambda i: (0, i, 0)), pl.BlockSpec((hq, ts, HEAD), lambda i: (0, i, 0)),
                  pl.BlockSpec((1, mix), lambda i: (0, 0))],
        out_specs=pl.BlockSpec((ts, mix), lambda i: (i, 0)),
        out_shape=jax.ShapeDtypeStruct((s, mix), BF16), compiler_params=_params(),
    )(oa, ob, gains)


def _onorm_bwd(oa, ob, gains, don, name):
    ha, s, _ = oa.shape
    hq = ob.shape[0]
    ts = _tile(s, 256, 16)
    mix = (ha + hq) * HEAD

    def body(oa_ref, ob_ref, g_ref, don_ref, doa_ref, dob_ref, dg_ref):
        @pl.when(pl.program_id(0) == 0)
        def _():
            dg_ref[...] = jnp.zeros_like(dg_ref)

        col0 = 0
        for ref, d_ref, nh in ((oa_ref, doa_ref, ha), (ob_ref, dob_ref, hq)):
            ss = sum(jnp.sum(ref[h] * ref[h], axis=-1, keepdims=True) for h in range(nh))
            r = lax.rsqrt(ss / (nh * HEAD) + EPS)
            dot = jnp.zeros((ts, 1), F32)
            for h in range(nh):
                cols = slice((col0 + h) * HEAD, (col0 + h + 1) * HEAD)
                dot = dot + jnp.sum(don_ref[:, cols] * g_ref[:, cols] * ref[h], axis=-1, keepdims=True)
            mean = dot * r / (nh * HEAD)
            for h in range(nh):
                cols = slice((col0 + h) * HEAD, (col0 + h + 1) * HEAD)
                y = ref[h] * r
                dn = don_ref[:, cols]
                d_ref[h] = (r * (dn * g_ref[:, cols] - y * mean)).astype(BF16)
                dg_ref[0:1, cols] += jnp.sum(dn * y, axis=0, keepdims=True)
            col0 += nh

    return pl.pallas_call(
        body, name=name, grid=(s // ts,),
        in_specs=[pl.BlockSpec((ha, ts, HEAD), lambda i: (0, i, 0)), pl.BlockSpec((hq, ts, HEAD), lambda i: (0, i, 0)),
                  pl.BlockSpec((1, mix), lambda i: (0, 0)), pl.BlockSpec((ts, mix), lambda i: (i, 0))],
        out_specs=[pl.BlockSpec((ha, ts, HEAD), lambda i: (0, i, 0)), pl.BlockSpec((hq, ts, HEAD), lambda i: (0, i, 0)),
                   pl.BlockSpec((8, mix), lambda i: (0, 0))],
        out_shape=[jax.ShapeDtypeStruct((ha, s, HEAD), BF16), jax.ShapeDtypeStruct((hq, s, HEAD), BF16),
                   jax.ShapeDtypeStruct((8, mix), F32)],
        compiler_params=_params(),
    )(oa, ob, gains, don)


HALO = 8
PACKED = 16


def _halo_specs(ts, tc, col_off):
    per = ts // HALO
    cur = pl.BlockSpec((ts, tc), lambda j, i: (i, j + col_off))
    prev = pl.BlockSpec((HALO, tc), lambda j, i: (jnp.maximum(i * per - 1, 0), j + col_off))

    def nxt_map(n_blocks):
        return pl.BlockSpec((HALO, tc), lambda j, i: (jnp.minimum((i + 1) * per, n_blocks - 1), j + col_off))

    return cur, prev, nxt_map


def _sigmoid(x):
    return 1.0 / (1.0 + jnp.exp(-x))


def _ffn_tiles(s, f):
    return _tile(s, 512, 16), _tile(f, 512, LANES)


def _gate_fwd(u, cw, cb, f, name):
    s = u.shape[0]
    ts, tc = _ffn_tiles(s, f)
    nj, ni = f // tc, s // ts

    def body(g_ref, gp_ref, gn_ref, u_ref, up_ref, un_ref, wg_ref, wu_ref, bg_ref, bu_ref, a_ref, gu_ref):
        i = pl.program_id(1)

        def conv(c_ref, p_ref, n_ref, w_ref, b_ref):
            ext = jnp.concatenate([jnp.where(i > 0, p_ref[...], 0.0), c_ref[...], jnp.where(i < ni - 1, n_ref[...], 0.0)], axis=0)
            rows = ts + 2 * HALO
            out = (pltpu.roll(ext, 1, axis=0) * w_ref[0:1, :] + ext * w_ref[1:2, :]
                   + pltpu.roll(ext, rows - 1, axis=0) * w_ref[2:3, :] + b_ref[...])
            return out[HALO:HALO + ts]

        gate = conv(g_ref, gp_ref, gn_ref, wg_ref, bg_ref)
        up = conv(u_ref, up_ref, un_ref, wu_ref, bu_ref)
        gu_ref[0] = gate.astype(BF16)
        gu_ref[1] = up.astype(BF16)
        a_ref[...] = (gate * _sigmoid(gate) * up).astype(BF16)

    gc, gp, gn = _halo_specs(ts, tc, 0)
    uc, up_, un = _halo_specs(ts, tc, nj)
    wg = pl.BlockSpec((3, tc), lambda j, i: (0, j))
    wu = pl.BlockSpec((3, tc), lambda j, i: (0, j + nj))
    bg = pl.BlockSpec((1, tc), lambda j, i: (0, j))
    bu = pl.BlockSpec((1, tc), lambda j, i: (0, j + nj))
    return pl.pallas_call(
        body, name=name, grid=(nj, ni),
        in_specs=[gc, gp, gn(s // HALO), uc, up_, un(s // HALO), wg, wu, bg, bu],
        out_specs=[pl.BlockSpec((ts, tc), lambda j, i: (i, j)), pl.BlockSpec((2, ts, tc), lambda j, i: (0, i, j))],
        out_shape=[jax.ShapeDtypeStruct((s, f), BF16), jax.ShapeDtypeStruct((2, s, f), BF16)], compiler_params=_params(),
    )(u, u, u, u, u, u, cw, cw, cb, cb)


def _ffn_bwd(gu, u, da, cw, name):
    _, s, f = gu.shape
    ts, tc = _ffn_tiles(s, f)
    nj, ni = f // tc, s // ts

    def body(gu_ref, gup_ref, gun_ref, da_ref, dap_ref, dan_ref, xg_ref, xu_ref, wg_ref, wu_ref,
             du_ref, dcw_ref, dcb_ref):
        i = pl.program_id(1)

        @pl.when(i == 0)
        def _():
            dcw_ref[...] = jnp.zeros_like(dcw_ref)
            dcb_ref[...] = jnp.zeros_like(dcb_ref)

        rows = ts + 2 * HALO
        mid = slice(HALO, HALO + ts)
        da = jnp.concatenate([jnp.where(i > 0, dap_ref[...], 0.0), da_ref[...], jnp.where(i < ni - 1, dan_ref[...], 0.0)], axis=0)
        def rows_of(half):
            before = gup_ref[half].astype(F32)[PACKED - HALO:]
            after = gun_ref[half].astype(F32)[:HALO]
            return jnp.concatenate([before, gu_ref[half].astype(F32), after], axis=0)

        gate, up = rows_of(0), rows_of(1)
        sg = _sigmoid(gate)
        d_up = da * gate * sg
        d_gate = da * up * (sg * (1.0 + gate * (1.0 - sg)))
        for half, (dd, x_ref, w_ref) in enumerate(((d_gate, xg_ref, wg_ref), (d_up, xu_ref, wu_ref))):
            before = pltpu.roll(dd, 1, axis=0)
            after = pltpu.roll(dd, rows - 1, axis=0)
            du_ref[half] = (before * w_ref[2:3, :] + dd * w_ref[1:2, :] + after * w_ref[0:1, :])[mid].astype(BF16)
            x = x_ref[...]
            dcb_ref[half, 0:1, :] += jnp.sum(dd[mid], axis=0, keepdims=True)
            for k, shifted in enumerate((after, dd, before)):
                dcw_ref[half, k, 0:1, :] += jnp.sum(shifted[mid] * x, axis=0, keepdims=True)

    per = ts // PACKED
    cur3 = pl.BlockSpec((2, ts, tc), lambda j, i: (0, i, j))
    prev3 = pl.BlockSpec((2, PACKED, tc), lambda j, i: (0, jnp.maximum(i * per - 1, 0), j))
    next3 = pl.BlockSpec((2, PACKED, tc), lambda j, i: (0, jnp.minimum((i + 1) * per, s // PACKED - 1), j))
    cur, prev, nxt = _halo_specs(ts, tc, 0)
    return pl.pallas_call(
        body, name=name, grid=(nj, ni),
        in_specs=[cur3, prev3, next3, cur, prev, nxt(s // HALO),
                  pl.BlockSpec((ts, tc), lambda j, i: (i, j)), pl.BlockSpec((ts, tc), lambda j, i: (i, j + nj)),
                  pl.BlockSpec((3, tc), lambda j, i: (0, j)), pl.BlockSpec((3, tc), lambda j, i: (0, j + nj))],
        out_specs=[cur3, pl.BlockSpec((2, 3, 8, tc), lambda j, i: (0, 0, 0, j)),
                   pl.BlockSpec((2, 8, tc), lambda j, i: (0, 0, j))],
        out_shape=[jax.ShapeDtypeStruct((2, s, f), BF16),
                   jax.ShapeDtypeStruct((2, 3, 8, f), F32), jax.ShapeDtypeStruct((2, 8, f), F32)],
        compiler_params=_params(),
    )(gu, gu, gu, da, da, da, u, u, cw, cw)


def _loss_head(y, target, name):
    s, d = y.shape
    ts = _tile(s, 256, 16)

    def body(y_ref, t_ref, dy_ref, dyb_ref, l_ref):
        @pl.when(pl.program_id(0) == 0)
        def _():
            l_ref[...] = jnp.zeros_like(l_ref)

        err = y_ref[...] - t_ref[...]
        dy = err / d
        dy_ref[...] = dy
        dyb_ref[...] = dy.astype(BF16)
        l_ref[...] += jnp.zeros((8, LANES), F32) + 0.5 * jnp.sum(jnp.sum(err * err, axis=-1, keepdims=True) / d)

    blk = pl.BlockSpec((ts, d), lambda i: (i, 0))
    return pl.pallas_call(
        body, name=name, grid=(s // ts,),
        in_specs=[blk, blk], out_specs=[blk, blk, pl.BlockSpec((8, LANES), lambda i: (0, 0))],
        out_shape=[jax.ShapeDtypeStruct((s, d), F32), jax.ShapeDtypeStruct((s, d), BF16), jax.ShapeDtypeStruct((8, LANES), F32)],
        compiler_params=_params(),
    )(y, target)


SMALL = ("ln1_g", "qn_a", "kn_a", "rpb", "qn_b", "kn_b", "sink", "on_a", "on_b", "ln2_g", "conv_b", "conv_w")
PACK_ALIGN = 8 * LANES


def _pack(arrays):
    flat = []
    for a in arrays:
        a = a.reshape(-1)
        flat.append(jnp.pad(a, (0, -a.size % PACK_ALIGN)))
    return jnp.concatenate(flat).reshape(-1, LANES)


def _unpack(packed, like):
    out, at = [], 0
    flat = packed.reshape(-1)
    for a in like:
        out.append(flat[at:at + a.size].reshape(a.shape))
        at += a.size + (-a.size % PACK_ALIGN)
    return out


def _matmul_tiles(s, k, j):
    return dict(ti=_tile(s, 512, 16), tj=_tile(j, 1536, LANES), tk=_tile(k, 2048, LANES))


def kernel(x, positions, ln1_g, w_in, qn_a, kn_a, rpb, qn_b, kn_b, sink, on_a, on_b, w_out, ln2_g, w_up, conv_w, conv_b, w_down, loss_target, m_ln1_g, m_w_in, m_qn_a, m_kn_a, m_rpb, m_qn_b, m_kn_b, m_sink, m_on_a, m_on_b, m_w_out, m_ln2_g, m_w_up, m_conv_w, m_conv_b, m_w_down, v_ln1_g, v_w_in, v_qn_a, v_kn_a, v_rpb, v_qn_b, v_kn_b, v_sink, v_on_a, v_on_b, v_w_out, v_ln2_g, v_w_up, v_conv_w, v_conv_b, v_w_down):
    weights = dict(ln1_g=ln1_g, w_in=w_in, qn_a=qn_a, kn_a=kn_a, rpb=rpb, qn_b=qn_b, kn_b=kn_b, sink=sink, on_a=on_a,
                   on_b=on_b, w_out=w_out, ln2_g=ln2_g, w_up=w_up, conv_w=conv_w, conv_b=conv_b, w_down=w_down)
    mom1 = dict(ln1_g=m_ln1_g, w_in=m_w_in, qn_a=m_qn_a, kn_a=m_kn_a, rpb=m_rpb, qn_b=m_qn_b, kn_b=m_kn_b, sink=m_sink,
                on_a=m_on_a, on_b=m_on_b, w_out=m_w_out, ln2_g=m_ln2_g, w_up=m_w_up, conv_w=m_conv_w, conv_b=m_conv_b,
                w_down=m_w_down)
    mom2 = dict(ln1_g=v_ln1_g, w_in=v_w_in, qn_a=v_qn_a, kn_a=v_kn_a, rpb=v_rpb, qn_b=v_qn_b, kn_b=v_kn_b, sink=v_sink,
                on_a=v_on_a, on_b=v_on_b, w_out=v_w_out, ln2_g=v_ln2_g, w_up=v_w_up, conv_w=v_conv_w, conv_b=v_conv_b,
                w_down=v_w_down)
    order = ("ln1_g", "w_in", "qn_a", "kn_a", "rpb", "qn_b", "kn_b", "sink", "on_a", "on_b", "w_out", "ln2_g", "w_up",
             "conv_w", "conv_b", "w_down")

    depth, d = ln1_g.shape
    s = x.shape[1]
    ha = on_a.shape[1] // HEAD
    hq = on_b.shape[1] // HEAD
    pw = w_in.shape[2] * N_DEV
    hkv = (pw - 3 * ha * HEAD - hq * HEAD) // (2 * HEAD)
    f = w_down.shape[1] * N_DEV
    mix = (ha + hq) * HEAD
    cfg = (ha, hq, hkv)
    fs = conv_w.shape[2]
    dev = 4 * lax.axis_index("x") + 2 * lax.axis_index("y") + lax.axis_index("c")
    core = lax.axis_index("c").astype(jnp.int32).reshape(1)

    shard = {n: weights[n].astype(BF16) for n in ("w_in", "w_out", "w_up", "w_down")}

    def unshard(n, g):
        if n in ("w_in", "w_up"):
            return g.transpose(1, 0, 2).reshape(g.shape[1], N_DEV * g.shape[2])
        return g.reshape(N_DEV * g.shape[1], g.shape[2])

    full = {n: [None] * depth for n in shard}
    full["w_in"][0] = unshard("w_in", _allgather(shard["w_in"][0], "gather0_w_in"))
    cw_rows = depth * 3
    cw_pad = jnp.pad(conv_w.reshape(cw_rows, fs), ((0, -cw_rows % 8), (0, 0)))
    g_cw = _allgather(cw_pad, "gather_conv_w")
    full_cw = g_cw[:, :cw_rows].reshape(N_DEV, depth, 3, fs).transpose(1, 2, 0, 3).reshape(depth, 3, 2 * f)

    inv = ROPE_THETA ** (-jnp.arange(0, HEAD, 2, dtype=F32) / HEAD)
    ang = positions.astype(F32)[:, None] * inv[None, :]
    cos = jnp.concatenate([jnp.cos(ang), jnp.cos(ang)], axis=-1)
    sin = jnp.concatenate([-jnp.sin(ang), jnp.sin(ang)], axis=-1)
    qk = jnp.arange(GRID_W * GRID_W)
    dc_of = (qk % GRID_W) - (qk // GRID_W) + (WIN_C - 1)
    onehot = (dc_of[:, None] == jnp.arange(LANES)[None, :]).astype(BF16)

    tiles_s = _tile(s, 512, 16)
    tiles_l = _tile(s, 1024, 16)

    xs = x.reshape(s, d)
    saved = []
    for l in range(depth):
        more = l + 1 < depth

        def fwd_matmul(n, a_op, name, also=(), **kw):
            wanted = ([(n, l + 1)] if more else []) + list(also)
            if not wanted:
                return _matmul(a_op, full[n][l], dims="nn", out_dtype=F32, name=name + "_last", **kw)
            out, *got = _matmul(a_op, full[n][l], dims="nn", out_dtype=F32, name=name + "_also" * bool(also),
                                comm=_gather_comm([shard[m][k] for m, k in wanted]), **kw)
            for (m, k), g in zip(wanted, got):
                full[m][k] = unshard(m, g)
            return out

        gains = jnp.zeros((8, HEAD), F32).at[0].set(qn_a[l]).at[1].set(kn_a[l]).at[2].set(qn_b[l]).at[3].set(kn_b[l])
        on_g = jnp.concatenate([on_a[l], on_b[l]]).reshape(1, mix)
        h = _rms_fwd(xs, ln1_g[l].reshape(1, d), "ln1_fwd")
        proj = fwd_matmul("w_in", h, "proj_fwd", also=[("w_out", 0)] if l == 0 else (), ti=tiles_l, tj=_tile(pw, 1536, LANES), tk=d)
        qa, ka, va, qb, kb, vb = _qkv_fwd(proj, gains, cos, sin, cfg, "qkv_fwd")
        tb = _na_bias(rpb[l].reshape(-1), ha, "na_bias")
        if l == 0:
            oa, got = _na_fwd(qa, ka, va, tb, "na_fwd_also", comm=_gather_comm([shard["w_up"][0]]))
            full["w_up"][0] = unshard("w_up", got)
            ob, got = _wa_fwd(qb, kb, vb, sink[l], "wa_fwd_also", comm=_gather_comm([shard["w_down"][0]]))
            full["w_down"][0] = unshard("w_down", got)
        else:
            oa = _na_fwd(qa, ka, va, tb, "na_fwd")
            ob = _wa_fwd(qb, kb, vb, sink[l], "wa_fwd")
        o_n = _onorm_fwd(oa, ob, on_g, "onorm_fwd")
        x1 = fwd_matmul("w_out", o_n, "out_fwd", ti=tiles_l, tj=_tile(d, 1024, LANES), tk=mix, resid=xs)
        h2 = _rms_fwd(x1, ln2_g[l].reshape(1, d), "ln2_fwd")
        u = fwd_matmul("w_up", h2, "up_fwd", ti=tiles_l, tj=_tile(2 * f, 1024, 2 * LANES), tk=d)
        a, gu = _gate_fwd(u, full_cw[l], conv_b[l].reshape(1, 2 * f), f, "gate_fwd")
        x2 = fwd_matmul("w_down", a, "down_fwd", ti=tiles_s, tj=_tile(d, 512, LANES), tk=f, resid=x1)
        saved.append(dict(x=xs, h=h, proj=proj, gains=gains, on_g=on_g, qkv=(qa, ka, va, qb, kb, vb), tb=tb, oa=oa, ob=ob,
                          o_n=o_n, x1=x1, h2=h2, u=u, gu=gu, a=a))
        xs = x2

    dx, dx_b, loss_part = _loss_head(xs, loss_target.reshape(s, d), "loss_head")
    tile_c = _tile(s, 2048, 16)
    half_k = _tile(2 * f, f, fs)
    loss = lax.psum(loss_part[0, 0], ("x", "y", "c"))

    small_grads = [None] * depth
    big = {n: None for n in ("w_in", "w_out", "w_up", "w_down")}
    pending = None
    for l in reversed(range(depth)):
        sv = saved[l]
        qa, ka, va, qb, kb, vb = sv["qkv"]

        def update(n, layer, got):
            big[n] = _adamw(got, weights[n], mom1[n], mom2[n], "adamw_" + n, layer=layer, into=big[n])

        def grad_matmul(n, a_op, b_op, name, **kw):
            if pending is None:
                return _matmul(a_op, b_op, dims="tn", out_dtype=BF16, name=name + "_first", **kw)
            out, got = _matmul(a_op, b_op, dims="tn", out_dtype=BF16, name=name, comm=_scatter_comm([pending[n]]), **kw)
            update(n, l + 1, got)
            return out

        def own(blocks):
            return _scatter_comm([blocks]) if l == 0 else None

        gw_down = grad_matmul("w_down", sv["a"], dx_b, "down_bwd_w", ti=_tile(f, 1408, LANES), tj=_tile(d, 1024, LANES),
                              tk=tile_c, j_outer=False)
        da = _matmul(dx_b, full["w_down"][l], dims="nt", ti=tiles_s, tj=_tile(f, 2816, 2 * LANES), tk=d, out_dtype=F32,
                     name="down_bwd_x")
        du, dcw, dcb = _ffn_bwd(sv["gu"], sv["u"], da, full_cw[l], "ffn_bwd")
        gw_down = gw_down.reshape(N_DEV, f // N_DEV, d)
        dh2 = None
        for part in range(2 * f // half_k):
            comm = own(gw_down) if part == 0 else None
            dh2 = _matmul(du, full["w_up"][l], dims="nt", ti=tiles_s, tj=_tile(d, 1024, LANES), tk=half_k, out_dtype=F32,
                          name=f"up_bwd_x{part}" + "_own" * bool(comm), k_blocks=(part, 1), resid=dh2, halved="a", comm=comm)
            if comm:
                dh2, got = dh2
                update("w_down", 0, got)
        gw_up = grad_matmul("w_up", sv["h2"], du, "up_bwd_w", ti=_tile(d, 1024, LANES), tj=fs, tk=tile_c, dev_major=True,
                            halved="b")
        dx1, dx1_b, dln2 = _rms_bwd(sv["x1"], ln2_g[l].reshape(1, d), dh2, dx, "ln2_bwd")
        don = _matmul(dx1_b, full["w_out"][l], dims="nt", ti=tiles_l, tj=mix, tk=d, out_dtype=F32, name="out_bwd_x")
        gw_out = grad_matmul("w_out", sv["o_n"], dx1_b, "out_bwd_w", ti=_tile(mix, 1024, LANES), tj=_tile(d, 1024, LANES),
                             tk=tile_c, j_outer=False)
        gw_out = gw_out.reshape(N_DEV, mix // N_DEV, d)
        doa, dob, don_g = _onorm_bwd(sv["oa"], sv["ob"], sv["on_g"], don, "onorm_bwd")
        comm = _chip_comm([_pair_sums(gw_up, core, "rs0_w_up")]) if l == 0 else None
        dqa, dka, dva, dtb, *got = _na_bwd(qa, ka, va, sv["tb"], doa, "na_bwd" + "_own" * (l == 0), comm=comm)
        if got:
            update("w_up", 0, got[0])
        dqb, dkb, dvb, dsink, *got = _wa_bwd(qb, kb, vb, sink[l], dob, "wa_bwd" + "_own" * (l == 0), comm=own(gw_out))
        if got:
            update("w_out", 0, got[0])
        drpb = _rpb_grad(dtb, onehot, "rpb_grad")
        dproj, dgains = _qkv_bwd(sv["proj"], sv["gains"], cos, sin, (dqa, dka, dva, dqb, dkb, dvb), cfg, "qkv_bwd")
        dh = _matmul(dproj, full["w_in"][l], dims="nt", ti=tiles_s, tj=_tile(d, 1024, LANES), tk=pw, out_dtype=F32, name="proj_bwd_x")
        gw_in = grad_matmul("w_in", sv["h"], dproj, "proj_bwd_w", ti=_tile(d, 1024, LANES), tj=_tile(pw, 1536, LANES), tk=tile_c)
        dx, dx_b, dln1 = _rms_bwd(sv["x"], ln1_g[l].reshape(1, d), dh, dx1, "ln1_bwd")

        small_grads[l] = dict(
            ln1_g=dln1[0], qn_a=dgains[0], kn_a=dgains[1], rpb=drpb, qn_b=dgains[2], kn_b=dgains[3], sink=dsink[:, 0, 0],
            on_a=don_g[0, :ha * HEAD], on_b=don_g[0, ha * HEAD:], ln2_g=dln2[0],
            conv_b=dcb[:, 0, :].reshape(2 * f), conv_w=dcw[:, :, 0, :].transpose(1, 0, 2).reshape(3, 2 * f))

        pending = dict(w_in=gw_in.reshape(d, N_DEV, pw // N_DEV).transpose(1, 0, 2), w_out=gw_out, w_up=gw_up, w_down=gw_down)

    summed = _chip_exchange(_pair_sums(pending["w_in"], core, "rs0_w_in"), "rs0_w_in_ici")
    big["w_in"] = _adamw(summed, weights["w_in"], mom1["w_in"], mom2["w_in"], "adamw0_w_in", layer=0, into=big["w_in"])

    grads_l = [small_grads[l][n] for l in range(depth) for n in SMALL]
    gathered = _allgather(_pack(grads_l), "gather_small")
    zeros_cw = jnp.zeros((3, 2 * f), F32)

    def small_state(src):
        return _pack([zeros_cw if n == "conv_w" else src[n][l] for l in range(depth) for n in SMALL])

    sm = _adamw(gathered, small_state(weights), small_state(mom1), small_state(mom2), "adamw_small")
    sm = [_unpack(t, grads_l) for t in sm]
    small_out = {n: [jnp.stack([sm[k][l * len(SMALL) + i] for l in range(depth)]) for k in range(4)]
                 for i, n in enumerate(SMALL)}
    cw_grad = lax.dynamic_slice_in_dim(small_out["conv_w"][0], dev * fs, fs, axis=2)
    cw_rows_pad = cw_rows + (-cw_rows % 8)

    def rows8(a):
        return jnp.pad(a.reshape(cw_rows, fs), ((0, cw_rows_pad - cw_rows), (0, 0)))

    cw_res = _adamw(rows8(cw_grad)[None], rows8(conv_w), rows8(m_conv_w), rows8(v_conv_w), "adamw_conv_w")
    small_out["conv_w"] = [t[:cw_rows].reshape(depth, 3, fs) for t in cw_res]

    results = {n: (big[n] if n in big else small_out[n]) for n in order}
    grad_x = dx.reshape(1, s, d)
    return (loss, grad_x, *[results[n][0] for n in order], *[results[n][1] for n in order],
            *[results[n][2] for n in order], *[results[n][3] for n in order])
```

```python
import math

import jax
import jax.numpy as jnp
from jax import lax
from jax.experimental import pallas as pl
from jax.experimental.pallas import tpu as pltpu

F32 = jnp.float32
BF16 = jnp.bfloat16

HEAD = 128
GRID_W = 64
WIN_R = 8
WIN_C = 16
BAND = 128
ROPE_THETA = 10000.0
EPS = 1e-6
NEG = -1e30
SCALE = 1.0 / math.sqrt(HEAD)

ADAM_LR = 0.001
ADAM_B1 = 0.9
ADAM_B2 = 0.999
ADAM_EPS = 1e-08
ADAM_WD = 0.01
ADAM_STEP = 10

N_DEV = 8
LANES = 128
VMEM_LIMIT_BYTES = 56 * 2 ** 20
MESH = pl.DeviceIdType.MESH
ANY = pl.BlockSpec(memory_space=pl.ANY)
SMEM = pl.BlockSpec(memory_space=pltpu.SMEM)


def _params():
    return pltpu.CompilerParams(vmem_limit_bytes=VMEM_LIMIT_BYTES)


def _tile(n, pref, align):
    t = min(n, pref)
    t -= t % align
    while t > 0 and n % t:
        t -= align
    return t if t > 0 else n


def _place():
    x, y, c = lax.axis_index("x"), lax.axis_index("y"), lax.axis_index("c")
    chips = [(1 - x, y), (x, 1 - y), (1 - x, 1 - y)]
    return x, y, c, chips


COPIES_PER_ARRAY = N_DEV - 1


def _comm_scratch(n_arrays):
    return [pltpu.SemaphoreType.DMA((COPIES_PER_ARRAY * n_arrays,)), pltpu.SemaphoreType.DMA((COPIES_PER_ARRAY * n_arrays,)),
            pltpu.SemaphoreType.DMA((n_arrays,))]


def _gather_plan(src_refs, out_refs, send_sems, recv_sems, local_sems):
    x, y, c, chips = _place()
    me, sibling = (x, y, c), (x, y, 1 - c)

    def slot(a, px, py, pc):
        return out_refs[a].at[4 * px + 2 * py + pc]

    def copy(a, k, block, to, src=None):
        return pltpu.make_async_remote_copy(
            src_ref=slot(a, *block) if src is None else src, dst_ref=slot(a, *block),
            send_sem=send_sems.at[COPIES_PER_ARRAY * a + k], recv_sem=recv_sems.at[COPIES_PER_ARRAY * a + k],
            device_id=to, device_id_type=MESH)

    def mine(a):
        return pltpu.make_async_copy(src_refs[a], slot(a, *me), local_sems.at[a])

    def first(a):
        return [copy(a, 0, me, sibling, src=src_refs[a])] + [
            copy(a, 1 + j, me, (*chip, c), src=src_refs[a]) for j, chip in enumerate(chips)]

    def passed(a):
        return [copy(a, 4 + j, (*chip, c), sibling) for j, chip in enumerate(chips)]

    def start():
        for a in range(len(src_refs)):
            mine(a).start()
            for cp in first(a):
                cp.start()

    def finish():
        forwards = [passed(a) for a in range(len(src_refs))]
        for a in range(len(src_refs)):
            for j, chip in enumerate(chips):
                copy(a, 1 + j, (*chip, c), me).wait_recv()
                forwards[a][j].start()
        for a in range(len(src_refs)):
            copy(a, 0, sibling, me).wait_recv()
            for j, chip in enumerate(chips):
                copy(a, 4 + j, (*chip, 1 - c), me).wait_recv()
            for cp in first(a) + forwards[a]:
                cp.wait_send()
            mine(a).wait()

    return start, finish


def _scatter_plan(src_refs, out_refs, send_sems, recv_sems, local_sems):
    x, y, c, _ = _place()
    me = 4 * x + 2 * y + c

    def peer(k):
        px = 1 - x if k & 4 else x
        py = 1 - y if k & 2 else y
        pc = 1 - c if k & 1 else c
        return (px, py, pc), 4 * px + 2 * py + pc

    def copy(a, k, outgoing):
        to, idx = peer(k)
        return pltpu.make_async_remote_copy(
            src_ref=src_refs[a].at[idx], dst_ref=out_refs[a].at[me if outgoing else idx],
            send_sem=send_sems.at[COPIES_PER_ARRAY * a + k - 1], recv_sem=recv_sems.at[COPIES_PER_ARRAY * a + k - 1],
            device_id=to, device_id_type=MESH)

    def mine(a):
        return pltpu.make_async_copy(src_refs[a].at[me], out_refs[a].at[me], local_sems.at[a])

    def start():
        for a in range(len(src_refs)):
            mine(a).start()
            for k in range(1, N_DEV):
                copy(a, k, True).start()

    def finish():
        for a in range(len(src_refs)):
            for k in range(1, N_DEV):
                copy(a, k, False).wait_recv()
            for k in range(1, N_DEV):
                copy(a, k, True).wait_send()
            mine(a).wait()

    return start, finish


def _allgather(v, name):
    def body(v_ref, out_ref, send_sems, recv_sems, local_sems):
        start, finish = _gather_plan([v_ref], [out_ref], send_sems, recv_sems, local_sems)
        start()
        finish()

    return pl.pallas_call(
        body, name=name,
        out_shape=jax.ShapeDtypeStruct((N_DEV,) + v.shape, v.dtype),
        in_specs=[ANY], out_specs=ANY, scratch_shapes=_comm_scratch(1),
    )(v)


def _sibling_exchange(g, name):
    def body(g_ref, out_ref, send_sems, recv_sems):
        x, y, c, _ = _place()
        sibling = (x, y, 1 - c)
        copies = []
        for j in range(4):
            copies.append(pltpu.make_async_remote_copy(
                src_ref=g_ref.at[2 * j + (1 - c)], dst_ref=out_ref.at[j],
                send_sem=send_sems.at[j], recv_sem=recv_sems.at[j], device_id=sibling, device_id_type=MESH))
        for cp in copies:
            cp.start()
        for cp in copies:
            cp.wait_recv()
        for cp in copies:
            cp.wait_send()

    return pl.pallas_call(
        body, name=name,
        out_shape=jax.ShapeDtypeStruct((4,) + g.shape[1:], g.dtype),
        in_specs=[ANY], out_specs=ANY,
        scratch_shapes=[pltpu.SemaphoreType.DMA((4,)), pltpu.SemaphoreType.DMA((4,))],
    )(g)


def _pair_sum(g, got, core, name):
    _, r, c = g.shape
    tr = _tile(r, max(16, (1 << 20) // c), 16)

    def body(core_ref, g_ref, got_ref, o_ref):
        del core_ref
        o_ref[...] = (g_ref[...].astype(F32) + got_ref[...].astype(F32)).astype(o_ref.dtype)

    return pl.pallas_call(
        body, name=name,
        out_shape=jax.ShapeDtypeStruct((4, r, c), g.dtype),
        grid_spec=pltpu.PrefetchScalarGridSpec(
            num_scalar_prefetch=1, grid=(4, r // tr),
            in_specs=[pl.BlockSpec((None, tr, c), lambda j, i, core_ref: (2 * j + core_ref[0], i, 0)),
                      pl.BlockSpec((None, tr, c), lambda j, i, core_ref: (j, i, 0))],
            out_specs=pl.BlockSpec((None, tr, c), lambda j, i, core_ref: (j, i, 0))),
        compiler_params=_params(),
    )(core, g, got)


def _chip_plan(src_refs, out_refs, send_sems, recv_sems, local_sems):
    x, y, c, chips = _place()

    def copies(a):
        return [pltpu.make_async_remote_copy(
            src_ref=src_refs[a].at[2 * px + py], dst_ref=out_refs[a].at[k],
            send_sem=send_sems.at[COPIES_PER_ARRAY * a + k], recv_sem=recv_sems.at[COPIES_PER_ARRAY * a + k],
            device_id=(px, py, c), device_id_type=MESH) for k, (px, py) in enumerate(chips)]

    def mine(a):
        return pltpu.make_async_copy(src_refs[a].at[2 * x + y], out_refs[a].at[3], local_sems.at[a])

    def start():
        for a in range(len(src_refs)):
            mine(a).start()
            for cp in copies(a):
                cp.start()

    def finish():
        for a in range(len(src_refs)):
            for cp in copies(a):
                cp.wait_recv()
            for cp in copies(a):
                cp.wait_send()
            mine(a).wait()

    return start, finish


def _chip_comm(blocks):
    return _chip_plan, blocks, [jax.ShapeDtypeStruct(p.shape, p.dtype) for p in blocks]


def _pair_sums(g, core, name):
    got = _sibling_exchange(g, name + "_d2d")
    return _pair_sum(g, got, core, name + "_pair")


def _adamw(parts, w, m, v, name, layer=None, into=None):
    n_parts, r, c = parts.shape
    tr = _tile(r, max(8, (1 << 19) // c), 16 if parts.dtype == BF16 else 8)
    c1 = 1.0 - ADAM_B1 ** ADAM_STEP
    c2 = 1.0 - ADAM_B2 ** ADAM_STEP
    n_into = 0 if into is None else len(into)

    def body(p_ref, w_ref, m_ref, v_ref, *rest):
        g_out, d_out, m_out, v_out = rest[n_into:]
        g = p_ref[0].astype(F32)
        for k in range(1, n_parts):
            g = g + p_ref[k].astype(F32)
        m2 = ADAM_B1 * m_ref[...] + (1.0 - ADAM_B1) * g
        v2 = ADAM_B2 * v_ref[...] + (1.0 - ADAM_B2) * (g * g)
        g_out[...] = g
        m_out[...] = m2
        v_out[...] = v2
        d_out[...] = -ADAM_LR * ((m2 / c1) / (jnp.sqrt(v2 / c2) + ADAM_EPS) + ADAM_WD * w_ref[...])

    if layer is None:
        blk = pl.BlockSpec((tr, c), lambda i: (i, 0))
        out = jax.ShapeDtypeStruct((r, c), F32)
    else:
        blk = pl.BlockSpec((None, tr, c), lambda i: (layer, i, 0))
        out = jax.ShapeDtypeStruct(w.shape, F32)
    return pl.pallas_call(
        body, name=name, grid=(r // tr,),
        in_specs=[pl.BlockSpec((n_parts, tr, c), lambda i: (0, i, 0)), blk, blk, blk] + [ANY] * n_into,
        out_specs=[blk, blk, blk, blk], out_shape=[out, out, out, out],
        input_output_aliases={4 + k: k for k in range(n_into)},
        compiler_params=_params(),
    )(parts, w, m, v, *(into or ()))


def _host_exchange(body, comm, grid, n_in, n_out):
    plan, comm_in, comm_out = comm
    n = len(comm_in)

    def wrapped(*refs):
        ins, cin = refs[:n_in], refs[n_in:n_in + n]
        outs, cout = refs[n_in + n:n_in + n + n_out], refs[n_in + n + n_out:n_in + 2 * n + n_out]
        rest = refs[n_in + 2 * n + n_out:]
        start, finish = plan(cin, cout, *rest[len(rest) - 3:])
        steps = [pl.program_id(axis) for axis in range(len(grid))]
        first, last = steps[0] == 0, steps[0] == grid[0] - 1
        for axis in range(1, len(grid)):
            first, last = first & (steps[axis] == 0), last & (steps[axis] == grid[axis] - 1)

        @pl.when(first)
        def _():
            start()

        body(*ins, *outs, *rest[:len(rest) - 3])

        @pl.when(last)
        def _():
            finish()

    return wrapped, list(comm_in), [ANY] * n, list(comm_out), _comm_scratch(n)


def _gather_comm(shards):
    return _gather_plan, shards, [jax.ShapeDtypeStruct((N_DEV,) + v.shape, v.dtype) for v in shards]


def _scatter_comm(blocks):
    return _scatter_plan, blocks, [jax.ShapeDtypeStruct(g.shape, g.dtype) for g in blocks]


def _matmul(a, b, *, dims, ti, tj, tk, out_dtype, name, j_outer=True, resid=None, dev_major=False, comm=None,
            k_blocks=None, halved=None):
    a_shape = (a.shape[1], 2 * a.shape[2]) if halved == "a" else a.shape
    b_shape = (b.shape[1], 2 * b.shape[2]) if halved == "b" else b.shape
    if dims == "nn":
        (I, K), (K2, J) = a_shape, b_shape
    elif dims == "nt":
        (I, K), (J, K2) = a_shape, b_shape
    else:
        (K, I), (K2, J) = a_shape, b_shape
    assert K == K2 and I % ti == 0 and J % tj == 0 and K % tk == 0, (name, a.shape, b.shape, ti, tj, tk)
    assert halved is None or (halved, dims) in (("a", "nt"), ("b", "tn")), (name, halved, dims)
    k0, nk = k_blocks if k_blocks is not None else (0, K // tk)
    ni, nj = I // ti, J // tj

    def ij(g0, g1):
        return (g1, g0) if j_outer else (g0, g1)

    if dims == "nn":
        a_spec = pl.BlockSpec((ti, tk), lambda g0, g1, k: (ij(g0, g1)[0], k0 + k))
        b_spec = pl.BlockSpec((tk, tj), lambda g0, g1, k: (k0 + k, ij(g0, g1)[1]))
        dn = (((1,), (0,)), ((), ()))
    elif dims == "nt":
        a_spec = pl.BlockSpec((ti, tk), lambda g0, g1, k: (ij(g0, g1)[0], k0 + k))
        if halved == "a":
            per = K // 2 // tk
            a_spec = pl.BlockSpec((None, ti, tk), lambda g0, g1, k: ((k0 + k) // per, ij(g0, g1)[0], (k0 + k) % per))
        b_spec = pl.BlockSpec((tj, tk), lambda g0, g1, k: (ij(g0, g1)[1], k0 + k))
        dn = (((1,), (1,)), ((), ()))
    else:
        a_spec = pl.BlockSpec((tk, ti), lambda g0, g1, k: (k0 + k, ij(g0, g1)[0]))
        b_spec = pl.BlockSpec((tk, tj), lambda g0, g1, k: (k0 + k, ij(g0, g1)[1]))
        if halved == "b":
            per = J // 2 // tj
            b_spec = pl.BlockSpec((None, tk, tj), lambda g0, g1, k: (ij(g0, g1)[1] // per, k0 + k, ij(g0, g1)[1] % per))
        dn = (((0,), (0,)), ((), ()))
    in_specs = [a_spec, b_spec]
    operands = [a, b]
    if resid is not None:
        in_specs.append(pl.BlockSpec((ti, tj), lambda g0, g1, k: ij(g0, g1)))
        operands.append(resid)
    if dev_major:
        out_spec = pl.BlockSpec((None, ti, tj), lambda g0, g1, k: (ij(g0, g1)[1], ij(g0, g1)[0], 0))
        out_shape = jax.ShapeDtypeStruct((nj, I, tj), out_dtype)
    else:
        out_spec = pl.BlockSpec((ti, tj), lambda g0, g1, k: ij(g0, g1))
        out_shape = jax.ShapeDtypeStruct((I, J), out_dtype)

    grid = (nj, ni, nk) if j_outer else (ni, nj, nk)
    n_in = len(operands)
    n_comm = 0
    out_specs, out_shapes = [out_spec], [out_shape]
    scratch = [pltpu.VMEM((ti, tj), F32)] if nk > 1 else []
    if comm is not None:
        plan, comm_in, comm_out = comm
        n_comm = len(comm_in)
        operands += list(comm_in)
        in_specs += [ANY] * n_comm
        out_specs += [ANY] * n_comm
        out_shapes += list(comm_out)
        scratch += _comm_scratch(n_comm)

    def body(*refs):
        a_ref, b_ref = refs[0], refs[1]
        r_ref = refs[2] if resid is not None else None
        o_ref = refs[n_in + n_comm]
        if comm is not None:
            start, finish_comm = plan(refs[n_in:n_in + n_comm], refs[n_in + n_comm + 1:n_in + 2 * n_comm + 1], *refs[-3:])
            steps = [pl.program_id(axis) for axis in range(3)]

            @pl.when((steps[0] == 0) & (steps[1] == 0) & (steps[2] == 0))
            def _():
                start()

        part = lax.dot_general(a_ref[...].astype(BF16), b_ref[...].astype(BF16), dn, preferred_element_type=F32)

        def finish(acc):
            if r_ref is not None:
                acc = acc + r_ref[...]
            o_ref[...] = acc.astype(o_ref.dtype)

        if nk == 1:
            finish(part)
        else:
            acc_ref = refs[n_in + 2 * n_comm + 1]
            k = pl.program_id(2)

            @pl.when(k == 0)
            def _():
                acc_ref[...] = part

            @pl.when(k > 0)
            def _():
                acc_ref[...] += part

            @pl.when(k == nk - 1)
            def _():
                finish(acc_ref[...])

        if comm is not None:
            @pl.when((steps[0] == grid[0] - 1) & (steps[1] == grid[1] - 1) & (steps[2] == grid[2] - 1))
            def _():
                finish_comm()

    res = pl.pallas_call(
        body, name=name, grid=grid,
        in_specs=in_specs, out_specs=out_specs, out_shape=out_shapes,
        scratch_shapes=scratch, compiler_params=_params(),
    )(*operands)
    return res[0] if comm is None else res


ROW_TILE = 512


def _rms_fwd(x, g, name, comm=None):
    s, d = x.shape
    ts = _tile(s, ROW_TILE, 16)

    def body(x_ref, g_ref, h_ref):
        xv = x_ref[...]
        r = lax.rsqrt(jnp.mean(xv * xv, axis=-1, keepdims=True) + EPS)
        h_ref[...] = (xv * r * g_ref[...]).astype(BF16)

    return _call(
        body, comm, name=name, grid=(s // ts,), operands=[x, g],
        in_specs=[pl.BlockSpec((ts, d), lambda i: (i, 0)), pl.BlockSpec((1, d), lambda i: (0, 0))],
        out_specs=[pl.BlockSpec((ts, d), lambda i: (i, 0))],
        out_shape=[jax.ShapeDtypeStruct((s, d), BF16)], scratch_shapes=[])


def _rms_bwd(x, g, dh, dres, name, comm=None):
    s, d = x.shape
    ts = _tile(s, ROW_TILE, 16)

    def body(x_ref, g_ref, dh_ref, dres_ref, dx_ref, dxb_ref, dg_ref):
        xv = x_ref[...]
        r = lax.rsqrt(jnp.mean(xv * xv, axis=-1, keepdims=True) + EPS)
        y = xv * r
        dhv = dh_ref[...]
        gd = dhv * g_ref[...]
        dxv = dres_ref[...] + r * (gd - y * jnp.mean(gd * y, axis=-1, keepdims=True))
        dx_ref[...] = dxv
        dxb_ref[...] = dxv.astype(BF16)

        @pl.when(pl.program_id(0) == 0)
        def _():
            dg_ref[...] = jnp.zeros_like(dg_ref)

        dg_ref[0:1, :] += jnp.sum(dhv * y, axis=0, keepdims=True)

    blk = pl.BlockSpec((ts, d), lambda i: (i, 0))
    return _call(
        body, comm, name=name, grid=(s // ts,), operands=[x, g, dh, dres],
        in_specs=[blk, pl.BlockSpec((1, d), lambda i: (0, 0)), blk, blk],
        out_specs=[blk, blk, pl.BlockSpec((8, d), lambda i: (0, 0))],
        out_shape=[jax.ShapeDtypeStruct((s, d), F32), jax.ShapeDtypeStruct((s, d), BF16), jax.ShapeDtypeStruct((8, d), F32)],
        scratch_shapes=[])


def _head_norm(t, gain):
    r = lax.rsqrt(jnp.mean(t * t, axis=-1, keepdims=True) + EPS)
    return t * r * gain


def _head_norm_bwd(t, gain, dn):
    r = lax.rsqrt(jnp.mean(t * t, axis=-1, keepdims=True) + EPS)
    y = t * r
    gd = dn * gain
    dt = r * (gd - y * jnp.mean(gd * y, axis=-1, keepdims=True))
    return dt, jnp.sum(dn * y, axis=0, keepdims=True)


def _rope(n, cos, sin):
    return n * cos + pltpu.roll(n, HEAD // 2, axis=1) * sin


def _rope_bwd(do, cos, sin):
    return do * cos + pltpu.roll(do * sin, HEAD // 2, axis=1)


def _qkv_fwd(proj, gains, cos, sin, cfg, name):
    s, pw = proj.shape
    ha, hq, hkv = cfg
    ts = _tile(s, ROW_TILE, 16)

    def body(p_ref, gn_ref, cos_ref, sin_ref, qa_ref, ka_ref, va_ref, qb_ref, kb_ref, vb_ref):
        cosv, sinv = cos_ref[...], sin_ref[...]
        col = 0
        for out_ref, nh, gi, rot in ((qa_ref, ha, 0, False), (ka_ref, ha, 1, False), (va_ref, ha, None, False),
                                     (qb_ref, hq, 2, True), (kb_ref, hkv, 3, True), (vb_ref, hkv, None, False)):
            for h in range(nh):
                t = p_ref[:, col * HEAD:(col + 1) * HEAD]
                if gi is not None:
                    t = _head_norm(t, gn_ref[gi:gi + 1, :])
                if rot:
                    t = _rope(t, cosv, sinv)
                out_ref[h] = t.astype(BF16)
                col += 1

    def hm(nh):
        return pl.BlockSpec((nh, ts, HEAD), lambda i: (0, i, 0)), jax.ShapeDtypeStruct((nh, s, HEAD), BF16)

    specs, shapes = zip(hm(ha), hm(ha), hm(ha), hm(hq), hm(hkv), hm(hkv))
    tok = pl.BlockSpec((ts, HEAD), lambda i: (i, 0))
    return pl.pallas_call(
        body, name=name, grid=(s // ts,),
        in_specs=[pl.BlockSpec((ts, pw), lambda i: (i, 0)), pl.BlockSpec((8, HEAD), lambda i: (0, 0)), tok, tok],
        out_specs=list(specs), out_shape=list(shapes), compiler_params=_params(),
    )(proj, gains, cos, sin)


def _qkv_bwd(proj, gains, cos, sin, grads, cfg, name):
    s, pw = proj.shape
    ha, hq, hkv = cfg
    ts = _tile(s, 256, 16)

    def body(p_ref, gn_ref, cos_ref, sin_ref, dqa, dka, dva, dqb, dkb, dvb, dp_ref, dgn_ref):
        cosv, sinv = cos_ref[...], sin_ref[...]

        @pl.when(pl.program_id(0) == 0)
        def _():
            dgn_ref[...] = jnp.zeros_like(dgn_ref)

        col = 0
        for d_ref, nh, gi, rot in ((dqa, ha, 0, False), (dka, ha, 1, False), (dva, ha, None, False),
                                   (dqb, hq, 2, True), (dkb, hkv, 3, True), (dvb, hkv, None, False)):
            dgain = jnp.zeros((1, HEAD), F32)
            for h in range(nh):
                dt = d_ref[h]
                if rot:
                    dt = _rope_bwd(dt, cosv, sinv)
                if gi is not None:
                    dt, dg = _head_norm_bwd(p_ref[:, col * HEAD:(col + 1) * HEAD], gn_ref[gi:gi + 1, :], dt)
                    dgain = dgain + dg
                dp_ref[:, col * HEAD:(col + 1) * HEAD] = dt.astype(BF16)
                col += 1
            if gi is not None:
                dgn_ref[gi:gi + 1, :] += dgain

    def hm(nh):
        return pl.BlockSpec((nh, ts, HEAD), lambda i: (0, i, 0))

    tok = pl.BlockSpec((ts, HEAD), lambda i: (i, 0))
    small = pl.BlockSpec((8, HEAD), lambda i: (0, 0))
    return pl.pallas_call(
        body, name=name, grid=(s // ts,),
        in_specs=[pl.BlockSpec((ts, pw), lambda i: (i, 0)), small, tok, tok,
                  hm(ha), hm(ha), hm(ha), hm(hq), hm(hkv), hm(hkv)],
        out_specs=[pl.BlockSpec((ts, pw), lambda i: (i, 0)), small],
        out_shape=[jax.ShapeDtypeStruct((s, pw), BF16), jax.ShapeDtypeStruct((8, HEAD), F32)],
        compiler_params=_params(),
    )(proj, gains, cos, sin, *grads)


NA_QROWS = 32
NA_KEYS = WIN_R * GRID_W
N_DR = 2 * WIN_R - 1
N_DC = 2 * WIN_C - 1


def _na_bias(rpb_flat, n_heads, name):
    def body(rpb_ref, tb_ref):
        h = pl.program_id(0)
        qi = lax.broadcasted_iota(jnp.int32, (GRID_W, LANES), 0)
        lane = lax.broadcasted_iota(jnp.int32, (GRID_W, LANES), 1)
        kk = lane & (GRID_W - 1)
        upper = lane >= GRID_W
        dcm = kk - qi + (WIN_C - 1)
        cs = jnp.clip(qi - WIN_C // 2, 0, GRID_W - WIN_C)
        valid = (kk >= cs) & (kk < cs + WIN_C)
        base = h * (N_DR * N_DC)
        for dra in range(N_DR - 1):
            def step(j, acc, dra=dra):
                va = rpb_ref[base + dra * N_DC + j]
                vb = rpb_ref[base + (dra + 1) * N_DC + j]
                return jnp.where(dcm == j, jnp.where(upper, vb, va), acc)

            pair = lax.fori_loop(0, N_DC, step, jnp.zeros((GRID_W, LANES), F32))
            pair = jnp.where(valid, pair, NEG)
            for dr0 in range(WIN_R):
                wp, odd = divmod(dra - dr0, 2)
                if odd == 0 and 0 <= wp < WIN_R // 2:
                    tb_ref[0, dr0, :, wp * LANES:(wp + 1) * LANES] = pair

    return pl.pallas_call(
        body, name=name, grid=(n_heads,),
        in_specs=[SMEM],
        out_specs=pl.BlockSpec((1, WIN_R, GRID_W, NA_KEYS), lambda h: (h, 0, 0, 0)),
        out_shape=jax.ShapeDtypeStruct((n_heads, WIN_R, GRID_W, NA_KEYS), F32),
        compiler_params=_params(),
    )(rpb_flat)


def _na_row(b, i, nrows, qrows):
    r = b * qrows + i
    rs = jnp.clip(r - WIN_R // 2, 0, nrows - WIN_R)
    return pl.ds(pl.multiple_of(rs * GRID_W, GRID_W), NA_KEYS), rs - r + (WIN_R - 1)


def _softmax(s):
    e = jnp.exp(s - jnp.max(s, axis=-1, keepdims=True))
    return e * (1.0 / jnp.sum(e, axis=-1, keepdims=True))


_NT = (((1,), (1,)), ((), ()))
_NN = (((1,), (0,)), ((), ()))
_TN = (((0,), (0,)), ((), ()))


def _dot(a, b, dn):
    return lax.dot_general(a, b, dn, preferred_element_type=F32)


def _call(body, comm, *, name, grid, operands, in_specs, out_specs, out_shape, scratch_shapes):
    if comm is not None:
        body, more_operands, more_specs, more_shapes, sems = _host_exchange(body, comm, grid, len(operands), len(out_shape))
        operands = operands + more_operands
        in_specs = in_specs + more_specs
        out_specs = out_specs + more_specs
        out_shape = out_shape + more_shapes
        scratch_shapes = scratch_shapes + sems
    res = pl.pallas_call(body, name=name, grid=grid, in_specs=in_specs, out_specs=out_specs, out_shape=out_shape,
                         scratch_shapes=scratch_shapes, compiler_params=_params())(*operands)
    return res[0] if len(res) == 1 else res


def _na_fwd(q, k, v, tb, name, comm=None):
    nh, s, _ = q.shape
    nrows = s // GRID_W
    qrows = _tile(nrows, NA_QROWS, WIN_R)
    tq = qrows * GRID_W

    def body(q_ref, k_ref, v_ref, tb_ref, o_ref, s_scr, p_scr):
        b = pl.program_id(1)
        rows = [slice(i * GRID_W, (i + 1) * GRID_W) for i in range(qrows)]
        at = [_na_row(b, i, nrows, qrows) for i in range(qrows)]
        for i, (keys, dr0) in enumerate(at):
            s_scr[i] = _dot(q_ref[rows[i], :], k_ref[keys, :], _NT) * SCALE + tb_ref[0, dr0]
        for i in range(qrows):
            p_scr[i] = _softmax(s_scr[i]).astype(BF16)
        for i, (keys, _) in enumerate(at):
            o_ref[rows[i], :] = _dot(p_scr[i], v_ref[keys, :], _NN)

    qspec = pl.BlockSpec((None, tq, HEAD), lambda h, b: (h, b, 0))
    full = pl.BlockSpec((None, s, HEAD), lambda h, b: (h, 0, 0))
    return _call(
        body, comm, name=name, grid=(nh, nrows // qrows), operands=[q, k, v, tb],
        in_specs=[qspec, full, full, pl.BlockSpec((1, WIN_R, GRID_W, NA_KEYS), lambda h, b: (h, 0, 0, 0))],
        out_specs=[qspec], out_shape=[jax.ShapeDtypeStruct((nh, s, HEAD), F32)],
        scratch_shapes=[pltpu.VMEM((qrows, GRID_W, NA_KEYS), F32), pltpu.VMEM((qrows, GRID_W, NA_KEYS), BF16)])


def _na_bwd(q, k, v, tb, do, name, comm=None):
    nh, s, _ = q.shape
    nrows = s // GRID_W
    qrows = _tile(nrows, NA_QROWS, WIN_R)
    tq = qrows * GRID_W

    def body(q_ref, do_ref, k_ref, v_ref, tb_ref, dq_ref, dk_ref, dv_ref, dtb_ref, s_scr, dp_scr, p_scr, ds_scr):
        b = pl.program_id(1)

        @pl.when(b == 0)
        def _():
            dk_ref[...] = jnp.zeros_like(dk_ref)
            dv_ref[...] = jnp.zeros_like(dv_ref)
            dtb_ref[...] = jnp.zeros_like(dtb_ref)

        rows = [slice(i * GRID_W, (i + 1) * GRID_W) for i in range(qrows)]
        at = [_na_row(b, i, nrows, qrows) for i in range(qrows)]
        for i, (keys, dr0) in enumerate(at):
            s_scr[i] = _dot(q_ref[rows[i], :], k_ref[keys, :], _NT) * SCALE + tb_ref[0, dr0]
            dp_scr[i] = _dot(do_ref[rows[i], :], v_ref[keys, :], _NT)
        for i in range(qrows):
            p = _softmax(s_scr[i])
            dp = dp_scr[i]
            ds = p * (dp - jnp.sum(p * dp, axis=-1, keepdims=True))
            p_scr[i] = p.astype(BF16)
            s_scr[i] = ds
            ds_scr[i] = (ds * SCALE).astype(BF16)
        for i, (keys, _) in enumerate(at):
            dq_ref[rows[i], :] = _dot(ds_scr[i], k_ref[keys, :], _NN)
        for i, (keys, dr0) in enumerate(at):
            dv_ref[keys, :] += _dot(p_scr[i], do_ref[rows[i], :], _TN)
            dk_ref[keys, :] += _dot(ds_scr[i], q_ref[rows[i], :], _TN)
            dtb_ref[0, dr0] += s_scr[i]

    qspec = pl.BlockSpec((None, tq, HEAD), lambda h, b: (h, b, 0))
    full = pl.BlockSpec((None, s, HEAD), lambda h, b: (h, 0, 0))
    tbs = pl.BlockSpec((1, WIN_R, GRID_W, NA_KEYS), lambda h, b: (h, 0, 0, 0))
    hm = jax.ShapeDtypeStruct((nh, s, HEAD), F32)
    tile = (qrows, GRID_W, NA_KEYS)
    return _call(
        body, comm, name=name, grid=(nh, nrows // qrows), operands=[q, do, k, v, tb],
        in_specs=[qspec, qspec, full, full, tbs],
        out_specs=[qspec, full, full, tbs],
        out_shape=[hm, hm, hm, jax.ShapeDtypeStruct((nh, WIN_R, GRID_W, NA_KEYS), F32)],
        scratch_shapes=[pltpu.VMEM(tile, F32), pltpu.VMEM(tile, F32), pltpu.VMEM(tile, BF16), pltpu.VMEM(tile, BF16)])


def _rpb_fold(y, n_heads, name):
    def body(y_ref, o_ref):
        for h in range(n_heads):
            for dr in range(2 * WIN_R):
                acc = jnp.zeros((1, LANES), F32)
                for dr0 in range(WIN_R):
                    w = dr - dr0
                    if 0 <= w < WIN_R:
                        acc = acc + y_ref[h, dr0, w:w + 1, :]
                o_ref[h, dr:dr + 1, :] = acc

    return pl.pallas_call(
        body, name=name, out_shape=jax.ShapeDtypeStruct((n_heads, 2 * WIN_R, LANES), F32),
    )(y)


def _rpb_grad(dtb, onehot, name):
    nh = dtb.shape[0]
    rows = dtb.reshape(nh, WIN_R, GRID_W, WIN_R, GRID_W).transpose(0, 1, 3, 2, 4).reshape(nh * WIN_R * WIN_R, GRID_W * GRID_W)
    y = _matmul(rows, onehot, dims="nn", ti=rows.shape[0], tj=LANES, tk=GRID_W * GRID_W, out_dtype=F32, name=name + "_dc")
    folded = _rpb_fold(y.reshape(nh, WIN_R, WIN_R, LANES), nh, name + "_dr")
    return folded[:, :N_DR, :N_DC]


WA_WIN_TOK = 3 * BAND
WA_QBLOCKS = 2


def _wa_scores(q, kwin, t0, j, sink_ref, head0, grp):
    rows = grp * BAND
    s = _dot(q, kwin, _NT) * SCALE
    row = lax.broadcasted_iota(jnp.int32, (rows, WA_WIN_TOK), 0)
    qpos = j * BAND + (row & (BAND - 1))
    kpos = t0 + lax.broadcasted_iota(jnp.int32, (rows, WA_WIN_TOK), 1)
    s = jnp.where(jnp.abs(kpos - qpos) <= BAND, s, NEG)
    head = lax.broadcasted_iota(jnp.int32, (rows, 1), 0) // BAND
    sink = jnp.zeros((rows, 1), F32) + sink_ref[head0]
    for g in range(1, grp):
        sink = jnp.where(head == g, sink_ref[head0 + g], sink)
    m = jnp.maximum(jnp.max(s, axis=-1, keepdims=True), sink)
    e = jnp.exp(s - m)
    es = jnp.exp(sink - m)
    rz = 1.0 / (jnp.sum(e, axis=-1, keepdims=True) + es)
    return e * rz, es * rz


def _wa_window(j, s):
    return pl.multiple_of(jnp.clip((j - 1) * BAND, 0, s - WA_WIN_TOK), BAND)


def _wa_fwd(q, k, v, sink, name, comm=None):
    hq, s, _ = q.shape
    hkv = k.shape[0]
    grp = hq // hkv

    def body(sink_ref, q_ref, k_ref, v_ref, o_ref):
        kh, step = pl.program_id(0), pl.program_id(1)
        for sub in range(WA_QBLOCKS):
            j = step * WA_QBLOCKS + sub
            rows = slice(sub * BAND, (sub + 1) * BAND)
            t0 = _wa_window(j, s)
            keys = pl.ds(t0, WA_WIN_TOK)
            p, _ = _wa_scores(q_ref[:, rows, :].reshape(grp * BAND, HEAD), k_ref[keys, :], t0, j, sink_ref, kh * grp, grp)
            o_ref[:, rows, :] = _dot(p.astype(BF16), v_ref[keys, :], _NN).reshape(grp, BAND, HEAD)

    qspec = pl.BlockSpec((grp, WA_QBLOCKS * BAND, HEAD), lambda kh, j: (kh, j, 0))
    full = pl.BlockSpec((None, s, HEAD), lambda kh, j: (kh, 0, 0))
    return _call(
        body, comm, name=name, grid=(hkv, s // (WA_QBLOCKS * BAND)), operands=[sink, q, k, v],
        in_specs=[SMEM, qspec, full, full],
        out_specs=[qspec], out_shape=[jax.ShapeDtypeStruct((hq, s, HEAD), F32)], scratch_shapes=[])


def _wa_bwd(q, k, v, sink, do, name, comm=None):
    hq, s, _ = q.shape
    hkv = k.shape[0]
    grp = hq // hkv

    def body(sink_ref, q_ref, do_ref, k_ref, v_ref, dq_ref, dk_ref, dv_ref, dsink_ref):
        kh, step = pl.program_id(0), pl.program_id(1)

        @pl.when(step == 0)
        def _():
            dk_ref[...] = jnp.zeros_like(dk_ref)
            dv_ref[...] = jnp.zeros_like(dv_ref)
            dsink_ref[...] = jnp.zeros_like(dsink_ref)

        for sub in range(WA_QBLOCKS):
            j = step * WA_QBLOCKS + sub
            rows = slice(sub * BAND, (sub + 1) * BAND)
            t0 = _wa_window(j, s)
            keys = pl.ds(t0, WA_WIN_TOK)
            qs = q_ref[:, rows, :].reshape(grp * BAND, HEAD)
            dos = do_ref[:, rows, :].reshape(grp * BAND, HEAD)
            kwin, vwin = k_ref[keys, :], v_ref[keys, :]
            p, ps = _wa_scores(qs, kwin, t0, j, sink_ref, kh * grp, grp)
            dp = _dot(dos, vwin, _NT)
            dv_ref[keys, :] += _dot(p.astype(BF16), dos, _TN)
            rowdot = jnp.sum(p * dp, axis=-1, keepdims=True)
            to_sink = ps * rowdot
            for g in range(grp):
                dsink_ref[g] += jnp.zeros((8, LANES), F32) - jnp.sum(to_sink[g * BAND:(g + 1) * BAND])
            dss = (p * (dp - rowdot) * SCALE).astype(BF16)
            dq_ref[:, rows, :] = _dot(dss, kwin, _NN).reshape(grp, BAND, HEAD)
            dk_ref[keys, :] += _dot(dss, qs, _TN)

    qspec = pl.BlockSpec((grp, WA_QBLOCKS * BAND, HEAD), lambda kh, j: (kh, j, 0))
    full = pl.BlockSpec((None, s, HEAD), lambda kh, j: (kh, 0, 0))
    kv = jax.ShapeDtypeStruct((hkv, s, HEAD), F32)
    return _call(
        body, comm, name=name, grid=(hkv, s // (WA_QBLOCKS * BAND)), operands=[sink, q, do, k, v],
        in_specs=[SMEM, qspec, qspec, full, full],
        out_specs=[qspec, full, full, pl.BlockSpec((grp, 8, LANES), lambda kh, j: (kh, 0, 0))],
        out_shape=[jax.ShapeDtypeStruct((hq, s, HEAD), F32), kv, kv, jax.ShapeDtypeStruct((hq, 8, LANES), F32)],
        scratch_shapes=[])


def _onorm_fwd(oa, ob, gains, name):
    ha, s, _ = oa.shape
    hq = ob.shape[0]
    ts = _tile(s, ROW_TILE, 16)

    def body(oa_ref, ob_ref, g_ref, o_ref):
        col = 0
        for ref, nh in ((oa_ref, ha), (ob_ref, hq)):
            ss = sum(jnp.sum(ref[h] * ref[h], axis=-1, keepdims=True) for h in range(nh))
            r = lax.rsqrt(ss / (nh * HEAD) + EPS)
            for h in range(nh):
                o_ref[:, col * HEAD:(col + 1) * HEAD] = (ref[h] * r * g_ref[:, col * HEAD:(col + 1) * HEAD]).astype(BF16)
                col += 1

    mix = (ha + hq) * HEAD
    return pl.pallas_call(
        body, name=name, grid=(s // ts,),
        in_specs=[pl.BlockSpec((ha, ts, HEAD), lambda i: (0, i, 0)), pl.BlockSpec((hq, ts, HEAD), lambda i: (0, i, 0)),
                  pl.BlockSpec((1, mix), lambda i: (0, 0))],
        out_specs=pl.BlockSpec((ts, mix), lambda i: (i, 0)),
        out_shape=jax.ShapeDtypeStruct((s, mix), BF16), compiler_params=_params(),
    )(oa, ob, gains)


def _onorm_bwd(oa, ob, gains, don, name):
    ha, s, _ = oa.shape
    hq = ob.shape[0]
    ts = _tile(s, ROW_TILE, 16)
    mix = (ha + hq) * HEAD

    def body(oa_ref, ob_ref, g_ref, don_ref, doa_ref, dob_ref, dg_ref):
        @pl.when(pl.program_id(0) == 0)
        def _():
            dg_ref[...] = jnp.zeros_like(dg_ref)

        col0 = 0
        for ref, d_ref, nh in ((oa_ref, doa_ref, ha), (ob_ref, dob_ref, hq)):
            ss = sum(jnp.sum(ref[h] * ref[h], axis=-1, keepdims=True) for h in range(nh))
            r = lax.rsqrt(ss / (nh * HEAD) + EPS)
            dot = jnp.zeros((ts, 1), F32)
            for h in range(nh):
                cols = slice((col0 + h) * HEAD, (col0 + h + 1) * HEAD)
                dot = dot + jnp.sum(don_ref[:, cols] * g_ref[:, cols] * ref[h], axis=-1, keepdims=True)
            mean = dot * r / (nh * HEAD)
            for h in range(nh):
                cols = slice((col0 + h) * HEAD, (col0 + h + 1) * HEAD)
                y = ref[h] * r
                dn = don_ref[:, cols]
                d_ref[h] = (r * (dn * g_ref[:, cols] - y * mean)).astype(BF16)
                dg_ref[0:1, cols] += jnp.sum(dn * y, axis=0, keepdims=True)
            col0 += nh

    return pl.pallas_call(
        body, name=name, grid=(s // ts,),
        in_specs=[pl.BlockSpec((ha, ts, HEAD), lambda i: (0, i, 0)), pl.BlockSpec((hq, ts, HEAD), lambda i: (0, i, 0)),
                  pl.BlockSpec((1, mix), lambda i: (0, 0)), pl.BlockSpec((ts, mix), lambda i: (i, 0))],
        out_specs=[pl.BlockSpec((ha, ts, HEAD), lambda i: (0, i, 0)), pl.BlockSpec((hq, ts, HEAD), lambda i: (0, i, 0)),
                   pl.BlockSpec((8, mix), lambda i: (0, 0))],
        out_shape=[jax.ShapeDtypeStruct((ha, s, HEAD), BF16), jax.ShapeDtypeStruct((hq, s, HEAD), BF16),
                   jax.ShapeDtypeStruct((8, mix), F32)],
        compiler_params=_params(),
    )(oa, ob, gains, don)


HALO = 8
PACKED = 16


def _halo_specs(ts, tc, col_off):
    per = ts // HALO
    cur = pl.BlockSpec((ts, tc), lambda j, i: (i, j + col_off))
    prev = pl.BlockSpec((HALO, tc), lambda j, i: (jnp.maximum(i * per - 1, 0), j + col_off))

    def nxt_map(n_blocks):
        return pl.BlockSpec((HALO, tc), lambda j, i: (jnp.minimum((i + 1) * per, n_blocks - 1), j + col_off))

    return cur, prev, nxt_map


def _sigmoid(x):
    return 1.0 / (1.0 + jnp.exp(-x))


def _ffn_tiles(s, f):
    return _tile(s, 512, 16), _tile(f, 512, LANES)


def _gate_fwd(u, cw, cb, f, name):
    s = u.shape[0]
    ts, tc = _ffn_tiles(s, f)
    nj, ni = f // tc, s // ts

    def body(g_ref, gp_ref, gn_ref, u_ref, up_ref, un_ref, wg_ref, wu_ref, bg_ref, bu_ref, a_ref, gu_ref):
        i = pl.program_id(1)

        def conv(c_ref, p_ref, n_ref, w_ref, b_ref):
            ext = jnp.concatenate([jnp.where(i > 0, p_ref[...], 0.0), c_ref[...], jnp.where(i < ni - 1, n_ref[...], 0.0)], axis=0)
            rows = ts + 2 * HALO
            out = (pltpu.roll(ext, 1, axis=0) * w_ref[0:1, :] + ext * w_ref[1:2, :]
                   + pltpu.roll(ext, rows - 1, axis=0) * w_ref[2:3, :] + b_ref[...])
            return out[HALO:HALO + ts]

        gate = conv(g_ref, gp_ref, gn_ref, wg_ref, bg_ref)
        up = conv(u_ref, up_ref, un_ref, wu_ref, bu_ref)
        gu_ref[0] = gate.astype(BF16)
        gu_ref[1] = up.astype(BF16)
        a_ref[...] = (gate * _sigmoid(gate) * up).astype(BF16)

    gc, gp, gn = _halo_specs(ts, tc, 0)
    uc, up_, un = _halo_specs(ts, tc, nj)
    wg = pl.BlockSpec((3, tc), lambda j, i: (0, j))
    wu = pl.BlockSpec((3, tc), lambda j, i: (0, j + nj))
    bg = pl.BlockSpec((1, tc), lambda j, i: (0, j))
    bu = pl.BlockSpec((1, tc), lambda j, i: (0, j + nj))
    return pl.pallas_call(
        body, name=name, grid=(nj, ni),
        in_specs=[gc, gp, gn(s // HALO), uc, up_, un(s // HALO), wg, wu, bg, bu],
        out_specs=[pl.BlockSpec((ts, tc), lambda j, i: (i, j)), pl.BlockSpec((2, ts, tc), lambda j, i: (0, i, j))],
        out_shape=[jax.ShapeDtypeStruct((s, f), BF16), jax.ShapeDtypeStruct((2, s, f), BF16)], compiler_params=_params(),
    )(u, u, u, u, u, u, cw, cw, cb, cb)


def _ffn_bwd(gu, u, da, cw, name):
    _, s, f = gu.shape
    ts, tc = _ffn_tiles(s, f)
    nj, ni = f // tc, s // ts

    def body(gu_ref, gup_ref, gun_ref, da_ref, dap_ref, dan_ref, xg_ref, xu_ref, wg_ref, wu_ref,
             du_ref, dcw_ref, dcb_ref):
        i = pl.program_id(1)

        @pl.when(i == 0)
        def _():
            dcw_ref[...] = jnp.zeros_like(dcw_ref)
            dcb_ref[...] = jnp.zeros_like(dcb_ref)

        rows = ts + 2 * HALO
        mid = slice(HALO, HALO + ts)
        da = jnp.concatenate([jnp.where(i > 0, dap_ref[...], 0.0), da_ref[...], jnp.where(i < ni - 1, dan_ref[...], 0.0)], axis=0)
        def rows_of(half):
            before = gup_ref[half].astype(F32)[PACKED - HALO:]
            after = gun_ref[half].astype(F32)[:HALO]
            return jnp.concatenate([before, gu_ref[half].astype(F32), after], axis=0)

        gate, up = rows_of(0), rows_of(1)
        sg = _sigmoid(gate)
        d_up = da * gate * sg
        d_gate = da * up * (sg * (1.0 + gate * (1.0 - sg)))
        for half, (dd, x_ref, w_ref) in enumerate(((d_gate, xg_ref, wg_ref), (d_up, xu_ref, wu_ref))):
            before = pltpu.roll(dd, 1, axis=0)
            after = pltpu.roll(dd, rows - 1, axis=0)
            du_ref[half] = (before * w_ref[2:3, :] + dd * w_ref[1:2, :] + after * w_ref[0:1, :])[mid].astype(BF16)
            x = x_ref[...]
            dcb_ref[half, 0:1, :] += jnp.sum(dd[mid], axis=0, keepdims=True)
            for k, shifted in enumerate((after, dd, before)):
                dcw_ref[half, k, 0:1, :] += jnp.sum(shifted[mid] * x, axis=0, keepdims=True)

    per = ts // PACKED
    cur3 = pl.BlockSpec((2, ts, tc), lambda j, i: (0, i, j))
    prev3 = pl.BlockSpec((2, PACKED, tc), lambda j, i: (0, jnp.maximum(i * per - 1, 0), j))
    next3 = pl.BlockSpec((2, PACKED, tc), lambda j, i: (0, jnp.minimum((i + 1) * per, s // PACKED - 1), j))
    cur, prev, nxt = _halo_specs(ts, tc, 0)
    return pl.pallas_call(
        body, name=name, grid=(nj, ni),
        in_specs=[cur3, prev3, next3, cur, prev, nxt(s // HALO),
                  pl.BlockSpec((ts, tc), lambda j, i: (i, j)), pl.BlockSpec((ts, tc), lambda j, i: (i, j + nj)),
                  pl.BlockSpec((3, tc), lambda j, i: (0, j)), pl.BlockSpec((3, tc), lambda j, i: (0, j + nj))],
        out_specs=[cur3, pl.BlockSpec((2, 3, 8, tc), lambda j, i: (0, 0, 0, j)),
                   pl.BlockSpec((2, 8, tc), lambda j, i: (0, 0, j))],
        out_shape=[jax.ShapeDtypeStruct((2, s, f), BF16),
                   jax.ShapeDtypeStruct((2, 3, 8, f), F32), jax.ShapeDtypeStruct((2, 8, f), F32)],
        compiler_params=_params(),
    )(gu, gu, gu, da, da, da, u, u, cw, cw)


def _loss_head(y, target, name):
    s, d = y.shape
    ts = _tile(s, ROW_TILE, 16)

    def body(y_ref, t_ref, dy_ref, dyb_ref, l_ref):
        @pl.when(pl.program_id(0) == 0)
        def _():
            l_ref[...] = jnp.zeros_like(l_ref)

        err = y_ref[...] - t_ref[...]
        dy = err / d
        dy_ref[...] = dy
        dyb_ref[...] = dy.astype(BF16)
        l_ref[...] += jnp.zeros((8, LANES), F32) + 0.5 * jnp.sum(jnp.sum(err * err, axis=-1, keepdims=True) / d)

    blk = pl.BlockSpec((ts, d), lambda i: (i, 0))
    return pl.pallas_call(
        body, name=name, grid=(s // ts,),
        in_specs=[blk, blk], out_specs=[blk, blk, pl.BlockSpec((8, LANES), lambda i: (0, 0))],
        out_shape=[jax.ShapeDtypeStruct((s, d), F32), jax.ShapeDtypeStruct((s, d), BF16), jax.ShapeDtypeStruct((8, LANES), F32)],
        compiler_params=_params(),
    )(y, target)


SMALL = ("ln1_g", "qn_a", "kn_a", "rpb", "qn_b", "kn_b", "sink", "on_a", "on_b", "ln2_g", "conv_b", "conv_w")
PACK_ALIGN = 8 * LANES


def _pack(arrays):
    flat = []
    for a in arrays:
        a = a.reshape(-1)
        flat.append(jnp.pad(a, (0, -a.size % PACK_ALIGN)))
    return jnp.concatenate(flat).reshape(-1, LANES)


def _unpack(packed, like):
    out, at = [], 0
    flat = packed.reshape(-1)
    for a in like:
        out.append(flat[at:at + a.size].reshape(a.shape))
        at += a.size + (-a.size % PACK_ALIGN)
    return out


def kernel(x, positions, ln1_g, w_in, qn_a, kn_a, rpb, qn_b, kn_b, sink, on_a, on_b, w_out, ln2_g, w_up, conv_w, conv_b, w_down, loss_target, m_ln1_g, m_w_in, m_qn_a, m_kn_a, m_rpb, m_qn_b, m_kn_b, m_sink, m_on_a, m_on_b, m_w_out, m_ln2_g, m_w_up, m_conv_w, m_conv_b, m_w_down, v_ln1_g, v_w_in, v_qn_a, v_kn_a, v_rpb, v_qn_b, v_kn_b, v_sink, v_on_a, v_on_b, v_w_out, v_ln2_g, v_w_up, v_conv_w, v_conv_b, v_w_down):
    weights = dict(ln1_g=ln1_g, w_in=w_in, qn_a=qn_a, kn_a=kn_a, rpb=rpb, qn_b=qn_b, kn_b=kn_b, sink=sink, on_a=on_a,
                   on_b=on_b, w_out=w_out, ln2_g=ln2_g, w_up=w_up, conv_w=conv_w, conv_b=conv_b, w_down=w_down)
    mom1 = dict(ln1_g=m_ln1_g, w_in=m_w_in, qn_a=m_qn_a, kn_a=m_kn_a, rpb=m_rpb, qn_b=m_qn_b, kn_b=m_kn_b, sink=m_sink,
                on_a=m_on_a, on_b=m_on_b, w_out=m_w_out, ln2_g=m_ln2_g, w_up=m_w_up, conv_w=m_conv_w, conv_b=m_conv_b,
                w_down=m_w_down)
    mom2 = dict(ln1_g=v_ln1_g, w_in=v_w_in, qn_a=v_qn_a, kn_a=v_kn_a, rpb=v_rpb, qn_b=v_qn_b, kn_b=v_kn_b, sink=v_sink,
                on_a=v_on_a, on_b=v_on_b, w_out=v_w_out, ln2_g=v_ln2_g, w_up=v_w_up, conv_w=v_conv_w, conv_b=v_conv_b,
                w_down=v_w_down)
    order = ("ln1_g", "w_in", "qn_a", "kn_a", "rpb", "qn_b", "kn_b", "sink", "on_a", "on_b", "w_out", "ln2_g", "w_up",
             "conv_w", "conv_b", "w_down")

    depth, d = ln1_g.shape
    s = x.shape[1]
    ha = on_a.shape[1] // HEAD
    hq = on_b.shape[1] // HEAD
    pw = w_in.shape[2] * N_DEV
    hkv = (pw - 3 * ha * HEAD - hq * HEAD) // (2 * HEAD)
    f = w_down.shape[1] * N_DEV
    mix = (ha + hq) * HEAD
    cfg = (ha, hq, hkv)
    fs = conv_w.shape[2]
    dev = 4 * lax.axis_index("x") + 2 * lax.axis_index("y") + lax.axis_index("c")
    core = lax.axis_index("c").astype(jnp.int32).reshape(1)

    shard = {n: weights[n].astype(BF16) for n in ("w_in", "w_out", "w_up", "w_down")}

    def unshard(n, g):
        if n in ("w_in", "w_up"):
            return g.transpose(1, 0, 2).reshape(g.shape[1], N_DEV * g.shape[2])
        return g.reshape(N_DEV * g.shape[1], g.shape[2])

    full = {n: [None] * depth for n in shard}
    half_d = _tile(d, d // 2, 16)
    up0 = [shard["w_up"][0][:half_d], shard["w_up"][0][half_d:]]

    def travel(l, host):
        plan = {}
        if l == 0:
            plan = {"ln1": [("w_in", 0, None)], "proj": [("w_out", 0, None), ("w_up", 0, 0)], "na": [("w_up", 0, 1)],
                    "wa": [("w_down", 0, None)], "down": [("w_down", 1, None), ("w_in", 1, None)]}
        elif l + 1 < depth:
            plan = {"proj": [("w_in", l + 1, None)], "down": [("w_down", l + 1, None)]}
        if l + 1 < depth:
            plan.update({"out": [("w_out", l + 1, None)], "up": [("w_up", l + 1, None)]})
        return [key for key in plan.get(host, []) if key[1] < depth]

    arrived = {}

    def gather_of(keys):
        return _gather_comm([shard[n][k] if part is None else up0[part] for n, k, part in keys]) if keys else None

    def landed(keys, blocks):
        for (n, k, part), g in zip(keys, blocks):
            if part is None:
                full[n][k] = unshard(n, g)
            else:
                arrived[part] = g
                if len(arrived) == 2:
                    full[n][k] = unshard(n, jnp.concatenate([arrived[0], arrived[1]], axis=1))

    cw_rows = depth * 3
    cw_pad = jnp.pad(conv_w.reshape(cw_rows, fs), ((0, -cw_rows % 8), (0, 0)))
    g_cw = _allgather(cw_pad, "gather_conv_w")
    full_cw = g_cw[:, :cw_rows].reshape(N_DEV, depth, 3, fs).transpose(1, 2, 0, 3).reshape(depth, 3, 2 * f)

    inv = ROPE_THETA ** (-jnp.arange(0, HEAD, 2, dtype=F32) / HEAD)
    ang = positions.astype(F32)[:, None] * inv[None, :]
    cos = jnp.concatenate([jnp.cos(ang), jnp.cos(ang)], axis=-1)
    sin = jnp.concatenate([-jnp.sin(ang), jnp.sin(ang)], axis=-1)
    qk = jnp.arange(GRID_W * GRID_W)
    dc_of = (qk % GRID_W) - (qk // GRID_W) + (WIN_C - 1)
    onehot = (dc_of[:, None] == jnp.arange(LANES)[None, :]).astype(BF16)

    tiles_s = _tile(s, 512, 16)
    tiles_l = _tile(s, 1024, 16)

    xs = x.reshape(s, d)
    saved = []
    for l in range(depth):
        def hosting(call, host, name, *args, **kw):
            keys = travel(l, host)
            if not keys:
                return call(*args, name=name, **kw)
            out, *blocks = call(*args, name=f"{name}_g{len(keys)}", comm=gather_of(keys), **kw)
            landed(keys, blocks)
            return out

        def fwd_matmul(n, host, a_op, name, **kw):
            return hosting(lambda **k2: _matmul(a_op, full[n][l], dims="nn", out_dtype=F32, **k2), host, name, **kw)

        gains = jnp.zeros((8, HEAD), F32).at[0].set(qn_a[l]).at[1].set(kn_a[l]).at[2].set(qn_b[l]).at[3].set(kn_b[l])
        on_g = jnp.concatenate([on_a[l], on_b[l]]).reshape(1, mix)
        h = hosting(_rms_fwd, "ln1", "ln1_fwd", xs, ln1_g[l].reshape(1, d))
        proj = fwd_matmul("w_in", "proj", h, "proj_fwd", ti=tiles_l, tj=_tile(pw, 1536, LANES), tk=d)
        qa, ka, va, qb, kb, vb = _qkv_fwd(proj, gains, cos, sin, cfg, "qkv_fwd")
        tb = _na_bias(rpb[l].reshape(-1), ha, "na_bias")
        oa = hosting(_na_fwd, "na", "na_fwd", qa, ka, va, tb)
        ob = hosting(_wa_fwd, "wa", "wa_fwd", qb, kb, vb, sink[l])
        o_n = _onorm_fwd(oa, ob, on_g, "onorm_fwd")
        x1 = fwd_matmul("w_out", "out", o_n, "out_fwd", ti=tiles_l, tj=_tile(d, 1024, LANES), tk=mix, resid=xs)
        h2 = _rms_fwd(x1, ln2_g[l].reshape(1, d), "ln2_fwd")
        u = fwd_matmul("w_up", "up", h2, "up_fwd", ti=tiles_l, tj=_tile(2 * f, 1024, 2 * LANES), tk=d)
        a, gu = _gate_fwd(u, full_cw[l], conv_b[l].reshape(1, 2 * f), f, "gate_fwd")
        x2 = fwd_matmul("w_down", "down", a, "down_fwd", ti=tiles_s, tj=_tile(d, 512, LANES), tk=f, resid=x1)
        saved.append(dict(x=xs, h=h, proj=proj, gains=gains, on_g=on_g, qkv=(qa, ka, va, qb, kb, vb), tb=tb, oa=oa, ob=ob,
                          o_n=o_n, x1=x1, h2=h2, u=u, gu=gu, a=a))
        xs = x2

    dx, dx_b, loss_part = _loss_head(xs, loss_target.reshape(s, d), "loss_head")
    tile_c = _tile(s, 2048, 16)
    half_k = _tile(2 * f, f, fs)
    loss = lax.psum(loss_part[0, 0], ("x", "y", "c"))

    small_grads = [None] * depth
    big = {n: None for n in ("w_in", "w_out", "w_up", "w_down")}
    pending = None
    for l in reversed(range(depth)):
        sv = saved[l]
        qa, ka, va, qb, kb, vb = sv["qkv"]

        def update(n, layer, got):
            big[n] = _adamw(got, weights[n], mom1[n], mom2[n], "adamw_" + n, layer=layer, into=big[n])

        def grad_matmul(n, a_op, b_op, name, **kw):
            if pending is None:
                return _matmul(a_op, b_op, dims="tn", out_dtype=BF16, name=name + "_first", **kw)
            out, got = _matmul(a_op, b_op, dims="tn", out_dtype=BF16, name=name, comm=_scatter_comm([pending[n]]), **kw)
            update(n, l + 1, got)
            return out

        def own(blocks):
            return _scatter_comm([blocks]) if l == 0 else None

        gw_down = grad_matmul("w_down", sv["a"], dx_b, "down_bwd_w", ti=_tile(f, 1408, LANES), tj=_tile(d, 1024, LANES),
                              tk=tile_c, j_outer=False)
        da = _matmul(dx_b, full["w_down"][l], dims="nt", ti=tiles_s, tj=_tile(f, 2816, 2 * LANES), tk=d, out_dtype=F32,
                     name="down_bwd_x")
        du, dcw, dcb = _ffn_bwd(sv["gu"], sv["u"], da, full_cw[l], "ffn_bwd")
        gw_down = gw_down.reshape(N_DEV, f // N_DEV, d)
        dh2 = None
        for part in range(2 * f // half_k):
            comm = own(gw_down) if part == 0 else None
            dh2 = _matmul(du, full["w_up"][l], dims="nt", ti=tiles_s, tj=_tile(d, 1024, LANES), tk=half_k, out_dtype=F32,
                          name=f"up_bwd_x{part}" + "_own" * bool(comm), k_blocks=(part, 1), resid=dh2, halved="a", comm=comm)
            if comm:
                dh2, got = dh2
                update("w_down", 0, got)
        gw_up = grad_matmul("w_up", sv["h2"], du, "up_bwd_w", ti=_tile(d, 1024, LANES), tj=fs, tk=tile_c, dev_major=True,
                            halved="b")
        dx1, dx1_b, dln2 = _rms_bwd(sv["x1"], ln2_g[l].reshape(1, d), dh2, dx, "ln2_bwd")
        don = _matmul(dx1_b, full["w_out"][l], dims="nt", ti=tiles_l, tj=mix, tk=d, out_dtype=F32, name="out_bwd_x")
        gw_out = grad_matmul("w_out", sv["o_n"], dx1_b, "out_bwd_w", ti=_tile(mix, 1024, LANES), tj=_tile(d, 1024, LANES),
                             tk=tile_c, j_outer=False)
        gw_out = gw_out.reshape(N_DEV, mix // N_DEV, d)
        doa, dob, don_g = _onorm_bwd(sv["oa"], sv["ob"], sv["on_g"], don, "onorm_bwd")
        comm = _chip_comm([_pair_sums(gw_up, core, "rs0_w_up")]) if l == 0 else None
        dqa, dka, dva, dtb, *got = _na_bwd(qa, ka, va, sv["tb"], doa, "na_bwd" + "_own" * (l == 0), comm=comm)
        if got:
            update("w_up", 0, got[0])
        dqb, dkb, dvb, dsink, *got = _wa_bwd(qb, kb, vb, sink[l], dob, "wa_bwd" + "_own" * (l == 0), comm=own(gw_out))
        if got:
            update("w_out", 0, got[0])
        drpb = _rpb_grad(dtb, onehot, "rpb_grad")
        dproj, dgains = _qkv_bwd(sv["proj"], sv["gains"], cos, sin, (dqa, dka, dva, dqb, dkb, dvb), cfg, "qkv_bwd")
        dh = _matmul(dproj, full["w_in"][l], dims="nt", ti=tiles_s, tj=_tile(d, 1024, LANES), tk=pw, out_dtype=F32, name="proj_bwd_x")
        gw_in = grad_matmul("w_in", sv["h"], dproj, "proj_bwd_w", ti=_tile(d, 1024, LANES), tj=_tile(pw, 1536, LANES), tk=tile_c)
        gw_in = gw_in.reshape(d, N_DEV, pw // N_DEV).transpose(1, 0, 2)
        comm = _chip_comm([_pair_sums(gw_in, core, "rs0_w_in")]) if l == 0 else None
        dx, dx_b, dln1, *got = _rms_bwd(sv["x"], ln1_g[l].reshape(1, d), dh, dx1, "ln1_bwd" + "_own" * (l == 0), comm=comm)
        if got:
            update("w_in", 0, got[0])

        small_grads[l] = dict(
            ln1_g=dln1[0], qn_a=dgains[0], kn_a=dgains[1], rpb=drpb, qn_b=dgains[2], kn_b=dgains[3], sink=dsink[:, 0, 0],
            on_a=don_g[0, :ha * HEAD], on_b=don_g[0, ha * HEAD:], ln2_g=dln2[0],
            conv_b=dcb[:, 0, :].reshape(2 * f), conv_w=dcw[:, :, 0, :].transpose(1, 0, 2).reshape(3, 2 * f))

        pending = dict(w_in=gw_in, w_out=gw_out, w_up=gw_up, w_down=gw_down)

    grads_l = [small_grads[l][n] for l in range(depth) for n in SMALL]
    gathered = _allgather(_pack(grads_l), "gather_small")
    zeros_cw = jnp.zeros((3, 2 * f), F32)

    def small_state(src):
        return _pack([zeros_cw if n == "conv_w" else src[n][l] for l in range(depth) for n in SMALL])

    sm = _adamw(gathered, small_state(weights), small_state(mom1), small_state(mom2), "adamw_small")
    sm = [_unpack(t, grads_l) for t in sm]
    small_out = {n: [jnp.stack([sm[k][l * len(SMALL) + i] for l in range(depth)]) for k in range(4)]
                 for i, n in enumerate(SMALL)}
    cw_grad = lax.dynamic_slice_in_dim(small_out["conv_w"][0], dev * fs, fs, axis=2)
    cw_rows_pad = cw_rows + (-cw_rows % 8)

    def rows8(a):
        return jnp.pad(a.reshape(cw_rows, fs), ((0, cw_rows_pad - cw_rows), (0, 0)))

    cw_res = _adamw(rows8(cw_grad)[None], rows8(conv_w), rows8(m_conv_w), rows8(v_conv_w), "adamw_conv_w")
    small_out["conv_w"] = [t[:cw_rows].reshape(depth, 3, fs) for t in cw_res]

    results = {n: (big[n] if n in big else small_out[n]) for n in order}
    grad_x = dx.reshape(1, s, d)
    return (loss, grad_x, *[results[n][0] for n in order], *[results[n][1] for n in order],
            *[results[n][2] for n in order], *[results[n][3] for n in order])
```

```python
import math

import jax
import jax.numpy as jnp
from jax import lax
from jax.experimental import pallas as pl
from jax.experimental.pallas import tpu as pltpu

F32 = jnp.float32
BF16 = jnp.bfloat16

HEAD = 128
GRID_W = 64
WIN_R = 8
WIN_C = 16
BAND = 128
ROPE_THETA = 10000.0
EPS = 1e-6
NEG = -1e30
SCALE = 1.0 / math.sqrt(HEAD)

ADAM_LR = 0.001
ADAM_B1 = 0.9
ADAM_B2 = 0.999
ADAM_EPS = 1e-08
ADAM_WD = 0.01
ADAM_STEP = 10

N_DEV = 8
LANES = 128
VMEM_LIMIT_BYTES = 56 * 2 ** 20
MESH = pl.DeviceIdType.MESH
ANY = pl.BlockSpec(memory_space=pl.ANY)
SMEM = pl.BlockSpec(memory_space=pltpu.SMEM)


def _params():
    return pltpu.CompilerParams(vmem_limit_bytes=VMEM_LIMIT_BYTES)


def _tile(n, pref, align):
    t = min(n, pref)
    t -= t % align
    while t > 0 and n % t:
        t -= align
    return t if t > 0 else n


def _place():
    x, y, c = lax.axis_index("x"), lax.axis_index("y"), lax.axis_index("c")
    chips = [(1 - x, y), (x, 1 - y), (1 - x, 1 - y)]
    return x, y, c, chips


COPIES_PER_ARRAY = N_DEV - 1


def _comm_scratch(n_arrays):
    return [pltpu.SemaphoreType.DMA((COPIES_PER_ARRAY * n_arrays,)), pltpu.SemaphoreType.DMA((COPIES_PER_ARRAY * n_arrays,)),
            pltpu.SemaphoreType.DMA((n_arrays,))]


def _gather_plan(src_refs, out_refs, send_sems, recv_sems, local_sems):
    x, y, c, chips = _place()
    me, sibling = (x, y, c), (x, y, 1 - c)

    def slot(a, px, py, pc):
        return out_refs[a].at[4 * px + 2 * py + pc]

    def copy(a, k, block, to, src=None):
        return pltpu.make_async_remote_copy(
            src_ref=slot(a, *block) if src is None else src, dst_ref=slot(a, *block),
            send_sem=send_sems.at[COPIES_PER_ARRAY * a + k], recv_sem=recv_sems.at[COPIES_PER_ARRAY * a + k],
            device_id=to, device_id_type=MESH)

    def mine(a):
        return pltpu.make_async_copy(src_refs[a], slot(a, *me), local_sems.at[a])

    def first(a):
        return [copy(a, 0, me, sibling, src=src_refs[a])] + [
            copy(a, 1 + j, me, (*chip, c), src=src_refs[a]) for j, chip in enumerate(chips)]

    def passed(a):
        return [copy(a, 4 + j, (*chip, c), sibling) for j, chip in enumerate(chips)]

    def start():
        for a in range(len(src_refs)):
            mine(a).start()
            for cp in first(a):
                cp.start()

    def middle():
        for a in range(len(src_refs)):
            forwards = passed(a)
            for j, chip in enumerate(chips):
                copy(a, 1 + j, (*chip, c), me).wait_recv()
                forwards[j].start()

    def finish():
        for a in range(len(src_refs)):
            copy(a, 0, sibling, me).wait_recv()
            for j, chip in enumerate(chips):
                copy(a, 4 + j, (*chip, 1 - c), me).wait_recv()
            for cp in first(a) + passed(a):
                cp.wait_send()
            mine(a).wait()

    return start, middle, finish


def _scatter_plan(src_refs, out_refs, send_sems, recv_sems, local_sems):
    x, y, c, _ = _place()
    me = 4 * x + 2 * y + c

    def peer(k):
        px = 1 - x if k & 4 else x
        py = 1 - y if k & 2 else y
        pc = 1 - c if k & 1 else c
        return (px, py, pc), 4 * px + 2 * py + pc

    def copy(a, k, outgoing):
        to, idx = peer(k)
        return pltpu.make_async_remote_copy(
            src_ref=src_refs[a].at[idx], dst_ref=out_refs[a].at[me if outgoing else idx],
            send_sem=send_sems.at[COPIES_PER_ARRAY * a + k - 1], recv_sem=recv_sems.at[COPIES_PER_ARRAY * a + k - 1],
            device_id=to, device_id_type=MESH)

    def mine(a):
        return pltpu.make_async_copy(src_refs[a].at[me], out_refs[a].at[me], local_sems.at[a])

    def start():
        for a in range(len(src_refs)):
            mine(a).start()
            for k in range(1, N_DEV):
                copy(a, k, True).start()

    def finish():
        for a in range(len(src_refs)):
            for k in range(1, N_DEV):
                copy(a, k, False).wait_recv()
            for k in range(1, N_DEV):
                copy(a, k, True).wait_send()
            mine(a).wait()

    return start, lambda: None, finish


def _allgather(v, name):
    def body(v_ref, out_ref, send_sems, recv_sems, local_sems):
        start, middle, finish = _gather_plan([v_ref], [out_ref], send_sems, recv_sems, local_sems)
        start()
        middle()
        finish()

    return pl.pallas_call(
        body, name=name,
        out_shape=jax.ShapeDtypeStruct((N_DEV,) + v.shape, v.dtype),
        in_specs=[ANY], out_specs=ANY, scratch_shapes=_comm_scratch(1),
    )(v)


def _sibling_exchange(g, name):
    def body(g_ref, out_ref, send_sems, recv_sems):
        x, y, c, _ = _place()
        sibling = (x, y, 1 - c)
        copies = []
        for j in range(4):
            copies.append(pltpu.make_async_remote_copy(
                src_ref=g_ref.at[2 * j + (1 - c)], dst_ref=out_ref.at[j],
                send_sem=send_sems.at[j], recv_sem=recv_sems.at[j], device_id=sibling, device_id_type=MESH))
        for cp in copies:
            cp.start()
        for cp in copies:
            cp.wait_recv()
        for cp in copies:
            cp.wait_send()

    return pl.pallas_call(
        body, name=name,
        out_shape=jax.ShapeDtypeStruct((4,) + g.shape[1:], g.dtype),
        in_specs=[ANY], out_specs=ANY,
        scratch_shapes=[pltpu.SemaphoreType.DMA((4,)), pltpu.SemaphoreType.DMA((4,))],
    )(g)


def _pair_sum(g, got, core, name):
    _, r, c = g.shape
    tr = _tile(r, max(16, (1 << 20) // c), 16)

    def body(core_ref, g_ref, got_ref, o_ref):
        del core_ref
        o_ref[...] = (g_ref[...].astype(F32) + got_ref[...].astype(F32)).astype(o_ref.dtype)

    return pl.pallas_call(
        body, name=name,
        out_shape=jax.ShapeDtypeStruct((4, r, c), g.dtype),
        grid_spec=pltpu.PrefetchScalarGridSpec(
            num_scalar_prefetch=1, grid=(4, r // tr),
            in_specs=[pl.BlockSpec((None, tr, c), lambda j, i, core_ref: (2 * j + core_ref[0], i, 0)),
                      pl.BlockSpec((None, tr, c), lambda j, i, core_ref: (j, i, 0))],
            out_specs=pl.BlockSpec((None, tr, c), lambda j, i, core_ref: (j, i, 0))),
        compiler_params=_params(),
    )(core, g, got)


def _chip_plan(src_refs, out_refs, send_sems, recv_sems, local_sems):
    x, y, c, chips = _place()

    def copies(a):
        return [pltpu.make_async_remote_copy(
            src_ref=src_refs[a].at[2 * px + py], dst_ref=out_refs[a].at[k],
            send_sem=send_sems.at[COPIES_PER_ARRAY * a + k], recv_sem=recv_sems.at[COPIES_PER_ARRAY * a + k],
            device_id=(px, py, c), device_id_type=MESH) for k, (px, py) in enumerate(chips)]

    def mine(a):
        return pltpu.make_async_copy(src_refs[a].at[2 * x + y], out_refs[a].at[3], local_sems.at[a])

    def start():
        for a in range(len(src_refs)):
            mine(a).start()
            for cp in copies(a):
                cp.start()

    def finish():
        for a in range(len(src_refs)):
            for cp in copies(a):
                cp.wait_recv()
            for cp in copies(a):
                cp.wait_send()
            mine(a).wait()

    return start, lambda: None, finish


def _chip_comm(blocks):
    return _chip_plan, blocks, [jax.ShapeDtypeStruct(p.shape, p.dtype) for p in blocks]


def _pair_sums(g, core, name):
    got = _sibling_exchange(g, name + "_d2d")
    return _pair_sum(g, got, core, name + "_pair")


def _adamw(parts, w, m, v, name, layer=None, into=None):
    n_parts, r, c = parts.shape
    tr = _tile(r, max(8, (1 << 19) // c), 16 if parts.dtype == BF16 else 8)
    c1 = 1.0 - ADAM_B1 ** ADAM_STEP
    c2 = 1.0 - ADAM_B2 ** ADAM_STEP
    n_into = 0 if into is None else len(into)

    def body(p_ref, w_ref, m_ref, v_ref, *rest):
        g_out, d_out, m_out, v_out = rest[n_into:]
        g = p_ref[0].astype(F32)
        for k in range(1, n_parts):
            g = g + p_ref[k].astype(F32)
        m2 = ADAM_B1 * m_ref[...] + (1.0 - ADAM_B1) * g
        v2 = ADAM_B2 * v_ref[...] + (1.0 - ADAM_B2) * (g * g)
        g_out[...] = g
        m_out[...] = m2
        v_out[...] = v2
        d_out[...] = -ADAM_LR * ((m2 / c1) / (jnp.sqrt(v2 / c2) + ADAM_EPS) + ADAM_WD * w_ref[...])

    if layer is None:
        blk = pl.BlockSpec((tr, c), lambda i: (i, 0))
        out = jax.ShapeDtypeStruct((r, c), F32)
    else:
        blk = pl.BlockSpec((None, tr, c), lambda i: (layer, i, 0))
        out = jax.ShapeDtypeStruct(w.shape, F32)
    return pl.pallas_call(
        body, name=name, grid=(r // tr,),
        in_specs=[pl.BlockSpec((n_parts, tr, c), lambda i: (0, i, 0)), blk, blk, blk] + [ANY] * n_into,
        out_specs=[blk, blk, blk, blk], out_shape=[out, out, out, out],
        input_output_aliases={4 + k: k for k in range(n_into)},
        compiler_params=_params(),
    )(parts, w, m, v, *(into or ()))


def _exchange_steps(grid):
    flat, total = pl.program_id(0), grid[0]
    for axis in range(1, len(grid)):
        flat, total = flat * grid[axis] + pl.program_id(axis), total * grid[axis]
    return flat == 0, flat == max(3 * total // 4, min(1, total - 1)), flat == total - 1


def _run_exchange(hooks, grid, compute):
    start, middle, finish = hooks
    first, later, last = _exchange_steps(grid)
    pl.when(first)(start)
    compute()
    pl.when(later)(middle)
    pl.when(last)(finish)


def _host_exchange(body, comm, grid, n_in, n_out):
    plan, comm_in, comm_out = comm
    n = len(comm_in)

    def wrapped(*refs):
        ins, cin = refs[:n_in], refs[n_in:n_in + n]
        outs, cout = refs[n_in + n:n_in + n + n_out], refs[n_in + n + n_out:n_in + 2 * n + n_out]
        rest = refs[n_in + 2 * n + n_out:]
        _run_exchange(plan(cin, cout, *rest[len(rest) - 3:]), grid, lambda: body(*ins, *outs, *rest[:len(rest) - 3]))

    return wrapped, list(comm_in), [ANY] * n, list(comm_out), _comm_scratch(n)


def _gather_comm(shards):
    return _gather_plan, shards, [jax.ShapeDtypeStruct((N_DEV,) + v.shape, v.dtype) for v in shards]


def _scatter_comm(blocks):
    return _scatter_plan, blocks, [jax.ShapeDtypeStruct(g.shape, g.dtype) for g in blocks]


def _matmul(a, b, *, dims, ti, tj, tk, out_dtype, name, j_outer=True, resid=None, dev_major=False, comm=None,
            k_blocks=None, halved=None):
    a_shape = (a.shape[1], 2 * a.shape[2]) if halved == "a" else a.shape
    b_shape = (b.shape[1], 2 * b.shape[2]) if halved == "b" else b.shape
    if dims == "nn":
        (I, K), (K2, J) = a_shape, b_shape
    elif dims == "nt":
        (I, K), (J, K2) = a_shape, b_shape
    else:
        (K, I), (K2, J) = a_shape, b_shape
    assert K == K2 and I % ti == 0 and J % tj == 0 and K % tk == 0, (name, a.shape, b.shape, ti, tj, tk)
    assert halved is None or (halved, dims) in (("a", "nt"), ("b", "tn")), (name, halved, dims)
    k0, nk = k_blocks if k_blocks is not None else (0, K // tk)
    ni, nj = I // ti, J // tj

    def ij(g0, g1):
        return (g1, g0) if j_outer else (g0, g1)

    if dims == "nn":
        a_spec = pl.BlockSpec((ti, tk), lambda g0, g1, k: (ij(g0, g1)[0], k0 + k))
        b_spec = pl.BlockSpec((tk, tj), lambda g0, g1, k: (k0 + k, ij(g0, g1)[1]))
        dn = (((1,), (0,)), ((), ()))
    elif dims == "nt":
        a_spec = pl.BlockSpec((ti, tk), lambda g0, g1, k: (ij(g0, g1)[0], k0 + k))
        if halved == "a":
            per = K // 2 // tk
            a_spec = pl.BlockSpec((None, ti, tk), lambda g0, g1, k: ((k0 + k) // per, ij(g0, g1)[0], (k0 + k) % per))
        b_spec = pl.BlockSpec((tj, tk), lambda g0, g1, k: (ij(g0, g1)[1], k0 + k))
        dn = (((1,), (1,)), ((), ()))
    else:
        a_spec = pl.BlockSpec((tk, ti), lambda g0, g1, k: (k0 + k, ij(g0, g1)[0]))
        b_spec = pl.BlockSpec((tk, tj), lambda g0, g1, k: (k0 + k, ij(g0, g1)[1]))
        if halved == "b":
            per = J // 2 // tj
            b_spec = pl.BlockSpec((None, tk, tj), lambda g0, g1, k: (ij(g0, g1)[1] // per, k0 + k, ij(g0, g1)[1] % per))
        dn = (((0,), (0,)), ((), ()))
    in_specs = [a_spec, b_spec]
    operands = [a, b]
    if resid is not None:
        in_specs.append(pl.BlockSpec((ti, tj), lambda g0, g1, k: ij(g0, g1)))
        operands.append(resid)
    if dev_major:
        out_spec = pl.BlockSpec((None, ti, tj), lambda g0, g1, k: (ij(g0, g1)[1], ij(g0, g1)[0], 0))
        out_shape = jax.ShapeDtypeStruct((nj, I, tj), out_dtype)
    else:
        out_spec = pl.BlockSpec((ti, tj), lambda g0, g1, k: ij(g0, g1))
        out_shape = jax.ShapeDtypeStruct((I, J), out_dtype)

    grid = (nj, ni, nk) if j_outer else (ni, nj, nk)
    n_in = len(operands)
    n_comm = 0
    out_specs, out_shapes = [out_spec], [out_shape]
    scratch = [pltpu.VMEM((ti, tj), F32)] if nk > 1 else []
    if comm is not None:
        plan, comm_in, comm_out = comm
        n_comm = len(comm_in)
        operands += list(comm_in)
        in_specs += [ANY] * n_comm
        out_specs += [ANY] * n_comm
        out_shapes += list(comm_out)
        scratch += _comm_scratch(n_comm)

    def product(*refs):
        a_ref, b_ref = refs[0], refs[1]
        r_ref = refs[2] if resid is not None else None
        o_ref = refs[n_in + n_comm]
        part = lax.dot_general(a_ref[...].astype(BF16), b_ref[...].astype(BF16), dn, preferred_element_type=F32)

        def finish(acc):
            if r_ref is not None:
                acc = acc + r_ref[...]
            o_ref[...] = acc.astype(o_ref.dtype)

        if nk == 1:
            finish(part)
        else:
            acc_ref = refs[n_in + 2 * n_comm + 1]
            k = pl.program_id(2)

            @pl.when(k == 0)
            def _():
                acc_ref[...] = part

            @pl.when(k > 0)
            def _():
                acc_ref[...] += part

            @pl.when(k == nk - 1)
            def _():
                finish(acc_ref[...])

    def body(*refs):
        if comm is None:
            product(*refs)
        else:
            hooks = plan(refs[n_in:n_in + n_comm], refs[n_in + n_comm + 1:n_in + 2 * n_comm + 1], *refs[-3:])
            _run_exchange(hooks, grid, lambda: product(*refs))

    res = pl.pallas_call(
        body, name=name, grid=grid,
        in_specs=in_specs, out_specs=out_specs, out_shape=out_shapes,
        scratch_shapes=scratch, compiler_params=_params(),
    )(*operands)
    return res[0] if comm is None else res


ROW_TILE = 512


def _rms_fwd(x, g, name, comm=None):
    s, d = x.shape
    ts = _tile(s, ROW_TILE, 16)

    def body(x_ref, g_ref, h_ref):
        xv = x_ref[...]
        r = lax.rsqrt(jnp.mean(xv * xv, axis=-1, keepdims=True) + EPS)
        h_ref[...] = (xv * r * g_ref[...]).astype(BF16)

    return _call(
        body, comm, name=name, grid=(s // ts,), operands=[x, g],
        in_specs=[pl.BlockSpec((ts, d), lambda i: (i, 0)), pl.BlockSpec((1, d), lambda i: (0, 0))],
        out_specs=[pl.BlockSpec((ts, d), lambda i: (i, 0))],
        out_shape=[jax.ShapeDtypeStruct((s, d), BF16)], scratch_shapes=[])


def _rms_bwd(x, g, dh, dres, name, comm=None):
    s, d = x.shape
    ts = _tile(s, ROW_TILE, 16)

    def body(x_ref, g_ref, dh_ref, dres_ref, dx_ref, dxb_ref, dg_ref):
        xv = x_ref[...]
        r = lax.rsqrt(jnp.mean(xv * xv, axis=-1, keepdims=True) + EPS)
        y = xv * r
        dhv = dh_ref[...]
        gd = dhv * g_ref[...]
        dxv = dres_ref[...] + r * (gd - y * jnp.mean(gd * y, axis=-1, keepdims=True))
        dx_ref[...] = dxv
        dxb_ref[...] = dxv.astype(BF16)

        @pl.when(pl.program_id(0) == 0)
        def _():
            dg_ref[...] = jnp.zeros_like(dg_ref)

        dg_ref[0:1, :] += jnp.sum(dhv * y, axis=0, keepdims=True)

    blk = pl.BlockSpec((ts, d), lambda i: (i, 0))
    return _call(
        body, comm, name=name, grid=(s // ts,), operands=[x, g, dh, dres],
        in_specs=[blk, pl.BlockSpec((1, d), lambda i: (0, 0)), blk, blk],
        out_specs=[blk, blk, pl.BlockSpec((8, d), lambda i: (0, 0))],
        out_shape=[jax.ShapeDtypeStruct((s, d), F32), jax.ShapeDtypeStruct((s, d), BF16), jax.ShapeDtypeStruct((8, d), F32)],
        scratch_shapes=[])


def _head_norm(t, gain):
    r = lax.rsqrt(jnp.mean(t * t, axis=-1, keepdims=True) + EPS)
    return t * r * gain


def _head_norm_bwd(t, gain, dn):
    r = lax.rsqrt(jnp.mean(t * t, axis=-1, keepdims=True) + EPS)
    y = t * r
    gd = dn * gain
    dt = r * (gd - y * jnp.mean(gd * y, axis=-1, keepdims=True))
    return dt, jnp.sum(dn * y, axis=0, keepdims=True)


def _rope(n, cos, sin):
    return n * cos + pltpu.roll(n, HEAD // 2, axis=1) * sin


def _rope_bwd(do, cos, sin):
    return do * cos + pltpu.roll(do * sin, HEAD // 2, axis=1)


def _qkv_fwd(proj, gains, cos, sin, cfg, name):
    s, pw = proj.shape
    ha, hq, hkv = cfg
    ts = _tile(s, ROW_TILE, 16)

    def body(p_ref, gn_ref, cos_ref, sin_ref, qa_ref, ka_ref, va_ref, qb_ref, kb_ref, vb_ref):
        cosv, sinv = cos_ref[...], sin_ref[...]
        col = 0
        for out_ref, nh, gi, rot in ((qa_ref, ha, 0, False), (ka_ref, ha, 1, False), (va_ref, ha, None, False),
                                     (qb_ref, hq, 2, True), (kb_ref, hkv, 3, True), (vb_ref, hkv, None, False)):
            for h in range(nh):
                t = p_ref[:, col * HEAD:(col + 1) * HEAD]
                if gi is not None:
                    t = _head_norm(t, gn_ref[gi:gi + 1, :])
                if rot:
                    t = _rope(t, cosv, sinv)
                out_ref[h] = t.astype(BF16)
                col += 1

    def hm(nh):
        return pl.BlockSpec((nh, ts, HEAD), lambda i: (0, i, 0)), jax.ShapeDtypeStruct((nh, s, HEAD), BF16)

    specs, shapes = zip(hm(ha), hm(ha), hm(ha), hm(hq), hm(hkv), hm(hkv))
    tok = pl.BlockSpec((ts, HEAD), lambda i: (i, 0))
    return pl.pallas_call(
        body, name=name, grid=(s // ts,),
        in_specs=[pl.BlockSpec((ts, pw), lambda i: (i, 0)), pl.BlockSpec((8, HEAD), lambda i: (0, 0)), tok, tok],
        out_specs=list(specs), out_shape=list(shapes), compiler_params=_params(),
    )(proj, gains, cos, sin)


def _qkv_bwd(proj, gains, cos, sin, grads, cfg, name):
    s, pw = proj.shape
    ha, hq, hkv = cfg
    ts = _tile(s, 256, 16)

    def body(p_ref, gn_ref, cos_ref, sin_ref, dqa, dka, dva, dqb, dkb, dvb, dp_ref, dgn_ref):
        cosv, sinv = cos_ref[...], sin_ref[...]

        @pl.when(pl.program_id(0) == 0)
        def _():
            dgn_ref[...] = jnp.zeros_like(dgn_ref)

        col = 0
        for d_ref, nh, gi, rot in ((dqa, ha, 0, False), (dka, ha, 1, False), (dva, ha, None, False),
                                   (dqb, hq, 2, True), (dkb, hkv, 3, True), (dvb, hkv, None, False)):
            dgain = jnp.zeros((1, HEAD), F32)
            for h in range(nh):
                dt = d_ref[h]
                if rot:
                    dt = _rope_bwd(dt, cosv, sinv)
                if gi is not None:
                    dt, dg = _head_norm_bwd(p_ref[:, col * HEAD:(col + 1) * HEAD], gn_ref[gi:gi + 1, :], dt)
                    dgain = dgain + dg
                dp_ref[:, col * HEAD:(col + 1) * HEAD] = dt.astype(BF16)
                col += 1
            if gi is not None:
                dgn_ref[gi:gi + 1, :] += dgain

    def hm(nh):
        return pl.BlockSpec((nh, ts, HEAD), lambda i: (0, i, 0))

    tok = pl.BlockSpec((ts, HEAD), lambda i: (i, 0))
    small = pl.BlockSpec((8, HEAD), lambda i: (0, 0))
    return pl.pallas_call(
        body, name=name, grid=(s // ts,),
        in_specs=[pl.BlockSpec((ts, pw), lambda i: (i, 0)), small, tok, tok,
                  hm(ha), hm(ha), hm(ha), hm(hq), hm(hkv), hm(hkv)],
        out_specs=[pl.BlockSpec((ts, pw), lambda i: (i, 0)), small],
        out_shape=[jax.ShapeDtypeStruct((s, pw), BF16), jax.ShapeDtypeStruct((8, HEAD), F32)],
        compiler_params=_params(),
    )(proj, gains, cos, sin, *grads)


NA_QROWS = 32
NA_KEYS = WIN_R * GRID_W
N_DR = 2 * WIN_R - 1
N_DC = 2 * WIN_C - 1


def _na_bias(rpb_flat, n_heads, name):
    def body(rpb_ref, tb_ref):
        h = pl.program_id(0)
        qi = lax.broadcasted_iota(jnp.int32, (GRID_W, LANES), 0)
        lane = lax.broadcasted_iota(jnp.int32, (GRID_W, LANES), 1)
        kk = lane & (GRID_W - 1)
        upper = lane >= GRID_W
        dcm = kk - qi + (WIN_C - 1)
        cs = jnp.clip(qi - WIN_C // 2, 0, GRID_W - WIN_C)
        valid = (kk >= cs) & (kk < cs + WIN_C)
        base = h * (N_DR * N_DC)
        for dra in range(N_DR - 1):
            def step(j, acc, dra=dra):
                va = rpb_ref[base + dra * N_DC + j]
                vb = rpb_ref[base + (dra + 1) * N_DC + j]
                return jnp.where(dcm == j, jnp.where(upper, vb, va), acc)

            pair = lax.fori_loop(0, N_DC, step, jnp.zeros((GRID_W, LANES), F32))
            pair = jnp.where(valid, pair, NEG)
            for dr0 in range(WIN_R):
                wp, odd = divmod(dra - dr0, 2)
                if odd == 0 and 0 <= wp < WIN_R // 2:
                    tb_ref[0, dr0, :, wp * LANES:(wp + 1) * LANES] = pair

    return pl.pallas_call(
        body, name=name, grid=(n_heads,),
        in_specs=[SMEM],
        out_specs=pl.BlockSpec((1, WIN_R, GRID_W, NA_KEYS), lambda h: (h, 0, 0, 0)),
        out_shape=jax.ShapeDtypeStruct((n_heads, WIN_R, GRID_W, NA_KEYS), F32),
        compiler_params=_params(),
    )(rpb_flat)


def _na_row(b, i, nrows, qrows):
    r = b * qrows + i
    rs = jnp.clip(r - WIN_R // 2, 0, nrows - WIN_R)
    return pl.ds(pl.multiple_of(rs * GRID_W, GRID_W), NA_KEYS), rs - r + (WIN_R - 1)


def _softmax(s):
    e = jnp.exp(s - jnp.max(s, axis=-1, keepdims=True))
    return e * (1.0 / jnp.sum(e, axis=-1, keepdims=True))


_NT = (((1,), (1,)), ((), ()))
_NN = (((1,), (0,)), ((), ()))
_TN = (((0,), (0,)), ((), ()))


def _dot(a, b, dn):
    return lax.dot_general(a, b, dn, preferred_element_type=F32)


def _call(body, comm, *, name, grid, operands, in_specs, out_specs, out_shape, scratch_shapes):
    if comm is not None:
        body, more_operands, more_specs, more_shapes, sems = _host_exchange(body, comm, grid, len(operands), len(out_shape))
        operands = operands + more_operands
        in_specs = in_specs + more_specs
        out_specs = out_specs + more_specs
        out_shape = out_shape + more_shapes
        scratch_shapes = scratch_shapes + sems
    res = pl.pallas_call(body, name=name, grid=grid, in_specs=in_specs, out_specs=out_specs, out_shape=out_shape,
                         scratch_shapes=scratch_shapes, compiler_params=_params())(*operands)
    return res[0] if len(res) == 1 else res


def _na_fwd(q, k, v, tb, name, comm=None):
    nh, s, _ = q.shape
    nrows = s // GRID_W
    qrows = _tile(nrows, NA_QROWS, WIN_R)
    tq = qrows * GRID_W

    def body(q_ref, k_ref, v_ref, tb_ref, o_ref, s_scr, p_scr):
        b = pl.program_id(1)
        rows = [slice(i * GRID_W, (i + 1) * GRID_W) for i in range(qrows)]
        at = [_na_row(b, i, nrows, qrows) for i in range(qrows)]
        for i, (keys, dr0) in enumerate(at):
            s_scr[i] = _dot(q_ref[rows[i], :], k_ref[keys, :], _NT) * SCALE + tb_ref[0, dr0]
        for i in range(qrows):
            p_scr[i] = _softmax(s_scr[i]).astype(BF16)
        for i, (keys, _) in enumerate(at):
            o_ref[rows[i], :] = _dot(p_scr[i], v_ref[keys, :], _NN)

    qspec = pl.BlockSpec((None, tq, HEAD), lambda h, b: (h, b, 0))
    full = pl.BlockSpec((None, s, HEAD), lambda h, b: (h, 0, 0))
    return _call(
        body, comm, name=name, grid=(nh, nrows // qrows), operands=[q, k, v, tb],
        in_specs=[qspec, full, full, pl.BlockSpec((1, WIN_R, GRID_W, NA_KEYS), lambda h, b: (h, 0, 0, 0))],
        out_specs=[qspec], out_shape=[jax.ShapeDtypeStruct((nh, s, HEAD), F32)],
        scratch_shapes=[pltpu.VMEM((qrows, GRID_W, NA_KEYS), F32), pltpu.VMEM((qrows, GRID_W, NA_KEYS), BF16)])


def _na_bwd(q, k, v, tb, do, name, comm=None):
    nh, s, _ = q.shape
    nrows = s // GRID_W
    qrows = _tile(nrows, NA_QROWS, WIN_R)
    tq = qrows * GRID_W

    def body(q_ref, do_ref, k_ref, v_ref, tb_ref, dq_ref, dk_ref, dv_ref, dtb_ref, s_scr, dp_scr, p_scr, ds_scr):
        b = pl.program_id(1)

        @pl.when(b == 0)
        def _():
            dk_ref[...] = jnp.zeros_like(dk_ref)
            dv_ref[...] = jnp.zeros_like(dv_ref)
            dtb_ref[...] = jnp.zeros_like(dtb_ref)

        rows = [slice(i * GRID_W, (i + 1) * GRID_W) for i in range(qrows)]
        at = [_na_row(b, i, nrows, qrows) for i in range(qrows)]
        for i, (keys, dr0) in enumerate(at):
            s_scr[i] = _dot(q_ref[rows[i], :], k_ref[keys, :], _NT) * SCALE + tb_ref[0, dr0]
            dp_scr[i] = _dot(do_ref[rows[i], :], v_ref[keys, :], _NT)
        for i in range(qrows):
            p = _softmax(s_scr[i])
            dp = dp_scr[i]
            ds = p * (dp - jnp.sum(p * dp, axis=-1, keepdims=True))
            p_scr[i] = p.astype(BF16)
            s_scr[i] = ds
            ds_scr[i] = (ds * SCALE).astype(BF16)
        for i, (keys, _) in enumerate(at):
            dq_ref[rows[i], :] = _dot(ds_scr[i], k_ref[keys, :], _NN)
        for i, (keys, dr0) in enumerate(at):
            dv_ref[keys, :] += _dot(p_scr[i], do_ref[rows[i], :], _TN)
            dk_ref[keys, :] += _dot(ds_scr[i], q_ref[rows[i], :], _TN)
            dtb_ref[0, dr0] += s_scr[i]

    qspec = pl.BlockSpec((None, tq, HEAD), lambda h, b: (h, b, 0))
    full = pl.BlockSpec((None, s, HEAD), lambda h, b: (h, 0, 0))
    tbs = pl.BlockSpec((1, WIN_R, GRID_W, NA_KEYS), lambda h, b: (h, 0, 0, 0))
    hm = jax.ShapeDtypeStruct((nh, s, HEAD), F32)
    tile = (qrows, GRID_W, NA_KEYS)
    return _call(
        body, comm, name=name, grid=(nh, nrows // qrows), operands=[q, do, k, v, tb],
        in_specs=[qspec, qspec, full, full, tbs],
        out_specs=[qspec, full, full, tbs],
        out_shape=[hm, hm, hm, jax.ShapeDtypeStruct((nh, WIN_R, GRID_W, NA_KEYS), F32)],
        scratch_shapes=[pltpu.VMEM(tile, F32), pltpu.VMEM(tile, F32), pltpu.VMEM(tile, BF16), pltpu.VMEM(tile, BF16)])


def _rpb_fold(y, n_heads, name):
    def body(y_ref, o_ref):
        for h in range(n_heads):
            for dr in range(2 * WIN_R):
                acc = jnp.zeros((1, LANES), F32)
                for dr0 in range(WIN_R):
                    w = dr - dr0
                    if 0 <= w < WIN_R:
                        acc = acc + y_ref[h, dr0, w:w + 1, :]
                o_ref[h, dr:dr + 1, :] = acc

    return pl.pallas_call(
        body, name=name, out_shape=jax.ShapeDtypeStruct((n_heads, 2 * WIN_R, LANES), F32),
    )(y)


def _rpb_grad(dtb, onehot, name):
    nh = dtb.shape[0]
    rows = dtb.reshape(nh, WIN_R, GRID_W, WIN_R, GRID_W).transpose(0, 1, 3, 2, 4).reshape(nh * WIN_R * WIN_R, GRID_W * GRID_W)
    y = _matmul(rows, onehot, dims="nn", ti=rows.shape[0], tj=LANES, tk=GRID_W * GRID_W, out_dtype=F32, name=name + "_dc")
    folded = _rpb_fold(y.reshape(nh, WIN_R, WIN_R, LANES), nh, name + "_dr")
    return folded[:, :N_DR, :N_DC]


WA_WIN_TOK = 3 * BAND
WA_QBLOCKS = 2


def _wa_scores(q, kwin, t0, j, sink_ref, head0, grp):
    rows = grp * BAND
    s = _dot(q, kwin, _NT) * SCALE
    row = lax.broadcasted_iota(jnp.int32, (rows, WA_WIN_TOK), 0)
    qpos = j * BAND + (row & (BAND - 1))
    kpos = t0 + lax.broadcasted_iota(jnp.int32, (rows, WA_WIN_TOK), 1)
    s = jnp.where(jnp.abs(kpos - qpos) <= BAND, s, NEG)
    head = lax.broadcasted_iota(jnp.int32, (rows, 1), 0) // BAND
    sink = jnp.zeros((rows, 1), F32) + sink_ref[head0]
    for g in range(1, grp):
        sink = jnp.where(head == g, sink_ref[head0 + g], sink)
    m = jnp.maximum(jnp.max(s, axis=-1, keepdims=True), sink)
    e = jnp.exp(s - m)
    es = jnp.exp(sink - m)
    rz = 1.0 / (jnp.sum(e, axis=-1, keepdims=True) + es)
    return e * rz, es * rz


def _wa_window(j, s):
    return pl.multiple_of(jnp.clip((j - 1) * BAND, 0, s - WA_WIN_TOK), BAND)


def _wa_fwd(q, k, v, sink, name, comm=None):
    hq, s, _ = q.shape
    hkv = k.shape[0]
    grp = hq // hkv

    def body(sink_ref, q_ref, k_ref, v_ref, o_ref):
        kh, step = pl.program_id(0), pl.program_id(1)
        for sub in range(WA_QBLOCKS):
            j = step * WA_QBLOCKS + sub
            rows = slice(sub * BAND, (sub + 1) * BAND)
            t0 = _wa_window(j, s)
            keys = pl.ds(t0, WA_WIN_TOK)
            p, _ = _wa_scores(q_ref[:, rows, :].reshape(grp * BAND, HEAD), k_ref[keys, :], t0, j, sink_ref, kh * grp, grp)
            o_ref[:, rows, :] = _dot(p.astype(BF16), v_ref[keys, :], _NN).reshape(grp, BAND, HEAD)

    qspec = pl.BlockSpec((grp, WA_QBLOCKS * BAND, HEAD), lambda kh, j: (kh, j, 0))
    full = pl.BlockSpec((None, s, HEAD), lambda kh, j: (kh, 0, 0))
    return _call(
        body, comm, name=name, grid=(hkv, s // (WA_QBLOCKS * BAND)), operands=[sink, q, k, v],
        in_specs=[SMEM, qspec, full, full],
        out_specs=[qspec], out_shape=[jax.ShapeDtypeStruct((hq, s, HEAD), F32)], scratch_shapes=[])


def _wa_bwd(q, k, v, sink, do, name, comm=None):
    hq, s, _ = q.shape
    hkv = k.shape[0]
    grp = hq // hkv

    def body(sink_ref, q_ref, do_ref, k_ref, v_ref, dq_ref, dk_ref, dv_ref, dsink_ref):
        kh, step = pl.program_id(0), pl.program_id(1)

        @pl.when(step == 0)
        def _():
            dk_ref[...] = jnp.zeros_like(dk_ref)
            dv_ref[...] = jnp.zeros_like(dv_ref)
            dsink_ref[...] = jnp.zeros_like(dsink_ref)

        for sub in range(WA_QBLOCKS):
            j = step * WA_QBLOCKS + sub
            rows = slice(sub * BAND, (sub + 1) * BAND)
            t0 = _wa_window(j, s)
            keys = pl.ds(t0, WA_WIN_TOK)
            qs = q_ref[:, rows, :].reshape(grp * BAND, HEAD)
            dos = do_ref[:, rows, :].reshape(grp * BAND, HEAD)
            kwin, vwin = k_ref[keys, :], v_ref[keys, :]
            p, ps = _wa_scores(qs, kwin, t0, j, sink_ref, kh * grp, grp)
            dp = _dot(dos, vwin, _NT)
            dv_ref[keys, :] += _dot(p.astype(BF16), dos, _TN)
            rowdot = jnp.sum(p * dp, axis=-1, keepdims=True)
            to_sink = ps * rowdot
            for g in range(grp):
                dsink_ref[g] += jnp.zeros((8, LANES), F32) - jnp.sum(to_sink[g * BAND:(g + 1) * BAND])
            dss = (p * (dp - rowdot) * SCALE).astype(BF16)
            dq_ref[:, rows, :] = _dot(dss, kwin, _NN).reshape(grp, BAND, HEAD)
            dk_ref[keys, :] += _dot(dss, qs, _TN)

    qspec = pl.BlockSpec((grp, WA_QBLOCKS * BAND, HEAD), lambda kh, j: (kh, j, 0))
    full = pl.BlockSpec((None, s, HEAD), lambda kh, j: (kh, 0, 0))
    kv = jax.ShapeDtypeStruct((hkv, s, HEAD), F32)
    return _call(
        body, comm, name=name, grid=(hkv, s // (WA_QBLOCKS * BAND)), operands=[sink, q, do, k, v],
        in_specs=[SMEM, qspec, qspec, full, full],
        out_specs=[qspec, full, full, pl.BlockSpec((grp, 8, LANES), lambda kh, j: (kh, 0, 0))],
        out_shape=[jax.ShapeDtypeStruct((hq, s, HEAD), F32), kv, kv, jax.ShapeDtypeStruct((hq, 8, LANES), F32)],
        scratch_shapes=[])


def _onorm_fwd(oa, ob, gains, name):
    ha, s, _ = oa.shape
    hq = ob.shape[0]
    ts = _tile(s, ROW_TILE, 16)

    def body(oa_ref, ob_ref, g_ref, o_ref):
        col = 0
        for ref, nh in ((oa_ref, ha), (ob_ref, hq)):
            ss = sum(jnp.sum(ref[h] * ref[h], axis=-1, keepdims=True) for h in range(nh))
            r = lax.rsqrt(ss / (nh * HEAD) + EPS)
            for h in range(nh):
                o_ref[:, col * HEAD:(col + 1) * HEAD] = (ref[h] * r * g_ref[:, col * HEAD:(col + 1) * HEAD]).astype(BF16)
                col += 1

    mix = (ha + hq) * HEAD
    return pl.pallas_call(
        body, name=name, grid=(s // ts,),
        in_specs=[pl.BlockSpec((ha, ts, HEAD), lambda i: (0, i, 0)), pl.BlockSpec((hq, ts, HEAD), lambda i: (0, i, 0)),
                  pl.BlockSpec((1, mix), lambda i: (0, 0))],
        out_specs=pl.BlockSpec((ts, mix), lambda i: (i, 0)),
        out_shape=jax.ShapeDtypeStruct((s, mix), BF16), compiler_params=_params(),
    )(oa, ob, gains)


def _onorm_bwd(oa, ob, gains, don, name):
    ha, s, _ = oa.shape
    hq = ob.shape[0]
    ts = _tile(s, ROW_TILE, 16)
    mix = (ha + hq) * HEAD

    def body(oa_ref, ob_ref, g_ref, don_ref, doa_ref, dob_ref, dg_ref):
        @pl.when(pl.program_id(0) == 0)
        def _():
            dg_ref[...] = jnp.zeros_like(dg_ref)

        col0 = 0
        for ref, d_ref, nh in ((oa_ref, doa_ref, ha), (ob_ref, dob_ref, hq)):
            ss = sum(jnp.sum(ref[h] * ref[h], axis=-1, keepdims=True) for h in range(nh))
            r = lax.rsqrt(ss / (nh * HEAD) + EPS)
            dot = jnp.zeros((ts, 1), F32)
            for h in range(nh):
                cols = slice((col0 + h) * HEAD, (col0 + h + 1) * HEAD)
                dot = dot + jnp.sum(don_ref[:, cols] * g_ref[:, cols] * ref[h], axis=-1, keepdims=True)
            mean = dot * r / (nh * HEAD)
            for h in range(nh):
                cols = slice((col0 + h) * HEAD, (col0 + h + 1) * HEAD)
                y = ref[h] * r
                dn = don_ref[:, cols]
                d_ref[h] = (r * (dn * g_ref[:, cols] - y * mean)).astype(BF16)
                dg_ref[0:1, cols] += jnp.sum(dn * y, axis=0, keepdims=True)
            col0 += nh

    return pl.pallas_call(
        body, name=name, grid=(s // ts,),
        in_specs=[pl.BlockSpec((ha, ts, HEAD), lambda i: (0, i, 0)), pl.BlockSpec((hq, ts, HEAD), lambda i: (0, i, 0)),
                  pl.BlockSpec((1, mix), lambda i: (0, 0)), pl.BlockSpec((ts, mix), lambda i: (i, 0))],
        out_specs=[pl.BlockSpec((ha, ts, HEAD), lambda i: (0, i, 0)), pl.BlockSpec((hq, ts, HEAD), lambda i: (0, i, 0)),
                   pl.BlockSpec((8, mix), lambda i: (0, 0))],
        out_shape=[jax.ShapeDtypeStruct((ha, s, HEAD), BF16), jax.ShapeDtypeStruct((hq, s, HEAD), BF16),
                   jax.ShapeDtypeStruct((8, mix), F32)],
        compiler_params=_params(),
    )(oa, ob, gains, don)


HALO = 8
PACKED = 16


def _halo_specs(ts, tc, col_off):
    per = ts // HALO
    cur = pl.BlockSpec((ts, tc), lambda j, i: (i, j + col_off))
    prev = pl.BlockSpec((HALO, tc), lambda j, i: (jnp.maximum(i * per - 1, 0), j + col_off))

    def nxt_map(n_blocks):
        return pl.BlockSpec((HALO, tc), lambda j, i: (jnp.minimum((i + 1) * per, n_blocks - 1), j + col_off))

    return cur, prev, nxt_map


def _sigmoid(x):
    return 1.0 / (1.0 + jnp.exp(-x))


def _ffn_tiles(s, f):
    return _tile(s, 512, 16), _tile(f, 512, LANES)


def _gate_fwd(u, cw, cb, f, name):
    s = u.shape[0]
    ts, tc = _ffn_tiles(s, f)
    nj, ni = f // tc, s // ts

    def body(g_ref, gp_ref, gn_ref, u_ref, up_ref, un_ref, wg_ref, wu_ref, bg_ref, bu_ref, a_ref, gu_ref):
        i = pl.program_id(1)

        def conv(c_ref, p_ref, n_ref, w_ref, b_ref):
            ext = jnp.concatenate([jnp.where(i > 0, p_ref[...], 0.0), c_ref[...], jnp.where(i < ni - 1, n_ref[...], 0.0)], axis=0)
            rows = ts + 2 * HALO
            out = (pltpu.roll(ext, 1, axis=0) * w_ref[0:1, :] + ext * w_ref[1:2, :]
                   + pltpu.roll(ext, rows - 1, axis=0) * w_ref[2:3, :] + b_ref[...])
            return out[HALO:HALO + ts]

        gate = conv(g_ref, gp_ref, gn_ref, wg_ref, bg_ref)
        up = conv(u_ref, up_ref, un_ref, wu_ref, bu_ref)
        gu_ref[0] = gate.astype(BF16)
        gu_ref[1] = up.astype(BF16)
        a_ref[...] = (gate * _sigmoid(gate) * up).astype(BF16)

    gc, gp, gn = _halo_specs(ts, tc, 0)
    uc, up_, un = _halo_specs(ts, tc, nj)
    wg = pl.BlockSpec((3, tc), lambda j, i: (0, j))
    wu = pl.BlockSpec((3, tc), lambda j, i: (0, j + nj))
    bg = pl.BlockSpec((1, tc), lambda j, i: (0, j))
    bu = pl.BlockSpec((1, tc), lambda j, i: (0, j + nj))
    return pl.pallas_call(
        body, name=name, grid=(nj, ni),
        in_specs=[gc, gp, gn(s // HALO), uc, up_, un(s // HALO), wg, wu, bg, bu],
        out_specs=[pl.BlockSpec((ts, tc), lambda j, i: (i, j)), pl.BlockSpec((2, ts, tc), lambda j, i: (0, i, j))],
        out_shape=[jax.ShapeDtypeStruct((s, f), BF16), jax.ShapeDtypeStruct((2, s, f), BF16)], compiler_params=_params(),
    )(u, u, u, u, u, u, cw, cw, cb, cb)


def _ffn_bwd(gu, u, da, cw, name):
    _, s, f = gu.shape
    ts, tc = _ffn_tiles(s, f)
    nj, ni = f // tc, s // ts

    def body(gu_ref, gup_ref, gun_ref, da_ref, dap_ref, dan_ref, xg_ref, xu_ref, wg_ref, wu_ref,
             du_ref, dcw_ref, dcb_ref):
        i = pl.program_id(1)

        @pl.when(i == 0)
        def _():
            dcw_ref[...] = jnp.zeros_like(dcw_ref)
            dcb_ref[...] = jnp.zeros_like(dcb_ref)

        rows = ts + 2 * HALO
        mid = slice(HALO, HALO + ts)
        da = jnp.concatenate([jnp.where(i > 0, dap_ref[...], 0.0), da_ref[...], jnp.where(i < ni - 1, dan_ref[...], 0.0)], axis=0)
        def rows_of(half):
            before = gup_ref[half].astype(F32)[PACKED - HALO:]
            after = gun_ref[half].astype(F32)[:HALO]
            return jnp.concatenate([before, gu_ref[half].astype(F32), after], axis=0)

        gate, up = rows_of(0), rows_of(1)
        sg = _sigmoid(gate)
        d_up = da * gate * sg
        d_gate = da * up * (sg * (1.0 + gate * (1.0 - sg)))
        for half, (dd, x_ref, w_ref) in enumerate(((d_gate, xg_ref, wg_ref), (d_up, xu_ref, wu_ref))):
            before = pltpu.roll(dd, 1, axis=0)
            after = pltpu.roll(dd, rows - 1, axis=0)
            du_ref[half] = (before * w_ref[2:3, :] + dd * w_ref[1:2, :] + after * w_ref[0:1, :])[mid].astype(BF16)
            x = x_ref[...]
            dcb_ref[half, 0:1, :] += jnp.sum(dd[mid], axis=0, keepdims=True)
            for k, shifted in enumerate((after, dd, before)):
                dcw_ref[half, k, 0:1, :] += jnp.sum(shifted[mid] * x, axis=0, keepdims=True)

    per = ts // PACKED
    cur3 = pl.BlockSpec((2, ts, tc), lambda j, i: (0, i, j))
    prev3 = pl.BlockSpec((2, PACKED, tc), lambda j, i: (0, jnp.maximum(i * per - 1, 0), j))
    next3 = pl.BlockSpec((2, PACKED, tc), lambda j, i: (0, jnp.minimum((i + 1) * per, s // PACKED - 1), j))
    cur, prev, nxt = _halo_specs(ts, tc, 0)
    return pl.pallas_call(
        body, name=name, grid=(nj, ni),
        in_specs=[cur3, prev3, next3, cur, prev, nxt(s // HALO),
                  pl.BlockSpec((ts, tc), lambda j, i: (i, j)), pl.BlockSpec((ts, tc), lambda j, i: (i, j + nj)),
                  pl.BlockSpec((3, tc), lambda j, i: (0, j)), pl.BlockSpec((3, tc), lambda j, i: (0, j + nj))],
        out_specs=[cur3, pl.BlockSpec((2, 3, 8, tc), lambda j, i: (0, 0, 0, j)),
                   pl.BlockSpec((2, 8, tc), lambda j, i: (0, 0, j))],
        out_shape=[jax.ShapeDtypeStruct((2, s, f), BF16),
                   jax.ShapeDtypeStruct((2, 3, 8, f), F32), jax.ShapeDtypeStruct((2, 8, f), F32)],
        compiler_params=_params(),
    )(gu, gu, gu, da, da, da, u, u, cw, cw)


def _loss_head(y, target, name):
    s, d = y.shape
    ts = _tile(s, ROW_TILE, 16)

    def body(y_ref, t_ref, dy_ref, dyb_ref, l_ref):
        @pl.when(pl.program_id(0) == 0)
        def _():
            l_ref[...] = jnp.zeros_like(l_ref)

        err = y_ref[...] - t_ref[...]
        dy = err / d
        dy_ref[...] = dy
        dyb_ref[...] = dy.astype(BF16)
        l_ref[...] += jnp.zeros((8, LANES), F32) + 0.5 * jnp.sum(jnp.sum(err * err, axis=-1, keepdims=True) / d)

    blk = pl.BlockSpec((ts, d), lambda i: (i, 0))
    return pl.pallas_call(
        body, name=name, grid=(s // ts,),
        in_specs=[blk, blk], out_specs=[blk, blk, pl.BlockSpec((8, LANES), lambda i: (0, 0))],
        out_shape=[jax.ShapeDtypeStruct((s, d), F32), jax.ShapeDtypeStruct((s, d), BF16), jax.ShapeDtypeStruct((8, LANES), F32)],
        compiler_params=_params(),
    )(y, target)


SMALL = ("ln1_g", "qn_a", "kn_a", "rpb", "qn_b", "kn_b", "sink", "on_a", "on_b", "ln2_g", "conv_b", "conv_w")
PACK_ALIGN = 8 * LANES


def _pack(arrays):
    flat = []
    for a in arrays:
        a = a.reshape(-1)
        flat.append(jnp.pad(a, (0, -a.size % PACK_ALIGN)))
    return jnp.concatenate(flat).reshape(-1, LANES)


def _unpack(packed, like):
    out, at = [], 0
    flat = packed.reshape(-1)
    for a in like:
        out.append(flat[at:at + a.size].reshape(a.shape))
        at += a.size + (-a.size % PACK_ALIGN)
    return out


def kernel(x, positions, ln1_g, w_in, qn_a, kn_a, rpb, qn_b, kn_b, sink, on_a, on_b, w_out, ln2_g, w_up, conv_w, conv_b, w_down, loss_target, m_ln1_g, m_w_in, m_qn_a, m_kn_a, m_rpb, m_qn_b, m_kn_b, m_sink, m_on_a, m_on_b, m_w_out, m_ln2_g, m_w_up, m_conv_w, m_conv_b, m_w_down, v_ln1_g, v_w_in, v_qn_a, v_kn_a, v_rpb, v_qn_b, v_kn_b, v_sink, v_on_a, v_on_b, v_w_out, v_ln2_g, v_w_up, v_conv_w, v_conv_b, v_w_down):
    weights = dict(ln1_g=ln1_g, w_in=w_in, qn_a=qn_a, kn_a=kn_a, rpb=rpb, qn_b=qn_b, kn_b=kn_b, sink=sink, on_a=on_a,
                   on_b=on_b, w_out=w_out, ln2_g=ln2_g, w_up=w_up, conv_w=conv_w, conv_b=conv_b, w_down=w_down)
    mom1 = dict(ln1_g=m_ln1_g, w_in=m_w_in, qn_a=m_qn_a, kn_a=m_kn_a, rpb=m_rpb, qn_b=m_qn_b, kn_b=m_kn_b, sink=m_sink,
                on_a=m_on_a, on_b=m_on_b, w_out=m_w_out, ln2_g=m_ln2_g, w_up=m_w_up, conv_w=m_conv_w, conv_b=m_conv_b,
                w_down=m_w_down)
    mom2 = dict(ln1_g=v_ln1_g, w_in=v_w_in, qn_a=v_qn_a, kn_a=v_kn_a, rpb=v_rpb, qn_b=v_qn_b, kn_b=v_kn_b, sink=v_sink,
                on_a=v_on_a, on_b=v_on_b, w_out=v_w_out, ln2_g=v_ln2_g, w_up=v_w_up, conv_w=v_conv_w, conv_b=v_conv_b,
                w_down=v_w_down)
    order = ("ln1_g", "w_in", "qn_a", "kn_a", "rpb", "qn_b", "kn_b", "sink", "on_a", "on_b", "w_out", "ln2_g", "w_up",
             "conv_w", "conv_b", "w_down")

    depth, d = ln1_g.shape
    s = x.shape[1]
    ha = on_a.shape[1] // HEAD
    hq = on_b.shape[1] // HEAD
    pw = w_in.shape[2] * N_DEV
    hkv = (pw - 3 * ha * HEAD - hq * HEAD) // (2 * HEAD)
    f = w_down.shape[1] * N_DEV
    mix = (ha + hq) * HEAD
    cfg = (ha, hq, hkv)
    fs = conv_w.shape[2]
    dev = 4 * lax.axis_index("x") + 2 * lax.axis_index("y") + lax.axis_index("c")
    core = lax.axis_index("c").astype(jnp.int32).reshape(1)

    shard = {n: weights[n].astype(BF16) for n in ("w_in", "w_out", "w_up", "w_down")}

    def unshard(n, g):
        if n in ("w_in", "w_up"):
            return g.transpose(1, 0, 2).reshape(g.shape[1], N_DEV * g.shape[2])
        return g.reshape(N_DEV * g.shape[1], g.shape[2])

    full = {n: [None] * depth for n in shard}
    half_d = _tile(d, d // 2, 16)
    up0 = [shard["w_up"][0][:half_d], shard["w_up"][0][half_d:]]

    def travel(l, host):
        plan = {}
        if l == 0:
            plan = {"ln1": [("w_in", 0, None)], "proj": [("w_out", 0, None), ("w_up", 0, 0)], "na": [("w_up", 0, 1)],
                    "wa": [("w_down", 0, None)], "down": [("w_down", 1, None), ("w_in", 1, None)]}
        elif l + 1 < depth:
            plan = {"proj": [("w_in", l + 1, None)], "down": [("w_down", l + 1, None)]}
        if l + 1 < depth:
            plan.update({"out": [("w_out", l + 1, None)], "up": [("w_up", l + 1, None)]})
        return [key for key in plan.get(host, []) if key[1] < depth]

    arrived = {}

    def gather_of(keys):
        return _gather_comm([shard[n][k] if part is None else up0[part] for n, k, part in keys]) if keys else None

    def landed(keys, blocks):
        for (n, k, part), g in zip(keys, blocks):
            if part is None:
                full[n][k] = unshard(n, g)
            else:
                arrived[part] = g
                if len(arrived) == 2:
                    full[n][k] = unshard(n, jnp.concatenate([arrived[0], arrived[1]], axis=1))

    cw_rows = depth * 3
    cw_pad = jnp.pad(conv_w.reshape(cw_rows, fs), ((0, -cw_rows % 8), (0, 0)))
    g_cw = _allgather(cw_pad, "gather_conv_w")
    full_cw = g_cw[:, :cw_rows].reshape(N_DEV, depth, 3, fs).transpose(1, 2, 0, 3).reshape(depth, 3, 2 * f)

    inv = ROPE_THETA ** (-jnp.arange(0, HEAD, 2, dtype=F32) / HEAD)
    ang = positions.astype(F32)[:, None] * inv[None, :]
    cos = jnp.concatenate([jnp.cos(ang), jnp.cos(ang)], axis=-1)
    sin = jnp.concatenate([-jnp.sin(ang), jnp.sin(ang)], axis=-1)
    qk = jnp.arange(GRID_W * GRID_W)
    dc_of = (qk % GRID_W) - (qk // GRID_W) + (WIN_C - 1)
    onehot = (dc_of[:, None] == jnp.arange(LANES)[None, :]).astype(BF16)

    tiles_s = _tile(s, 512, 16)
    tiles_l = _tile(s, 1024, 16)

    xs = x.reshape(s, d)
    saved = []
    for l in range(depth):
        def hosting(call, host, name, *args, **kw):
            keys = travel(l, host)
            if not keys:
                return call(*args, name=name, **kw)
            out, *blocks = call(*args, name=f"{name}_g{len(keys)}", comm=gather_of(keys), **kw)
            landed(keys, blocks)
            return out

        def fwd_matmul(n, host, a_op, name, **kw):
            return hosting(lambda **k2: _matmul(a_op, full[n][l], dims="nn", out_dtype=F32, **k2), host, name, **kw)

        gains = jnp.zeros((8, HEAD), F32).at[0].set(qn_a[l]).at[1].set(kn_a[l]).at[2].set(qn_b[l]).at[3].set(kn_b[l])
        on_g = jnp.concatenate([on_a[l], on_b[l]]).reshape(1, mix)
        h = hosting(_rms_fwd, "ln1", "ln1_fwd", xs, ln1_g[l].reshape(1, d))
        proj = fwd_matmul("w_in", "proj", h, "proj_fwd", ti=tiles_l, tj=_tile(pw, 1536, LANES), tk=d)
        qa, ka, va, qb, kb, vb = _qkv_fwd(proj, gains, cos, sin, cfg, "qkv_fwd")
        tb = _na_bias(rpb[l].reshape(-1), ha, "na_bias")
        oa = hosting(_na_fwd, "na", "na_fwd", qa, ka, va, tb)
        ob = hosting(_wa_fwd, "wa", "wa_fwd", qb, kb, vb, sink[l])
        o_n = _onorm_fwd(oa, ob, on_g, "onorm_fwd")
        x1 = fwd_matmul("w_out", "out", o_n, "out_fwd", ti=tiles_l, tj=_tile(d, 1024, LANES), tk=mix, resid=xs)
        h2 = _rms_fwd(x1, ln2_g[l].reshape(1, d), "ln2_fwd")
        u = fwd_matmul("w_up", "up", h2, "up_fwd", ti=tiles_l, tj=_tile(2 * f, 1024, 2 * LANES), tk=d)
        a, gu = _gate_fwd(u, full_cw[l], conv_b[l].reshape(1, 2 * f), f, "gate_fwd")
        x2 = fwd_matmul("w_down", "down", a, "down_fwd", ti=tiles_s, tj=_tile(d, 512, LANES), tk=f, resid=x1)
        saved.append(dict(x=xs, h=h, proj=proj, gains=gains, on_g=on_g, qkv=(qa, ka, va, qb, kb, vb), tb=tb, oa=oa, ob=ob,
                          o_n=o_n, x1=x1, h2=h2, u=u, gu=gu, a=a))
        xs = x2

    dx, dx_b, loss_part = _loss_head(xs, loss_target.reshape(s, d), "loss_head")
    tile_c = _tile(s, 2048, 16)
    half_k = _tile(2 * f, f, fs)
    loss = lax.psum(loss_part[0, 0], ("x", "y", "c"))

    small_grads = [None] * depth
    big = {n: None for n in ("w_in", "w_out", "w_up", "w_down")}
    pending = None
    for l in reversed(range(depth)):
        sv = saved[l]
        qa, ka, va, qb, kb, vb = sv["qkv"]

        def update(n, layer, got):
            big[n] = _adamw(got, weights[n], mom1[n], mom2[n], "adamw_" + n, layer=layer, into=big[n])

        def carrying(n, name, **kw):
            if pending is None or n is None:
                return _matmul(name=name, **kw)
            out, got = _matmul(name=name + "_carry", comm=_scatter_comm([pending[n]]), **kw)
            update(n, l + 1, got)
            return out

        def grad_matmul(n, a_op, b_op, name, **kw):
            carried = n if n in ("w_up", "w_down") else None
            return carrying(carried, name, a=a_op, b=b_op, dims="tn", out_dtype=BF16, **kw)

        def own(blocks):
            return _scatter_comm([blocks]) if l == 0 else None

        gw_down = grad_matmul("w_down", sv["a"], dx_b, "down_bwd_w", ti=_tile(f, 1408, LANES), tj=_tile(d, 1024, LANES),
                              tk=tile_c, j_outer=False)
        da = carrying("w_in", "down_bwd_x", a=dx_b, b=full["w_down"][l], dims="nt", ti=tiles_s, tj=_tile(f, 2816, 2 * LANES),
                      tk=d, out_dtype=F32)
        du, dcw, dcb = _ffn_bwd(sv["gu"], sv["u"], da, full_cw[l], "ffn_bwd")
        gw_down = gw_down.reshape(N_DEV, f // N_DEV, d)
        dh2 = None
        for part in range(2 * f // half_k):
            comm = own(gw_down) if part == 0 else None
            dh2 = _matmul(du, full["w_up"][l], dims="nt", ti=tiles_s, tj=_tile(d, 1024, LANES), tk=half_k, out_dtype=F32,
                          name=f"up_bwd_x{part}" + "_own" * bool(comm), k_blocks=(part, 1), resid=dh2, halved="a", comm=comm)
            if comm:
                dh2, got = dh2
                update("w_down", 0, got)
        gw_up = grad_matmul("w_up", sv["h2"], du, "up_bwd_w", ti=_tile(d, 1024, LANES), tj=fs, tk=tile_c, dev_major=True,
                            halved="b")
        dx1, dx1_b, dln2 = _rms_bwd(sv["x1"], ln2_g[l].reshape(1, d), dh2, dx, "ln2_bwd")
        don = _matmul(dx1_b, full["w_out"][l], dims="nt", ti=tiles_l, tj=mix, tk=d, out_dtype=F32, name="out_bwd_x")
        gw_out = grad_matmul("w_out", sv["o_n"], dx1_b, "out_bwd_w", ti=_tile(mix, 1024, LANES), tj=_tile(d, 1024, LANES),
                             tk=tile_c, j_outer=False)
        gw_out = gw_out.reshape(N_DEV, mix // N_DEV, d)
        doa, dob, don_g = _onorm_bwd(sv["oa"], sv["ob"], sv["on_g"], don, "onorm_bwd")
        comm = _chip_comm([_pair_sums(gw_up, core, "rs0_w_up")]) if l == 0 else None
        dqa, dka, dva, dtb, *got = _na_bwd(qa, ka, va, sv["tb"], doa, "na_bwd" + "_own" * (l == 0), comm=comm)
        if got:
            update("w_up", 0, got[0])
        dqb, dkb, dvb, dsink, *got = _wa_bwd(qb, kb, vb, sink[l], dob, "wa_bwd" + "_own" * (l == 0), comm=own(gw_out))
        if got:
            update("w_out", 0, got[0])
        drpb = _rpb_grad(dtb, onehot, "rpb_grad")
        dproj, dgains = _qkv_bwd(sv["proj"], sv["gains"], cos, sin, (dqa, dka, dva, dqb, dkb, dvb), cfg, "qkv_bwd")
        dh = carrying("w_out", "proj_bwd_x", a=dproj, b=full["w_in"][l], dims="nt", ti=tiles_s, tj=_tile(d, 1024, LANES), tk=pw,
                      out_dtype=F32)
        gw_in = grad_matmul("w_in", sv["h"], dproj, "proj_bwd_w", ti=_tile(d, 1024, LANES), tj=_tile(pw, 1536, LANES), tk=tile_c)
        gw_in = gw_in.reshape(d, N_DEV, pw // N_DEV).transpose(1, 0, 2)
        comm = _chip_comm([_pair_sums(gw_in, core, "rs0_w_in")]) if l == 0 else None
        dx, dx_b, dln1, *got = _rms_bwd(sv["x"], ln1_g[l].reshape(1, d), dh, dx1, "ln1_bwd" + "_own" * (l == 0), comm=comm)
        if got:
            update("w_in", 0, got[0])

        small_grads[l] = dict(
            ln1_g=dln1[0], qn_a=dgains[0], kn_a=dgains[1], rpb=drpb, qn_b=dgains[2], kn_b=dgains[3], sink=dsink[:, 0, 0],
            on_a=don_g[0, :ha * HEAD], on_b=don_g[0, ha * HEAD:], ln2_g=dln2[0],
            conv_b=dcb[:, 0, :].reshape(2 * f), conv_w=dcw[:, :, 0, :].transpose(1, 0, 2).reshape(3, 2 * f))

        pending = dict(w_in=gw_in, w_out=gw_out, w_up=gw_up, w_down=gw_down)

    grads_l = [small_grads[l][n] for l in range(depth) for n in SMALL]
    gathered = _allgather(_pack(grads_l), "gather_small")
    zeros_cw = jnp.zeros((3, 2 * f), F32)

    def small_state(src):
        return _pack([zeros_cw if n == "conv_w" else src[n][l] for l in range(depth) for n in SMALL])

    sm = _adamw(gathered, small_state(weights), small_state(mom1), small_state(mom2), "adamw_small")
    sm = [_unpack(t, grads_l) for t in sm]
    small_out = {n: [jnp.stack([sm[k][l * len(SMALL) + i] for l in range(depth)]) for k in range(4)]
                 for i, n in enumerate(SMALL)}
    cw_grad = lax.dynamic_slice_in_dim(small_out["conv_w"][0], dev * fs, fs, axis=2)
    cw_rows_pad = cw_rows + (-cw_rows % 8)

    def rows8(a):
        return jnp.pad(a.reshape(cw_rows, fs), ((0, cw_rows_pad - cw_rows), (0, 0)))

    cw_res = _adamw(rows8(cw_grad)[None], rows8(conv_w), rows8(m_conv_w), rows8(v_conv_w), "adamw_conv_w")
    small_out["conv_w"] = [t[:cw_rows].reshape(depth, 3, fs) for t in cw_res]

    results = {n: (big[n] if n in big else small_out[n]) for n in order}
    grad_x = dx.reshape(1, s, d)
    return (loss, grad_x, *[results[n][0] for n in order], *[results[n][1] for n in order],
            *[results[n][2] for n in order], *[results[n][3] for n in order])
```

```python
import math

import jax
import jax.numpy as jnp
from jax import lax
from jax.experimental import pallas as pl
from jax.experimental.pallas import tpu as pltpu

F32 = jnp.float32
BF16 = jnp.bfloat16

HEAD = 128
GRID_W = 64
WIN_R = 8
WIN_C = 16
BAND = 128
ROPE_THETA = 10000.0
EPS = 1e-6
NEG = -1e30
SCALE = 1.0 / math.sqrt(HEAD)

ADAM_LR = 0.001
ADAM_B1 = 0.9
ADAM_B2 = 0.999
ADAM_EPS = 1e-08
ADAM_WD = 0.01
ADAM_STEP = 10

N_DEV = 8
LANES = 128
VMEM_LIMIT_BYTES = 56 * 2 ** 20
MESH = pl.DeviceIdType.MESH
ANY = pl.BlockSpec(memory_space=pl.ANY)
SMEM = pl.BlockSpec(memory_space=pltpu.SMEM)


def _params():
    return pltpu.CompilerParams(vmem_limit_bytes=VMEM_LIMIT_BYTES)


def _tile(n, pref, align):
    t = min(n, pref)
    t -= t % align
    while t > 0 and n % t:
        t -= align
    return t if t > 0 else n


def _place():
    x, y, c = lax.axis_index("x"), lax.axis_index("y"), lax.axis_index("c")
    chips = [(1 - x, y), (x, 1 - y), (1 - x, 1 - y)]
    return x, y, c, chips


COPIES_PER_ARRAY = N_DEV - 1


def _comm_scratch(n_arrays):
    return [pltpu.SemaphoreType.DMA((COPIES_PER_ARRAY * n_arrays,)), pltpu.SemaphoreType.DMA((COPIES_PER_ARRAY * n_arrays,)),
            pltpu.SemaphoreType.DMA((n_arrays,))]


def _gather_plan(src_refs, out_refs, send_sems, recv_sems, local_sems):
    x, y, c, chips = _place()
    me, sibling = (x, y, c), (x, y, 1 - c)

    def slot(a, px, py, pc):
        return out_refs[a].at[4 * px + 2 * py + pc]

    def copy(a, k, block, to, src=None):
        return pltpu.make_async_remote_copy(
            src_ref=slot(a, *block) if src is None else src, dst_ref=slot(a, *block),
            send_sem=send_sems.at[COPIES_PER_ARRAY * a + k], recv_sem=recv_sems.at[COPIES_PER_ARRAY * a + k],
            device_id=to, device_id_type=MESH)

    def mine(a):
        return pltpu.make_async_copy(src_refs[a], slot(a, *me), local_sems.at[a])

    def first(a):
        return [copy(a, 0, me, sibling, src=src_refs[a])] + [
            copy(a, 1 + j, me, (*chip, c), src=src_refs[a]) for j, chip in enumerate(chips)]

    def passed(a):
        return [copy(a, 4 + j, (*chip, c), sibling) for j, chip in enumerate(chips)]

    def start():
        for a in range(len(src_refs)):
            mine(a).start()
            for cp in first(a):
                cp.start()

    def middle():
        for a in range(len(src_refs)):
            forwards = passed(a)
            for j, chip in enumerate(chips):
                copy(a, 1 + j, (*chip, c), me).wait_recv()
                forwards[j].start()

    def finish():
        for a in range(len(src_refs)):
            copy(a, 0, sibling, me).wait_recv()
            for j, chip in enumerate(chips):
                copy(a, 4 + j, (*chip, 1 - c), me).wait_recv()
            for cp in first(a) + passed(a):
                cp.wait_send()
            mine(a).wait()

    return start, middle, finish


def _scatter_plan(src_refs, out_refs, send_sems, recv_sems, local_sems):
    x, y, c, _ = _place()
    me = 4 * x + 2 * y + c

    def peer(k):
        px = 1 - x if k & 4 else x
        py = 1 - y if k & 2 else y
        pc = 1 - c if k & 1 else c
        return (px, py, pc), 4 * px + 2 * py + pc

    def copy(a, k, outgoing):
        to, idx = peer(k)
        return pltpu.make_async_remote_copy(
            src_ref=src_refs[a].at[idx], dst_ref=out_refs[a].at[me if outgoing else idx],
            send_sem=send_sems.at[COPIES_PER_ARRAY * a + k - 1], recv_sem=recv_sems.at[COPIES_PER_ARRAY * a + k - 1],
            device_id=to, device_id_type=MESH)

    def mine(a):
        return pltpu.make_async_copy(src_refs[a].at[me], out_refs[a].at[me], local_sems.at[a])

    def start():
        for a in range(len(src_refs)):
            mine(a).start()
            for k in range(1, N_DEV):
                copy(a, k, True).start()

    def finish():
        for a in range(len(src_refs)):
            for k in range(1, N_DEV):
                copy(a, k, False).wait_recv()
            for k in range(1, N_DEV):
                copy(a, k, True).wait_send()
            mine(a).wait()

    return start, lambda: None, finish


def _allgather(v, name):
    def body(v_ref, out_ref, send_sems, recv_sems, local_sems):
        start, middle, finish = _gather_plan([v_ref], [out_ref], send_sems, recv_sems, local_sems)
        start()
        middle()
        finish()

    return pl.pallas_call(
        body, name=name,
        out_shape=jax.ShapeDtypeStruct((N_DEV,) + v.shape, v.dtype),
        in_specs=[ANY], out_specs=ANY, scratch_shapes=_comm_scratch(1),
    )(v)


def _sibling_exchange(g, name):
    def body(g_ref, out_ref, send_sems, recv_sems):
        x, y, c, _ = _place()
        sibling = (x, y, 1 - c)
        copies = []
        for j in range(4):
            copies.append(pltpu.make_async_remote_copy(
                src_ref=g_ref.at[2 * j + (1 - c)], dst_ref=out_ref.at[j],
                send_sem=send_sems.at[j], recv_sem=recv_sems.at[j], device_id=sibling, device_id_type=MESH))
        for cp in copies:
            cp.start()
        for cp in copies:
            cp.wait_recv()
        for cp in copies:
            cp.wait_send()

    return pl.pallas_call(
        body, name=name,
        out_shape=jax.ShapeDtypeStruct((4,) + g.shape[1:], g.dtype),
        in_specs=[ANY], out_specs=ANY,
        scratch_shapes=[pltpu.SemaphoreType.DMA((4,)), pltpu.SemaphoreType.DMA((4,))],
    )(g)


def _pair_sum(g, got, core, name):
    _, r, c = g.shape
    tr = _tile(r, max(16, (1 << 20) // c), 16)

    def body(core_ref, g_ref, got_ref, o_ref):
        del core_ref
        o_ref[...] = (g_ref[...].astype(F32) + got_ref[...].astype(F32)).astype(o_ref.dtype)

    return pl.pallas_call(
        body, name=name,
        out_shape=jax.ShapeDtypeStruct((4, r, c), g.dtype),
        grid_spec=pltpu.PrefetchScalarGridSpec(
            num_scalar_prefetch=1, grid=(4, r // tr),
            in_specs=[pl.BlockSpec((None, tr, c), lambda j, i, core_ref: (2 * j + core_ref[0], i, 0)),
                      pl.BlockSpec((None, tr, c), lambda j, i, core_ref: (j, i, 0))],
            out_specs=pl.BlockSpec((None, tr, c), lambda j, i, core_ref: (j, i, 0))),
        compiler_params=_params(),
    )(core, g, got)


def _chip_plan(src_refs, out_refs, send_sems, recv_sems, local_sems):
    x, y, c, chips = _place()

    def copies(a):
        return [pltpu.make_async_remote_copy(
            src_ref=src_refs[a].at[2 * px + py], dst_ref=out_refs[a].at[k],
            send_sem=send_sems.at[COPIES_PER_ARRAY * a + k], recv_sem=recv_sems.at[COPIES_PER_ARRAY * a + k],
            device_id=(px, py, c), device_id_type=MESH) for k, (px, py) in enumerate(chips)]

    def mine(a):
        return pltpu.make_async_copy(src_refs[a].at[2 * x + y], out_refs[a].at[3], local_sems.at[a])

    def start():
        for a in range(len(src_refs)):
            mine(a).start()
            for cp in copies(a):
                cp.start()

    def finish():
        for a in range(len(src_refs)):
            for cp in copies(a):
                cp.wait_recv()
            for cp in copies(a):
                cp.wait_send()
            mine(a).wait()

    return start, lambda: None, finish


def _chip_comm(blocks):
    return _chip_plan, blocks, [jax.ShapeDtypeStruct(p.shape, p.dtype) for p in blocks]


def _pair_sums(g, core, name):
    got = _sibling_exchange(g, name + "_d2d")
    return _pair_sum(g, got, core, name + "_pair")


def _adamw(parts, w, m, v, name, layer=None, into=None):
    n_parts, r, c = parts.shape
    tr = _tile(r, max(8, (1 << 19) // c), 16 if parts.dtype == BF16 else 8)
    c1 = 1.0 - ADAM_B1 ** ADAM_STEP
    c2 = 1.0 - ADAM_B2 ** ADAM_STEP
    n_into = 0 if into is None else len(into)

    def body(p_ref, w_ref, m_ref, v_ref, *rest):
        g_out, d_out, m_out, v_out = rest[n_into:]
        g = p_ref[0].astype(F32)
        for k in range(1, n_parts):
            g = g + p_ref[k].astype(F32)
        m2 = ADAM_B1 * m_ref[...] + (1.0 - ADAM_B1) * g
        v2 = ADAM_B2 * v_ref[...] + (1.0 - ADAM_B2) * (g * g)
        g_out[...] = g
        m_out[...] = m2
        v_out[...] = v2
        d_out[...] = -ADAM_LR * ((m2 / c1) / (jnp.sqrt(v2 / c2) + ADAM_EPS) + ADAM_WD * w_ref[...])

    if layer is None:
        blk = pl.BlockSpec((tr, c), lambda i: (i, 0))
        out = jax.ShapeDtypeStruct((r, c), F32)
    else:
        blk = pl.BlockSpec((None, tr, c), lambda i: (layer, i, 0))
        out = jax.ShapeDtypeStruct(w.shape, F32)
    return pl.pallas_call(
        body, name=name, grid=(r // tr,),
        in_specs=[pl.BlockSpec((n_parts, tr, c), lambda i: (0, i, 0)), blk, blk, blk] + [ANY] * n_into,
        out_specs=[blk, blk, blk, blk], out_shape=[out, out, out, out],
        input_output_aliases={4 + k: k for k in range(n_into)},
        compiler_params=_params(),
    )(parts, w, m, v, *(into or ()))


def _exchange_steps(grid):
    flat, total = pl.program_id(0), grid[0]
    for axis in range(1, len(grid)):
        flat, total = flat * grid[axis] + pl.program_id(axis), total * grid[axis]
    return flat == 0, flat == max(3 * total // 4, min(1, total - 1)), flat == total - 1


def _run_exchange(hooks, grid, compute):
    start, middle, finish = hooks
    first, later, last = _exchange_steps(grid)
    pl.when(first)(start)
    compute()
    pl.when(later)(middle)
    pl.when(last)(finish)


def _host_exchange(body, comm, grid, n_in, n_out):
    plan, comm_in, comm_out = comm
    n = len(comm_in)

    def wrapped(*refs):
        ins, cin = refs[:n_in], refs[n_in:n_in + n]
        outs, cout = refs[n_in + n:n_in + n + n_out], refs[n_in + n + n_out:n_in + 2 * n + n_out]
        rest = refs[n_in + 2 * n + n_out:]
        _run_exchange(plan(cin, cout, *rest[len(rest) - 3:]), grid, lambda: body(*ins, *outs, *rest[:len(rest) - 3]))

    return wrapped, list(comm_in), [ANY] * n, list(comm_out), _comm_scratch(n)


def _gather_comm(shards):
    return _gather_plan, shards, [jax.ShapeDtypeStruct((N_DEV,) + v.shape, v.dtype) for v in shards]


def _scatter_comm(blocks):
    return _scatter_plan, blocks, [jax.ShapeDtypeStruct(g.shape, g.dtype) for g in blocks]


def _matmul(a, b, *, dims, ti, tj, tk, out_dtype, name, j_outer=True, resid=None, dev_major=False, comm=None,
            k_blocks=None, halved=None):
    a_shape = (a.shape[1], 2 * a.shape[2]) if halved == "a" else a.shape
    b_shape = (b.shape[1], 2 * b.shape[2]) if halved == "b" else b.shape
    if dims == "nn":
        (I, K), (K2, J) = a_shape, b_shape
    elif dims == "nt":
        (I, K), (J, K2) = a_shape, b_shape
    else:
        (K, I), (K2, J) = a_shape, b_shape
    assert K == K2 and I % ti == 0 and J % tj == 0 and K % tk == 0, (name, a.shape, b.shape, ti, tj, tk)
    assert halved is None or (halved, dims) in (("a", "nt"), ("b", "tn")), (name, halved, dims)
    k0, nk = k_blocks if k_blocks is not None else (0, K // tk)
    ni, nj = I // ti, J // tj

    def ij(g0, g1):
        return (g1, g0) if j_outer else (g0, g1)

    if dims == "nn":
        a_spec = pl.BlockSpec((ti, tk), lambda g0, g1, k: (ij(g0, g1)[0], k0 + k))
        b_spec = pl.BlockSpec((tk, tj), lambda g0, g1, k: (k0 + k, ij(g0, g1)[1]))
        dn = (((1,), (0,)), ((), ()))
    elif dims == "nt":
        a_spec = pl.BlockSpec((ti, tk), lambda g0, g1, k: (ij(g0, g1)[0], k0 + k))
        if halved == "a":
            per = K // 2 // tk
            a_spec = pl.BlockSpec((None, ti, tk), lambda g0, g1, k: ((k0 + k) // per, ij(g0, g1)[0], (k0 + k) % per))
        b_spec = pl.BlockSpec((tj, tk), lambda g0, g1, k: (ij(g0, g1)[1], k0 + k))
        dn = (((1,), (1,)), ((), ()))
    else:
        a_spec = pl.BlockSpec((tk, ti), lambda g0, g1, k: (k0 + k, ij(g0, g1)[0]))
        b_spec = pl.BlockSpec((tk, tj), lambda g0, g1, k: (k0 + k, ij(g0, g1)[1]))
        if halved == "b":
            per = J // 2 // tj
            b_spec = pl.BlockSpec((None, tk, tj), lambda g0, g1, k: (ij(g0, g1)[1] // per, k0 + k, ij(g0, g1)[1] % per))
        dn = (((0,), (0,)), ((), ()))
    in_specs = [a_spec, b_spec]
    operands = [a, b]
    if resid is not None:
        in_specs.append(pl.BlockSpec((ti, tj), lambda g0, g1, k: ij(g0, g1)))
        operands.append(resid)
    if dev_major:
        out_spec = pl.BlockSpec((None, ti, tj), lambda g0, g1, k: (ij(g0, g1)[1], ij(g0, g1)[0], 0))
        out_shape = jax.ShapeDtypeStruct((nj, I, tj), out_dtype)
    else:
        out_spec = pl.BlockSpec((ti, tj), lambda g0, g1, k: ij(g0, g1))
        out_shape = jax.ShapeDtypeStruct((I, J), out_dtype)

    grid = (nj, ni, nk) if j_outer else (ni, nj, nk)
    n_in = len(operands)
    n_comm = 0
    out_specs, out_shapes = [out_spec], [out_shape]
    scratch = [pltpu.VMEM((ti, tj), F32)] if nk > 1 else []
    if comm is not None:
        plan, comm_in, comm_out = comm
        n_comm = len(comm_in)
        operands += list(comm_in)
        in_specs += [ANY] * n_comm
        out_specs += [ANY] * n_comm
        out_shapes += list(comm_out)
        scratch += _comm_scratch(n_comm)

    def product(*refs):
        a_ref, b_ref = refs[0], refs[1]
        r_ref = refs[2] if resid is not None else None
        o_ref = refs[n_in + n_comm]
        part = lax.dot_general(a_ref[...].astype(BF16), b_ref[...].astype(BF16), dn, preferred_element_type=F32)

        def finish(acc):
            if r_ref is not None:
                acc = acc + r_ref[...]
            o_ref[...] = acc.astype(o_ref.dtype)

        if nk == 1:
            finish(part)
        else:
            acc_ref = refs[n_in + 2 * n_comm + 1]
            k = pl.program_id(2)

            @pl.when(k == 0)
            def _():
                acc_ref[...] = part

            @pl.when(k > 0)
            def _():
                acc_ref[...] += part

            @pl.when(k == nk - 1)
            def _():
                finish(acc_ref[...])

    def body(*refs):
        if comm is None:
            product(*refs)
        else:
            hooks = plan(refs[n_in:n_in + n_comm], refs[n_in + n_comm + 1:n_in + 2 * n_comm + 1], *refs[-3:])
            _run_exchange(hooks, grid, lambda: product(*refs))

    res = pl.pallas_call(
        body, name=name, grid=grid,
        in_specs=in_specs, out_specs=out_specs, out_shape=out_shapes,
        scratch_shapes=scratch, compiler_params=_params(),
    )(*operands)
    return res[0] if comm is None else res


ROW_TILE = 512


def _rms_fwd(x, g, name, comm=None):
    s, d = x.shape
    ts = _tile(s, ROW_TILE, 16)

    def body(x_ref, g_ref, h_ref):
        xv = x_ref[...]
        r = lax.rsqrt(jnp.mean(xv * xv, axis=-1, keepdims=True) + EPS)
        h_ref[...] = (xv * r * g_ref[...]).astype(BF16)

    return _call(
        body, comm, name=name, grid=(s // ts,), operands=[x, g],
        in_specs=[pl.BlockSpec((ts, d), lambda i: (i, 0)), pl.BlockSpec((1, d), lambda i: (0, 0))],
        out_specs=[pl.BlockSpec((ts, d), lambda i: (i, 0))],
        out_shape=[jax.ShapeDtypeStruct((s, d), BF16)], scratch_shapes=[])


def _rms_bwd(x, g, dh, dres, name, comm=None):
    s, d = x.shape
    ts = _tile(s, ROW_TILE, 16)

    def body(x_ref, g_ref, dh_ref, dres_ref, dx_ref, dxb_ref, dg_ref):
        xv = x_ref[...]
        r = lax.rsqrt(jnp.mean(xv * xv, axis=-1, keepdims=True) + EPS)
        y = xv * r
        dhv = dh_ref[...]
        gd = dhv * g_ref[...]
        dxv = dres_ref[...] + r * (gd - y * jnp.mean(gd * y, axis=-1, keepdims=True))
        dx_ref[...] = dxv
        dxb_ref[...] = dxv.astype(BF16)

        @pl.when(pl.program_id(0) == 0)
        def _():
            dg_ref[...] = jnp.zeros_like(dg_ref)

        dg_ref[0:1, :] += jnp.sum(dhv * y, axis=0, keepdims=True)

    blk = pl.BlockSpec((ts, d), lambda i: (i, 0))
    return _call(
        body, comm, name=name, grid=(s // ts,), operands=[x, g, dh, dres],
        in_specs=[blk, pl.BlockSpec((1, d), lambda i: (0, 0)), blk, blk],
        out_specs=[blk, blk, pl.BlockSpec((8, d), lambda i: (0, 0))],
        out_shape=[jax.ShapeDtypeStruct((s, d), F32), jax.ShapeDtypeStruct((s, d), BF16), jax.ShapeDtypeStruct((8, d), F32)],
        scratch_shapes=[])


def _head_norm(t, gain):
    r = lax.rsqrt(jnp.mean(t * t, axis=-1, keepdims=True) + EPS)
    return t * r * gain


def _head_norm_bwd(t, gain, dn):
    r = lax.rsqrt(jnp.mean(t * t, axis=-1, keepdims=True) + EPS)
    y = t * r
    gd = dn * gain
    dt = r * (gd - y * jnp.mean(gd * y, axis=-1, keepdims=True))
    return dt, jnp.sum(dn * y, axis=0, keepdims=True)


def _rope(n, cos, sin):
    return n * cos + pltpu.roll(n, HEAD // 2, axis=1) * sin


def _rope_bwd(do, cos, sin):
    return do * cos + pltpu.roll(do * sin, HEAD // 2, axis=1)


def _qkv_fwd(proj, gains, cos, sin, cfg, name):
    s, pw = proj.shape
    ha, hq, hkv = cfg
    ts = _tile(s, ROW_TILE, 16)

    def body(p_ref, gn_ref, cos_ref, sin_ref, qa_ref, ka_ref, va_ref, qb_ref, kb_ref, vb_ref):
        cosv, sinv = cos_ref[...], sin_ref[...]
        col = 0
        for out_ref, nh, gi, rot in ((qa_ref, ha, 0, False), (ka_ref, ha, 1, False), (va_ref, ha, None, False),
                                     (qb_ref, hq, 2, True), (kb_ref, hkv, 3, True), (vb_ref, hkv, None, False)):
            for h in range(nh):
                t = p_ref[:, col * HEAD:(col + 1) * HEAD]
                if gi is not None:
                    t = _head_norm(t, gn_ref[gi:gi + 1, :])
                if rot:
                    t = _rope(t, cosv, sinv)
                out_ref[h] = t.astype(BF16)
                col += 1

    def hm(nh):
        return pl.BlockSpec((nh, ts, HEAD), lambda i: (0, i, 0)), jax.ShapeDtypeStruct((nh, s, HEAD), BF16)

    specs, shapes = zip(hm(ha), hm(ha), hm(ha), hm(hq), hm(hkv), hm(hkv))
    tok = pl.BlockSpec((ts, HEAD), lambda i: (i, 0))
    return pl.pallas_call(
        body, name=name, grid=(s // ts,),
        in_specs=[pl.BlockSpec((ts, pw), lambda i: (i, 0)), pl.BlockSpec((8, HEAD), lambda i: (0, 0)), tok, tok],
        out_specs=list(specs), out_shape=list(shapes), compiler_params=_params(),
    )(proj, gains, cos, sin)


def _qkv_bwd(proj, gains, cos, sin, grads, cfg, name):
    s, pw = proj.shape
    ha, hq, hkv = cfg
    ts = _tile(s, 256, 16)

    def body(p_ref, gn_ref, cos_ref, sin_ref, dqa, dka, dva, dqb, dkb, dvb, dp_ref, dgn_ref):
        cosv, sinv = cos_ref[...], sin_ref[...]

        @pl.when(pl.program_id(0) == 0)
        def _():
            dgn_ref[...] = jnp.zeros_like(dgn_ref)

        col = 0
        for d_ref, nh, gi, rot in ((dqa, ha, 0, False), (dka, ha, 1, False), (dva, ha, None, False),
                                   (dqb, hq, 2, True), (dkb, hkv, 3, True), (dvb, hkv, None, False)):
            dgain = jnp.zeros((1, HEAD), F32)
            for h in range(nh):
                dt = d_ref[h]
                if rot:
                    dt = _rope_bwd(dt, cosv, sinv)
                if gi is not None:
                    dt, dg = _head_norm_bwd(p_ref[:, col * HEAD:(col + 1) * HEAD], gn_ref[gi:gi + 1, :], dt)
                    dgain = dgain + dg
                dp_ref[:, col * HEAD:(col + 1) * HEAD] = dt.astype(BF16)
                col += 1
            if gi is not None:
                dgn_ref[gi:gi + 1, :] += dgain

    def hm(nh):
        return pl.BlockSpec((nh, ts, HEAD), lambda i: (0, i, 0))

    tok = pl.BlockSpec((ts, HEAD), lambda i: (i, 0))
    small = pl.BlockSpec((8, HEAD), lambda i: (0, 0))
    return pl.pallas_call(
        body, name=name, grid=(s // ts,),
        in_specs=[pl.BlockSpec((ts, pw), lambda i: (i, 0)), small, tok, tok,
                  hm(ha), hm(ha), hm(ha), hm(hq), hm(hkv), hm(hkv)],
        out_specs=[pl.BlockSpec((ts, pw), lambda i: (i, 0)), small],
        out_shape=[jax.ShapeDtypeStruct((s, pw), BF16), jax.ShapeDtypeStruct((8, HEAD), F32)],
        compiler_params=_params(),
    )(proj, gains, cos, sin, *grads)


NA_QROWS = 32
NA_KEYS = WIN_R * GRID_W
N_DR = 2 * WIN_R - 1
N_DC = 2 * WIN_C - 1


def _na_bias(rpb_flat, n_heads, name):
    def body(rpb_ref, tb_ref):
        h = pl.program_id(0)
        qi = lax.broadcasted_iota(jnp.int32, (GRID_W, LANES), 0)
        lane = lax.broadcasted_iota(jnp.int32, (GRID_W, LANES), 1)
        kk = lane & (GRID_W - 1)
        upper = lane >= GRID_W
        dcm = kk - qi + (WIN_C - 1)
        cs = jnp.clip(qi - WIN_C // 2, 0, GRID_W - WIN_C)
        valid = (kk >= cs) & (kk < cs + WIN_C)
        base = h * (N_DR * N_DC)
        for dra in range(N_DR - 1):
            def step(j, acc, dra=dra):
                va = rpb_ref[base + dra * N_DC + j]
                vb = rpb_ref[base + (dra + 1) * N_DC + j]
                return jnp.where(dcm == j, jnp.where(upper, vb, va), acc)

            pair = lax.fori_loop(0, N_DC, step, jnp.zeros((GRID_W, LANES), F32))
            pair = jnp.where(valid, pair, NEG)
            for dr0 in range(WIN_R):
                wp, odd = divmod(dra - dr0, 2)
                if odd == 0 and 0 <= wp < WIN_R // 2:
                    tb_ref[0, dr0, :, wp * LANES:(wp + 1) * LANES] = pair

    return pl.pallas_call(
        body, name=name, grid=(n_heads,),
        in_specs=[SMEM],
        out_specs=pl.BlockSpec((1, WIN_R, GRID_W, NA_KEYS), lambda h: (h, 0, 0, 0)),
        out_shape=jax.ShapeDtypeStruct((n_heads, WIN_R, GRID_W, NA_KEYS), F32),
        compiler_params=_params(),
    )(rpb_flat)


def _na_row(b, i, nrows, qrows):
    r = b * qrows + i
    rs = jnp.clip(r - WIN_R // 2, 0, nrows - WIN_R)
    return pl.ds(pl.multiple_of(rs * GRID_W, GRID_W), NA_KEYS), rs - r + (WIN_R - 1)


def _softmax(s):
    e = jnp.exp(s - jnp.max(s, axis=-1, keepdims=True))
    return e * (1.0 / jnp.sum(e, axis=-1, keepdims=True))


_NT = (((1,), (1,)), ((), ()))
_NN = (((1,), (0,)), ((), ()))
_TN = (((0,), (0,)), ((), ()))


def _dot(a, b, dn):
    return lax.dot_general(a, b, dn, preferred_element_type=F32)


def _call(body, comm, *, name, grid, operands, in_specs, out_specs, out_shape, scratch_shapes):
    if comm is not None:
        body, more_operands, more_specs, more_shapes, sems = _host_exchange(body, comm, grid, len(operands), len(out_shape))
        operands = operands + more_operands
        in_specs = in_specs + more_specs
        out_specs = out_specs + more_specs
        out_shape = out_shape + more_shapes
        scratch_shapes = scratch_shapes + sems
    res = pl.pallas_call(body, name=name, grid=grid, in_specs=in_specs, out_specs=out_specs, out_shape=out_shape,
                         scratch_shapes=scratch_shapes, compiler_params=_params())(*operands)
    return res[0] if len(res) == 1 else res


def _na_fwd(q, k, v, tb, name, comm=None):
    nh, s, _ = q.shape
    nrows = s // GRID_W
    qrows = _tile(nrows, NA_QROWS, WIN_R)
    tq = qrows * GRID_W

    def body(q_ref, k_ref, v_ref, tb_ref, o_ref, s_scr, p_scr):
        b = pl.program_id(1)
        rows = [slice(i * GRID_W, (i + 1) * GRID_W) for i in range(qrows)]
        at = [_na_row(b, i, nrows, qrows) for i in range(qrows)]
        for i, (keys, dr0) in enumerate(at):
            s_scr[i] = _dot(q_ref[rows[i], :], k_ref[keys, :], _NT) * SCALE + tb_ref[0, dr0]
        for i in range(qrows):
            p_scr[i] = _softmax(s_scr[i]).astype(BF16)
        for i, (keys, _) in enumerate(at):
            o_ref[rows[i], :] = _dot(p_scr[i], v_ref[keys, :], _NN)

    qspec = pl.BlockSpec((None, tq, HEAD), lambda h, b: (h, b, 0))
    full = pl.BlockSpec((None, s, HEAD), lambda h, b: (h, 0, 0))
    return _call(
        body, comm, name=name, grid=(nh, nrows // qrows), operands=[q, k, v, tb],
        in_specs=[qspec, full, full, pl.BlockSpec((1, WIN_R, GRID_W, NA_KEYS), lambda h, b: (h, 0, 0, 0))],
        out_specs=[qspec], out_shape=[jax.ShapeDtypeStruct((nh, s, HEAD), F32)],
        scratch_shapes=[pltpu.VMEM((qrows, GRID_W, NA_KEYS), F32), pltpu.VMEM((qrows, GRID_W, NA_KEYS), BF16)])


def _na_bwd(q, k, v, tb, do, name, comm=None):
    nh, s, _ = q.shape
    nrows = s // GRID_W
    qrows = _tile(nrows, NA_QROWS, WIN_R)
    tq = qrows * GRID_W

    def body(q_ref, do_ref, k_ref, v_ref, tb_ref, dq_ref, dk_ref, dv_ref, dtb_ref, s_scr, dp_scr, p_scr, ds_scr):
        b = pl.program_id(1)

        @pl.when(b == 0)
        def _():
            dk_ref[...] = jnp.zeros_like(dk_ref)
            dv_ref[...] = jnp.zeros_like(dv_ref)
            dtb_ref[...] = jnp.zeros_like(dtb_ref)

        rows = [slice(i * GRID_W, (i + 1) * GRID_W) for i in range(qrows)]
        at = [_na_row(b, i, nrows, qrows) for i in range(qrows)]
        for i, (keys, dr0) in enumerate(at):
            s_scr[i] = _dot(q_ref[rows[i], :], k_ref[keys, :], _NT) * SCALE + tb_ref[0, dr0]
            dp_scr[i] = _dot(do_ref[rows[i], :], v_ref[keys, :], _NT)
        for i in range(qrows):
            p = _softmax(s_scr[i])
            dp = dp_scr[i]
            ds = p * (dp - jnp.sum(p * dp, axis=-1, keepdims=True))
            p_scr[i] = p.astype(BF16)
            s_scr[i] = ds
            ds_scr[i] = (ds * SCALE).astype(BF16)
        for i, (keys, _) in enumerate(at):
            dq_ref[rows[i], :] = _dot(ds_scr[i], k_ref[keys, :], _NN)
        for i, (keys, dr0) in enumerate(at):
            dv_ref[keys, :] += _dot(p_scr[i], do_ref[rows[i], :], _TN)
            dk_ref[keys, :] += _dot(ds_scr[i], q_ref[rows[i], :], _TN)
            dtb_ref[0, dr0] += s_scr[i]

    qspec = pl.BlockSpec((None, tq, HEAD), lambda h, b: (h, b, 0))
    full = pl.BlockSpec((None, s, HEAD), lambda h, b: (h, 0, 0))
    tbs = pl.BlockSpec((1, WIN_R, GRID_W, NA_KEYS), lambda h, b: (h, 0, 0, 0))
    hm = jax.ShapeDtypeStruct((nh, s, HEAD), F32)
    tile = (qrows, GRID_W, NA_KEYS)
    return _call(
        body, comm, name=name, grid=(nh, nrows // qrows), operands=[q, do, k, v, tb],
        in_specs=[qspec, qspec, full, full, tbs],
        out_specs=[qspec, full, full, tbs],
        out_shape=[hm, hm, hm, jax.ShapeDtypeStruct((nh, WIN_R, GRID_W, NA_KEYS), F32)],
        scratch_shapes=[pltpu.VMEM(tile, F32), pltpu.VMEM(tile, F32), pltpu.VMEM(tile, BF16), pltpu.VMEM(tile, BF16)])


def _rpb_fold(y, n_heads, name):
    def body(y_ref, o_ref):
        for h in range(n_heads):
            for dr in range(2 * WIN_R):
                acc = jnp.zeros((1, LANES), F32)
                for dr0 in range(WIN_R):
                    w = dr - dr0
                    if 0 <= w < WIN_R:
                        acc = acc + y_ref[h, dr0, w:w + 1, :]
                o_ref[h, dr:dr + 1, :] = acc

    return pl.pallas_call(
        body, name=name, out_shape=jax.ShapeDtypeStruct((n_heads, 2 * WIN_R, LANES), F32),
    )(y)


def _rpb_grad(dtb, onehot, name):
    nh = dtb.shape[0]
    rows = dtb.reshape(nh, WIN_R, GRID_W, WIN_R, GRID_W).transpose(0, 1, 3, 2, 4).reshape(nh * WIN_R * WIN_R, GRID_W * GRID_W)
    y = _matmul(rows, onehot, dims="nn", ti=rows.shape[0], tj=LANES, tk=GRID_W * GRID_W, out_dtype=F32, name=name + "_dc")
    folded = _rpb_fold(y.reshape(nh, WIN_R, WIN_R, LANES), nh, name + "_dr")
    return folded[:, :N_DR, :N_DC]


WA_WIN_TOK = 3 * BAND
WA_QBLOCKS = 4


def _wa_scores(q, kwin, t0, j, sink_ref, head0, grp):
    rows = grp * BAND
    s = _dot(q, kwin, _NT) * SCALE
    row = lax.broadcasted_iota(jnp.int32, (rows, WA_WIN_TOK), 0)
    qpos = j * BAND + (row & (BAND - 1))
    kpos = t0 + lax.broadcasted_iota(jnp.int32, (rows, WA_WIN_TOK), 1)
    s = jnp.where(jnp.abs(kpos - qpos) <= BAND, s, NEG)
    head = lax.broadcasted_iota(jnp.int32, (rows, 1), 0) // BAND
    sink = jnp.zeros((rows, 1), F32) + sink_ref[head0]
    for g in range(1, grp):
        sink = jnp.where(head == g, sink_ref[head0 + g], sink)
    m = jnp.maximum(jnp.max(s, axis=-1, keepdims=True), sink)
    e = jnp.exp(s - m)
    es = jnp.exp(sink - m)
    rz = 1.0 / (jnp.sum(e, axis=-1, keepdims=True) + es)
    return e * rz, es * rz


def _wa_window(j, s):
    return pl.multiple_of(jnp.clip((j - 1) * BAND, 0, s - WA_WIN_TOK), BAND)


def _wa_fwd(q, k, v, sink, name, comm=None):
    hq, s, _ = q.shape
    hkv = k.shape[0]
    grp = hq // hkv

    def body(sink_ref, q_ref, k_ref, v_ref, o_ref):
        kh, step = pl.program_id(0), pl.program_id(1)
        for sub in range(WA_QBLOCKS):
            j = step * WA_QBLOCKS + sub
            rows = slice(sub * BAND, (sub + 1) * BAND)
            t0 = _wa_window(j, s)
            keys = pl.ds(t0, WA_WIN_TOK)
            p, _ = _wa_scores(q_ref[:, rows, :].reshape(grp * BAND, HEAD), k_ref[keys, :], t0, j, sink_ref, kh * grp, grp)
            o_ref[:, rows, :] = _dot(p.astype(BF16), v_ref[keys, :], _NN).reshape(grp, BAND, HEAD)

    qspec = pl.BlockSpec((grp, WA_QBLOCKS * BAND, HEAD), lambda kh, j: (kh, j, 0))
    full = pl.BlockSpec((None, s, HEAD), lambda kh, j: (kh, 0, 0))
    return _call(
        body, comm, name=name, grid=(hkv, s // (WA_QBLOCKS * BAND)), operands=[sink, q, k, v],
        in_specs=[SMEM, qspec, full, full],
        out_specs=[qspec], out_shape=[jax.ShapeDtypeStruct((hq, s, HEAD), F32)], scratch_shapes=[])


def _wa_bwd(q, k, v, sink, do, name, comm=None):
    hq, s, _ = q.shape
    hkv = k.shape[0]
    grp = hq // hkv

    def body(sink_ref, q_ref, do_ref, k_ref, v_ref, dq_ref, dk_ref, dv_ref, dsink_ref):
        kh, step = pl.program_id(0), pl.program_id(1)

        @pl.when(step == 0)
        def _():
            dk_ref[...] = jnp.zeros_like(dk_ref)
            dv_ref[...] = jnp.zeros_like(dv_ref)
            dsink_ref[...] = jnp.zeros_like(dsink_ref)

        for sub in range(WA_QBLOCKS):
            j = step * WA_QBLOCKS + sub
            rows = slice(sub * BAND, (sub + 1) * BAND)
            t0 = _wa_window(j, s)
            keys = pl.ds(t0, WA_WIN_TOK)
            qs = q_ref[:, rows, :].reshape(grp * BAND, HEAD)
            dos = do_ref[:, rows, :].reshape(grp * BAND, HEAD)
            kwin, vwin = k_ref[keys, :], v_ref[keys, :]
            p, ps = _wa_scores(qs, kwin, t0, j, sink_ref, kh * grp, grp)
            dp = _dot(dos, vwin, _NT)
            dv_ref[keys, :] += _dot(p.astype(BF16), dos, _TN)
            rowdot = jnp.sum(p * dp, axis=-1, keepdims=True)
            to_sink = ps * rowdot
            for g in range(grp):
                dsink_ref[g] += jnp.zeros((8, LANES), F32) - jnp.sum(to_sink[g * BAND:(g + 1) * BAND])
            dss = (p * (dp - rowdot) * SCALE).astype(BF16)
            dq_ref[:, rows, :] = _dot(dss, kwin, _NN).reshape(grp, BAND, HEAD)
            dk_ref[keys, :] += _dot(dss, qs, _TN)

    qspec = pl.BlockSpec((grp, WA_QBLOCKS * BAND, HEAD), lambda kh, j: (kh, j, 0))
    full = pl.BlockSpec((None, s, HEAD), lambda kh, j: (kh, 0, 0))
    kv = jax.ShapeDtypeStruct((hkv, s, HEAD), F32)
    return _call(
        body, comm, name=name, grid=(hkv, s // (WA_QBLOCKS * BAND)), operands=[sink, q, do, k, v],
        in_specs=[SMEM, qspec, qspec, full, full],
        out_specs=[qspec, full, full, pl.BlockSpec((grp, 8, LANES), lambda kh, j: (kh, 0, 0))],
        out_shape=[jax.ShapeDtypeStruct((hq, s, HEAD), F32), kv, kv, jax.ShapeDtypeStruct((hq, 8, LANES), F32)],
        scratch_shapes=[])


def _onorm_fwd(oa, ob, gains, name):
    ha, s, _ = oa.shape
    hq = ob.shape[0]
    ts = _tile(s, ROW_TILE, 16)

    def body(oa_ref, ob_ref, g_ref, o_ref):
        col = 0
        for ref, nh in ((oa_ref, ha), (ob_ref, hq)):
            ss = sum(jnp.sum(ref[h] * ref[h], axis=-1, keepdims=True) for h in range(nh))
            r = lax.rsqrt(ss / (nh * HEAD) + EPS)
            for h in range(nh):
                o_ref[:, col * HEAD:(col + 1) * HEAD] = (ref[h] * r * g_ref[:, col * HEAD:(col + 1) * HEAD]).astype(BF16)
                col += 1

    mix = (ha + hq) * HEAD
    return pl.pallas_call(
        body, name=name, grid=(s // ts,),
        in_specs=[pl.BlockSpec((ha, ts, HEAD), lambda i: (0, i, 0)), pl.BlockSpec((hq, ts, HEAD), lambda i: (0, i, 0)),
                  pl.BlockSpec((1, mix), lambda i: (0, 0))],
        out_specs=pl.BlockSpec((ts, mix), lambda i: (i, 0)),
        out_shape=jax.ShapeDtypeStruct((s, mix), BF16), compiler_params=_params(),
    )(oa, ob, gains)


def _onorm_bwd(oa, ob, gains, don, name):
    ha, s, _ = oa.shape
    hq = ob.shape[0]
    ts = _tile(s, ROW_TILE, 16)
    mix = (ha + hq) * HEAD

    def body(oa_ref, ob_ref, g_ref, don_ref, doa_ref, dob_ref, dg_ref):
        @pl.when(pl.program_id(0) == 0)
        def _():
            dg_ref[...] = jnp.zeros_like(dg_ref)

        col0 = 0
        for ref, d_ref, nh in ((oa_ref, doa_ref, ha), (ob_ref, dob_ref, hq)):
            ss = sum(jnp.sum(ref[h] * ref[h], axis=-1, keepdims=True) for h in range(nh))
            r = lax.rsqrt(ss / (nh * HEAD) + EPS)
            dot = jnp.zeros((ts, 1), F32)
            for h in range(nh):
                cols = slice((col0 + h) * HEAD, (col0 + h + 1) * HEAD)
                dot = dot + jnp.sum(don_ref[:, cols] * g_ref[:, cols] * ref[h], axis=-1, keepdims=True)
            mean = dot * r / (nh * HEAD)
            for h in range(nh):
                cols = slice((col0 + h) * HEAD, (col0 + h + 1) * HEAD)
                y = ref[h] * r
                dn = don_ref[:, cols]
                d_ref[h] = (r * (dn * g_ref[:, cols] - y * mean)).astype(BF16)
                dg_ref[0:1, cols] += jnp.sum(dn * y, axis=0, keepdims=True)
            col0 += nh

    return pl.pallas_call(
        body, name=name, grid=(s // ts,),
        in_specs=[pl.BlockSpec((ha, ts, HEAD), lambda i: (0, i, 0)), pl.BlockSpec((hq, ts, HEAD), lambda i: (0, i, 0)),
                  pl.BlockSpec((1, mix), lambda i: (0, 0)), pl.BlockSpec((ts, mix), lambda i: (i, 0))],
        out_specs=[pl.BlockSpec((ha, ts, HEAD), lambda i: (0, i, 0)), pl.BlockSpec((hq, ts, HEAD), lambda i: (0, i, 0)),
                   pl.BlockSpec((8, mix), lambda i: (0, 0))],
        out_shape=[jax.ShapeDtypeStruct((ha, s, HEAD), BF16), jax.ShapeDtypeStruct((hq, s, HEAD), BF16),
                   jax.ShapeDtypeStruct((8, mix), F32)],
        compiler_params=_params(),
    )(oa, ob, gains, don)


HALO = 8
PACKED = 16


def _halo_specs(ts, tc, col_off):
    per = ts // HALO
    cur = pl.BlockSpec((ts, tc), lambda j, i: (i, j + col_off))
    prev = pl.BlockSpec((HALO, tc), lambda j, i: (jnp.maximum(i * per - 1, 0), j + col_off))

    def nxt_map(n_blocks):
        return pl.BlockSpec((HALO, tc), lambda j, i: (jnp.minimum((i + 1) * per, n_blocks - 1), j + col_off))

    return cur, prev, nxt_map


def _sigmoid(x):
    return 1.0 / (1.0 + jnp.exp(-x))


def _ffn_tiles(s, f):
    return _tile(s, 512, 16), _tile(f, 512, LANES)


def _gate_fwd(u, cw, cb, f, name):
    s = u.shape[0]
    ts, tc = _ffn_tiles(s, f)
    nj, ni = f // tc, s // ts

    def body(g_ref, gp_ref, gn_ref, u_ref, up_ref, un_ref, wg_ref, wu_ref, bg_ref, bu_ref, a_ref, gu_ref):
        i = pl.program_id(1)

        def conv(c_ref, p_ref, n_ref, w_ref, b_ref):
            ext = jnp.concatenate([jnp.where(i > 0, p_ref[...], 0.0), c_ref[...], jnp.where(i < ni - 1, n_ref[...], 0.0)], axis=0)
            rows = ts + 2 * HALO
            out = (pltpu.roll(ext, 1, axis=0) * w_ref[0:1, :] + ext * w_ref[1:2, :]
                   + pltpu.roll(ext, rows - 1, axis=0) * w_ref[2:3, :] + b_ref[...])
            return out[HALO:HALO + ts]

        gate = conv(g_ref, gp_ref, gn_ref, wg_ref, bg_ref)
        up = conv(u_ref, up_ref, un_ref, wu_ref, bu_ref)
        gu_ref[0] = gate.astype(BF16)
        gu_ref[1] = up.astype(BF16)
        a_ref[...] = (gate * _sigmoid(gate) * up).astype(BF16)

    gc, gp, gn = _halo_specs(ts, tc, 0)
    uc, up_, un = _halo_specs(ts, tc, nj)
    wg = pl.BlockSpec((3, tc), lambda j, i: (0, j))
    wu = pl.BlockSpec((3, tc), lambda j, i: (0, j + nj))
    bg = pl.BlockSpec((1, tc), lambda j, i: (0, j))
    bu = pl.BlockSpec((1, tc), lambda j, i: (0, j + nj))
    return pl.pallas_call(
        body, name=name, grid=(nj, ni),
        in_specs=[gc, gp, gn(s // HALO), uc, up_, un(s // HALO), wg, wu, bg, bu],
        out_specs=[pl.BlockSpec((ts, tc), lambda j, i: (i, j)), pl.BlockSpec((2, ts, tc), lambda j, i: (0, i, j))],
        out_shape=[jax.ShapeDtypeStruct((s, f), BF16), jax.ShapeDtypeStruct((2, s, f), BF16)], compiler_params=_params(),
    )(u, u, u, u, u, u, cw, cw, cb, cb)


def _ffn_bwd(gu, u, da, cw, name):
    _, s, f = gu.shape
    ts, tc = _ffn_tiles(s, f)
    nj, ni = f // tc, s // ts

    def body(gu_ref, gup_ref, gun_ref, da_ref, dap_ref, dan_ref, xg_ref, xu_ref, wg_ref, wu_ref,
             du_ref, dcw_ref, dcb_ref):
        i = pl.program_id(1)

        @pl.when(i == 0)
        def _():
            dcw_ref[...] = jnp.zeros_like(dcw_ref)
            dcb_ref[...] = jnp.zeros_like(dcb_ref)

        rows = ts + 2 * HALO
        mid = slice(HALO, HALO + ts)
        da = jnp.concatenate([jnp.where(i > 0, dap_ref[...], 0.0), da_ref[...], jnp.where(i < ni - 1, dan_ref[...], 0.0)], axis=0)
        def rows_of(half):
            before = gup_ref[half].astype(F32)[PACKED - HALO:]
            after = gun_ref[half].astype(F32)[:HALO]
            return jnp.concatenate([before, gu_ref[half].astype(F32), after], axis=0)

        gate, up = rows_of(0), rows_of(1)
        sg = _sigmoid(gate)
        d_up = da * gate * sg
        d_gate = da * up * (sg * (1.0 + gate * (1.0 - sg)))
        for half, (dd, x_ref, w_ref) in enumerate(((d_gate, xg_ref, wg_ref), (d_up, xu_ref, wu_ref))):
            before = pltpu.roll(dd, 1, axis=0)
            after = pltpu.roll(dd, rows - 1, axis=0)
            du_ref[half] = (before * w_ref[2:3, :] + dd * w_ref[1:2, :] + after * w_ref[0:1, :])[mid].astype(BF16)
            x = x_ref[...]
            dcb_ref[half, 0:1, :] += jnp.sum(dd[mid], axis=0, keepdims=True)
            for k, shifted in enumerate((after, dd, before)):
                dcw_ref[half, k, 0:1, :] += jnp.sum(shifted[mid] * x, axis=0, keepdims=True)

    per = ts // PACKED
    cur3 = pl.BlockSpec((2, ts, tc), lambda j, i: (0, i, j))
    prev3 = pl.BlockSpec((2, PACKED, tc), lambda j, i: (0, jnp.maximum(i * per - 1, 0), j))
    next3 = pl.BlockSpec((2, PACKED, tc), lambda j, i: (0, jnp.minimum((i + 1) * per, s // PACKED - 1), j))
    cur, prev, nxt = _halo_specs(ts, tc, 0)
    return pl.pallas_call(
        body, name=name, grid=(nj, ni),
        in_specs=[cur3, prev3, next3, cur, prev, nxt(s // HALO),
                  pl.BlockSpec((ts, tc), lambda j, i: (i, j)), pl.BlockSpec((ts, tc), lambda j, i: (i, j + nj)),
                  pl.BlockSpec((3, tc), lambda j, i: (0, j)), pl.BlockSpec((3, tc), lambda j, i: (0, j + nj))],
        out_specs=[cur3, pl.BlockSpec((2, 3, 8, tc), lambda j, i: (0, 0, 0, j)),
                   pl.BlockSpec((2, 8, tc), lambda j, i: (0, 0, j))],
        out_shape=[jax.ShapeDtypeStruct((2, s, f), BF16),
                   jax.ShapeDtypeStruct((2, 3, 8, f), F32), jax.ShapeDtypeStruct((2, 8, f), F32)],
        compiler_params=_params(),
    )(gu, gu, gu, da, da, da, u, u, cw, cw)


def _loss_head(y, target, name):
    s, d = y.shape
    ts = _tile(s, ROW_TILE, 16)

    def body(y_ref, t_ref, dy_ref, dyb_ref, l_ref):
        @pl.when(pl.program_id(0) == 0)
        def _():
            l_ref[...] = jnp.zeros_like(l_ref)

        err = y_ref[...] - t_ref[...]
        dy = err / d
        dy_ref[...] = dy
        dyb_ref[...] = dy.astype(BF16)
        l_ref[...] += jnp.zeros((8, LANES), F32) + 0.5 * jnp.sum(jnp.sum(err * err, axis=-1, keepdims=True) / d)

    blk = pl.BlockSpec((ts, d), lambda i: (i, 0))
    return pl.pallas_call(
        body, name=name, grid=(s // ts,),
        in_specs=[blk, blk], out_specs=[blk, blk, pl.BlockSpec((8, LANES), lambda i: (0, 0))],
        out_shape=[jax.ShapeDtypeStruct((s, d), F32), jax.ShapeDtypeStruct((s, d), BF16), jax.ShapeDtypeStruct((8, LANES), F32)],
        compiler_params=_params(),
    )(y, target)


SMALL = ("ln1_g", "qn_a", "kn_a", "rpb", "qn_b", "kn_b", "sink", "on_a", "on_b", "ln2_g", "conv_b", "conv_w")
PACK_ALIGN = 8 * LANES


def _pack(arrays):
    flat = []
    for a in arrays:
        a = a.reshape(-1)
        flat.append(jnp.pad(a, (0, -a.size % PACK_ALIGN)))
    return jnp.concatenate(flat).reshape(-1, LANES)


def _unpack(packed, like):
    out, at = [], 0
    flat = packed.reshape(-1)
    for a in like:
        out.append(flat[at:at + a.size].reshape(a.shape))
        at += a.size + (-a.size % PACK_ALIGN)
    return out


def kernel(x, positions, ln1_g, w_in, qn_a, kn_a, rpb, qn_b, kn_b, sink, on_a, on_b, w_out, ln2_g, w_up, conv_w, conv_b, w_down, loss_target, m_ln1_g, m_w_in, m_qn_a, m_kn_a, m_rpb, m_qn_b, m_kn_b, m_sink, m_on_a, m_on_b, m_w_out, m_ln2_g, m_w_up, m_conv_w, m_conv_b, m_w_down, v_ln1_g, v_w_in, v_qn_a, v_kn_a, v_rpb, v_qn_b, v_kn_b, v_sink, v_on_a, v_on_b, v_w_out, v_ln2_g, v_w_up, v_conv_w, v_conv_b, v_w_down):
    weights = dict(ln1_g=ln1_g, w_in=w_in, qn_a=qn_a, kn_a=kn_a, rpb=rpb, qn_b=qn_b, kn_b=kn_b, sink=sink, on_a=on_a,
                   on_b=on_b, w_out=w_out, ln2_g=ln2_g, w_up=w_up, conv_w=conv_w, conv_b=conv_b, w_down=w_down)
    mom1 = dict(ln1_g=m_ln1_g, w_in=m_w_in, qn_a=m_qn_a, kn_a=m_kn_a, rpb=m_rpb, qn_b=m_qn_b, kn_b=m_kn_b, sink=m_sink,
                on_a=m_on_a, on_b=m_on_b, w_out=m_w_out, ln2_g=m_ln2_g, w_up=m_w_up, conv_w=m_conv_w, conv_b=m_conv_b,
                w_down=m_w_down)
    mom2 = dict(ln1_g=v_ln1_g, w_in=v_w_in, qn_a=v_qn_a, kn_a=v_kn_a, rpb=v_rpb, qn_b=v_qn_b, kn_b=v_kn_b, sink=v_sink,
                on_a=v_on_a, on_b=v_on_b, w_out=v_w_out, ln2_g=v_ln2_g, w_up=v_w_up, conv_w=v_conv_w, conv_b=v_conv_b,
                w_down=v_w_down)
    order = ("ln1_g", "w_in", "qn_a", "kn_a", "rpb", "qn_b", "kn_b", "sink", "on_a", "on_b", "w_out", "ln2_g", "w_up",
             "conv_w", "conv_b", "w_down")

    depth, d = ln1_g.shape
    s = x.shape[1]
    ha = on_a.shape[1] // HEAD
    hq = on_b.shape[1] // HEAD
    pw = w_in.shape[2] * N_DEV
    hkv = (pw - 3 * ha * HEAD - hq * HEAD) // (2 * HEAD)
    f = w_down.shape[1] * N_DEV
    mix = (ha + hq) * HEAD
    cfg = (ha, hq, hkv)
    fs = conv_w.shape[2]
    dev = 4 * lax.axis_index("x") + 2 * lax.axis_index("y") + lax.axis_index("c")
    core = lax.axis_index("c").astype(jnp.int32).reshape(1)

    shard = {n: weights[n].astype(BF16) for n in ("w_in", "w_out", "w_up", "w_down")}

    def unshard(n, g):
        if n in ("w_in", "w_up"):
            return g.transpose(1, 0, 2).reshape(g.shape[1], N_DEV * g.shape[2])
        return g.reshape(N_DEV * g.shape[1], g.shape[2])

    full = {n: [None] * depth for n in shard}
    half_d = _tile(d, d // 2, 16)
    up0 = [shard["w_up"][0][:half_d], shard["w_up"][0][half_d:]]

    def travel(l, host):
        plan = {}
        if l == 0:
            plan = {"ln1": [("w_in", 0, None)], "proj": [("w_out", 0, None), ("w_up", 0, 0)], "na": [("w_up", 0, 1)],
                    "wa": [("w_down", 0, None)], "up": [("w_up", 1, None), ("w_in", 1, None)]}
        elif l + 1 < depth:
            plan = {"proj": [("w_in", l + 1, None)], "up": [("w_up", l + 1, None)]}
        if l + 1 < depth:
            plan.update({"out": [("w_out", l + 1, None)], "down": [("w_down", l + 1, None)]})
        return [key for key in plan.get(host, []) if key[1] < depth]

    arrived = {}

    def gather_of(keys):
        return _gather_comm([shard[n][k] if part is None else up0[part] for n, k, part in keys]) if keys else None

    def landed(keys, blocks):
        for (n, k, part), g in zip(keys, blocks):
            if part is None:
                full[n][k] = unshard(n, g)
            else:
                arrived[part] = g
                if len(arrived) == 2:
                    full[n][k] = unshard(n, jnp.concatenate([arrived[0], arrived[1]], axis=1))

    cw_rows = depth * 3
    cw_pad = jnp.pad(conv_w.reshape(cw_rows, fs), ((0, -cw_rows % 8), (0, 0)))
    g_cw = _allgather(cw_pad, "gather_conv_w")
    full_cw = g_cw[:, :cw_rows].reshape(N_DEV, depth, 3, fs).transpose(1, 2, 0, 3).reshape(depth, 3, 2 * f)

    inv = ROPE_THETA ** (-jnp.arange(0, HEAD, 2, dtype=F32) / HEAD)
    ang = positions.astype(F32)[:, None] * inv[None, :]
    cos = jnp.concatenate([jnp.cos(ang), jnp.cos(ang)], axis=-1)
    sin = jnp.concatenate([-jnp.sin(ang), jnp.sin(ang)], axis=-1)
    qk = jnp.arange(GRID_W * GRID_W)
    dc_of = (qk % GRID_W) - (qk // GRID_W) + (WIN_C - 1)
    onehot = (dc_of[:, None] == jnp.arange(LANES)[None, :]).astype(BF16)

    tiles_s = _tile(s, 512, 16)
    tiles_l = _tile(s, 1024, 16)

    xs = x.reshape(s, d)
    saved = []
    for l in range(depth):
        def hosting(call, host, name, *args, **kw):
            keys = travel(l, host)
            if not keys:
                return call(*args, name=name, **kw)
            out, *blocks = call(*args, name=f"{name}_g{len(keys)}", comm=gather_of(keys), **kw)
            landed(keys, blocks)
            return out

        def fwd_matmul(n, host, a_op, name, **kw):
            return hosting(lambda **k2: _matmul(a_op, full[n][l], dims="nn", out_dtype=F32, **k2), host, name, **kw)

        gains = jnp.zeros((8, HEAD), F32).at[0].set(qn_a[l]).at[1].set(kn_a[l]).at[2].set(qn_b[l]).at[3].set(kn_b[l])
        on_g = jnp.concatenate([on_a[l], on_b[l]]).reshape(1, mix)
        h = hosting(_rms_fwd, "ln1", "ln1_fwd", xs, ln1_g[l].reshape(1, d))
        proj = fwd_matmul("w_in", "proj", h, "proj_fwd", ti=tiles_l, tj=_tile(pw, 1536, LANES), tk=d)
        qa, ka, va, qb, kb, vb = _qkv_fwd(proj, gains, cos, sin, cfg, "qkv_fwd")
        tb = _na_bias(rpb[l].reshape(-1), ha, "na_bias")
        oa = hosting(_na_fwd, "na", "na_fwd", qa, ka, va, tb)
        ob = hosting(_wa_fwd, "wa", "wa_fwd", qb, kb, vb, sink[l])
        o_n = _onorm_fwd(oa, ob, on_g, "onorm_fwd")
        x1 = fwd_matmul("w_out", "out", o_n, "out_fwd", ti=tiles_l, tj=_tile(d, 1024, LANES), tk=mix, resid=xs)
        h2 = _rms_fwd(x1, ln2_g[l].reshape(1, d), "ln2_fwd")
        u = fwd_matmul("w_up", "up", h2, "up_fwd", ti=tiles_l, tj=_tile(2 * f, 1024, 2 * LANES), tk=d)
        a, gu = _gate_fwd(u, full_cw[l], conv_b[l].reshape(1, 2 * f), f, "gate_fwd")
        x2 = fwd_matmul("w_down", "down", a, "down_fwd", ti=tiles_s, tj=_tile(d, 512, LANES), tk=f, resid=x1)
        saved.append(dict(x=xs, h=h, proj=proj, gains=gains, on_g=on_g, qkv=(qa, ka, va, qb, kb, vb), tb=tb, oa=oa, ob=ob,
                          o_n=o_n, x1=x1, h2=h2, u=u, gu=gu, a=a))
        xs = x2

    dx, dx_b, loss_part = _loss_head(xs, loss_target.reshape(s, d), "loss_head")
    tile_c = _tile(s, 2048, 16)
    half_k = _tile(2 * f, f, fs)
    loss = lax.psum(loss_part[0, 0], ("x", "y", "c"))

    small_grads = [None] * depth
    big = {n: None for n in ("w_in", "w_out", "w_up", "w_down")}
    pending = None
    for l in reversed(range(depth)):
        sv = saved[l]
        qa, ka, va, qb, kb, vb = sv["qkv"]

        def update(n, layer, got):
            big[n] = _adamw(got, weights[n], mom1[n], mom2[n], "adamw_" + n, layer=layer, into=big[n])

        def carrying(n, name, **kw):
            if pending is None or n is None:
                return _matmul(name=name, **kw)
            out, got = _matmul(name=name + "_carry", comm=_scatter_comm([pending[n]]), **kw)
            update(n, l + 1, got)
            return out

        def grad_matmul(n, a_op, b_op, name, **kw):
            carried = n if n in ("w_up", "w_down") else None
            return carrying(carried, name, a=a_op, b=b_op, dims="tn", out_dtype=BF16, **kw)

        def own(blocks):
            return _scatter_comm([blocks]) if l == 0 else None

        gw_down = grad_matmul("w_down", sv["a"], dx_b, "down_bwd_w", ti=_tile(f, 1408, LANES), tj=_tile(d, 1024, LANES),
                              tk=tile_c, j_outer=False)
        da = carrying("w_in", "down_bwd_x", a=dx_b, b=full["w_down"][l], dims="nt", ti=tiles_s, tj=_tile(f, 2816, 2 * LANES),
                      tk=d, out_dtype=F32)
        du, dcw, dcb = _ffn_bwd(sv["gu"], sv["u"], da, full_cw[l], "ffn_bwd")
        gw_down = gw_down.reshape(N_DEV, f // N_DEV, d)
        dh2 = None
        for part in range(2 * f // half_k):
            comm = own(gw_down) if part == 0 else None
            dh2 = _matmul(du, full["w_up"][l], dims="nt", ti=tiles_s, tj=_tile(d, 1024, LANES), tk=half_k, out_dtype=F32,
                          name=f"up_bwd_x{part}" + "_own" * bool(comm), k_blocks=(part, 1), resid=dh2, halved="a", comm=comm)
            if comm:
                dh2, got = dh2
                update("w_down", 0, got)
        gw_up = grad_matmul("w_up", sv["h2"], du, "up_bwd_w", ti=_tile(d, 1024, LANES), tj=fs, tk=tile_c, dev_major=True,
                            halved="b")
        dx1, dx1_b, dln2 = _rms_bwd(sv["x1"], ln2_g[l].reshape(1, d), dh2, dx, "ln2_bwd")
        don = _matmul(dx1_b, full["w_out"][l], dims="nt", ti=tiles_l, tj=mix, tk=d, out_dtype=F32, name="out_bwd_x")
        gw_out = grad_matmul("w_out", sv["o_n"], dx1_b, "out_bwd_w", ti=_tile(mix, 1024, LANES), tj=_tile(d, 1024, LANES),
                             tk=tile_c, j_outer=False)
        gw_out = gw_out.reshape(N_DEV, mix // N_DEV, d)
        doa, dob, don_g = _onorm_bwd(sv["oa"], sv["ob"], sv["on_g"], don, "onorm_bwd")
        comm = _chip_comm([_pair_sums(gw_up, core, "rs0_w_up")]) if l == 0 else None
        dqa, dka, dva, dtb, *got = _na_bwd(qa, ka, va, sv["tb"], doa, "na_bwd" + "_own" * (l == 0), comm=comm)
        if got:
            update("w_up", 0, got[0])
        dqb, dkb, dvb, dsink, *got = _wa_bwd(qb, kb, vb, sink[l], dob, "wa_bwd" + "_own" * (l == 0), comm=own(gw_out))
        if got:
            update("w_out", 0, got[0])
        drpb = _rpb_grad(dtb, onehot, "rpb_grad")
        dproj, dgains = _qkv_bwd(sv["proj"], sv["gains"], cos, sin, (dqa, dka, dva, dqb, dkb, dvb), cfg, "qkv_bwd")
        dh = carrying("w_out", "proj_bwd_x", a=dproj, b=full["w_in"][l], dims="nt", ti=tiles_s, tj=_tile(d, 1024, LANES), tk=pw,
                      out_dtype=F32)
        gw_in = grad_matmul("w_in", sv["h"], dproj, "proj_bwd_w", ti=_tile(d, 1024, LANES), tj=_tile(pw, 1536, LANES), tk=tile_c)
        gw_in = gw_in.reshape(d, N_DEV, pw // N_DEV).transpose(1, 0, 2)
        comm = _chip_comm([_pair_sums(gw_in, core, "rs0_w_in")]) if l == 0 else None
        dx, dx_b, dln1, *got = _rms_bwd(sv["x"], ln1_g[l].reshape(1, d), dh, dx1, "ln1_bwd" + "_own" * (l == 0), comm=comm)
        if got:
            update("w_in", 0, got[0])

        small_grads[l] = dict(
            ln1_g=dln1[0], qn_a=dgains[0], kn_a=dgains[1], rpb=drpb, qn_b=dgains[2], kn_b=dgains[3], sink=dsink[:, 0, 0],
            on_a=don_g[0, :ha * HEAD], on_b=don_g[0, ha * HEAD:], ln2_g=dln2[0],
            conv_b=dcb[:, 0, :].reshape(2 * f), conv_w=dcw[:, :, 0, :].transpose(1, 0, 2).reshape(3, 2 * f))

        pending = dict(w_in=gw_in, w_out=gw_out, w_up=gw_up, w_down=gw_down)

    grads_l = [small_grads[l][n] for l in range(depth) for n in SMALL]
    gathered = _allgather(_pack(grads_l), "gather_small")
    zeros_cw = jnp.zeros((3, 2 * f), F32)

    def small_state(src):
        return _pack([zeros_cw if n == "conv_w" else src[n][l] for l in range(depth) for n in SMALL])

    sm = _adamw(gathered, small_state(weights), small_state(mom1), small_state(mom2), "adamw_small")
    sm = [_unpack(t, grads_l) for t in sm]
    small_out = {n: [jnp.stack([sm[k][l * len(SMALL) + i] for l in range(depth)]) for k in range(4)]
                 for i, n in enumerate(SMALL)}
    cw_grad = lax.dynamic_slice_in_dim(small_out["conv_w"][0], dev * fs, fs, axis=2)
    cw_rows_pad = cw_rows + (-cw_rows % 8)

    def rows8(a):
        return jnp.pad(a.reshape(cw_rows, fs), ((0, cw_rows_pad - cw_rows), (0, 0)))

    cw_res = _adamw(rows8(cw_grad)[None], rows8(conv_w), rows8(m_conv_w), rows8(v_conv_w), "adamw_conv_w")
    small_out["conv_w"] = [t[:cw_rows].reshape(depth, 3, fs) for t in cw_res]

    results = {n: (big[n] if n in big else small_out[n]) for n in order}
    grad_x = dx.reshape(1, s, d)
    return (loss, grad_x, *[results[n][0] for n in order], *[results[n][1] for n in order],
            *[results[n][2] for n in order], *[results[n][3] for n in order])
```

```python
import math

import jax
import jax.numpy as jnp
from jax import lax
from jax.experimental import pallas as pl
from jax.experimental.pallas import tpu as pltpu

F32 = jnp.float32
BF16 = jnp.bfloat16

HEAD = 128
GRID_W = 64
WIN_R = 8
WIN_C = 16
BAND = 128
ROPE_THETA = 10000.0
EPS = 1e-6
NEG = -1e30
SCALE = 1.0 / math.sqrt(HEAD)

ADAM_LR = 0.001
ADAM_B1 = 0.9
ADAM_B2 = 0.999
ADAM_EPS = 1e-08
ADAM_WD = 0.01
ADAM_STEP = 10

N_DEV = 8
LANES = 128
VMEM_LIMIT_BYTES = 56 * 2 ** 20
MESH = pl.DeviceIdType.MESH
ANY = pl.BlockSpec(memory_space=pl.ANY)
SMEM = pl.BlockSpec(memory_space=pltpu.SMEM)


def _params():
    return pltpu.CompilerParams(vmem_limit_bytes=VMEM_LIMIT_BYTES)


def _tile(n, pref, align):
    t = min(n, pref)
    t -= t % align
    while t > 0 and n % t:
        t -= align
    return t if t > 0 else n


def _place():
    x, y, c = lax.axis_index("x"), lax.axis_index("y"), lax.axis_index("c")
    chips = [(1 - x, y), (x, 1 - y), (1 - x, 1 - y)]
    return x, y, c, chips


COPIES_PER_ARRAY = N_DEV - 1


def _comm_scratch(n_arrays):
    return [pltpu.SemaphoreType.DMA((COPIES_PER_ARRAY * n_arrays,)), pltpu.SemaphoreType.DMA((COPIES_PER_ARRAY * n_arrays,)),
            pltpu.SemaphoreType.DMA((n_arrays,))]


def _gather_plan(src_refs, out_refs, send_sems, recv_sems, local_sems):
    x, y, c, chips = _place()
    me, sibling = (x, y, c), (x, y, 1 - c)

    def slot(a, px, py, pc):
        return out_refs[a].at[4 * px + 2 * py + pc]

    def copy(a, k, block, to, src=None):
        return pltpu.make_async_remote_copy(
            src_ref=slot(a, *block) if src is None else src, dst_ref=slot(a, *block),
            send_sem=send_sems.at[COPIES_PER_ARRAY * a + k], recv_sem=recv_sems.at[COPIES_PER_ARRAY * a + k],
            device_id=to, device_id_type=MESH)

    def mine(a):
        return pltpu.make_async_copy(src_refs[a], slot(a, *me), local_sems.at[a])

    def first(a):
        return [copy(a, 0, me, sibling, src=src_refs[a])] + [
            copy(a, 1 + j, me, (*chip, c), src=src_refs[a]) for j, chip in enumerate(chips)]

    def passed(a):
        return [copy(a, 4 + j, (*chip, c), sibling) for j, chip in enumerate(chips)]

    def start():
        for a in range(len(src_refs)):
            mine(a).start()
            for cp in first(a):
                cp.start()

    def middle():
        for a in range(len(src_refs)):
            forwards = passed(a)
            for j, chip in enumerate(chips):
                copy(a, 1 + j, (*chip, c), me).wait_recv()
                forwards[j].start()

    def finish():
        for a in range(len(src_refs)):
            copy(a, 0, sibling, me).wait_recv()
            for j, chip in enumerate(chips):
                copy(a, 4 + j, (*chip, 1 - c), me).wait_recv()
            for cp in first(a) + passed(a):
                cp.wait_send()
            mine(a).wait()

    return start, middle, finish


def _scatter_plan(src_refs, out_refs, send_sems, recv_sems, local_sems):
    x, y, c, _ = _place()
    me = 4 * x + 2 * y + c

    def peer(k):
        px = 1 - x if k & 4 else x
        py = 1 - y if k & 2 else y
        pc = 1 - c if k & 1 else c
        return (px, py, pc), 4 * px + 2 * py + pc

    def copy(a, k, outgoing):
        to, idx = peer(k)
        return pltpu.make_async_remote_copy(
            src_ref=src_refs[a].at[idx], dst_ref=out_refs[a].at[me if outgoing else idx],
            send_sem=send_sems.at[COPIES_PER_ARRAY * a + k - 1], recv_sem=recv_sems.at[COPIES_PER_ARRAY * a + k - 1],
            device_id=to, device_id_type=MESH)

    def mine(a):
        return pltpu.make_async_copy(src_refs[a].at[me], out_refs[a].at[me], local_sems.at[a])

    def start():
        for a in range(len(src_refs)):
            mine(a).start()
            for k in range(1, N_DEV):
                copy(a, k, True).start()

    def finish():
        for a in range(len(src_refs)):
            for k in range(1, N_DEV):
                copy(a, k, False).wait_recv()
            for k in range(1, N_DEV):
                copy(a, k, True).wait_send()
            mine(a).wait()

    return start, lambda: None, finish


def _allgather(v, name):
    def body(v_ref, out_ref, send_sems, recv_sems, local_sems):
        start, middle, finish = _gather_plan([v_ref], [out_ref], send_sems, recv_sems, local_sems)
        start()
        middle()
        finish()

    return pl.pallas_call(
        body, name=name,
        out_shape=jax.ShapeDtypeStruct((N_DEV,) + v.shape, v.dtype),
        in_specs=[ANY], out_specs=ANY, scratch_shapes=_comm_scratch(1),
    )(v)


def _sibling_exchange(g, name):
    def body(g_ref, out_ref, send_sems, recv_sems):
        x, y, c, _ = _place()
        sibling = (x, y, 1 - c)
        copies = []
        for j in range(4):
            copies.append(pltpu.make_async_remote_copy(
                src_ref=g_ref.at[2 * j + (1 - c)], dst_ref=out_ref.at[j],
                send_sem=send_sems.at[j], recv_sem=recv_sems.at[j], device_id=sibling, device_id_type=MESH))
        for cp in copies:
            cp.start()
        for cp in copies:
            cp.wait_recv()
        for cp in copies:
            cp.wait_send()

    return pl.pallas_call(
        body, name=name,
        out_shape=jax.ShapeDtypeStruct((4,) + g.shape[1:], g.dtype),
        in_specs=[ANY], out_specs=ANY,
        scratch_shapes=[pltpu.SemaphoreType.DMA((4,)), pltpu.SemaphoreType.DMA((4,))],
    )(g)


def _pair_sum(g, got, core, name):
    _, r, c = g.shape
    tr = _tile(r, max(16, (1 << 20) // c), 16)

    def body(core_ref, g_ref, got_ref, o_ref):
        del core_ref
        o_ref[...] = (g_ref[...].astype(F32) + got_ref[...].astype(F32)).astype(o_ref.dtype)

    return pl.pallas_call(
        body, name=name,
        out_shape=jax.ShapeDtypeStruct((4, r, c), g.dtype),
        grid_spec=pltpu.PrefetchScalarGridSpec(
            num_scalar_prefetch=1, grid=(4, r // tr),
            in_specs=[pl.BlockSpec((None, tr, c), lambda j, i, core_ref: (2 * j + core_ref[0], i, 0)),
                      pl.BlockSpec((None, tr, c), lambda j, i, core_ref: (j, i, 0))],
            out_specs=pl.BlockSpec((None, tr, c), lambda j, i, core_ref: (j, i, 0))),
        compiler_params=_params(),
    )(core, g, got)


def _chip_plan(src_refs, out_refs, send_sems, recv_sems, local_sems):
    x, y, c, chips = _place()

    def copies(a):
        return [pltpu.make_async_remote_copy(
            src_ref=src_refs[a].at[2 * px + py], dst_ref=out_refs[a].at[k],
            send_sem=send_sems.at[COPIES_PER_ARRAY * a + k], recv_sem=recv_sems.at[COPIES_PER_ARRAY * a + k],
            device_id=(px, py, c), device_id_type=MESH) for k, (px, py) in enumerate(chips)]

    def mine(a):
        return pltpu.make_async_copy(src_refs[a].at[2 * x + y], out_refs[a].at[3], local_sems.at[a])

    def start():
        for a in range(len(src_refs)):
            mine(a).start()
            for cp in copies(a):
                cp.start()

    def finish():
        for a in range(len(src_refs)):
            for cp in copies(a):
                cp.wait_recv()
            for cp in copies(a):
                cp.wait_send()
            mine(a).wait()

    return start, lambda: None, finish


def _chip_comm(blocks):
    return _chip_plan, blocks, [jax.ShapeDtypeStruct(p.shape, p.dtype) for p in blocks]


def _pair_sums(g, core, name):
    got = _sibling_exchange(g, name + "_d2d")
    return _pair_sum(g, got, core, name + "_pair")


def _adamw(parts, w, m, v, name, layer=None, into=None):
    n_parts, r, c = parts.shape
    tr = _tile(r, max(8, (1 << 19) // c), 16 if parts.dtype == BF16 else 8)
    c1 = 1.0 - ADAM_B1 ** ADAM_STEP
    c2 = 1.0 - ADAM_B2 ** ADAM_STEP
    n_into = 0 if into is None else len(into)

    def body(p_ref, w_ref, m_ref, v_ref, *rest):
        g_out, d_out, m_out, v_out = rest[n_into:]
        g = p_ref[0].astype(F32)
        for k in range(1, n_parts):
            g = g + p_ref[k].astype(F32)
        m2 = ADAM_B1 * m_ref[...] + (1.0 - ADAM_B1) * g
        v2 = ADAM_B2 * v_ref[...] + (1.0 - ADAM_B2) * (g * g)
        g_out[...] = g
        m_out[...] = m2
        v_out[...] = v2
        d_out[...] = -ADAM_LR * ((m2 / c1) / (jnp.sqrt(v2 / c2) + ADAM_EPS) + ADAM_WD * w_ref[...])

    if layer is None:
        blk = pl.BlockSpec((tr, c), lambda i: (i, 0))
        out = jax.ShapeDtypeStruct((r, c), F32)
    else:
        blk = pl.BlockSpec((None, tr, c), lambda i: (layer, i, 0))
        out = jax.ShapeDtypeStruct(w.shape, F32)
    return pl.pallas_call(
        body, name=name, grid=(r // tr,),
        in_specs=[pl.BlockSpec((n_parts, tr, c), lambda i: (0, i, 0)), blk, blk, blk] + [ANY] * n_into,
        out_specs=[blk, blk, blk, blk], out_shape=[out, out, out, out],
        input_output_aliases={4 + k: k for k in range(n_into)},
        compiler_params=_params(),
    )(parts, w, m, v, *(into or ()))


def _exchange_steps(grid):
    flat, total = pl.program_id(0), grid[0]
    for axis in range(1, len(grid)):
        flat, total = flat * grid[axis] + pl.program_id(axis), total * grid[axis]
    return flat == 0, flat == max(3 * total // 4, min(1, total - 1)), flat == total - 1


def _run_exchange(hooks, grid, compute):
    start, middle, finish = hooks
    first, later, last = _exchange_steps(grid)
    pl.when(first)(start)
    compute()
    pl.when(later)(middle)
    pl.when(last)(finish)


def _host_exchange(body, comm, grid, n_in, n_out):
    plan, comm_in, comm_out = comm
    n = len(comm_in)

    def wrapped(*refs):
        ins, cin = refs[:n_in], refs[n_in:n_in + n]
        outs, cout = refs[n_in + n:n_in + n + n_out], refs[n_in + n + n_out:n_in + 2 * n + n_out]
        rest = refs[n_in + 2 * n + n_out:]
        _run_exchange(plan(cin, cout, *rest[len(rest) - 3:]), grid, lambda: body(*ins, *outs, *rest[:len(rest) - 3]))

    return wrapped, list(comm_in), [ANY] * n, list(comm_out), _comm_scratch(n)


def _gather_comm(shards):
    return _gather_plan, shards, [jax.ShapeDtypeStruct((N_DEV,) + v.shape, v.dtype) for v in shards]


def _scatter_comm(blocks):
    return _scatter_plan, blocks, [jax.ShapeDtypeStruct(g.shape, g.dtype) for g in blocks]


def _matmul(a, b, *, dims, ti, tj, tk, out_dtype, name, j_outer=True, resid=None, dev_major=False, comm=None,
            k_blocks=None, halved=None):
    a_shape = (a.shape[1], 2 * a.shape[2]) if halved == "a" else a.shape
    b_shape = (b.shape[1], 2 * b.shape[2]) if halved == "b" else b.shape
    if dims == "nn":
        (I, K), (K2, J) = a_shape, b_shape
    elif dims == "nt":
        (I, K), (J, K2) = a_shape, b_shape
    else:
        (K, I), (K2, J) = a_shape, b_shape
    assert K == K2 and I % ti == 0 and J % tj == 0 and K % tk == 0, (name, a.shape, b.shape, ti, tj, tk)
    assert halved is None or (halved, dims) in (("a", "nt"), ("b", "tn")), (name, halved, dims)
    k0, nk = k_blocks if k_blocks is not None else (0, K // tk)
    ni, nj = I // ti, J // tj

    def ij(g0, g1):
        return (g1, g0) if j_outer else (g0, g1)

    if dims == "nn":
        a_spec = pl.BlockSpec((ti, tk), lambda g0, g1, k: (ij(g0, g1)[0], k0 + k))
        b_spec = pl.BlockSpec((tk, tj), lambda g0, g1, k: (k0 + k, ij(g0, g1)[1]))
        dn = (((1,), (0,)), ((), ()))
    elif dims == "nt":
        a_spec = pl.BlockSpec((ti, tk), lambda g0, g1, k: (ij(g0, g1)[0], k0 + k))
        if halved == "a":
            per = K // 2 // tk
            a_spec = pl.BlockSpec((None, ti, tk), lambda g0, g1, k: ((k0 + k) // per, ij(g0, g1)[0], (k0 + k) % per))
        b_spec = pl.BlockSpec((tj, tk), lambda g0, g1, k: (ij(g0, g1)[1], k0 + k))
        dn = (((1,), (1,)), ((), ()))
    else:
        a_spec = pl.BlockSpec((tk, ti), lambda g0, g1, k: (k0 + k, ij(g0, g1)[0]))
        b_spec = pl.BlockSpec((tk, tj), lambda g0, g1, k: (k0 + k, ij(g0, g1)[1]))
        if halved == "b":
            per = J // 2 // tj
            b_spec = pl.BlockSpec((None, tk, tj), lambda g0, g1, k: (ij(g0, g1)[1] // per, k0 + k, ij(g0, g1)[1] % per))
        dn = (((0,), (0,)), ((), ()))
    in_specs = [a_spec, b_spec]
    operands = [a, b]
    if resid is not None:
        in_specs.append(pl.BlockSpec((ti, tj), lambda g0, g1, k: ij(g0, g1)))
        operands.append(resid)
    if dev_major:
        out_spec = pl.BlockSpec((None, ti, tj), lambda g0, g1, k: (ij(g0, g1)[1], ij(g0, g1)[0], 0))
        out_shape = jax.ShapeDtypeStruct((nj, I, tj), out_dtype)
    else:
        out_spec = pl.BlockSpec((ti, tj), lambda g0, g1, k: ij(g0, g1))
        out_shape = jax.ShapeDtypeStruct((I, J), out_dtype)

    grid = (nj, ni, nk) if j_outer else (ni, nj, nk)
    n_in = len(operands)
    n_comm = 0
    out_specs, out_shapes = [out_spec], [out_shape]
    scratch = [pltpu.VMEM((ti, tj), F32)] if nk > 1 else []
    if comm is not None:
        plan, comm_in, comm_out = comm
        n_comm = len(comm_in)
        operands += list(comm_in)
        in_specs += [ANY] * n_comm
        out_specs += [ANY] * n_comm
        out_shapes += list(comm_out)
        scratch += _comm_scratch(n_comm)

    def product(*refs):
        a_ref, b_ref = refs[0], refs[1]
        r_ref = refs[2] if resid is not None else None
        o_ref = refs[n_in + n_comm]
        part = lax.dot_general(a_ref[...].astype(BF16), b_ref[...].astype(BF16), dn, preferred_element_type=F32)

        def finish(acc):
            if r_ref is not None:
                acc = acc + r_ref[...]
            o_ref[...] = acc.astype(o_ref.dtype)

        if nk == 1:
            finish(part)
        else:
            acc_ref = refs[n_in + 2 * n_comm + 1]
            k = pl.program_id(2)

            @pl.when(k == 0)
            def _():
                acc_ref[...] = part

            @pl.when(k > 0)
            def _():
                acc_ref[...] += part

            @pl.when(k == nk - 1)
            def _():
                finish(acc_ref[...])

    def body(*refs):
        if comm is None:
            product(*refs)
        else:
            hooks = plan(refs[n_in:n_in + n_comm], refs[n_in + n_comm + 1:n_in + 2 * n_comm + 1], *refs[-3:])
            _run_exchange(hooks, grid, lambda: product(*refs))

    res = pl.pallas_call(
        body, name=name, grid=grid,
        in_specs=in_specs, out_specs=out_specs, out_shape=out_shapes,
        scratch_shapes=scratch, compiler_params=_params(),
    )(*operands)
    return res[0] if comm is None else res


ROW_TILE = 512


def _rms_fwd(x, g, name, comm=None):
    s, d = x.shape
    ts = _tile(s, ROW_TILE, 16)

    def body(x_ref, g_ref, h_ref):
        xv = x_ref[...]
        r = lax.rsqrt(jnp.mean(xv * xv, axis=-1, keepdims=True) + EPS)
        h_ref[...] = (xv * r * g_ref[...]).astype(BF16)

    return _call(
        body, comm, name=name, grid=(s // ts,), operands=[x, g],
        in_specs=[pl.BlockSpec((ts, d), lambda i: (i, 0)), pl.BlockSpec((1, d), lambda i: (0, 0))],
        out_specs=[pl.BlockSpec((ts, d), lambda i: (i, 0))],
        out_shape=[jax.ShapeDtypeStruct((s, d), BF16)], scratch_shapes=[])


def _rms_bwd(x, g, dh, dres, name, comm=None):
    s, d = x.shape
    ts = _tile(s, ROW_TILE, 16)

    def body(x_ref, g_ref, dh_ref, dres_ref, dx_ref, dxb_ref, dg_ref):
        xv = x_ref[...]
        r = lax.rsqrt(jnp.mean(xv * xv, axis=-1, keepdims=True) + EPS)
        y = xv * r
        dhv = dh_ref[...]
        gd = dhv * g_ref[...]
        dxv = dres_ref[...] + r * (gd - y * jnp.mean(gd * y, axis=-1, keepdims=True))
        dx_ref[...] = dxv
        dxb_ref[...] = dxv.astype(BF16)

        @pl.when(pl.program_id(0) == 0)
        def _():
            dg_ref[...] = jnp.zeros_like(dg_ref)

        dg_ref[0:1, :] += jnp.sum(dhv * y, axis=0, keepdims=True)

    blk = pl.BlockSpec((ts, d), lambda i: (i, 0))
    return _call(
        body, comm, name=name, grid=(s // ts,), operands=[x, g, dh, dres],
        in_specs=[blk, pl.BlockSpec((1, d), lambda i: (0, 0)), blk, blk],
        out_specs=[blk, blk, pl.BlockSpec((8, d), lambda i: (0, 0))],
        out_shape=[jax.ShapeDtypeStruct((s, d), F32), jax.ShapeDtypeStruct((s, d), BF16), jax.ShapeDtypeStruct((8, d), F32)],
        scratch_shapes=[])


def _head_norm(t, gain):
    r = lax.rsqrt(jnp.mean(t * t, axis=-1, keepdims=True) + EPS)
    return t * r * gain


def _head_norm_bwd(t, gain, dn):
    r = lax.rsqrt(jnp.mean(t * t, axis=-1, keepdims=True) + EPS)
    y = t * r
    gd = dn * gain
    dt = r * (gd - y * jnp.mean(gd * y, axis=-1, keepdims=True))
    return dt, jnp.sum(dn * y, axis=0, keepdims=True)


def _rope(n, cos, sin):
    return n * cos + pltpu.roll(n, HEAD // 2, axis=1) * sin


def _rope_bwd(do, cos, sin):
    return do * cos + pltpu.roll(do * sin, HEAD // 2, axis=1)


def _qkv_fwd(proj, gains, cos, sin, cfg, name):
    s, pw = proj.shape
    ha, hq, hkv = cfg
    ts = _tile(s, ROW_TILE, 16)

    def body(p_ref, gn_ref, cos_ref, sin_ref, qa_ref, ka_ref, va_ref, qb_ref, kb_ref, vb_ref):
        cosv, sinv = cos_ref[...], sin_ref[...]
        col = 0
        for out_ref, nh, gi, rot in ((qa_ref, ha, 0, False), (ka_ref, ha, 1, False), (va_ref, ha, None, False),
                                     (qb_ref, hq, 2, True), (kb_ref, hkv, 3, True), (vb_ref, hkv, None, False)):
            for h in range(nh):
                t = p_ref[:, col * HEAD:(col + 1) * HEAD]
                if gi is not None:
                    t = _head_norm(t, gn_ref[gi:gi + 1, :])
                if rot:
                    t = _rope(t, cosv, sinv)
                out_ref[h] = t.astype(BF16)
                col += 1

    def hm(nh):
        return pl.BlockSpec((nh, ts, HEAD), lambda i: (0, i, 0)), jax.ShapeDtypeStruct((nh, s, HEAD), BF16)

    specs, shapes = zip(hm(ha), hm(ha), hm(ha), hm(hq), hm(hkv), hm(hkv))
    tok = pl.BlockSpec((ts, HEAD), lambda i: (i, 0))
    return pl.pallas_call(
        body, name=name, grid=(s // ts,),
        in_specs=[pl.BlockSpec((ts, pw), lambda i: (i, 0)), pl.BlockSpec((8, HEAD), lambda i: (0, 0)), tok, tok],
        out_specs=list(specs), out_shape=list(shapes), compiler_params=_params(),
    )(proj, gains, cos, sin)


def _qkv_bwd(proj, gains, cos, sin, grads, cfg, name):
    s, pw = proj.shape
    ha, hq, hkv = cfg
    ts = _tile(s, 256, 16)

    def body(p_ref, gn_ref, cos_ref, sin_ref, dqa, dka, dva, dqb, dkb, dvb, dp_ref, dgn_ref):
        cosv, sinv = cos_ref[...], sin_ref[...]

        @pl.when(pl.program_id(0) == 0)
        def _():
            dgn_ref[...] = jnp.zeros_like(dgn_ref)

        col = 0
        for d_ref, nh, gi, rot in ((dqa, ha, 0, False), (dka, ha, 1, False), (dva, ha, None, False),
                                   (dqb, hq, 2, True), (dkb, hkv, 3, True), (dvb, hkv, None, False)):
            dgain = jnp.zeros((1, HEAD), F32)
            for h in range(nh):
                dt = d_ref[h]
                if rot:
                    dt = _rope_bwd(dt, cosv, sinv)
                if gi is not None:
                    dt, dg = _head_norm_bwd(p_ref[:, col * HEAD:(col + 1) * HEAD], gn_ref[gi:gi + 1, :], dt)
                    dgain = dgain + dg
                dp_ref[:, col * HEAD:(col + 1) * HEAD] = dt.astype(BF16)
                col += 1
            if gi is not None:
                dgn_ref[gi:gi + 1, :] += dgain

    def hm(nh):
        return pl.BlockSpec((nh, ts, HEAD), lambda i: (0, i, 0))

    tok = pl.BlockSpec((ts, HEAD), lambda i: (i, 0))
    small = pl.BlockSpec((8, HEAD), lambda i: (0, 0))
    return pl.pallas_call(
        body, name=name, grid=(s // ts,),
        in_specs=[pl.BlockSpec((ts, pw), lambda i: (i, 0)), small, tok, tok,
                  hm(ha), hm(ha), hm(ha), hm(hq), hm(hkv), hm(hkv)],
        out_specs=[pl.BlockSpec((ts, pw), lambda i: (i, 0)), small],
        out_shape=[jax.ShapeDtypeStruct((s, pw), BF16), jax.ShapeDtypeStruct((8, HEAD), F32)],
        compiler_params=_params(),
    )(proj, gains, cos, sin, *grads)


NA_QROWS = 32
NA_KEYS = WIN_R * GRID_W
N_DR = 2 * WIN_R - 1
N_DC = 2 * WIN_C - 1


def _na_bias(rpb_flat, n_heads, name):
    def body(rpb_ref, tb_ref):
        h = pl.program_id(0)
        qi = lax.broadcasted_iota(jnp.int32, (GRID_W, LANES), 0)
        lane = lax.broadcasted_iota(jnp.int32, (GRID_W, LANES), 1)
        kk = lane & (GRID_W - 1)
        upper = lane >= GRID_W
        dcm = kk - qi + (WIN_C - 1)
        cs = jnp.clip(qi - WIN_C // 2, 0, GRID_W - WIN_C)
        valid = (kk >= cs) & (kk < cs + WIN_C)
        base = h * (N_DR * N_DC)
        for dra in range(N_DR - 1):
            def step(j, acc, dra=dra):
                va = rpb_ref[base + dra * N_DC + j]
                vb = rpb_ref[base + (dra + 1) * N_DC + j]
                return jnp.where(dcm == j, jnp.where(upper, vb, va), acc)

            pair = lax.fori_loop(0, N_DC, step, jnp.zeros((GRID_W, LANES), F32))
            pair = jnp.where(valid, pair, NEG)
            for dr0 in range(WIN_R):
                wp, odd = divmod(dra - dr0, 2)
                if odd == 0 and 0 <= wp < WIN_R // 2:
                    tb_ref[0, dr0, :, wp * LANES:(wp + 1) * LANES] = pair

    return pl.pallas_call(
        body, name=name, grid=(n_heads,),
        in_specs=[SMEM],
        out_specs=pl.BlockSpec((1, WIN_R, GRID_W, NA_KEYS), lambda h: (h, 0, 0, 0)),
        out_shape=jax.ShapeDtypeStruct((n_heads, WIN_R, GRID_W, NA_KEYS), F32),
        compiler_params=_params(),
    )(rpb_flat)


def _na_row(b, i, nrows, qrows):
    r = b * qrows + i
    rs = jnp.clip(r - WIN_R // 2, 0, nrows - WIN_R)
    return pl.ds(pl.multiple_of(rs * GRID_W, GRID_W), NA_KEYS), rs - r + (WIN_R - 1)


def _softmax(s):
    e = jnp.exp(s - jnp.max(s, axis=-1, keepdims=True))
    return e * (1.0 / jnp.sum(e, axis=-1, keepdims=True))


_NT = (((1,), (1,)), ((), ()))
_NN = (((1,), (0,)), ((), ()))
_TN = (((0,), (0,)), ((), ()))


def _dot(a, b, dn):
    return lax.dot_general(a, b, dn, preferred_element_type=F32)


def _call(body, comm, *, name, grid, operands, in_specs, out_specs, out_shape, scratch_shapes):
    if comm is not None:
        body, more_operands, more_specs, more_shapes, sems = _host_exchange(body, comm, grid, len(operands), len(out_shape))
        operands = operands + more_operands
        in_specs = in_specs + more_specs
        out_specs = out_specs + more_specs
        out_shape = out_shape + more_shapes
        scratch_shapes = scratch_shapes + sems
    res = pl.pallas_call(body, name=name, grid=grid, in_specs=in_specs, out_specs=out_specs, out_shape=out_shape,
                         scratch_shapes=scratch_shapes, compiler_params=_params())(*operands)
    return res[0] if len(res) == 1 else res


def _na_fwd(q, k, v, tb, name, comm=None):
    nh, s, _ = q.shape
    nrows = s // GRID_W
    qrows = _tile(nrows, NA_QROWS, WIN_R)
    tq = qrows * GRID_W

    def body(q_ref, k_ref, v_ref, tb_ref, o_ref, s_scr, p_scr):
        b = pl.program_id(1)
        rows = [slice(i * GRID_W, (i + 1) * GRID_W) for i in range(qrows)]
        at = [_na_row(b, i, nrows, qrows) for i in range(qrows)]
        for i, (keys, dr0) in enumerate(at):
            s_scr[i] = _dot(q_ref[rows[i], :], k_ref[keys, :], _NT) * SCALE + tb_ref[0, dr0]
        for i in range(qrows):
            p_scr[i] = _softmax(s_scr[i]).astype(BF16)
        for i, (keys, _) in enumerate(at):
            o_ref[rows[i], :] = _dot(p_scr[i], v_ref[keys, :], _NN)

    qspec = pl.BlockSpec((None, tq, HEAD), lambda h, b: (h, b, 0))
    full = pl.BlockSpec((None, s, HEAD), lambda h, b: (h, 0, 0))
    return _call(
        body, comm, name=name, grid=(nh, nrows // qrows), operands=[q, k, v, tb],
        in_specs=[qspec, full, full, pl.BlockSpec((1, WIN_R, GRID_W, NA_KEYS), lambda h, b: (h, 0, 0, 0))],
        out_specs=[qspec], out_shape=[jax.ShapeDtypeStruct((nh, s, HEAD), F32)],
        scratch_shapes=[pltpu.VMEM((qrows, GRID_W, NA_KEYS), F32), pltpu.VMEM((qrows, GRID_W, NA_KEYS), BF16)])


def _na_bwd(q, k, v, tb, do, name, comm=None):
    nh, s, _ = q.shape
    nrows = s // GRID_W
    qrows = _tile(nrows, NA_QROWS, WIN_R)
    tq = qrows * GRID_W

    def body(q_ref, do_ref, k_ref, v_ref, tb_ref, dq_ref, dk_ref, dv_ref, dtb_ref, s_scr, dp_scr, p_scr, ds_scr):
        b = pl.program_id(1)

        @pl.when(b == 0)
        def _():
            dk_ref[...] = jnp.zeros_like(dk_ref)
            dv_ref[...] = jnp.zeros_like(dv_ref)
            dtb_ref[...] = jnp.zeros_like(dtb_ref)

        rows = [slice(i * GRID_W, (i + 1) * GRID_W) for i in range(qrows)]
        at = [_na_row(b, i, nrows, qrows) for i in range(qrows)]
        for i, (keys, dr0) in enumerate(at):
            s_scr[i] = _dot(q_ref[rows[i], :], k_ref[keys, :], _NT) * SCALE + tb_ref[0, dr0]
            dp_scr[i] = _dot(do_ref[rows[i], :], v_ref[keys, :], _NT)
        for i in range(qrows):
            p = _softmax(s_scr[i])
            dp = dp_scr[i]
            ds = p * (dp - jnp.sum(p * dp, axis=-1, keepdims=True))
            p_scr[i] = p.astype(BF16)
            s_scr[i] = ds
            ds_scr[i] = (ds * SCALE).astype(BF16)
        for i, (keys, _) in enumerate(at):
            dq_ref[rows[i], :] = _dot(ds_scr[i], k_ref[keys, :], _NN)
        for i, (keys, dr0) in enumerate(at):
            dv_ref[keys, :] += _dot(p_scr[i], do_ref[rows[i], :], _TN)
            dk_ref[keys, :] += _dot(ds_scr[i], q_ref[rows[i], :], _TN)
            dtb_ref[0, dr0] += s_scr[i]

    qspec = pl.BlockSpec((None, tq, HEAD), lambda h, b: (h, b, 0))
    full = pl.BlockSpec((None, s, HEAD), lambda h, b: (h, 0, 0))
    tbs = pl.BlockSpec((1, WIN_R, GRID_W, NA_KEYS), lambda h, b: (h, 0, 0, 0))
    hm = jax.ShapeDtypeStruct((nh, s, HEAD), F32)
    tile = (qrows, GRID_W, NA_KEYS)
    return _call(
        body, comm, name=name, grid=(nh, nrows // qrows), operands=[q, do, k, v, tb],
        in_specs=[qspec, qspec, full, full, tbs],
        out_specs=[qspec, full, full, tbs],
        out_shape=[hm, hm, hm, jax.ShapeDtypeStruct((nh, WIN_R, GRID_W, NA_KEYS), F32)],
        scratch_shapes=[pltpu.VMEM(tile, F32), pltpu.VMEM(tile, F32), pltpu.VMEM(tile, BF16), pltpu.VMEM(tile, BF16)])


def _rpb_fold(y, n_heads, name):
    def body(y_ref, o_ref):
        for h in range(n_heads):
            for dr in range(2 * WIN_R):
                acc = jnp.zeros((1, LANES), F32)
                for dr0 in range(WIN_R):
                    w = dr - dr0
                    if 0 <= w < WIN_R:
                        acc = acc + y_ref[h, dr0, w:w + 1, :]
                o_ref[h, dr:dr + 1, :] = acc

    return pl.pallas_call(
        body, name=name, out_shape=jax.ShapeDtypeStruct((n_heads, 2 * WIN_R, LANES), F32),
    )(y)


def _rpb_grad(dtb, onehot, name):
    nh = dtb.shape[0]
    rows = dtb.reshape(nh, WIN_R, GRID_W, WIN_R, GRID_W).transpose(0, 1, 3, 2, 4).reshape(nh * WIN_R * WIN_R, GRID_W * GRID_W)
    y = _matmul(rows, onehot, dims="nn", ti=rows.shape[0], tj=LANES, tk=GRID_W * GRID_W, out_dtype=F32, name=name + "_dc")
    folded = _rpb_fold(y.reshape(nh, WIN_R, WIN_R, LANES), nh, name + "_dr")
    return folded[:, :N_DR, :N_DC]


WA_WIN_TOK = 3 * BAND
WA_QBLOCKS = 8


def _wa_scores(q, kwin, t0, j, sink_ref, head0, grp):
    rows = grp * BAND
    s = _dot(q, kwin, _NT) * SCALE
    row = lax.broadcasted_iota(jnp.int32, (rows, WA_WIN_TOK), 0)
    qpos = j * BAND + (row & (BAND - 1))
    kpos = t0 + lax.broadcasted_iota(jnp.int32, (rows, WA_WIN_TOK), 1)
    s = jnp.where(jnp.abs(kpos - qpos) <= BAND, s, NEG)
    head = lax.broadcasted_iota(jnp.int32, (rows, 1), 0) // BAND
    sink = jnp.zeros((rows, 1), F32) + sink_ref[head0]
    for g in range(1, grp):
        sink = jnp.where(head == g, sink_ref[head0 + g], sink)
    m = jnp.maximum(jnp.max(s, axis=-1, keepdims=True), sink)
    e = jnp.exp(s - m)
    es = jnp.exp(sink - m)
    rz = 1.0 / (jnp.sum(e, axis=-1, keepdims=True) + es)
    return e * rz, es * rz


def _wa_window(j, s):
    return pl.multiple_of(jnp.clip((j - 1) * BAND, 0, s - WA_WIN_TOK), BAND)


def _wa_fwd(q, k, v, sink, name, comm=None):
    hq, s, _ = q.shape
    hkv = k.shape[0]
    grp = hq // hkv

    def body(sink_ref, q_ref, k_ref, v_ref, o_ref):
        kh, step = pl.program_id(0), pl.program_id(1)
        for sub in range(WA_QBLOCKS):
            j = step * WA_QBLOCKS + sub
            rows = slice(sub * BAND, (sub + 1) * BAND)
            t0 = _wa_window(j, s)
            keys = pl.ds(t0, WA_WIN_TOK)
            p, _ = _wa_scores(q_ref[:, rows, :].reshape(grp * BAND, HEAD), k_ref[keys, :], t0, j, sink_ref, kh * grp, grp)
            o_ref[:, rows, :] = _dot(p.astype(BF16), v_ref[keys, :], _NN).reshape(grp, BAND, HEAD)

    qspec = pl.BlockSpec((grp, WA_QBLOCKS * BAND, HEAD), lambda kh, j: (kh, j, 0))
    full = pl.BlockSpec((None, s, HEAD), lambda kh, j: (kh, 0, 0))
    return _call(
        body, comm, name=name, grid=(hkv, s // (WA_QBLOCKS * BAND)), operands=[sink, q, k, v],
        in_specs=[SMEM, qspec, full, full],
        out_specs=[qspec], out_shape=[jax.ShapeDtypeStruct((hq, s, HEAD), F32)], scratch_shapes=[])


def _wa_bwd(q, k, v, sink, do, name, comm=None):
    hq, s, _ = q.shape
    hkv = k.shape[0]
    grp = hq // hkv

    def body(sink_ref, q_ref, do_ref, k_ref, v_ref, dq_ref, dk_ref, dv_ref, dsink_ref):
        kh, step = pl.program_id(0), pl.program_id(1)

        @pl.when(step == 0)
        def _():
            dk_ref[...] = jnp.zeros_like(dk_ref)
            dv_ref[...] = jnp.zeros_like(dv_ref)
            dsink_ref[...] = jnp.zeros_like(dsink_ref)

        for sub in range(WA_QBLOCKS):
            j = step * WA_QBLOCKS + sub
            rows = slice(sub * BAND, (sub + 1) * BAND)
            t0 = _wa_window(j, s)
            keys = pl.ds(t0, WA_WIN_TOK)
            qs = q_ref[:, rows, :].reshape(grp * BAND, HEAD)
            dos = do_ref[:, rows, :].reshape(grp * BAND, HEAD)
            kwin, vwin = k_ref[keys, :], v_ref[keys, :]
            p, ps = _wa_scores(qs, kwin, t0, j, sink_ref, kh * grp, grp)
            dp = _dot(dos, vwin, _NT)
            dv_ref[keys, :] += _dot(p.astype(BF16), dos, _TN)
            rowdot = jnp.sum(p * dp, axis=-1, keepdims=True)
            to_sink = ps * rowdot
            for g in range(grp):
                dsink_ref[g] += jnp.zeros((8, LANES), F32) - jnp.sum(to_sink[g * BAND:(g + 1) * BAND])
            dss = (p * (dp - rowdot) * SCALE).astype(BF16)
            dq_ref[:, rows, :] = _dot(dss, kwin, _NN).reshape(grp, BAND, HEAD)
            dk_ref[keys, :] += _dot(dss, qs, _TN)

    qspec = pl.BlockSpec((grp, WA_QBLOCKS * BAND, HEAD), lambda kh, j: (kh, j, 0))
    full = pl.BlockSpec((None, s, HEAD), lambda kh, j: (kh, 0, 0))
    kv = jax.ShapeDtypeStruct((hkv, s, HEAD), F32)
    return _call(
        body, comm, name=name, grid=(hkv, s // (WA_QBLOCKS * BAND)), operands=[sink, q, do, k, v],
        in_specs=[SMEM, qspec, qspec, full, full],
        out_specs=[qspec, full, full, pl.BlockSpec((grp, 8, LANES), lambda kh, j: (kh, 0, 0))],
        out_shape=[jax.ShapeDtypeStruct((hq, s, HEAD), F32), kv, kv, jax.ShapeDtypeStruct((hq, 8, LANES), F32)],
        scratch_shapes=[])


def _onorm_fwd(oa, ob, gains, name):
    ha, s, _ = oa.shape
    hq = ob.shape[0]
    ts = _tile(s, ROW_TILE, 16)

    def body(oa_ref, ob_ref, g_ref, o_ref):
        col = 0
        for ref, nh in ((oa_ref, ha), (ob_ref, hq)):
            ss = sum(jnp.sum(ref[h] * ref[h], axis=-1, keepdims=True) for h in range(nh))
            r = lax.rsqrt(ss / (nh * HEAD) + EPS)
            for h in range(nh):
                o_ref[:, col * HEAD:(col + 1) * HEAD] = (ref[h] * r * g_ref[:, col * HEAD:(col + 1) * HEAD]).astype(BF16)
                col += 1

    mix = (ha + hq) * HEAD
    return pl.pallas_call(
        body, name=name, grid=(s // ts,),
        in_specs=[pl.BlockSpec((ha, ts, HEAD), lambda i: (0, i, 0)), pl.BlockSpec((hq, ts, HEAD), lambda i: (0, i, 0)),
                  pl.BlockSpec((1, mix), lambda i: (0, 0))],
        out_specs=pl.BlockSpec((ts, mix), lambda i: (i, 0)),
        out_shape=jax.ShapeDtypeStruct((s, mix), BF16), compiler_params=_params(),
    )(oa, ob, gains)


def _onorm_bwd(oa, ob, gains, don, name):
    ha, s, _ = oa.shape
    hq = ob.shape[0]
    ts = _tile(s, ROW_TILE, 16)
    mix = (ha + hq) * HEAD

    def body(oa_ref, ob_ref, g_ref, don_ref, doa_ref, dob_ref, dg_ref):
        @pl.when(pl.program_id(0) == 0)
        def _():
            dg_ref[...] = jnp.zeros_like(dg_ref)

        col0 = 0
        for ref, d_ref, nh in ((oa_ref, doa_ref, ha), (ob_ref, dob_ref, hq)):
            ss = sum(jnp.sum(ref[h] * ref[h], axis=-1, keepdims=True) for h in range(nh))
            r = lax.rsqrt(ss / (nh * HEAD) + EPS)
            dot = jnp.zeros((ts, 1), F32)
            for h in range(nh):
                cols = slice((col0 + h) * HEAD, (col0 + h + 1) * HEAD)
                dot = dot + jnp.sum(don_ref[:, cols] * g_ref[:, cols] * ref[h], axis=-1, keepdims=True)
            mean = dot * r / (nh * HEAD)
            for h in range(nh):
                cols = slice((col0 + h) * HEAD, (col0 + h + 1) * HEAD)
                y = ref[h] * r
                dn = don_ref[:, cols]
                d_ref[h] = (r * (dn * g_ref[:, cols] - y * mean)).astype(BF16)
                dg_ref[0:1, cols] += jnp.sum(dn * y, axis=0, keepdims=True)
            col0 += nh

    return pl.pallas_call(
        body, name=name, grid=(s // ts,),
        in_specs=[pl.BlockSpec((ha, ts, HEAD), lambda i: (0, i, 0)), pl.BlockSpec((hq, ts, HEAD), lambda i: (0, i, 0)),
                  pl.BlockSpec((1, mix), lambda i: (0, 0)), pl.BlockSpec((ts, mix), lambda i: (i, 0))],
        out_specs=[pl.BlockSpec((ha, ts, HEAD), lambda i: (0, i, 0)), pl.BlockSpec((hq, ts, HEAD), lambda i: (0, i, 0)),
                   pl.BlockSpec((8, mix), lambda i: (0, 0))],
        out_shape=[jax.ShapeDtypeStruct((ha, s, HEAD), BF16), jax.ShapeDtypeStruct((hq, s, HEAD), BF16),
                   jax.ShapeDtypeStruct((8, mix), F32)],
        compiler_params=_params(),
    )(oa, ob, gains, don)


HALO = 8
PACKED = 16


def _halo_specs(ts, tc, col_off):
    per = ts // HALO
    cur = pl.BlockSpec((ts, tc), lambda j, i: (i, j + col_off))
    prev = pl.BlockSpec((HALO, tc), lambda j, i: (jnp.maximum(i * per - 1, 0), j + col_off))

    def nxt_map(n_blocks):
        return pl.BlockSpec((HALO, tc), lambda j, i: (jnp.minimum((i + 1) * per, n_blocks - 1), j + col_off))

    return cur, prev, nxt_map


def _sigmoid(x):
    return 1.0 / (1.0 + jnp.exp(-x))


def _ffn_tiles(s, f):
    return _tile(s, 512, 16), _tile(f, 512, LANES)


def _gate_fwd(u, cw, cb, f, name):
    s = u.shape[0]
    ts, tc = _ffn_tiles(s, f)
    nj, ni = f // tc, s // ts

    def body(g_ref, gp_ref, gn_ref, u_ref, up_ref, un_ref, wg_ref, wu_ref, bg_ref, bu_ref, a_ref, gu_ref):
        i = pl.program_id(1)

        def conv(c_ref, p_ref, n_ref, w_ref, b_ref):
            ext = jnp.concatenate([jnp.where(i > 0, p_ref[...], 0.0), c_ref[...], jnp.where(i < ni - 1, n_ref[...], 0.0)], axis=0)
            rows = ts + 2 * HALO
            out = (pltpu.roll(ext, 1, axis=0) * w_ref[0:1, :] + ext * w_ref[1:2, :]
                   + pltpu.roll(ext, rows - 1, axis=0) * w_ref[2:3, :] + b_ref[...])
            return out[HALO:HALO + ts]

        gate = conv(g_ref, gp_ref, gn_ref, wg_ref, bg_ref)
        up = conv(u_ref, up_ref, un_ref, wu_ref, bu_ref)
        gu_ref[0] = gate.astype(BF16)
        gu_ref[1] = up.astype(BF16)
        a_ref[...] = (gate * _sigmoid(gate) * up).astype(BF16)

    gc, gp, gn = _halo_specs(ts, tc, 0)
    uc, up_, un = _halo_specs(ts, tc, nj)
    wg = pl.BlockSpec((3, tc), lambda j, i: (0, j))
    wu = pl.BlockSpec((3, tc), lambda j, i: (0, j + nj))
    bg = pl.BlockSpec((1, tc), lambda j, i: (0, j))
    bu = pl.BlockSpec((1, tc), lambda j, i: (0, j + nj))
    return pl.pallas_call(
        body, name=name, grid=(nj, ni),
        in_specs=[gc, gp, gn(s // HALO), uc, up_, un(s // HALO), wg, wu, bg, bu],
        out_specs=[pl.BlockSpec((ts, tc), lambda j, i: (i, j)), pl.BlockSpec((2, ts, tc), lambda j, i: (0, i, j))],
        out_shape=[jax.ShapeDtypeStruct((s, f), BF16), jax.ShapeDtypeStruct((2, s, f), BF16)], compiler_params=_params(),
    )(u, u, u, u, u, u, cw, cw, cb, cb)


def _ffn_bwd(gu, u, da, cw, name):
    _, s, f = gu.shape
    ts, tc = _ffn_tiles(s, f)
    nj, ni = f // tc, s // ts

    def body(gu_ref, gup_ref, gun_ref, da_ref, dap_ref, dan_ref, xg_ref, xu_ref, wg_ref, wu_ref,
             du_ref, dcw_ref, dcb_ref):
        i = pl.program_id(1)

        @pl.when(i == 0)
        def _():
            dcw_ref[...] = jnp.zeros_like(dcw_ref)
            dcb_ref[...] = jnp.zeros_like(dcb_ref)

        rows = ts + 2 * HALO
        mid = slice(HALO, HALO + ts)
        da = jnp.concatenate([jnp.where(i > 0, dap_ref[...], 0.0), da_ref[...], jnp.where(i < ni - 1, dan_ref[...], 0.0)], axis=0)
        def rows_of(half):
            before = gup_ref[half].astype(F32)[PACKED - HALO:]
            after = gun_ref[half].astype(F32)[:HALO]
            return jnp.concatenate([before, gu_ref[half].astype(F32), after], axis=0)

        gate, up = rows_of(0), rows_of(1)
        sg = _sigmoid(gate)
        d_up = da * gate * sg
        d_gate = da * up * (sg * (1.0 + gate * (1.0 - sg)))
        for half, (dd, x_ref, w_ref) in enumerate(((d_gate, xg_ref, wg_ref), (d_up, xu_ref, wu_ref))):
            before = pltpu.roll(dd, 1, axis=0)
            after = pltpu.roll(dd, rows - 1, axis=0)
            du_ref[half] = (before * w_ref[2:3, :] + dd * w_ref[1:2, :] + after * w_ref[0:1, :])[mid].astype(BF16)
            x = x_ref[...]
            dcb_ref[half, 0:1, :] += jnp.sum(dd[mid], axis=0, keepdims=True)
            for k, shifted in enumerate((after, dd, before)):
                dcw_ref[half, k, 0:1, :] += jnp.sum(shifted[mid] * x, axis=0, keepdims=True)

    per = ts // PACKED
    cur3 = pl.BlockSpec((2, ts, tc), lambda j, i: (0, i, j))
    prev3 = pl.BlockSpec((2, PACKED, tc), lambda j, i: (0, jnp.maximum(i * per - 1, 0), j))
    next3 = pl.BlockSpec((2, PACKED, tc), lambda j, i: (0, jnp.minimum((i + 1) * per, s // PACKED - 1), j))
    cur, prev, nxt = _halo_specs(ts, tc, 0)
    return pl.pallas_call(
        body, name=name, grid=(nj, ni),
        in_specs=[cur3, prev3, next3, cur, prev, nxt(s // HALO),
                  pl.BlockSpec((ts, tc), lambda j, i: (i, j)), pl.BlockSpec((ts, tc), lambda j, i: (i, j + nj)),
                  pl.BlockSpec((3, tc), lambda j, i: (0, j)), pl.BlockSpec((3, tc), lambda j, i: (0, j + nj))],
        out_specs=[cur3, pl.BlockSpec((2, 3, 8, tc), lambda j, i: (0, 0, 0, j)),
                   pl.BlockSpec((2, 8, tc), lambda j, i: (0, 0, j))],
        out_shape=[jax.ShapeDtypeStruct((2, s, f), BF16),
                   jax.ShapeDtypeStruct((2, 3, 8, f), F32), jax.ShapeDtypeStruct((2, 8, f), F32)],
        compiler_params=_params(),
    )(gu, gu, gu, da, da, da, u, u, cw, cw)


def _loss_head(y, target, name):
    s, d = y.shape
    ts = _tile(s, ROW_TILE, 16)

    def body(y_ref, t_ref, dy_ref, dyb_ref, l_ref):
        @pl.when(pl.program_id(0) == 0)
        def _():
            l_ref[...] = jnp.zeros_like(l_ref)

        err = y_ref[...] - t_ref[...]
        dy = err / d
        dy_ref[...] = dy
        dyb_ref[...] = dy.astype(BF16)
        l_ref[...] += jnp.zeros((8, LANES), F32) + 0.5 * jnp.sum(jnp.sum(err * err, axis=-1, keepdims=True) / d)

    blk = pl.BlockSpec((ts, d), lambda i: (i, 0))
    return pl.pallas_call(
        body, name=name, grid=(s // ts,),
        in_specs=[blk, blk], out_specs=[blk, blk, pl.BlockSpec((8, LANES), lambda i: (0, 0))],
        out_shape=[jax.ShapeDtypeStruct((s, d), F32), jax.ShapeDtypeStruct((s, d), BF16), jax.ShapeDtypeStruct((8, LANES), F32)],
        compiler_params=_params(),
    )(y, target)


SMALL = ("ln1_g", "qn_a", "kn_a", "rpb", "qn_b", "kn_b", "sink", "on_a", "on_b", "ln2_g", "conv_b", "conv_w")
PACK_ALIGN = 8 * LANES


def _pack(arrays):
    flat = []
    for a in arrays:
        a = a.reshape(-1)
        flat.append(jnp.pad(a, (0, -a.size % PACK_ALIGN)))
    return jnp.concatenate(flat).reshape(-1, LANES)


def _unpack(packed, like):
    out, at = [], 0
    flat = packed.reshape(-1)
    for a in like:
        out.append(flat[at:at + a.size].reshape(a.shape))
        at += a.size + (-a.size % PACK_ALIGN)
    return out


def kernel(x, positions, ln1_g, w_in, qn_a, kn_a, rpb, qn_b, kn_b, sink, on_a, on_b, w_out, ln2_g, w_up, conv_w, conv_b, w_down, loss_target, m_ln1_g, m_w_in, m_qn_a, m_kn_a, m_rpb, m_qn_b, m_kn_b, m_sink, m_on_a, m_on_b, m_w_out, m_ln2_g, m_w_up, m_conv_w, m_conv_b, m_w_down, v_ln1_g, v_w_in, v_qn_a, v_kn_a, v_rpb, v_qn_b, v_kn_b, v_sink, v_on_a, v_on_b, v_w_out, v_ln2_g, v_w_up, v_conv_w, v_conv_b, v_w_down):
    weights = dict(ln1_g=ln1_g, w_in=w_in, qn_a=qn_a, kn_a=kn_a, rpb=rpb, qn_b=qn_b, kn_b=kn_b, sink=sink, on_a=on_a,
                   on_b=on_b, w_out=w_out, ln2_g=ln2_g, w_up=w_up, conv_w=conv_w, conv_b=conv_b, w_down=w_down)
    mom1 = dict(ln1_g=m_ln1_g, w_in=m_w_in, qn_a=m_qn_a, kn_a=m_kn_a, rpb=m_rpb, qn_b=m_qn_b, kn_b=m_kn_b, sink=m_sink,
                on_a=m_on_a, on_b=m_on_b, w_out=m_w_out, ln2_g=m_ln2_g, w_up=m_w_up, conv_w=m_conv_w, conv_b=m_conv_b,
                w_down=m_w_down)
    mom2 = dict(ln1_g=v_ln1_g, w_in=v_w_in, qn_a=v_qn_a, kn_a=v_kn_a, rpb=v_rpb, qn_b=v_qn_b, kn_b=v_kn_b, sink=v_sink,
                on_a=v_on_a, on_b=v_on_b, w_out=v_w_out, ln2_g=v_ln2_g, w_up=v_w_up, conv_w=v_conv_w, conv_b=v_conv_b,
                w_down=v_w_down)
    order = ("ln1_g", "w_in", "qn_a", "kn_a", "rpb", "qn_b", "kn_b", "sink", "on_a", "on_b", "w_out", "ln2_g", "w_up",
             "conv_w", "conv_b", "w_down")

    depth, d = ln1_g.shape
    s = x.shape[1]
    ha = on_a.shape[1] // HEAD
    hq = on_b.shape[1] // HEAD
    pw = w_in.shape[2] * N_DEV
    hkv = (pw - 3 * ha * HEAD - hq * HEAD) // (2 * HEAD)
    f = w_down.shape[1] * N_DEV
    mix = (ha + hq) * HEAD
    cfg = (ha, hq, hkv)
    fs = conv_w.shape[2]
    dev = 4 * lax.axis_index("x") + 2 * lax.axis_index("y") + lax.axis_index("c")
    core = lax.axis_index("c").astype(jnp.int32).reshape(1)

    shard = {n: weights[n].astype(BF16) for n in ("w_in", "w_out", "w_up", "w_down")}

    def unshard(n, g):
        if n in ("w_in", "w_up"):
            return g.transpose(1, 0, 2).reshape(g.shape[1], N_DEV * g.shape[2])
        return g.reshape(N_DEV * g.shape[1], g.shape[2])

    full = {n: [None] * depth for n in shard}
    half_d = _tile(d, d // 2, 16)
    up0 = [shard["w_up"][0][:half_d], shard["w_up"][0][half_d:]]

    def travel(l, host):
        plan = {}
        if l == 0:
            plan = {"ln1": [("w_in", 0, None)], "proj": [("w_out", 0, None), ("w_up", 0, 0)], "na": [("w_up", 0, 1)],
                    "wa": [("w_down", 0, None)], "up": [("w_up", 1, None), ("w_in", 1, None)]}
        elif l + 1 < depth:
            plan = {"proj": [("w_in", l + 1, None)], "up": [("w_up", l + 1, None)]}
        if l + 1 < depth:
            plan.update({"out": [("w_out", l + 1, None)], "down": [("w_down", l + 1, None)]})
        return [key for key in plan.get(host, []) if key[1] < depth]

    arrived = {}

    def gather_of(keys):
        return _gather_comm([shard[n][k] if part is None else up0[part] for n, k, part in keys]) if keys else None

    def landed(keys, blocks):
        for (n, k, part), g in zip(keys, blocks):
            if part is None:
                full[n][k] = unshard(n, g)
            else:
                arrived[part] = g
                if len(arrived) == 2:
                    full[n][k] = unshard(n, jnp.concatenate([arrived[0], arrived[1]], axis=1))

    cw_rows = depth * 3
    cw_pad = jnp.pad(conv_w.reshape(cw_rows, fs), ((0, -cw_rows % 8), (0, 0)))
    g_cw = _allgather(cw_pad, "gather_conv_w")
    full_cw = g_cw[:, :cw_rows].reshape(N_DEV, depth, 3, fs).transpose(1, 2, 0, 3).reshape(depth, 3, 2 * f)

    inv = ROPE_THETA ** (-jnp.arange(0, HEAD, 2, dtype=F32) / HEAD)
    ang = positions.astype(F32)[:, None] * inv[None, :]
    cos = jnp.concatenate([jnp.cos(ang), jnp.cos(ang)], axis=-1)
    sin = jnp.concatenate([-jnp.sin(ang), jnp.sin(ang)], axis=-1)
    qk = jnp.arange(GRID_W * GRID_W)
    dc_of = (qk % GRID_W) - (qk // GRID_W) + (WIN_C - 1)
    onehot = (dc_of[:, None] == jnp.arange(LANES)[None, :]).astype(BF16)

    tiles_s = _tile(s, 512, 16)
    tiles_l = _tile(s, 1024, 16)

    xs = x.reshape(s, d)
    saved = []
    for l in range(depth):
        def hosting(call, host, name, *args, **kw):
            keys = travel(l, host)
            if not keys:
                return call(*args, name=name, **kw)
            out, *blocks = call(*args, name=f"{name}_g{len(keys)}", comm=gather_of(keys), **kw)
            landed(keys, blocks)
            return out

        def fwd_matmul(n, host, a_op, name, **kw):
            return hosting(lambda **k2: _matmul(a_op, full[n][l], dims="nn", out_dtype=F32, **k2), host, name, **kw)

        gains = jnp.zeros((8, HEAD), F32).at[0].set(qn_a[l]).at[1].set(kn_a[l]).at[2].set(qn_b[l]).at[3].set(kn_b[l])
        on_g = jnp.concatenate([on_a[l], on_b[l]]).reshape(1, mix)
        h = hosting(_rms_fwd, "ln1", "ln1_fwd", xs, ln1_g[l].reshape(1, d))
        proj = fwd_matmul("w_in", "proj", h, "proj_fwd", ti=tiles_l, tj=_tile(pw, 1536, LANES), tk=d)
        qa, ka, va, qb, kb, vb = _qkv_fwd(proj, gains, cos, sin, cfg, "qkv_fwd")
        tb = _na_bias(rpb[l].reshape(-1), ha, "na_bias")
        oa = hosting(_na_fwd, "na", "na_fwd", qa, ka, va, tb)
        ob = hosting(_wa_fwd, "wa", "wa_fwd", qb, kb, vb, sink[l])
        o_n = _onorm_fwd(oa, ob, on_g, "onorm_fwd")
        x1 = fwd_matmul("w_out", "out", o_n, "out_fwd", ti=tiles_l, tj=_tile(d, 1024, LANES), tk=mix, resid=xs)
        h2 = _rms_fwd(x1, ln2_g[l].reshape(1, d), "ln2_fwd")
        u = fwd_matmul("w_up", "up", h2, "up_fwd", ti=tiles_l, tj=_tile(2 * f, 1024, 2 * LANES), tk=d)
        a, gu = _gate_fwd(u, full_cw[l], conv_b[l].reshape(1, 2 * f), f, "gate_fwd")
        x2 = fwd_matmul("w_down", "down", a, "down_fwd", ti=tiles_s, tj=_tile(d, 512, LANES), tk=f, resid=x1)
        saved.append(dict(x=xs, h=h, proj=proj, gains=gains, on_g=on_g, qkv=(qa, ka, va, qb, kb, vb), tb=tb, oa=oa, ob=ob,
                          o_n=o_n, x1=x1, h2=h2, u=u, gu=gu, a=a))
        xs = x2

    dx, dx_b, loss_part = _loss_head(xs, loss_target.reshape(s, d), "loss_head")
    tile_c = _tile(s, 2048, 16)
    half_k = _tile(2 * f, f, fs)
    loss = lax.psum(loss_part[0, 0], ("x", "y", "c"))

    small_grads = [None] * depth
    big = {n: None for n in ("w_in", "w_out", "w_up", "w_down")}
    pending = None
    for l in reversed(range(depth)):
        sv = saved[l]
        qa, ka, va, qb, kb, vb = sv["qkv"]

        def update(n, layer, got):
            big[n] = _adamw(got, weights[n], mom1[n], mom2[n], "adamw_" + n, layer=layer, into=big[n])

        def carrying(n, name, **kw):
            if pending is None or n is None:
                return _matmul(name=name, **kw)
            out, got = _matmul(name=name + "_carry", comm=_scatter_comm([pending[n]]), **kw)
            update(n, l + 1, got)
            return out

        def grad_matmul(n, a_op, b_op, name, **kw):
            carried = n if n in ("w_up", "w_down") else None
            return carrying(carried, name, a=a_op, b=b_op, dims="tn", out_dtype=BF16, **kw)

        def own(blocks):
            return _scatter_comm([blocks]) if l == 0 else None

        gw_down = grad_matmul("w_down", sv["a"], dx_b, "down_bwd_w", ti=_tile(f, 1408, LANES), tj=_tile(d, 1024, LANES),
                              tk=tile_c, j_outer=False)
        da = carrying("w_in", "down_bwd_x", a=dx_b, b=full["w_down"][l], dims="nt", ti=tiles_s, tj=_tile(f, 2816, 2 * LANES),
                      tk=d, out_dtype=F32)
        du, dcw, dcb = _ffn_bwd(sv["gu"], sv["u"], da, full_cw[l], "ffn_bwd")
        gw_down = gw_down.reshape(N_DEV, f // N_DEV, d)
        dh2 = None
        for part in range(2 * f // half_k):
            comm = own(gw_down) if part == 0 else None
            dh2 = _matmul(du, full["w_up"][l], dims="nt", ti=tiles_s, tj=_tile(d, 1024, LANES), tk=half_k, out_dtype=F32,
                          name=f"up_bwd_x{part}" + "_own" * bool(comm), k_blocks=(part, 1), resid=dh2, halved="a", comm=comm)
            if comm:
                dh2, got = dh2
                update("w_down", 0, got)
        gw_up = grad_matmul("w_up", sv["h2"], du, "up_bwd_w", ti=_tile(d, 1024, LANES), tj=fs, tk=tile_c, dev_major=True,
                            halved="b")
        dx1, dx1_b, dln2 = _rms_bwd(sv["x1"], ln2_g[l].reshape(1, d), dh2, dx, "ln2_bwd")
        don = _matmul(dx1_b, full["w_out"][l], dims="nt", ti=tiles_l, tj=mix, tk=d, out_dtype=F32, name="out_bwd_x")
        gw_out = grad_matmul("w_out", sv["o_n"], dx1_b, "out_bwd_w", ti=_tile(mix, 1024, LANES), tj=_tile(d, 1024, LANES),
                             tk=tile_c, j_outer=False)
        gw_out = gw_out.reshape(N_DEV, mix // N_DEV, d)
        doa, dob, don_g = _onorm_bwd(sv["oa"], sv["ob"], sv["on_g"], don, "onorm_bwd")
        comm = _chip_comm([_pair_sums(gw_up, core, "rs0_w_up")]) if l == 0 else None
        dqa, dka, dva, dtb, *got = _na_bwd(qa, ka, va, sv["tb"], doa, "na_bwd" + "_own" * (l == 0), comm=comm)
        if got:
            update("w_up", 0, got[0])
        dqb, dkb, dvb, dsink, *got = _wa_bwd(qb, kb, vb, sink[l], dob, "wa_bwd" + "_own" * (l == 0), comm=own(gw_out))
        if got:
            update("w_out", 0, got[0])
        drpb = _rpb_grad(dtb, onehot, "rpb_grad")
        dproj, dgains = _qkv_bwd(sv["proj"], sv["gains"], cos, sin, (dqa, dka, dva, dqb, dkb, dvb), cfg, "qkv_bwd")
        dh = carrying("w_out", "proj_bwd_x", a=dproj, b=full["w_in"][l], dims="nt", ti=tiles_s, tj=_tile(d, 1024, LANES), tk=pw,
                      out_dtype=F32)
        gw_in = grad_matmul("w_in", sv["h"], dproj, "proj_bwd_w", ti=_tile(d, 1024, LANES), tj=_tile(pw, 1536, LANES), tk=tile_c)
        gw_in = gw_in.reshape(d, N_DEV, pw // N_DEV).transpose(1, 0, 2)
        comm = _chip_comm([_pair_sums(gw_in, core, "rs0_w_in")]) if l == 0 else None
        dx, dx_b, dln1, *got = _rms_bwd(sv["x"], ln1_g[l].reshape(1, d), dh, dx1, "ln1_bwd" + "_own" * (l == 0), comm=comm)
        if got:
            update("w_in", 0, got[0])

        small_grads[l] = dict(
            ln1_g=dln1[0], qn_a=dgains[0], kn_a=dgains[1], rpb=drpb, qn_b=dgains[2], kn_b=dgains[3], sink=dsink[:, 0, 0],
            on_a=don_g[0, :ha * HEAD], on_b=don_g[0, ha * HEAD:], ln2_g=dln2[0],
            conv_b=dcb[:, 0, :].reshape(2 * f), conv_w=dcw[:, :, 0, :].transpose(1, 0, 2).reshape(3, 2 * f))

        pending = dict(w_in=gw_in, w_out=gw_out, w_up=gw_up, w_down=gw_down)

    grads_l = [small_grads[l][n] for l in range(depth) for n in SMALL]
    gathered = _allgather(_pack(grads_l), "gather_small")
    zeros_cw = jnp.zeros((3, 2 * f), F32)

    def small_state(src):
        return _pack([zeros_cw if n == "conv_w" else src[n][l] for l in range(depth) for n in SMALL])

    sm = _adamw(gathered, small_state(weights), small_state(mom1), small_state(mom2), "adamw_small")
    sm = [_unpack(t, grads_l) for t in sm]
    small_out = {n: [jnp.stack([sm[k][l * len(SMALL) + i] for l in range(depth)]) for k in range(4)]
                 for i, n in enumerate(SMALL)}
    cw_grad = lax.dynamic_slice_in_dim(small_out["conv_w"][0], dev * fs, fs, axis=2)
    cw_rows_pad = cw_rows + (-cw_rows % 8)

    def rows8(a):
        return jnp.pad(a.reshape(cw_rows, fs), ((0, cw_rows_pad - cw_rows), (0, 0)))

    cw_res = _adamw(rows8(cw_grad)[None], rows8(conv_w), rows8(m_conv_w), rows8(v_conv_w), "adamw_conv_w")
    small_out["conv_w"] = [t[:cw_rows].reshape(depth, 3, fs) for t in cw_res]

    results = {n: (big[n] if n in big else small_out[n]) for n in order}
    grad_x = dx.reshape(1, s, d)
    return (loss, grad_x, *[results[n][0] for n in order], *[results[n][1] for n in order],
            *[results[n][2] for n in order], *[results[n][3] for n in order])
```

```python
import math

import jax
import jax.numpy as jnp
from jax import lax
from jax.experimental import pallas as pl
from jax.experimental.pallas import tpu as pltpu

F32 = jnp.float32
BF16 = jnp.bfloat16

HEAD = 128
GRID_W = 64
WIN_R = 8
WIN_C = 16
BAND = 128
ROPE_THETA = 10000.0
EPS = 1e-6
NEG = -1e30
SCALE = 1.0 / math.sqrt(HEAD)

ADAM_LR = 0.001
ADAM_B1 = 0.9
ADAM_B2 = 0.999
ADAM_EPS = 1e-08
ADAM_WD = 0.01
ADAM_STEP = 10

N_DEV = 8
LANES = 128
VMEM_LIMIT_BYTES = 56 * 2 ** 20
MESH = pl.DeviceIdType.MESH
ANY = pl.BlockSpec(memory_space=pl.ANY)
SMEM = pl.BlockSpec(memory_space=pltpu.SMEM)


def _params():
    return pltpu.CompilerParams(vmem_limit_bytes=VMEM_LIMIT_BYTES)


def _tile(n, pref, align):
    t = min(n, pref)
    t -= t % align
    while t > 0 and n % t:
        t -= align
    return t if t > 0 else n


def _place():
    x, y, c = lax.axis_index("x"), lax.axis_index("y"), lax.axis_index("c")
    chips = [(1 - x, y), (x, 1 - y), (1 - x, 1 - y)]
    return x, y, c, chips


COPIES_PER_ARRAY = N_DEV - 1


def _comm_scratch(n_arrays):
    return [pltpu.SemaphoreType.DMA((COPIES_PER_ARRAY * n_arrays,)), pltpu.SemaphoreType.DMA((COPIES_PER_ARRAY * n_arrays,)),
            pltpu.SemaphoreType.DMA((n_arrays,))]


def _gather_plan(src_refs, out_refs, send_sems, recv_sems, local_sems):
    x, y, c, chips = _place()
    me, sibling = (x, y, c), (x, y, 1 - c)

    def slot(a, px, py, pc):
        idx = 4 * px + 2 * py + pc
        if len(out_refs[a].shape) == len(src_refs[a].shape):
            width = src_refs[a].shape[1]
            return out_refs[a].at[:, pl.ds(pl.multiple_of(idx * width, LANES), width)]
        return out_refs[a].at[idx]

    def copy(a, k, block, to, src=None):
        return pltpu.make_async_remote_copy(
            src_ref=slot(a, *block) if src is None else src, dst_ref=slot(a, *block),
            send_sem=send_sems.at[COPIES_PER_ARRAY * a + k], recv_sem=recv_sems.at[COPIES_PER_ARRAY * a + k],
            device_id=to, device_id_type=MESH)

    def mine(a):
        return pltpu.make_async_copy(src_refs[a], slot(a, *me), local_sems.at[a])

    def first(a):
        return [copy(a, 0, me, sibling, src=src_refs[a])] + [
            copy(a, 1 + j, me, (*chip, c), src=src_refs[a]) for j, chip in enumerate(chips)]

    def passed(a):
        return [copy(a, 4 + j, (*chip, c), sibling) for j, chip in enumerate(chips)]

    def start():
        for a in range(len(src_refs)):
            mine(a).start()
            for cp in first(a):
                cp.start()

    def middle():
        for a in range(len(src_refs)):
            forwards = passed(a)
            for j, chip in enumerate(chips):
                copy(a, 1 + j, (*chip, c), me).wait_recv()
                forwards[j].start()

    def finish():
        for a in range(len(src_refs)):
            copy(a, 0, sibling, me).wait_recv()
            for j, chip in enumerate(chips):
                copy(a, 4 + j, (*chip, 1 - c), me).wait_recv()
            for cp in first(a) + passed(a):
                cp.wait_send()
            mine(a).wait()

    return start, middle, finish


def _scatter_plan(src_refs, out_refs, send_sems, recv_sems, local_sems):
    x, y, c, _ = _place()
    me = 4 * x + 2 * y + c

    def peer(k):
        px = 1 - x if k & 4 else x
        py = 1 - y if k & 2 else y
        pc = 1 - c if k & 1 else c
        return (px, py, pc), 4 * px + 2 * py + pc

    def copy(a, k, outgoing):
        to, idx = peer(k)
        return pltpu.make_async_remote_copy(
            src_ref=src_refs[a].at[idx], dst_ref=out_refs[a].at[me if outgoing else idx],
            send_sem=send_sems.at[COPIES_PER_ARRAY * a + k - 1], recv_sem=recv_sems.at[COPIES_PER_ARRAY * a + k - 1],
            device_id=to, device_id_type=MESH)

    def mine(a):
        return pltpu.make_async_copy(src_refs[a].at[me], out_refs[a].at[me], local_sems.at[a])

    def start():
        for a in range(len(src_refs)):
            mine(a).start()
            for k in range(1, N_DEV):
                copy(a, k, True).start()

    def finish():
        for a in range(len(src_refs)):
            for k in range(1, N_DEV):
                copy(a, k, False).wait_recv()
            for k in range(1, N_DEV):
                copy(a, k, True).wait_send()
            mine(a).wait()

    return start, lambda: None, finish


def _allgather(v, name):
    def body(v_ref, out_ref, send_sems, recv_sems, local_sems):
        start, middle, finish = _gather_plan([v_ref], [out_ref], send_sems, recv_sems, local_sems)
        start()
        middle()
        finish()

    return pl.pallas_call(
        body, name=name,
        out_shape=jax.ShapeDtypeStruct((N_DEV,) + v.shape, v.dtype),
        in_specs=[ANY], out_specs=ANY, scratch_shapes=_comm_scratch(1),
    )(v)


def _sibling_exchange(g, name):
    def body(g_ref, out_ref, send_sems, recv_sems):
        x, y, c, _ = _place()
        sibling = (x, y, 1 - c)
        copies = []
        for j in range(4):
            copies.append(pltpu.make_async_remote_copy(
                src_ref=g_ref.at[2 * j + (1 - c)], dst_ref=out_ref.at[j],
                send_sem=send_sems.at[j], recv_sem=recv_sems.at[j], device_id=sibling, device_id_type=MESH))
        for cp in copies:
            cp.start()
        for cp in copies:
            cp.wait_recv()
        for cp in copies:
            cp.wait_send()

    return pl.pallas_call(
        body, name=name,
        out_shape=jax.ShapeDtypeStruct((4,) + g.shape[1:], g.dtype),
        in_specs=[ANY], out_specs=ANY,
        scratch_shapes=[pltpu.SemaphoreType.DMA((4,)), pltpu.SemaphoreType.DMA((4,))],
    )(g)


def _pair_sum(g, got, core, name):
    _, r, c = g.shape
    tr = _tile(r, max(16, (1 << 20) // c), 16)

    def body(core_ref, g_ref, got_ref, o_ref):
        del core_ref
        o_ref[...] = (g_ref[...].astype(F32) + got_ref[...].astype(F32)).astype(o_ref.dtype)

    return pl.pallas_call(
        body, name=name,
        out_shape=jax.ShapeDtypeStruct((4, r, c), g.dtype),
        grid_spec=pltpu.PrefetchScalarGridSpec(
            num_scalar_prefetch=1, grid=(4, r // tr),
            in_specs=[pl.BlockSpec((None, tr, c), lambda j, i, core_ref: (2 * j + core_ref[0], i, 0)),
                      pl.BlockSpec((None, tr, c), lambda j, i, core_ref: (j, i, 0))],
            out_specs=pl.BlockSpec((None, tr, c), lambda j, i, core_ref: (j, i, 0))),
        compiler_params=_params(),
    )(core, g, got)


def _chip_plan(src_refs, out_refs, send_sems, recv_sems, local_sems):
    x, y, c, chips = _place()

    def copies(a):
        return [pltpu.make_async_remote_copy(
            src_ref=src_refs[a].at[2 * px + py], dst_ref=out_refs[a].at[k],
            send_sem=send_sems.at[COPIES_PER_ARRAY * a + k], recv_sem=recv_sems.at[COPIES_PER_ARRAY * a + k],
            device_id=(px, py, c), device_id_type=MESH) for k, (px, py) in enumerate(chips)]

    def mine(a):
        return pltpu.make_async_copy(src_refs[a].at[2 * x + y], out_refs[a].at[3], local_sems.at[a])

    def start():
        for a in range(len(src_refs)):
            mine(a).start()
            for cp in copies(a):
                cp.start()

    def finish():
        for a in range(len(src_refs)):
            for cp in copies(a):
                cp.wait_recv()
            for cp in copies(a):
                cp.wait_send()
            mine(a).wait()

    return start, lambda: None, finish


def _chip_comm(blocks):
    return _chip_plan, blocks, [jax.ShapeDtypeStruct(p.shape, p.dtype) for p in blocks]


def _pair_sums(g, core, name):
    got = _sibling_exchange(g, name + "_d2d")
    return _pair_sum(g, got, core, name + "_pair")


def _adamw(parts, w, m, v, name, layer=None, into=None):
    n_parts, r, c = parts.shape
    tr = _tile(r, max(8, (1 << 19) // c), 16 if parts.dtype == BF16 else 8)
    c1 = 1.0 - ADAM_B1 ** ADAM_STEP
    c2 = 1.0 - ADAM_B2 ** ADAM_STEP
    n_into = 0 if into is None else len(into)

    def body(p_ref, w_ref, m_ref, v_ref, *rest):
        g_out, d_out, m_out, v_out = rest[n_into:]
        g = p_ref[0].astype(F32)
        for k in range(1, n_parts):
            g = g + p_ref[k].astype(F32)
        m2 = ADAM_B1 * m_ref[...] + (1.0 - ADAM_B1) * g
        v2 = ADAM_B2 * v_ref[...] + (1.0 - ADAM_B2) * (g * g)
        g_out[...] = g
        m_out[...] = m2
        v_out[...] = v2
        d_out[...] = -ADAM_LR * ((m2 / c1) / (jnp.sqrt(v2 / c2) + ADAM_EPS) + ADAM_WD * w_ref[...])

    if layer is None:
        blk = pl.BlockSpec((tr, c), lambda i: (i, 0))
        out = jax.ShapeDtypeStruct((r, c), F32)
    else:
        blk = pl.BlockSpec((None, tr, c), lambda i: (layer, i, 0))
        out = jax.ShapeDtypeStruct(w.shape, F32)
    return pl.pallas_call(
        body, name=name, grid=(r // tr,),
        in_specs=[pl.BlockSpec((n_parts, tr, c), lambda i: (0, i, 0)), blk, blk, blk] + [ANY] * n_into,
        out_specs=[blk, blk, blk, blk], out_shape=[out, out, out, out],
        input_output_aliases={4 + k: k for k in range(n_into)},
        compiler_params=_params(),
    )(parts, w, m, v, *(into or ()))


def _exchange_steps(grid):
    flat, total = pl.program_id(0), grid[0]
    for axis in range(1, len(grid)):
        flat, total = flat * grid[axis] + pl.program_id(axis), total * grid[axis]
    return flat == 0, flat == max(3 * total // 4, min(1, total - 1)), flat == total - 1


def _run_exchange(hooks, grid, compute):
    start, middle, finish = hooks
    first, later, last = _exchange_steps(grid)
    pl.when(first)(start)
    compute()
    pl.when(later)(middle)
    pl.when(last)(finish)


def _host_exchange(body, comm, grid, n_in, n_out):
    plan, comm_in, comm_out = comm
    n = len(comm_in)

    def wrapped(*refs):
        ins, cin = refs[:n_in], refs[n_in:n_in + n]
        outs, cout = refs[n_in + n:n_in + n + n_out], refs[n_in + n + n_out:n_in + 2 * n + n_out]
        rest = refs[n_in + 2 * n + n_out:]
        _run_exchange(plan(cin, cout, *rest[len(rest) - 3:]), grid, lambda: body(*ins, *outs, *rest[:len(rest) - 3]))

    return wrapped, list(comm_in), [ANY] * n, list(comm_out), _comm_scratch(n)


def _gather_comm(shards, as_columns=()):
    cols = list(as_columns) + [False] * (len(shards) - len(as_columns))
    return _gather_plan, shards, [
        jax.ShapeDtypeStruct((v.shape[0], N_DEV * v.shape[1]) if col else (N_DEV,) + v.shape, v.dtype)
        for v, col in zip(shards, cols)]


def _scatter_comm(blocks):
    return _scatter_plan, blocks, [jax.ShapeDtypeStruct(g.shape, g.dtype) for g in blocks]


def _matmul(a, b, *, dims, ti, tj, tk, out_dtype, name, j_outer=True, resid=None, dev_major=False, comm=None,
            k_blocks=None, halved=None):
    a_shape = (a.shape[1], 2 * a.shape[2]) if halved == "a" else a.shape
    b_shape = (b.shape[1], 2 * b.shape[2]) if halved == "b" else b.shape
    if dims == "nn":
        (I, K), (K2, J) = a_shape, b_shape
    elif dims == "nt":
        (I, K), (J, K2) = a_shape, b_shape
    else:
        (K, I), (K2, J) = a_shape, b_shape
    assert K == K2 and I % ti == 0 and J % tj == 0 and K % tk == 0, (name, a.shape, b.shape, ti, tj, tk)
    assert halved is None or (halved, dims) in (("a", "nt"), ("b", "tn")), (name, halved, dims)
    k0, nk = k_blocks if k_blocks is not None else (0, K // tk)
    ni, nj = I // ti, J // tj

    def ij(g0, g1):
        return (g1, g0) if j_outer else (g0, g1)

    if dims == "nn":
        a_spec = pl.BlockSpec((ti, tk), lambda g0, g1, k: (ij(g0, g1)[0], k0 + k))
        b_spec = pl.BlockSpec((tk, tj), lambda g0, g1, k: (k0 + k, ij(g0, g1)[1]))
        dn = (((1,), (0,)), ((), ()))
    elif dims == "nt":
        a_spec = pl.BlockSpec((ti, tk), lambda g0, g1, k: (ij(g0, g1)[0], k0 + k))
        if halved == "a":
            per = K // 2 // tk
            a_spec = pl.BlockSpec((None, ti, tk), lambda g0, g1, k: ((k0 + k) // per, ij(g0, g1)[0], (k0 + k) % per))
        b_spec = pl.BlockSpec((tj, tk), lambda g0, g1, k: (ij(g0, g1)[1], k0 + k))
        dn = (((1,), (1,)), ((), ()))
    else:
        a_spec = pl.BlockSpec((tk, ti), lambda g0, g1, k: (k0 + k, ij(g0, g1)[0]))
        b_spec = pl.BlockSpec((tk, tj), lambda g0, g1, k: (k0 + k, ij(g0, g1)[1]))
        if halved == "b":
            per = J // 2 // tj
            b_spec = pl.BlockSpec((None, tk, tj), lambda g0, g1, k: (ij(g0, g1)[1] // per, k0 + k, ij(g0, g1)[1] % per))
        dn = (((0,), (0,)), ((), ()))
    in_specs = [a_spec, b_spec]
    operands = [a, b]
    if resid is not None:
        in_specs.append(pl.BlockSpec((ti, tj), lambda g0, g1, k: ij(g0, g1)))
        operands.append(resid)
    if dev_major:
        out_spec = pl.BlockSpec((None, ti, tj), lambda g0, g1, k: (ij(g0, g1)[1], ij(g0, g1)[0], 0))
        out_shape = jax.ShapeDtypeStruct((nj, I, tj), out_dtype)
    else:
        out_spec = pl.BlockSpec((ti, tj), lambda g0, g1, k: ij(g0, g1))
        out_shape = jax.ShapeDtypeStruct((I, J), out_dtype)

    grid = (nj, ni, nk) if j_outer else (ni, nj, nk)
    n_in = len(operands)
    n_comm = 0
    out_specs, out_shapes = [out_spec], [out_shape]
    scratch = [pltpu.VMEM((ti, tj), F32)] if nk > 1 else []
    if comm is not None:
        plan, comm_in, comm_out = comm
        n_comm = len(comm_in)
        operands += list(comm_in)
        in_specs += [ANY] * n_comm
        out_specs += [ANY] * n_comm
        out_shapes += list(comm_out)
        scratch += _comm_scratch(n_comm)

    def product(*refs):
        a_ref, b_ref = refs[0], refs[1]
        r_ref = refs[2] if resid is not None else None
        o_ref = refs[n_in + n_comm]
        part = lax.dot_general(a_ref[...].astype(BF16), b_ref[...].astype(BF16), dn, preferred_element_type=F32)

        def finish(acc):
            if r_ref is not None:
                acc = acc + r_ref[...]
            o_ref[...] = acc.astype(o_ref.dtype)

        if nk == 1:
            finish(part)
        else:
            acc_ref = refs[n_in + 2 * n_comm + 1]
            k = pl.program_id(2)

            @pl.when(k == 0)
            def _():
                acc_ref[...] = part

            @pl.when(k > 0)
            def _():
                acc_ref[...] += part

            @pl.when(k == nk - 1)
            def _():
                finish(acc_ref[...])

    def body(*refs):
        if comm is None:
            product(*refs)
        else:
            hooks = plan(refs[n_in:n_in + n_comm], refs[n_in + n_comm + 1:n_in + 2 * n_comm + 1], *refs[-3:])
            _run_exchange(hooks, grid, lambda: product(*refs))

    res = pl.pallas_call(
        body, name=name, grid=grid,
        in_specs=in_specs, out_specs=out_specs, out_shape=out_shapes,
        scratch_shapes=scratch, compiler_params=_params(),
    )(*operands)
    return res[0] if comm is None else res


ROW_TILE = 512


def _rms_fwd(x, g, name, comm=None):
    s, d = x.shape
    ts = _tile(s, ROW_TILE, 16)

    def body(x_ref, g_ref, h_ref):
        xv = x_ref[...]
        r = lax.rsqrt(jnp.mean(xv * xv, axis=-1, keepdims=True) + EPS)
        h_ref[...] = (xv * r * g_ref[...]).astype(BF16)

    return _call(
        body, comm, name=name, grid=(s // ts,), operands=[x, g],
        in_specs=[pl.BlockSpec((ts, d), lambda i: (i, 0)), pl.BlockSpec((1, d), lambda i: (0, 0))],
        out_specs=[pl.BlockSpec((ts, d), lambda i: (i, 0))],
        out_shape=[jax.ShapeDtypeStruct((s, d), BF16)], scratch_shapes=[])


def _rms_bwd(x, g, dh, dres, name, comm=None):
    s, d = x.shape
    ts = _tile(s, ROW_TILE, 16)

    def body(x_ref, g_ref, dh_ref, dres_ref, dx_ref, dxb_ref, dg_ref):
        xv = x_ref[...]
        r = lax.rsqrt(jnp.mean(xv * xv, axis=-1, keepdims=True) + EPS)
        y = xv * r
        dhv = dh_ref[...]
        gd = dhv * g_ref[...]
        dxv = dres_ref[...] + r * (gd - y * jnp.mean(gd * y, axis=-1, keepdims=True))
        dx_ref[...] = dxv
        dxb_ref[...] = dxv.astype(BF16)

        @pl.when(pl.program_id(0) == 0)
        def _():
            dg_ref[...] = jnp.zeros_like(dg_ref)

        dg_ref[0:1, :] += jnp.sum(dhv * y, axis=0, keepdims=True)

    blk = pl.BlockSpec((ts, d), lambda i: (i, 0))
    return _call(
        body, comm, name=name, grid=(s // ts,), operands=[x, g, dh, dres],
        in_specs=[blk, pl.BlockSpec((1, d), lambda i: (0, 0)), blk, blk],
        out_specs=[blk, blk, pl.BlockSpec((8, d), lambda i: (0, 0))],
        out_shape=[jax.ShapeDtypeStruct((s, d), F32), jax.ShapeDtypeStruct((s, d), BF16), jax.ShapeDtypeStruct((8, d), F32)],
        scratch_shapes=[])


def _head_norm(t, gain):
    r = lax.rsqrt(jnp.mean(t * t, axis=-1, keepdims=True) + EPS)
    return t * r * gain


def _head_norm_bwd(t, gain, dn):
    r = lax.rsqrt(jnp.mean(t * t, axis=-1, keepdims=True) + EPS)
    y = t * r
    gd = dn * gain
    dt = r * (gd - y * jnp.mean(gd * y, axis=-1, keepdims=True))
    return dt, jnp.sum(dn * y, axis=0, keepdims=True)


def _rope(n, cos, sin):
    return n * cos + pltpu.roll(n, HEAD // 2, axis=1) * sin


def _rope_bwd(do, cos, sin):
    return do * cos + pltpu.roll(do * sin, HEAD // 2, axis=1)


def _qkv_fwd(proj, gains, cos, sin, cfg, name):
    s, pw = proj.shape
    ha, hq, hkv = cfg
    ts = _tile(s, ROW_TILE, 16)

    def body(p_ref, gn_ref, cos_ref, sin_ref, qa_ref, ka_ref, va_ref, qb_ref, kb_ref, vb_ref):
        cosv, sinv = cos_ref[...], sin_ref[...]
        col = 0
        for out_ref, nh, gi, rot in ((qa_ref, ha, 0, False), (ka_ref, ha, 1, False), (va_ref, ha, None, False),
                                     (qb_ref, hq, 2, True), (kb_ref, hkv, 3, True), (vb_ref, hkv, None, False)):
            for h in range(nh):
                t = p_ref[:, col * HEAD:(col + 1) * HEAD]
                if gi is not None:
                    t = _head_norm(t, gn_ref[gi:gi + 1, :])
                if rot:
                    t = _rope(t, cosv, sinv)
                out_ref[h] = t.astype(BF16)
                col += 1

    def hm(nh):
        return pl.BlockSpec((nh, ts, HEAD), lambda i: (0, i, 0)), jax.ShapeDtypeStruct((nh, s, HEAD), BF16)

    specs, shapes = zip(hm(ha), hm(ha), hm(ha), hm(hq), hm(hkv), hm(hkv))
    tok = pl.BlockSpec((ts, HEAD), lambda i: (i, 0))
    return pl.pallas_call(
        body, name=name, grid=(s // ts,),
        in_specs=[pl.BlockSpec((ts, pw), lambda i: (i, 0)), pl.BlockSpec((8, HEAD), lambda i: (0, 0)), tok, tok],
        out_specs=list(specs), out_shape=list(shapes), compiler_params=_params(),
    )(proj, gains, cos, sin)


def _qkv_bwd(proj, gains, cos, sin, grads, cfg, name):
    s, pw = proj.shape
    ha, hq, hkv = cfg
    ts = _tile(s, 256, 16)

    def body(p_ref, gn_ref, cos_ref, sin_ref, dqa, dka, dva, dqb, dkb, dvb, dp_ref, dgn_ref):
        cosv, sinv = cos_ref[...], sin_ref[...]

        @pl.when(pl.program_id(0) == 0)
        def _():
            dgn_ref[...] = jnp.zeros_like(dgn_ref)

        col = 0
        for d_ref, nh, gi, rot in ((dqa, ha, 0, False), (dka, ha, 1, False), (dva, ha, None, False),
                                   (dqb, hq, 2, True), (dkb, hkv, 3, True), (dvb, hkv, None, False)):
            dgain = jnp.zeros((1, HEAD), F32)
            for h in range(nh):
                dt = d_ref[h]
                if rot:
                    dt = _rope_bwd(dt, cosv, sinv)
                if gi is not None:
                    dt, dg = _head_norm_bwd(p_ref[:, col * HEAD:(col + 1) * HEAD], gn_ref[gi:gi + 1, :], dt)
                    dgain = dgain + dg
                dp_ref[:, col * HEAD:(col + 1) * HEAD] = dt.astype(BF16)
                col += 1
            if gi is not None:
                dgn_ref[gi:gi + 1, :] += dgain

    def hm(nh):
        return pl.BlockSpec((nh, ts, HEAD), lambda i: (0, i, 0))

    tok = pl.BlockSpec((ts, HEAD), lambda i: (i, 0))
    small = pl.BlockSpec((8, HEAD), lambda i: (0, 0))
    return pl.pallas_call(
        body, name=name, grid=(s // ts,),
        in_specs=[pl.BlockSpec((ts, pw), lambda i: (i, 0)), small, tok, tok,
                  hm(ha), hm(ha), hm(ha), hm(hq), hm(hkv), hm(hkv)],
        out_specs=[pl.BlockSpec((ts, pw), lambda i: (i, 0)), small],
        out_shape=[jax.ShapeDtypeStruct((s, pw), BF16), jax.ShapeDtypeStruct((8, HEAD), F32)],
        compiler_params=_params(),
    )(proj, gains, cos, sin, *grads)


NA_QROWS = 32
NA_KEYS = WIN_R * GRID_W
N_DR = 2 * WIN_R - 1
N_DC = 2 * WIN_C - 1


def _na_bias(rpb_flat, n_heads, name):
    def body(rpb_ref, tb_ref):
        h = pl.program_id(0)
        qi = lax.broadcasted_iota(jnp.int32, (GRID_W, LANES), 0)
        lane = lax.broadcasted_iota(jnp.int32, (GRID_W, LANES), 1)
        kk = lane & (GRID_W - 1)
        upper = lane >= GRID_W
        dcm = kk - qi + (WIN_C - 1)
        cs = jnp.clip(qi - WIN_C // 2, 0, GRID_W - WIN_C)
        valid = (kk >= cs) & (kk < cs + WIN_C)
        base = h * (N_DR * N_DC)
        for dra in range(N_DR - 1):
            def step(j, acc, dra=dra):
                va = rpb_ref[base + dra * N_DC + j]
                vb = rpb_ref[base + (dra + 1) * N_DC + j]
                return jnp.where(dcm == j, jnp.where(upper, vb, va), acc)

            pair = lax.fori_loop(0, N_DC, step, jnp.zeros((GRID_W, LANES), F32))
            pair = jnp.where(valid, pair, NEG)
            for dr0 in range(WIN_R):
                wp, odd = divmod(dra - dr0, 2)
                if odd == 0 and 0 <= wp < WIN_R // 2:
                    tb_ref[0, dr0, :, wp * LANES:(wp + 1) * LANES] = pair

    return pl.pallas_call(
        body, name=name, grid=(n_heads,),
        in_specs=[SMEM],
        out_specs=pl.BlockSpec((1, WIN_R, GRID_W, NA_KEYS), lambda h: (h, 0, 0, 0)),
        out_shape=jax.ShapeDtypeStruct((n_heads, WIN_R, GRID_W, NA_KEYS), F32),
        compiler_params=_params(),
    )(rpb_flat)


def _na_row(b, i, nrows, qrows):
    r = b * qrows + i
    rs = jnp.clip(r - WIN_R // 2, 0, nrows - WIN_R)
    return pl.ds(pl.multiple_of(rs * GRID_W, GRID_W), NA_KEYS), rs - r + (WIN_R - 1)


def _softmax(s):
    e = jnp.exp(s - jnp.max(s, axis=-1, keepdims=True))
    return e * (1.0 / jnp.sum(e, axis=-1, keepdims=True))


_NT = (((1,), (1,)), ((), ()))
_NN = (((1,), (0,)), ((), ()))
_TN = (((0,), (0,)), ((), ()))


def _dot(a, b, dn):
    return lax.dot_general(a, b, dn, preferred_element_type=F32)


def _call(body, comm, *, name, grid, operands, in_specs, out_specs, out_shape, scratch_shapes):
    if comm is not None:
        body, more_operands, more_specs, more_shapes, sems = _host_exchange(body, comm, grid, len(operands), len(out_shape))
        operands = operands + more_operands
        in_specs = in_specs + more_specs
        out_specs = out_specs + more_specs
        out_shape = out_shape + more_shapes
        scratch_shapes = scratch_shapes + sems
    res = pl.pallas_call(body, name=name, grid=grid, in_specs=in_specs, out_specs=out_specs, out_shape=out_shape,
                         scratch_shapes=scratch_shapes, compiler_params=_params())(*operands)
    return res[0] if len(res) == 1 else res


def _na_fwd(q, k, v, tb, name, comm=None):
    nh, s, _ = q.shape
    nrows = s // GRID_W
    qrows = _tile(nrows, NA_QROWS, WIN_R)
    tq = qrows * GRID_W

    def body(q_ref, k_ref, v_ref, tb_ref, o_ref, s_scr, p_scr):
        b = pl.program_id(1)
        rows = [slice(i * GRID_W, (i + 1) * GRID_W) for i in range(qrows)]
        at = [_na_row(b, i, nrows, qrows) for i in range(qrows)]
        for i, (keys, dr0) in enumerate(at):
            s_scr[i] = _dot(q_ref[rows[i], :], k_ref[keys, :], _NT) * SCALE + tb_ref[0, dr0]
        for i in range(qrows):
            p_scr[i] = _softmax(s_scr[i]).astype(BF16)
        for i, (keys, _) in enumerate(at):
            o_ref[rows[i], :] = _dot(p_scr[i], v_ref[keys, :], _NN)

    qspec = pl.BlockSpec((None, tq, HEAD), lambda h, b: (h, b, 0))
    full = pl.BlockSpec((None, s, HEAD), lambda h, b: (h, 0, 0))
    return _call(
        body, comm, name=name, grid=(nh, nrows // qrows), operands=[q, k, v, tb],
        in_specs=[qspec, full, full, pl.BlockSpec((1, WIN_R, GRID_W, NA_KEYS), lambda h, b: (h, 0, 0, 0))],
        out_specs=[qspec], out_shape=[jax.ShapeDtypeStruct((nh, s, HEAD), F32)],
        scratch_shapes=[pltpu.VMEM((qrows, GRID_W, NA_KEYS), F32), pltpu.VMEM((qrows, GRID_W, NA_KEYS), BF16)])


def _na_bwd(q, k, v, tb, do, name, comm=None):
    nh, s, _ = q.shape
    nrows = s // GRID_W
    qrows = _tile(nrows, NA_QROWS, WIN_R)
    tq = qrows * GRID_W

    def body(q_ref, do_ref, k_ref, v_ref, tb_ref, dq_ref, dk_ref, dv_ref, dtb_ref, s_scr, dp_scr, p_scr, ds_scr):
        b = pl.program_id(1)

        @pl.when(b == 0)
        def _():
            dk_ref[...] = jnp.zeros_like(dk_ref)
            dv_ref[...] = jnp.zeros_like(dv_ref)
            dtb_ref[...] = jnp.zeros_like(dtb_ref)

        rows = [slice(i * GRID_W, (i + 1) * GRID_W) for i in range(qrows)]
        at = [_na_row(b, i, nrows, qrows) for i in range(qrows)]
        for i, (keys, dr0) in enumerate(at):
            s_scr[i] = _dot(q_ref[rows[i], :], k_ref[keys, :], _NT) * SCALE + tb_ref[0, dr0]
            dp_scr[i] = _dot(do_ref[rows[i], :], v_ref[keys, :], _NT)
        for i in range(qrows):
            p = _softmax(s_scr[i])
            dp = dp_scr[i]
            ds = p * (dp - jnp.sum(p * dp, axis=-1, keepdims=True))
            p_scr[i] = p.astype(BF16)
            s_scr[i] = ds
            ds_scr[i] = (ds * SCALE).astype(BF16)
        for i, (keys, _) in enumerate(at):
            dq_ref[rows[i], :] = _dot(ds_scr[i], k_ref[keys, :], _NN)
        for i, (keys, dr0) in enumerate(at):
            dv_ref[keys, :] += _dot(p_scr[i], do_ref[rows[i], :], _TN)
            dk_ref[keys, :] += _dot(ds_scr[i], q_ref[rows[i], :], _TN)
            dtb_ref[0, dr0] += s_scr[i]

    qspec = pl.BlockSpec((None, tq, HEAD), lambda h, b: (h, b, 0))
    full = pl.BlockSpec((None, s, HEAD), lambda h, b: (h, 0, 0))
    tbs = pl.BlockSpec((1, WIN_R, GRID_W, NA_KEYS), lambda h, b: (h, 0, 0, 0))
    hm = jax.ShapeDtypeStruct((nh, s, HEAD), F32)
    tile = (qrows, GRID_W, NA_KEYS)
    return _call(
        body, comm, name=name, grid=(nh, nrows // qrows), operands=[q, do, k, v, tb],
        in_specs=[qspec, qspec, full, full, tbs],
        out_specs=[qspec, full, full, tbs],
        out_shape=[hm, hm, hm, jax.ShapeDtypeStruct((nh, WIN_R, GRID_W, NA_KEYS), F32)],
        scratch_shapes=[pltpu.VMEM(tile, F32), pltpu.VMEM(tile, F32), pltpu.VMEM(tile, BF16), pltpu.VMEM(tile, BF16)])


def _rpb_fold(y, n_heads, name):
    def body(y_ref, o_ref):
        for h in range(n_heads):
            for dr in range(2 * WIN_R):
                acc = jnp.zeros((1, LANES), F32)
                for dr0 in range(WIN_R):
                    w = dr - dr0
                    if 0 <= w < WIN_R:
                        acc = acc + y_ref[h, dr0, w:w + 1, :]
                o_ref[h, dr:dr + 1, :] = acc

    return pl.pallas_call(
        body, name=name, out_shape=jax.ShapeDtypeStruct((n_heads, 2 * WIN_R, LANES), F32),
    )(y)


def _rpb_grad(dtb, onehot, name):
    nh = dtb.shape[0]
    rows = dtb.reshape(nh, WIN_R, GRID_W, WIN_R, GRID_W).transpose(0, 1, 3, 2, 4).reshape(nh * WIN_R * WIN_R, GRID_W * GRID_W)
    y = _matmul(rows, onehot, dims="nn", ti=rows.shape[0], tj=LANES, tk=GRID_W * GRID_W, out_dtype=F32, name=name + "_dc")
    folded = _rpb_fold(y.reshape(nh, WIN_R, WIN_R, LANES), nh, name + "_dr")
    return folded[:, :N_DR, :N_DC]


WA_WIN_TOK = 3 * BAND
WA_QBLOCKS = 8


def _wa_scores(q, kwin, t0, j, sink_ref, head0, grp):
    rows = grp * BAND
    s = _dot(q, kwin, _NT) * SCALE
    row = lax.broadcasted_iota(jnp.int32, (rows, WA_WIN_TOK), 0)
    qpos = j * BAND + (row & (BAND - 1))
    kpos = t0 + lax.broadcasted_iota(jnp.int32, (rows, WA_WIN_TOK), 1)
    s = jnp.where(jnp.abs(kpos - qpos) <= BAND, s, NEG)
    head = lax.broadcasted_iota(jnp.int32, (rows, 1), 0) // BAND
    sink = jnp.zeros((rows, 1), F32) + sink_ref[head0]
    for g in range(1, grp):
        sink = jnp.where(head == g, sink_ref[head0 + g], sink)
    m = jnp.maximum(jnp.max(s, axis=-1, keepdims=True), sink)
    e = jnp.exp(s - m)
    es = jnp.exp(sink - m)
    rz = 1.0 / (jnp.sum(e, axis=-1, keepdims=True) + es)
    return e * rz, es * rz


def _wa_window(j, s):
    return pl.multiple_of(jnp.clip((j - 1) * BAND, 0, s - WA_WIN_TOK), BAND)


def _wa_fwd(q, k, v, sink, name, comm=None):
    hq, s, _ = q.shape
    hkv = k.shape[0]
    grp = hq // hkv

    def body(sink_ref, q_ref, k_ref, v_ref, o_ref):
        kh, step = pl.program_id(0), pl.program_id(1)
        for sub in range(WA_QBLOCKS):
            j = step * WA_QBLOCKS + sub
            rows = slice(sub * BAND, (sub + 1) * BAND)
            t0 = _wa_window(j, s)
            keys = pl.ds(t0, WA_WIN_TOK)
            p, _ = _wa_scores(q_ref[:, rows, :].reshape(grp * BAND, HEAD), k_ref[keys, :], t0, j, sink_ref, kh * grp, grp)
            o_ref[:, rows, :] = _dot(p.astype(BF16), v_ref[keys, :], _NN).reshape(grp, BAND, HEAD)

    qspec = pl.BlockSpec((grp, WA_QBLOCKS * BAND, HEAD), lambda kh, j: (kh, j, 0))
    full = pl.BlockSpec((None, s, HEAD), lambda kh, j: (kh, 0, 0))
    return _call(
        body, comm, name=name, grid=(hkv, s // (WA_QBLOCKS * BAND)), operands=[sink, q, k, v],
        in_specs=[SMEM, qspec, full, full],
        out_specs=[qspec], out_shape=[jax.ShapeDtypeStruct((hq, s, HEAD), F32)], scratch_shapes=[])


def _wa_bwd(q, k, v, sink, do, name, comm=None):
    hq, s, _ = q.shape
    hkv = k.shape[0]
    grp = hq // hkv

    def body(sink_ref, q_ref, do_ref, k_ref, v_ref, dq_ref, dk_ref, dv_ref, dsink_ref):
        kh, step = pl.program_id(0), pl.program_id(1)

        @pl.when(step == 0)
        def _():
            dk_ref[...] = jnp.zeros_like(dk_ref)
            dv_ref[...] = jnp.zeros_like(dv_ref)
            dsink_ref[...] = jnp.zeros_like(dsink_ref)

        for sub in range(WA_QBLOCKS):
            j = step * WA_QBLOCKS + sub
            rows = slice(sub * BAND, (sub + 1) * BAND)
            t0 = _wa_window(j, s)
            keys = pl.ds(t0, WA_WIN_TOK)
            qs = q_ref[:, rows, :].reshape(grp * BAND, HEAD)
            dos = do_ref[:, rows, :].reshape(grp * BAND, HEAD)
            kwin, vwin = k_ref[keys, :], v_ref[keys, :]
            p, ps = _wa_scores(qs, kwin, t0, j, sink_ref, kh * grp, grp)
            dp = _dot(dos, vwin, _NT)
            dv_ref[keys, :] += _dot(p.astype(BF16), dos, _TN)
            rowdot = jnp.sum(p * dp, axis=-1, keepdims=True)
            to_sink = ps * rowdot
            for g in range(grp):
                dsink_ref[g] += jnp.zeros((8, LANES), F32) - jnp.sum(to_sink[g * BAND:(g + 1) * BAND])
            dss = (p * (dp - rowdot) * SCALE).astype(BF16)
            dq_ref[:, rows, :] = _dot(dss, kwin, _NN).reshape(grp, BAND, HEAD)
            dk_ref[keys, :] += _dot(dss, qs, _TN)

    qspec = pl.BlockSpec((grp, WA_QBLOCKS * BAND, HEAD), lambda kh, j: (kh, j, 0))
    full = pl.BlockSpec((None, s, HEAD), lambda kh, j: (kh, 0, 0))
    kv = jax.ShapeDtypeStruct((hkv, s, HEAD), F32)
    return _call(
        body, comm, name=name, grid=(hkv, s // (WA_QBLOCKS * BAND)), operands=[sink, q, do, k, v],
        in_specs=[SMEM, qspec, qspec, full, full],
        out_specs=[qspec, full, full, pl.BlockSpec((grp, 8, LANES), lambda kh, j: (kh, 0, 0))],
        out_shape=[jax.ShapeDtypeStruct((hq, s, HEAD), F32), kv, kv, jax.ShapeDtypeStruct((hq, 8, LANES), F32)],
        scratch_shapes=[])


def _onorm_fwd(oa, ob, gains, name):
    ha, s, _ = oa.shape
    hq = ob.shape[0]
    ts = _tile(s, ROW_TILE, 16)

    def body(oa_ref, ob_ref, g_ref, o_ref):
        col = 0
        for ref, nh in ((oa_ref, ha), (ob_ref, hq)):
            ss = sum(jnp.sum(ref[h] * ref[h], axis=-1, keepdims=True) for h in range(nh))
            r = lax.rsqrt(ss / (nh * HEAD) + EPS)
            for h in range(nh):
                o_ref[:, col * HEAD:(col + 1) * HEAD] = (ref[h] * r * g_ref[:, col * HEAD:(col + 1) * HEAD]).astype(BF16)
                col += 1

    mix = (ha + hq) * HEAD
    return pl.pallas_call(
        body, name=name, grid=(s // ts,),
        in_specs=[pl.BlockSpec((ha, ts, HEAD), lambda i: (0, i, 0)), pl.BlockSpec((hq, ts, HEAD), lambda i: (0, i, 0)),
                  pl.BlockSpec((1, mix), lambda i: (0, 0))],
        out_specs=pl.BlockSpec((ts, mix), lambda i: (i, 0)),
        out_shape=jax.ShapeDtypeStruct((s, mix), BF16), compiler_params=_params(),
    )(oa, ob, gains)


def _onorm_bwd(oa, ob, gains, don, name):
    ha, s, _ = oa.shape
    hq = ob.shape[0]
    ts = _tile(s, ROW_TILE, 16)
    mix = (ha + hq) * HEAD

    def body(oa_ref, ob_ref, g_ref, don_ref, doa_ref, dob_ref, dg_ref):
        @pl.when(pl.program_id(0) == 0)
        def _():
            dg_ref[...] = jnp.zeros_like(dg_ref)

        col0 = 0
        for ref, d_ref, nh in ((oa_ref, doa_ref, ha), (ob_ref, dob_ref, hq)):
            ss = sum(jnp.sum(ref[h] * ref[h], axis=-1, keepdims=True) for h in range(nh))
            r = lax.rsqrt(ss / (nh * HEAD) + EPS)
            dot = jnp.zeros((ts, 1), F32)
            for h in range(nh):
                cols = slice((col0 + h) * HEAD, (col0 + h + 1) * HEAD)
                dot = dot + jnp.sum(don_ref[:, cols] * g_ref[:, cols] * ref[h], axis=-1, keepdims=True)
            mean = dot * r / (nh * HEAD)
            for h in range(nh):
                cols = slice((col0 + h) * HEAD, (col0 + h + 1) * HEAD)
                y = ref[h] * r
                dn = don_ref[:, cols]
                d_ref[h] = (r * (dn * g_ref[:, cols] - y * mean)).astype(BF16)
                dg_ref[0:1, cols] += jnp.sum(dn * y, axis=0, keepdims=True)
            col0 += nh

    return pl.pallas_call(
        body, name=name, grid=(s // ts,),
        in_specs=[pl.BlockSpec((ha, ts, HEAD), lambda i: (0, i, 0)), pl.BlockSpec((hq, ts, HEAD), lambda i: (0, i, 0)),
                  pl.BlockSpec((1, mix), lambda i: (0, 0)), pl.BlockSpec((ts, mix), lambda i: (i, 0))],
        out_specs=[pl.BlockSpec((ha, ts, HEAD), lambda i: (0, i, 0)), pl.BlockSpec((hq, ts, HEAD), lambda i: (0, i, 0)),
                   pl.BlockSpec((8, mix), lambda i: (0, 0))],
        out_shape=[jax.ShapeDtypeStruct((ha, s, HEAD), BF16), jax.ShapeDtypeStruct((hq, s, HEAD), BF16),
                   jax.ShapeDtypeStruct((8, mix), F32)],
        compiler_params=_params(),
    )(oa, ob, gains, don)


HALO = 8
PACKED = 16


def _halo_specs(ts, tc, col_off):
    per = ts // HALO
    cur = pl.BlockSpec((ts, tc), lambda j, i: (i, j + col_off))
    prev = pl.BlockSpec((HALO, tc), lambda j, i: (jnp.maximum(i * per - 1, 0), j + col_off))

    def nxt_map(n_blocks):
        return pl.BlockSpec((HALO, tc), lambda j, i: (jnp.minimum((i + 1) * per, n_blocks - 1), j + col_off))

    return cur, prev, nxt_map


def _sigmoid(x):
    return 1.0 / (1.0 + jnp.exp(-x))


def _ffn_tiles(s, f):
    return _tile(s, 512, 16), _tile(f, 512, LANES)


def _gate_fwd(u, cw, cb, f, name):
    s = u.shape[0]
    ts, tc = _ffn_tiles(s, f)
    nj, ni = f // tc, s // ts

    def body(g_ref, gp_ref, gn_ref, u_ref, up_ref, un_ref, wg_ref, wu_ref, bg_ref, bu_ref, a_ref, gu_ref):
        i = pl.program_id(1)

        def conv(c_ref, p_ref, n_ref, w_ref, b_ref):
            ext = jnp.concatenate([jnp.where(i > 0, p_ref[...], 0.0), c_ref[...], jnp.where(i < ni - 1, n_ref[...], 0.0)], axis=0)
            rows = ts + 2 * HALO
            out = (pltpu.roll(ext, 1, axis=0) * w_ref[0:1, :] + ext * w_ref[1:2, :]
                   + pltpu.roll(ext, rows - 1, axis=0) * w_ref[2:3, :] + b_ref[...])
            return out[HALO:HALO + ts]

        gate = conv(g_ref, gp_ref, gn_ref, wg_ref, bg_ref)
        up = conv(u_ref, up_ref, un_ref, wu_ref, bu_ref)
        gu_ref[0] = gate.astype(BF16)
        gu_ref[1] = up.astype(BF16)
        a_ref[...] = (gate * _sigmoid(gate) * up).astype(BF16)

    gc, gp, gn = _halo_specs(ts, tc, 0)
    uc, up_, un = _halo_specs(ts, tc, nj)
    wg = pl.BlockSpec((3, tc), lambda j, i: (0, j))
    wu = pl.BlockSpec((3, tc), lambda j, i: (0, j + nj))
    bg = pl.BlockSpec((1, tc), lambda j, i: (0, j))
    bu = pl.BlockSpec((1, tc), lambda j, i: (0, j + nj))
    return pl.pallas_call(
        body, name=name, grid=(nj, ni),
        in_specs=[gc, gp, gn(s // HALO), uc, up_, un(s // HALO), wg, wu, bg, bu],
        out_specs=[pl.BlockSpec((ts, tc), lambda j, i: (i, j)), pl.BlockSpec((2, ts, tc), lambda j, i: (0, i, j))],
        out_shape=[jax.ShapeDtypeStruct((s, f), BF16), jax.ShapeDtypeStruct((2, s, f), BF16)], compiler_params=_params(),
    )(u, u, u, u, u, u, cw, cw, cb, cb)


def _ffn_bwd(gu, u, da, cw, name):
    _, s, f = gu.shape
    ts, tc = _ffn_tiles(s, f)
    nj, ni = f // tc, s // ts

    def body(gu_ref, gup_ref, gun_ref, da_ref, dap_ref, dan_ref, xg_ref, xu_ref, wg_ref, wu_ref,
             du_ref, dcw_ref, dcb_ref):
        i = pl.program_id(1)

        @pl.when(i == 0)
        def _():
            dcw_ref[...] = jnp.zeros_like(dcw_ref)
            dcb_ref[...] = jnp.zeros_like(dcb_ref)

        rows = ts + 2 * HALO
        mid = slice(HALO, HALO + ts)
        da = jnp.concatenate([jnp.where(i > 0, dap_ref[...], 0.0), da_ref[...], jnp.where(i < ni - 1, dan_ref[...], 0.0)], axis=0)
        def rows_of(half):
            before = gup_ref[half].astype(F32)[PACKED - HALO:]
            after = gun_ref[half].astype(F32)[:HALO]
            return jnp.concatenate([before, gu_ref[half].astype(F32), after], axis=0)

        gate, up = rows_of(0), rows_of(1)
        sg = _sigmoid(gate)
        d_up = da * gate * sg
        d_gate = da * up * (sg * (1.0 + gate * (1.0 - sg)))
        for half, (dd, x_ref, w_ref) in enumerate(((d_gate, xg_ref, wg_ref), (d_up, xu_ref, wu_ref))):
            before = pltpu.roll(dd, 1, axis=0)
            after = pltpu.roll(dd, rows - 1, axis=0)
            du_ref[half] = (before * w_ref[2:3, :] + dd * w_ref[1:2, :] + after * w_ref[0:1, :])[mid].astype(BF16)
            x = x_ref[...]
            dcb_ref[half, 0:1, :] += jnp.sum(dd[mid], axis=0, keepdims=True)
            for k, shifted in enumerate((after, dd, before)):
                dcw_ref[half, k, 0:1, :] += jnp.sum(shifted[mid] * x, axis=0, keepdims=True)

    per = ts // PACKED
    cur3 = pl.BlockSpec((2, ts, tc), lambda j, i: (0, i, j))
    prev3 = pl.BlockSpec((2, PACKED, tc), lambda j, i: (0, jnp.maximum(i * per - 1, 0), j))
    next3 = pl.BlockSpec((2, PACKED, tc), lambda j, i: (0, jnp.minimum((i + 1) * per, s // PACKED - 1), j))
    cur, prev, nxt = _halo_specs(ts, tc, 0)
    return pl.pallas_call(
        body, name=name, grid=(nj, ni),
        in_specs=[cur3, prev3, next3, cur, prev, nxt(s // HALO),
                  pl.BlockSpec((ts, tc), lambda j, i: (i, j)), pl.BlockSpec((ts, tc), lambda j, i: (i, j + nj)),
                  pl.BlockSpec((3, tc), lambda j, i: (0, j)), pl.BlockSpec((3, tc), lambda j, i: (0, j + nj))],
        out_specs=[cur3, pl.BlockSpec((2, 3, 8, tc), lambda j, i: (0, 0, 0, j)),
                   pl.BlockSpec((2, 8, tc), lambda j, i: (0, 0, j))],
        out_shape=[jax.ShapeDtypeStruct((2, s, f), BF16),
                   jax.ShapeDtypeStruct((2, 3, 8, f), F32), jax.ShapeDtypeStruct((2, 8, f), F32)],
        compiler_params=_params(),
    )(gu, gu, gu, da, da, da, u, u, cw, cw)


def _loss_head(y, target, name):
    s, d = y.shape
    ts = _tile(s, ROW_TILE, 16)

    def body(y_ref, t_ref, dy_ref, dyb_ref, l_ref):
        @pl.when(pl.program_id(0) == 0)
        def _():
            l_ref[...] = jnp.zeros_like(l_ref)

        err = y_ref[...] - t_ref[...]
        dy = err / d
        dy_ref[...] = dy
        dyb_ref[...] = dy.astype(BF16)
        l_ref[...] += jnp.zeros((8, LANES), F32) + 0.5 * jnp.sum(jnp.sum(err * err, axis=-1, keepdims=True) / d)

    blk = pl.BlockSpec((ts, d), lambda i: (i, 0))
    return pl.pallas_call(
        body, name=name, grid=(s // ts,),
        in_specs=[blk, blk], out_specs=[blk, blk, pl.BlockSpec((8, LANES), lambda i: (0, 0))],
        out_shape=[jax.ShapeDtypeStruct((s, d), F32), jax.ShapeDtypeStruct((s, d), BF16), jax.ShapeDtypeStruct((8, LANES), F32)],
        compiler_params=_params(),
    )(y, target)


SMALL = ("ln1_g", "qn_a", "kn_a", "rpb", "qn_b", "kn_b", "sink", "on_a", "on_b", "ln2_g", "conv_b", "conv_w")
PACK_ALIGN = 8 * LANES


def _pack(arrays):
    flat = []
    for a in arrays:
        a = a.reshape(-1)
        flat.append(jnp.pad(a, (0, -a.size % PACK_ALIGN)))
    return jnp.concatenate(flat).reshape(-1, LANES)


def _unpack(packed, like):
    out, at = [], 0
    flat = packed.reshape(-1)
    for a in like:
        out.append(flat[at:at + a.size].reshape(a.shape))
        at += a.size + (-a.size % PACK_ALIGN)
    return out


def kernel(x, positions, ln1_g, w_in, qn_a, kn_a, rpb, qn_b, kn_b, sink, on_a, on_b, w_out, ln2_g, w_up, conv_w, conv_b, w_down, loss_target, m_ln1_g, m_w_in, m_qn_a, m_kn_a, m_rpb, m_qn_b, m_kn_b, m_sink, m_on_a, m_on_b, m_w_out, m_ln2_g, m_w_up, m_conv_w, m_conv_b, m_w_down, v_ln1_g, v_w_in, v_qn_a, v_kn_a, v_rpb, v_qn_b, v_kn_b, v_sink, v_on_a, v_on_b, v_w_out, v_ln2_g, v_w_up, v_conv_w, v_conv_b, v_w_down):
    weights = dict(ln1_g=ln1_g, w_in=w_in, qn_a=qn_a, kn_a=kn_a, rpb=rpb, qn_b=qn_b, kn_b=kn_b, sink=sink, on_a=on_a,
                   on_b=on_b, w_out=w_out, ln2_g=ln2_g, w_up=w_up, conv_w=conv_w, conv_b=conv_b, w_down=w_down)
    mom1 = dict(ln1_g=m_ln1_g, w_in=m_w_in, qn_a=m_qn_a, kn_a=m_kn_a, rpb=m_rpb, qn_b=m_qn_b, kn_b=m_kn_b, sink=m_sink,
                on_a=m_on_a, on_b=m_on_b, w_out=m_w_out, ln2_g=m_ln2_g, w_up=m_w_up, conv_w=m_conv_w, conv_b=m_conv_b,
                w_down=m_w_down)
    mom2 = dict(ln1_g=v_ln1_g, w_in=v_w_in, qn_a=v_qn_a, kn_a=v_kn_a, rpb=v_rpb, qn_b=v_qn_b, kn_b=v_kn_b, sink=v_sink,
                on_a=v_on_a, on_b=v_on_b, w_out=v_w_out, ln2_g=v_ln2_g, w_up=v_w_up, conv_w=v_conv_w, conv_b=v_conv_b,
                w_down=v_w_down)
    order = ("ln1_g", "w_in", "qn_a", "kn_a", "rpb", "qn_b", "kn_b", "sink", "on_a", "on_b", "w_out", "ln2_g", "w_up",
             "conv_w", "conv_b", "w_down")

    depth, d = ln1_g.shape
    s = x.shape[1]
    ha = on_a.shape[1] // HEAD
    hq = on_b.shape[1] // HEAD
    pw = w_in.shape[2] * N_DEV
    hkv = (pw - 3 * ha * HEAD - hq * HEAD) // (2 * HEAD)
    f = w_down.shape[1] * N_DEV
    mix = (ha + hq) * HEAD
    cfg = (ha, hq, hkv)
    fs = conv_w.shape[2]
    dev = 4 * lax.axis_index("x") + 2 * lax.axis_index("y") + lax.axis_index("c")
    core = lax.axis_index("c").astype(jnp.int32).reshape(1)

    shard = {n: weights[n].astype(BF16) for n in ("w_in", "w_out", "w_up", "w_down")}

    def unshard(n, g):
        if n in ("w_in", "w_up"):
            return g.transpose(1, 0, 2).reshape(g.shape[1], N_DEV * g.shape[2])
        return g.reshape(N_DEV * g.shape[1], g.shape[2])

    full = {n: [None] * depth for n in shard}
    half_d = _tile(d, d // 2, 16)
    up0 = [shard["w_up"][0][:half_d], shard["w_up"][0][half_d:]]

    def travel(l, host):
        plan = {}
        if l == 0:
            plan = {"ln1": [("w_in", 0, None)], "proj": [("w_out", 0, None), ("w_up", 0, 0)], "na": [("w_up", 0, 1)],
                    "wa": [("w_down", 0, None)], "up": [("w_up", 1, None), ("w_in", 1, None)]}
        elif l + 1 < depth:
            plan = {"proj": [("w_in", l + 1, None)], "up": [("w_up", l + 1, None)]}
        if l + 1 < depth:
            plan.update({"out": [("w_out", l + 1, None)], "down": [("w_down", l + 1, None)]})
        return [key for key in plan.get(host, []) if key[1] < depth]

    arrived = {}

    def in_place(key):
        return key[0] == "w_up" and key[2] is None and fs % LANES == 0

    def gather_of(keys):
        if not keys:
            return None
        return _gather_comm([shard[n][k] if part is None else up0[part] for n, k, part in keys], [in_place(key) for key in keys])

    def landed(keys, blocks):
        for (n, k, part), g in zip(keys, blocks):
            if in_place((n, k, part)):
                full[n][k] = g
            elif part is None:
                full[n][k] = unshard(n, g)
            else:
                arrived[part] = g
                if len(arrived) == 2:
                    full[n][k] = unshard(n, jnp.concatenate([arrived[0], arrived[1]], axis=1))

    cw_rows = depth * 3
    cw_pad = jnp.pad(conv_w.reshape(cw_rows, fs), ((0, -cw_rows % 8), (0, 0)))
    g_cw = _allgather(cw_pad, "gather_conv_w")
    full_cw = g_cw[:, :cw_rows].reshape(N_DEV, depth, 3, fs).transpose(1, 2, 0, 3).reshape(depth, 3, 2 * f)

    inv = ROPE_THETA ** (-jnp.arange(0, HEAD, 2, dtype=F32) / HEAD)
    ang = positions.astype(F32)[:, None] * inv[None, :]
    cos = jnp.concatenate([jnp.cos(ang), jnp.cos(ang)], axis=-1)
    sin = jnp.concatenate([-jnp.sin(ang), jnp.sin(ang)], axis=-1)
    qk = jnp.arange(GRID_W * GRID_W)
    dc_of = (qk % GRID_W) - (qk // GRID_W) + (WIN_C - 1)
    onehot = (dc_of[:, None] == jnp.arange(LANES)[None, :]).astype(BF16)

    tiles_s = _tile(s, 512, 16)
    tiles_l = _tile(s, 1024, 16)

    xs = x.reshape(s, d)
    saved = []
    for l in range(depth):
        def hosting(call, host, name, *args, **kw):
            keys = travel(l, host)
            if not keys:
                return call(*args, name=name, **kw)
            out, *blocks = call(*args, name=f"{name}_g{len(keys)}", comm=gather_of(keys), **kw)
            landed(keys, blocks)
            return out

        def fwd_matmul(n, host, a_op, name, **kw):
            return hosting(lambda **k2: _matmul(a_op, full[n][l], dims="nn", out_dtype=F32, **k2), host, name, **kw)

        gains = jnp.zeros((8, HEAD), F32).at[0].set(qn_a[l]).at[1].set(kn_a[l]).at[2].set(qn_b[l]).at[3].set(kn_b[l])
        on_g = jnp.concatenate([on_a[l], on_b[l]]).reshape(1, mix)
        h = hosting(_rms_fwd, "ln1", "ln1_fwd", xs, ln1_g[l].reshape(1, d))
        proj = fwd_matmul("w_in", "proj", h, "proj_fwd", ti=tiles_l, tj=_tile(pw, 1536, LANES), tk=d)
        qa, ka, va, qb, kb, vb = _qkv_fwd(proj, gains, cos, sin, cfg, "qkv_fwd")
        tb = _na_bias(rpb[l].reshape(-1), ha, "na_bias")
        oa = hosting(_na_fwd, "na", "na_fwd", qa, ka, va, tb)
        ob = hosting(_wa_fwd, "wa", "wa_fwd", qb, kb, vb, sink[l])
        o_n = _onorm_fwd(oa, ob, on_g, "onorm_fwd")
        x1 = fwd_matmul("w_out", "out", o_n, "out_fwd", ti=tiles_l, tj=_tile(d, 1024, LANES), tk=mix, resid=xs)
        h2 = _rms_fwd(x1, ln2_g[l].reshape(1, d), "ln2_fwd")
        u = fwd_matmul("w_up", "up", h2, "up_fwd", ti=tiles_l, tj=_tile(2 * f, 1024, 2 * LANES), tk=d)
        a, gu = _gate_fwd(u, full_cw[l], conv_b[l].reshape(1, 2 * f), f, "gate_fwd")
        x2 = fwd_matmul("w_down", "down", a, "down_fwd", ti=tiles_s, tj=_tile(d, 512, LANES), tk=f, resid=x1)
        saved.append(dict(x=xs, h=h, proj=proj, gains=gains, on_g=on_g, qkv=(qa, ka, va, qb, kb, vb), tb=tb, oa=oa, ob=ob,
                          o_n=o_n, x1=x1, h2=h2, u=u, gu=gu, a=a))
        xs = x2

    dx, dx_b, loss_part = _loss_head(xs, loss_target.reshape(s, d), "loss_head")
    tile_c = _tile(s, 2048, 16)
    half_k = _tile(2 * f, f, fs)
    loss = lax.psum(loss_part[0, 0], ("x", "y", "c"))

    small_grads = [None] * depth
    big = {n: None for n in ("w_in", "w_out", "w_up", "w_down")}
    pending = None
    for l in reversed(range(depth)):
        sv = saved[l]
        qa, ka, va, qb, kb, vb = sv["qkv"]

        def update(n, layer, got):
            big[n] = _adamw(got, weights[n], mom1[n], mom2[n], "adamw_" + n, layer=layer, into=big[n])

        def carrying(n, name, **kw):
            if pending is None or n is None:
                return _matmul(name=name, **kw)
            out, got = _matmul(name=name + "_carry", comm=_scatter_comm([pending[n]]), **kw)
            update(n, l + 1, got)
            return out

        def grad_matmul(n, a_op, b_op, name, **kw):
            carried = n if n in ("w_up", "w_down") else None
            return carrying(carried, name, a=a_op, b=b_op, dims="tn", out_dtype=BF16, **kw)

        def own(blocks):
            return _scatter_comm([blocks]) if l == 0 else None

        gw_down = grad_matmul("w_down", sv["a"], dx_b, "down_bwd_w", ti=_tile(f, 1408, LANES), tj=_tile(d, 1024, LANES),
                              tk=tile_c, j_outer=False)
        da = carrying("w_in", "down_bwd_x", a=dx_b, b=full["w_down"][l], dims="nt", ti=tiles_s, tj=_tile(f, 2816, 2 * LANES),
                      tk=d, out_dtype=F32)
        du, dcw, dcb = _ffn_bwd(sv["gu"], sv["u"], da, full_cw[l], "ffn_bwd")
        gw_down = gw_down.reshape(N_DEV, f // N_DEV, d)
        dh2 = None
        for part in range(2 * f // half_k):
            comm = own(gw_down) if part == 0 else None
            dh2 = _matmul(du, full["w_up"][l], dims="nt", ti=tiles_s, tj=_tile(d, 1024, LANES), tk=half_k, out_dtype=F32,
                          name=f"up_bwd_x{part}" + "_own" * bool(comm), k_blocks=(part, 1), resid=dh2, halved="a", comm=comm)
            if comm:
                dh2, got = dh2
                update("w_down", 0, got)
        gw_up = grad_matmul("w_up", sv["h2"], du, "up_bwd_w", ti=_tile(d, 1024, LANES), tj=fs, tk=tile_c, dev_major=True,
                            halved="b")
        dx1, dx1_b, dln2 = _rms_bwd(sv["x1"], ln2_g[l].reshape(1, d), dh2, dx, "ln2_bwd")
        don = _matmul(dx1_b, full["w_out"][l], dims="nt", ti=tiles_l, tj=mix, tk=d, out_dtype=F32, name="out_bwd_x")
        gw_out = grad_matmul("w_out", sv["o_n"], dx1_b, "out_bwd_w", ti=_tile(mix, 1024, LANES), tj=_tile(d, 1024, LANES),
                             tk=tile_c, j_outer=False)
        gw_out = gw_out.reshape(N_DEV, mix // N_DEV, d)
        doa, dob, don_g = _onorm_bwd(sv["oa"], sv["ob"], sv["on_g"], don, "onorm_bwd")
        comm = _chip_comm([_pair_sums(gw_up, core, "rs0_w_up")]) if l == 0 else None
        dqa, dka, dva, dtb, *got = _na_bwd(qa, ka, va, sv["tb"], doa, "na_bwd" + "_own" * (l == 0), comm=comm)
        if got:
            update("w_up", 0, got[0])
        dqb, dkb, dvb, dsink, *got = _wa_bwd(qb, kb, vb, sink[l], dob, "wa_bwd" + "_own" * (l == 0), comm=own(gw_out))
        if got:
            update("w_out", 0, got[0])
        drpb = _rpb_grad(dtb, onehot, "rpb_grad")
        dproj, dgains = _qkv_bwd(sv["proj"], sv["gains"], cos, sin, (dqa, dka, dva, dqb, dkb, dvb), cfg, "qkv_bwd")
        dh = carrying("w_out", "proj_bwd_x", a=dproj, b=full["w_in"][l], dims="nt", ti=tiles_s, tj=_tile(d, 1024, LANES), tk=pw,
                      out_dtype=F32)
        gw_in = grad_matmul("w_in", sv["h"], dproj, "proj_bwd_w", ti=_tile(d, 1024, LANES), tj=_tile(pw, 1536, LANES), tk=tile_c)
        gw_in = gw_in.reshape(d, N_DEV, pw // N_DEV).transpose(1, 0, 2)
        comm = _chip_comm([_pair_sums(gw_in, core, "rs0_w_in")]) if l == 0 else None
        dx, dx_b, dln1, *got = _rms_bwd(sv["x"], ln1_g[l].reshape(1, d), dh, dx1, "ln1_bwd" + "_own" * (l == 0), comm=comm)
        if got:
            update("w_in", 0, got[0])

        small_grads[l] = dict(
            ln1_g=dln1[0], qn_a=dgains[0], kn_a=dgains[1], rpb=drpb, qn_b=dgains[2], kn_b=dgains[3], sink=dsink[:, 0, 0],
            on_a=don_g[0, :ha * HEAD], on_b=don_g[0, ha * HEAD:], ln2_g=dln2[0],
            conv_b=dcb[:, 0, :].reshape(2 * f), conv_w=dcw[:, :, 0, :].transpose(1, 0, 2).reshape(3, 2 * f))

        pending = dict(w_in=gw_in, w_out=gw_out, w_up=gw_up, w_down=gw_down)

    grads_l = [small_grads[l][n] for l in range(depth) for n in SMALL]
    gathered = _allgather(_pack(grads_l), "gather_small")
    zeros_cw = jnp.zeros((3, 2 * f), F32)

    def small_state(src):
        return _pack([zeros_cw if n == "conv_w" else src[n][l] for l in range(depth) for n in SMALL])

    sm = _adamw(gathered, small_state(weights), small_state(mom1), small_state(mom2), "adamw_small")
    sm = [_unpack(t, grads_l) for t in sm]
    small_out = {n: [jnp.stack([sm[k][l * len(SMALL) + i] for l in range(depth)]) for k in range(4)]
                 for i, n in enumerate(SMALL)}
    cw_grad = lax.dynamic_slice_in_dim(small_out["conv_w"][0], dev * fs, fs, axis=2)
    cw_rows_pad = cw_rows + (-cw_rows % 8)

    def rows8(a):
        return jnp.pad(a.reshape(cw_rows, fs), ((0, cw_rows_pad - cw_rows), (0, 0)))

    cw_res = _adamw(rows8(cw_grad)[None], rows8(conv_w), rows8(m_conv_w), rows8(v_conv_w), "adamw_conv_w")
    small_out["conv_w"] = [t[:cw_rows].reshape(depth, 3, fs) for t in cw_res]

    results = {n: (big[n] if n in big else small_out[n]) for n in order}
    grad_x = dx.reshape(1, s, d)
    return (loss, grad_x, *[results[n][0] for n in order], *[results[n][1] for n in order],
            *[results[n][2] for n in order], *[results[n][3] for n in order])
```
